```python
import math
import jax, jax.numpy as jnp
from jax import lax
import numpy as np

D_MODEL = 1024
BATCH = 32
SEQ = 256
DEPTH = 4
DEC_BATCH = 4
DEC_SEQ = 4096
PAST_LEN = 256

GRID_W = 64
N_EVEN = (DEPTH + 1) // 2
N_ODD = DEPTH // 2
BRANCH = D_MODEL // 2
GLA_HEADS = 4
GLA_DK = BRANCH // (2 * GLA_HEADS)
GLA_DV = BRANCH // GLA_HEADS
GLA_RANK = 16
GLA_GATE_NORM = 16.0
GLA_CHUNK = 64
S5_GROUP = 16
S5_GROUPS = BRANCH // S5_GROUP
S5_STATE = 64
ATT_HEADS = 8
ATT_KV_HEADS = 2
ATT_GROUPS = ATT_HEADS // ATT_KV_HEADS
ATT_HD = BRANCH // ATT_HEADS
WINDOW = 128
BLK = 128
ROPE_BASE = 10000.0
CONV_W = 3
EPS = 1e-6
NEG = -1e30

E_SIZES = (GLA_HEADS * GLA_DK, GLA_HEADS * GLA_DK, BRANCH, GLA_RANK, GLA_RANK, BRANCH, BRANCH, BRANCH)
E_IN = sum(E_SIZES)
O_SIZES = (BRANCH, ATT_KV_HEADS * ATT_HD, ATT_KV_HEADS * ATT_HD, BRANCH, BRANCH, BRANCH, BRANCH, BRANCH)
O_IN = sum(O_SIZES)

kernel_name = "hybrid_gla_s5_swa_conv_diffusion_step"


def split_cols(x, sizes):
    out, start = [], 0
    for s in sizes:
        out.append(x[..., start:start + s])
        start += s
    return out


def rmsnorm(x, w):
    xf = x.astype(jnp.float32)
    y = xf * lax.rsqrt(jnp.mean(xf * xf, axis=-1, keepdims=True) + EPS) * w.astype(jnp.float32)
    return y.astype(x.dtype)


def adaln(cvec, w, b):
    return (jax.nn.silu(cvec) @ w + b)[:, None, :]


def modulate(x, norm_w, mod):
    shift, scale, gate = jnp.split(mod, 3, axis=-1)
    h = rmsnorm(x, norm_w) * (1.0 + scale) + shift
    return h, gate


def _rotate(x, ang):
    nf = ang.shape[-1]
    cos, sin = jnp.cos(ang), jnp.sin(ang)
    x1, x2 = x[..., :nf], x[..., nf:]
    return jnp.concatenate([x1 * cos - x2 * sin, x2 * cos + x1 * sin], axis=-1)


def rope_axial(x):
    L, d = x.shape[-2], x.shape[-1]
    rows = L // GRID_W
    row = jnp.repeat(jnp.arange(rows), GRID_W).astype(jnp.float32)
    col = jnp.tile(jnp.arange(GRID_W), rows).astype(jnp.float32)
    nf = d // 4
    freq = ROPE_BASE ** (-jnp.arange(nf, dtype=jnp.float32) / nf)
    xf = x.astype(jnp.float32)
    half = d // 2
    out = jnp.concatenate([_rotate(xf[..., :half], row[:, None] * freq),
                           _rotate(xf[..., half:], col[:, None] * freq)], axis=-1)
    return out.astype(x.dtype)


def gla_chunked(q, k, v, g, s0):
    Bsz, H, L, dk = q.shape
    dv = v.shape[-1]
    n = L // GLA_CHUNK

    def chunks(t):
        return t.astype(jnp.float32).reshape(Bsz, H, n, GLA_CHUNK, t.shape[-1]).transpose(2, 0, 1, 3, 4)

    causal = jnp.tril(jnp.ones((GLA_CHUNK, GLA_CHUNK), dtype=bool))[:, :, None]

    def step(S, inp):
        qc, kc, vc, gc = inp
        b = jnp.cumsum(gc, axis=2)
        o_inter = jnp.einsum('bhcd,bhde->bhce', qc * jnp.exp(b), S)
        diff = b[:, :, :, None, :] - b[:, :, None, :, :]
        decay = jnp.where(causal, jnp.exp(jnp.where(causal, diff, 0.0)), 0.0)
        att = jnp.einsum('bhid,bhjd,bhijd->bhij', qc, kc, decay)
        o = o_inter + jnp.einsum('bhij,bhje->bhie', att, vc)
        b_last = b[:, :, -1:, :]
        S_new = jnp.exp(b_last)[:, :, 0, :, None] * S + jnp.einsum('bhcd,bhce->bhde', kc * jnp.exp(b_last - b), vc)
        return S_new, o

    S, o = lax.scan(step, s0.astype(jnp.float32), (chunks(q), chunks(k), chunks(v), chunks(g)))
    o = o.transpose(1, 2, 0, 3, 4).reshape(Bsz, H, L, dv)
    return o, S


def gla_bidir(q, k, v, gf, gb, s0f, s0b):
    o_f, s_f = gla_chunked(q, k, v, gf, s0f)
    fl = lambda t: jnp.flip(t, axis=2)
    o_b, s_b = gla_chunked(fl(q), fl(k), fl(v), fl(gb), s0b)
    return o_f + fl(o_b), s_f, s_b


def s5_scan(u, lam_re, lam_im, log_dt, b_re, b_im, h0_re, h0_im):
    dt = jnp.exp(log_dt.astype(jnp.float32))[:, None]
    lam_re = lam_re.astype(jnp.float32)
    lam_im = lam_im.astype(jnp.float32)
    mag = jnp.exp(lam_re * dt)
    lb_re, lb_im = mag * jnp.cos(lam_im * dt), mag * jnp.sin(lam_im * dt)
    den = lam_re * lam_re + lam_im * lam_im
    nr, ni = lb_re - 1.0, lb_im
    cr = (nr * lam_re + ni * lam_im) / den
    ci = (ni * lam_re - nr * lam_im) / den
    b_re = b_re.astype(jnp.float32)
    b_im = b_im.astype(jnp.float32)
    bb_re = cr[..., None] * b_re - ci[..., None] * b_im
    bb_im = cr[..., None] * b_im + ci[..., None] * b_re
    bu_re = jnp.einsum('blgh,gph->blgp', u, bb_re)
    bu_im = jnp.einsum('blgh,gph->blgp', u, bb_im)
    h0_re = h0_re.astype(jnp.float32)
    h0_im = h0_im.astype(jnp.float32)
    bu_re = bu_re.at[:, 0].add(lb_re * h0_re - lb_im * h0_im)
    bu_im = bu_im.at[:, 0].add(lb_re * h0_im + lb_im * h0_re)
    a_re = jnp.broadcast_to(lb_re, bu_re.shape)
    a_im = jnp.broadcast_to(lb_im, bu_im.shape)

    def combine(e1, e2):
        ar1, ai1, br1, bi1 = e1
        ar2, ai2, br2, bi2 = e2
        return (ar2 * ar1 - ai2 * ai1, ar2 * ai1 + ai2 * ar1,
                ar2 * br1 - ai2 * bi1 + br2, ar2 * bi1 + ai2 * br1 + bi2)

    _, _, h_re, h_im = lax.associative_scan(combine, (a_re, a_im, bu_re, bu_im), axis=1)
    return h_re, h_im


def even_mixer(h, w_in, w_out, gla_w2, gla_b2, gla_onorm, lam_re, lam_im, log_dt, b_re, b_im,
               c_re, c_im, s5_d, w_glu, b_glu, gla_s0, s5_h0_re, s5_h0_im):
    Bsz, L, _ = h.shape
    q, k, v, lf, lb, z_gla, u, z_s5 = split_cols(h @ w_in, E_SIZES)
    heads = lambda t, d: t.reshape(Bsz, L, GLA_HEADS, d).transpose(0, 2, 1, 3)
    q = heads(q, GLA_DK) * (GLA_DK ** -0.5)
    k = heads(k, GLA_DK)
    v = heads(v, GLA_DV)
    gate = lambda lr, d: heads(jax.nn.log_sigmoid(lr.astype(jnp.float32) @ gla_w2[d].astype(jnp.float32)
                                                  + gla_b2[d].astype(jnp.float32)) / GLA_GATE_NORM, GLA_DK)
    o, s_f, s_b = gla_bidir(q, k, v, gate(lf, 0), gate(lb, 1), gla_s0[:, 0], gla_s0[:, 1])
    o = rmsnorm(o, gla_onorm).transpose(0, 2, 1, 3).reshape(Bsz, L, BRANCH)
    y_gla = o.astype(h.dtype) * jax.nn.silu(z_gla)

    uf = u.astype(jnp.float32)
    ug = uf.reshape(Bsz, L, S5_GROUPS, S5_GROUP)
    hf_re, hf_im = s5_scan(ug, lam_re[0], lam_im[0], log_dt[0], b_re, b_im, s5_h0_re[:, 0], s5_h0_im[:, 0])
    hb_re, hb_im = s5_scan(jnp.flip(ug, axis=1), lam_re[1], lam_im[1], log_dt[1], b_re, b_im,
                           s5_h0_re[:, 1], s5_h0_im[:, 1])
    last_re = jnp.stack([hf_re[:, -1], hb_re[:, -1]], axis=1)
    last_im = jnp.stack([hf_im[:, -1], hb_im[:, -1]], axis=1)
    h_re = hf_re + jnp.flip(hb_re, axis=1)
    h_im = hf_im + jnp.flip(hb_im, axis=1)
    y5 = (jnp.einsum('blgp,ghp->blgh', h_re, c_re.astype(jnp.float32))
          - jnp.einsum('blgp,ghp->blgh', h_im, c_im.astype(jnp.float32))).reshape(Bsz, L, BRANCH)
    y5 = y5 + s5_d.astype(jnp.float32) * uf
    z = jax.nn.gelu(y5)
    y5 = z * jax.nn.sigmoid(z @ w_glu.astype(jnp.float32) + b_glu.astype(jnp.float32))
    y_s5 = y5.astype(h.dtype) * jax.nn.silu(z_s5)

    out = jnp.concatenate([y_gla, y_s5], axis=-1) @ w_out
    return out, jnp.stack([s_f, s_b], axis=1), last_re, last_im


def _attend(qblk, key_sets, sink):
    qf = qblk.astype(jnp.float32) * (ATT_HD ** -0.5)
    scores = []
    for kk, vv, mask in key_sets:
        s = jnp.einsum('bkgqd,bksd->bkgqs', qf, kk.astype(jnp.float32))
        if mask is not None:
            s = jnp.where(mask, s, NEG)
        scores.append(s)
    s_sink = jnp.broadcast_to(sink.astype(jnp.float32).reshape(1, ATT_KV_HEADS, ATT_GROUPS, 1, 1),
                              scores[0].shape[:-1] + (1,))
    p = jax.nn.softmax(jnp.concatenate(scores + [s_sink], axis=-1), axis=-1)
    out, off = 0.0, 0
    for (kk, vv, _), s in zip(key_sets, scores):
        n = s.shape[-1]
        out = out + jnp.einsum('bkgqs,bksd->bkgqd', p[..., off:off + n], vv.astype(jnp.float32))
        off += n
    return out


def ctx_attention(q, k, v, sink):
    Bsz, Hk, G, L, d = q.shape
    nb = L // BLK
    qb = q.reshape(Bsz, Hk, G, nb, BLK, d).transpose(3, 0, 1, 2, 4, 5)
    o = lax.map(lambda qblk: _attend(qblk, [(k, v, None)], sink), qb)
    return o.transpose(1, 2, 3, 0, 4, 5).reshape(Bsz, Hk, G, L, d)


def latent_attention(q, k, v, k_ctx, v_ctx, sink):
    Bsz, Hk, G, L, d = q.shape
    nb = L // BLK
    kp = jnp.pad(k, ((0, 0), (0, 0), (BLK, BLK), (0, 0)))
    vp = jnp.pad(v, ((0, 0), (0, 0), (BLK, BLK), (0, 0)))
    qi = jnp.arange(BLK)[:, None]
    kj = jnp.arange(3 * BLK)[None, :]
    band = jnp.abs(kj - BLK - qi) <= WINDOW

    def one(n):
        start = n * BLK
        qblk = lax.dynamic_slice_in_dim(q, start, BLK, axis=3)
        kw = lax.dynamic_slice_in_dim(kp, start, 3 * BLK, axis=2)
        vw = lax.dynamic_slice_in_dim(vp, start, 3 * BLK, axis=2)
        kpos = start - BLK + kj
        mask = band & (kpos >= 0) & (kpos < L)
        return _attend(qblk, [(kw, vw, mask), (k_ctx, v_ctx, None)], sink)

    o = lax.map(one, jnp.arange(nb))
    return o.transpose(1, 2, 3, 0, 4, 5).reshape(Bsz, Hk, G, L, d)


def short_conv(x, w, b):
    xp = jnp.pad(x, ((0, 0), (1, 1), (0, 0)))
    return xp[:, :-2] * w[0] + xp[:, 1:-1] * w[1] + xp[:, 2:] * w[2] + b


def odd_mixer(h, w_in, w_out, q_norm_w, k_norm_w, sink, conv_w, conv_b, k_ctx=None, v_ctx=None):
    Bsz, L, _ = h.shape
    q, k, v, z_att, xc, bg, cg, z_conv = split_cols(h @ w_in, O_SIZES)
    q = rmsnorm(q.reshape(Bsz, L, ATT_KV_HEADS, ATT_GROUPS, ATT_HD), q_norm_w).transpose(0, 2, 3, 1, 4)
    k = rmsnorm(k.reshape(Bsz, L, ATT_KV_HEADS, ATT_HD), k_norm_w).transpose(0, 2, 1, 3)
    v = v.reshape(Bsz, L, ATT_KV_HEADS, ATT_HD).transpose(0, 2, 1, 3)
    if k_ctx is None:
        o = ctx_attention(q, k, v, sink)
    else:
        o = latent_attention(rope_axial(q), rope_axial(k), v, k_ctx, v_ctx, sink)
    o = o.transpose(0, 3, 1, 2, 4).reshape(Bsz, L, BRANCH).astype(h.dtype)
    y_att = o * jax.nn.silu(z_att)
    y_conv = bg * short_conv(cg * xc, conv_w, conv_b) * jax.nn.silu(z_conv)
    out = jnp.concatenate([y_att, y_conv], axis=-1) @ w_out
    return out, k, v


def setup_inputs(seed: int = 0) -> dict:
    key = jax.random.key(seed)
    ks = iter(jax.random.split(key, 48))
    nrm = lambda shape, s: jax.random.normal(next(ks), shape, jnp.float32) * s
    lam_im = jnp.pi * jnp.arange(S5_STATE, dtype=jnp.float32)
    return {
        "x_prompt": nrm((BATCH, SEQ, D_MODEL), 1.0),
        "x_sample": nrm((DEC_BATCH, DEC_SEQ, D_MODEL), 1.0),
        "c": nrm((DEC_BATCH, D_MODEL), 1.0),
        "state_gla": nrm((DEC_BATCH, N_EVEN, 2, GLA_HEADS, GLA_DK, GLA_DV), 1.0),
        "state_s5_re": nrm((DEC_BATCH, N_EVEN, 2, S5_GROUPS, S5_STATE), 1.0),
        "state_s5_im": nrm((DEC_BATCH, N_EVEN, 2, S5_GROUPS, S5_STATE), 1.0),
        "cache_k": nrm((DEC_BATCH, N_ODD, ATT_KV_HEADS, PAST_LEN, ATT_HD), 1.0),
        "cache_v": nrm((DEC_BATCH, N_ODD, ATT_KV_HEADS, PAST_LEN, ATT_HD), 1.0),
        "c_ctx": nrm((D_MODEL,), 1.0),
        "norm_w": 1.0 + nrm((DEPTH, D_MODEL), 0.02),
        "w_ada": nrm((DEPTH, D_MODEL, 3 * D_MODEL), 0.5 * D_MODEL ** -0.5),
        "b_ada": nrm((DEPTH, 3 * D_MODEL), 0.01),
        "w_in_e": nrm((N_EVEN, D_MODEL, E_IN), D_MODEL ** -0.5),
        "w_out_e": nrm((N_EVEN, 2 * BRANCH, D_MODEL), (2 * BRANCH) ** -0.5),
        "gla_w2": nrm((N_EVEN, 2, GLA_RANK, GLA_HEADS * GLA_DK), GLA_RANK ** -0.5),
        "gla_b2": nrm((N_EVEN, 2, GLA_HEADS * GLA_DK), 0.01),
        "gla_onorm": 1.0 + nrm((N_EVEN, GLA_DV), 0.02),
        "s5_lam_re": -0.5 + nrm((N_EVEN, 2, S5_GROUPS, S5_STATE), 0.01),
        "s5_lam_im": lam_im + nrm((N_EVEN, 2, S5_GROUPS, S5_STATE), 0.01),
        "s5_log_dt": jax.random.uniform(next(ks), (N_EVEN, 2, S5_GROUPS), jnp.float32,
                                        minval=math.log(1e-3), maxval=math.log(1e-1)),
        "s5_b_re": nrm((N_EVEN, S5_GROUPS, S5_STATE, S5_GROUP), (2 * S5_GROUP) ** -0.5),
        "s5_b_im": nrm((N_EVEN, S5_GROUPS, S5_STATE, S5_GROUP), (2 * S5_GROUP) ** -0.5),
        "s5_c_re": nrm((N_EVEN, S5_GROUPS, S5_GROUP, S5_STATE), (2 * S5_STATE) ** -0.5),
        "s5_c_im": nrm((N_EVEN, S5_GROUPS, S5_GROUP, S5_STATE), (2 * S5_STATE) ** -0.5),
        "s5_d": nrm((N_EVEN, BRANCH), 0.5),
        "s5_w_glu": nrm((N_EVEN, BRANCH, BRANCH), BRANCH ** -0.5),
        "s5_b_glu": nrm((N_EVEN, BRANCH), 0.01),
        "w_in_o": nrm((N_ODD, D_MODEL, O_IN), D_MODEL ** -0.5),
        "w_out_o": nrm((N_ODD, 2 * BRANCH, D_MODEL), (2 * BRANCH) ** -0.5),
        "q_norm_w": 1.0 + nrm((N_ODD, ATT_HD), 0.02),
        "k_norm_w": 1.0 + nrm((N_ODD, ATT_HD), 0.02),
        "sink": nrm((N_ODD, ATT_HEADS), 1.0),
        "conv_w": nrm((N_ODD, CONV_W, BRANCH), CONV_W ** -0.5),
        "conv_b": nrm((N_ODD, BRANCH), 0.01),
    }


def reference(x_prompt, x_sample, c, state_gla, state_s5_re, state_s5_im, cache_k, cache_v,
              c_ctx, norm_w, w_ada, b_ada, w_in_e, w_out_e, gla_w2, gla_b2, gla_onorm,
              s5_lam_re, s5_lam_im, s5_log_dt, s5_b_re, s5_b_im, s5_c_re, s5_c_im, s5_d,
              s5_w_glu, s5_b_glu, w_in_o, w_out_o, q_norm_w, k_norm_w, sink, conv_w, conv_b):
    xp, xs = x_prompt, x_sample
    bp = x_prompt.shape[0]
    new_gla, new_s5_re, new_s5_im, new_k, new_v = [], [], [], [], []
    for l in range(DEPTH):
        e = l // 2
        mod_ctx = adaln(c_ctx[None, :], w_ada[l], b_ada[l])
        mod_lat = adaln(c, w_ada[l], b_ada[l])
        hp, gp = modulate(xp, norm_w[l], mod_ctx)
        hs, gs = modulate(xs, norm_w[l], mod_lat)
        if l % 2 == 0:
            ep = (w_in_e[e], w_out_e[e], gla_w2[e], gla_b2[e], gla_onorm[e], s5_lam_re[e], s5_lam_im[e],
                  s5_log_dt[e], s5_b_re[e], s5_b_im[e], s5_c_re[e], s5_c_im[e], s5_d[e], s5_w_glu[e], s5_b_glu[e])
            z_gla = jnp.zeros((bp, 2, GLA_HEADS, GLA_DK, GLA_DV), jnp.float32)
            z_s5 = jnp.zeros((bp, 2, S5_GROUPS, S5_STATE), jnp.float32)
            op, sg, sr, si = even_mixer(hp, *ep, z_gla, z_s5, z_s5)
            os_, _, _, _ = even_mixer(hs, *ep, state_gla[:, e], state_s5_re[:, e], state_s5_im[:, e])
            new_gla.append(sg)
            new_s5_re.append(sr)
            new_s5_im.append(si)
        else:
            op_params = (w_in_o[e], w_out_o[e], q_norm_w[e], k_norm_w[e], sink[e], conv_w[e], conv_b[e])
            op, kc, vc = odd_mixer(hp, *op_params)
            os_, _, _ = odd_mixer(hs, *op_params, k_ctx=cache_k[:, e], v_ctx=cache_v[:, e])
            new_k.append(kc)
            new_v.append(vc)
        xp = xp + gp * op
        xs = xs + gs * os_
    new_state_gla = jnp.stack(new_gla, axis=1)
    new_state_s5_re = jnp.stack(new_s5_re, axis=1)
    new_state_s5_im = jnp.stack(new_s5_im, axis=1)
    new_cache_k = jnp.stack(new_k, axis=1)
    new_cache_v = jnp.stack(new_v, axis=1)
    return (xp, xs, new_state_gla, new_state_s5_re, new_state_s5_im, new_cache_k, new_cache_v)
```

```python
import functools
import math

import jax
import jax.numpy as jnp
from jax import lax
from jax.experimental import pallas as pl
from jax.experimental.pallas import tpu as pltpu

F32 = jnp.float32
BF16 = jnp.bfloat16
HIGHEST = lax.Precision.HIGHEST

D_MODEL = 1024
BRANCH = D_MODEL // 2
GLA_HEADS = 4
GLA_DK = 64
GLA_DV = 128
GLA_RANK = 16
GLA_GATE_NORM = 16.0
GLA_CHUNK = 64
S5_GROUP = 16
S5_GROUPS = BRANCH // S5_GROUP
S5_STATE = 64
S5_CHUNK = 16
S5_ROW = S5_CHUNK * S5_GROUP
ATT_HEADS = 8
ATT_KV_HEADS = 2
ATT_GROUPS = ATT_HEADS // ATT_KV_HEADS
ATT_HD = 64
WINDOW = 128
QBLK = 128
GRID_W = 64
ROPE_BASE = 10000.0
CONV_W = 3
EPS = 1e-6
NEG = -1e30

ROW_TILE = 512
VMEM_LIMIT = 48 * 1024 * 1024


def _cparams(*sem):
    return pltpu.CompilerParams(dimension_semantics=sem, vmem_limit_bytes=VMEM_LIMIT)


def _silu(x):
    return x * jax.nn.sigmoid(x)


def _dot(a, b):
    return jnp.dot(a, b, preferred_element_type=F32)


def _dot_nt(a, b):
    return lax.dot_general(a, b, (((1,), (1,)), ((), ())), preferred_element_type=F32)


def _ada_kernel(c_ref, w_ref, b_ref, o_ref):
    c = c_ref[...]
    o_ref[0] = jnp.dot(_silu(c), w_ref[0], precision=HIGHEST, preferred_element_type=F32) + b_ref[0]


def _ada_call(cs, w_ada, b_ada):
    depth = w_ada.shape[0]
    rows = cs.shape[0]
    nt = 3
    return pl.pallas_call(
        _ada_kernel,
        grid=(depth, nt),
        in_specs=[
            pl.BlockSpec((rows, D_MODEL), lambda l, j: (0, 0)),
            pl.BlockSpec((1, D_MODEL, D_MODEL), lambda l, j: (l, 0, j)),
            pl.BlockSpec((1, 1, D_MODEL), lambda l, j: (l, 0, j)),
        ],
        out_specs=pl.BlockSpec((1, rows, D_MODEL), lambda l, j: (l, 0, j)),
        out_shape=jax.ShapeDtypeStruct((depth, rows, 3 * D_MODEL), F32),
        compiler_params=_cparams("arbitrary", "arbitrary"),
        name="adaln",
    )(cs, w_ada, b_ada.reshape(depth, 1, 3 * D_MODEL))


def _modulated(x_ref, mod_ref, nw_ref):
    x = x_ref[0]
    ms = jnp.mean(x * x, axis=-1, keepdims=True)
    y = x * lax.rsqrt(ms + EPS) * nw_ref[...]
    shift = mod_ref[0, 0:1, :]
    scale = mod_ref[0, 1:2, :]
    return (y * (1.0 + scale) + shift).astype(BF16)


E_COLS = (0, 512, 1024, 1152, 1664, 2176, 2688)


def _even_in_kernel(x_ref, mod_ref, nw_ref, w_ref, qk_ref, v_ref, lr_ref, sg_ref, u_ref, ss_ref):
    h = _modulated(x_ref, mod_ref, nw_ref)
    c = E_COLS
    qk_ref[0] = _dot(h, w_ref[:, c[0]:c[1]])
    v_ref[0] = _dot(h, w_ref[:, c[1]:c[2]])
    lr_ref[0] = _dot(h, w_ref[:, c[2]:c[3]])
    sg_ref[0] = _silu(_dot(h, w_ref[:, c[3]:c[4]]))
    u_ref[0] = _dot(h, w_ref[:, c[4]:c[5]])
    ss_ref[0] = _silu(_dot(h, w_ref[:, c[5]:c[6]]))


O_COLS = (0, 512, 768, 1280, 1792, 2304, 2816, 3328)


def _odd_in_kernel(x_ref, mod_ref, nw_ref, w_ref, q_ref, kv_ref, sa_ref, p_ref, bgz_ref):
    h = _modulated(x_ref, mod_ref, nw_ref)
    c = O_COLS
    q_ref[0] = _dot(h, w_ref[:, c[0]:c[1]])
    kv_ref[0] = _dot(h, w_ref[:, c[1]:c[2]])
    sa_ref[0] = _silu(_dot(h, w_ref[:, c[2]:c[3]]))
    xc = _dot(h, w_ref[:, c[3]:c[4]])
    bg = _dot(h, w_ref[:, c[4]:c[5]])
    cg = _dot(h, w_ref[:, c[5]:c[6]])
    zc = _dot(h, w_ref[:, c[6]:c[7]])
    p_ref[0] = cg * xc
    bgz_ref[0] = bg * _silu(zc)


def _in_call(body, name, x, mod, nw, w, out_widths):
    bm, lm, _ = x.shape
    tm = min(ROW_TILE, lm)
    n_in = w.shape[1]
    return pl.pallas_call(
        body,
        grid=(bm, lm // tm),
        in_specs=[
            pl.BlockSpec((1, tm, D_MODEL), lambda b, i: (b, i, 0)),
            pl.BlockSpec((1, 3, D_MODEL), lambda b, i: (b, 0, 0)),
            pl.BlockSpec((1, D_MODEL), lambda b, i: (0, 0)),
            pl.BlockSpec((D_MODEL, n_in), lambda b, i: (0, 0)),
        ],
        out_specs=[pl.BlockSpec((1, tm, n), lambda b, i: (b, i, 0)) for n in out_widths],
        out_shape=[jax.ShapeDtypeStruct((bm, lm, n), F32) for n in out_widths],
        compiler_params=_cparams("parallel", "parallel"),
        name=name,
    )(x, mod, nw, w)


def _log_sigmoid(x):
    return jnp.minimum(x, 0.0) - jnp.log(1.0 + jnp.exp(-jnp.abs(x)))


def _gla_kernel(*refs, seq, has_state):
    if has_state:
        (qk_ref, v_ref, lr_ref, w2_ref, b2_ref, sg_ref, on_ref, s0_ref,
         y_ref, sn_ref, g_scr, of_scr, ob_scr) = refs
    else:
        (qk_ref, v_ref, lr_ref, w2_ref, b2_ref, sg_ref, on_ref,
         y_ref, sn_ref, g_scr, of_scr, ob_scr) = refs
        s0_ref = None
    C = GLA_CHUNK
    nc = seq // C
    ri = lax.broadcasted_iota(jnp.int32, (C, C), 0)
    ci = lax.broadcasted_iota(jnp.int32, (C, C), 1)
    tril = ri >= ci
    triu = ri <= ci
    tril_f = tril.astype(F32)
    triu_f = triu.astype(F32)
    lane = lax.broadcasted_iota(jnp.int32, (C, 2 * GLA_DK), 1)
    scale = GLA_DK ** -0.5

    def rows(c):
        return pl.ds(pl.multiple_of(c * C, C), C)

    def gate_body(c, carry):
        r = rows(c)
        pre = jnp.dot(lr_ref[0, r, :], w2_ref[0], precision=HIGHEST, preferred_element_type=F32) + b2_ref[0]
        g = _log_sigmoid(pre) / GLA_GATE_NORM
        cf = jnp.dot(tril_f, g, precision=HIGHEST, preferred_element_type=F32)
        cb = jnp.dot(triu_f, g, precision=HIGHEST, preferred_element_type=F32)
        g_scr[r, :] = jnp.where(lane < GLA_DK, cf, cb)
        return carry

    lax.fori_loop(0, nc, gate_body, 0)

    def chunk(c, st, reverse):
        r = rows(c)
        qk = qk_ref[0, r, :]
        q = qk[:, :GLA_DK] * scale
        k = qk[:, GLA_DK:]
        v = v_ref[0, r, :]
        gg = g_scr[r, :]
        if reverse:
            bc = gg[:, GLA_DK:]
            b_end = bc[0:1, :]
            mask = triu
        else:
            bc = gg[:, :GLA_DK]
            b_end = bc[C - 1:C, :]
            mask = tril
        b_mid = bc[C // 2:C // 2 + 1, :]
        qd = (q * jnp.exp(bc)).astype(BF16)
        qh = (q * jnp.exp(bc - b_mid)).astype(BF16)
        kh = (k * jnp.exp(b_mid - bc)).astype(BF16)
        att = jnp.where(mask, _dot_nt(qh, kh), 0.0).astype(BF16)
        vb = v.astype(BF16)
        o = _dot_nt(qd, st.astype(BF16)) + _dot(att, vb)
        kd = (k * jnp.exp(b_end - bc)).astype(BF16)
        st_new = st * jnp.exp(b_end) + _dot(v.T.astype(BF16), kd)
        return o, st_new

    def body(n, carry):
        sf, sb = carry
        o_f, sf = chunk(n, sf, False)
        of_scr[rows(n), :] = o_f
        m = nc - 1 - n
        o_b, sb = chunk(m, sb, True)
        ob_scr[rows(m), :] = o_b
        return sf, sb

    if has_state:
        init = (s0_ref[0, 0, 0].T, s0_ref[0, 1, 0].T)
    else:
        init = (jnp.zeros((GLA_DV, GLA_DK), F32), jnp.zeros((GLA_DV, GLA_DK), F32))
    sf, sb = lax.fori_loop(0, nc, body, init)
    sn_ref[0, 0, 0] = sf.T
    sn_ref[0, 1, 0] = sb.T

    def fin_body(c, carry):
        r = rows(c)
        o = of_scr[r, :] + ob_scr[r, :]
        ms = jnp.mean(o * o, axis=-1, keepdims=True)
        y_ref[0, r, :] = o * lax.rsqrt(ms + EPS) * on_ref[...] * sg_ref[0, r, :]
        return carry

    lax.fori_loop(0, nc, fin_body, 0)


def _gla_call(qk, v, lr, w2h, b2h, sg, onorm, s0):
    bsz, seq, _ = qk.shape
    has_state = s0 is not None
    seq_spec = lambda col: pl.BlockSpec((1, seq, 128), col)
    per_head = lambda b, h: (b, 0, h)
    in_specs = [
        seq_spec(per_head),
        seq_spec(per_head),
        seq_spec(lambda b, h: (b, 0, 0)),
        pl.BlockSpec((1, 128, 128), lambda b, h: (h, 0, 0)),
        pl.BlockSpec((1, 1, 128), lambda b, h: (h, 0, 0)),
        seq_spec(per_head),
        pl.BlockSpec((1, GLA_DV), lambda b, h: (0, 0)),
    ]
    args = [qk, v, lr, w2h, b2h, sg, onorm]
    state_spec = pl.BlockSpec((1, 2, 1, GLA_DK, GLA_DV), lambda b, h: (b, 0, h, 0, 0))
    if has_state:
        in_specs.append(state_spec)
        args.append(s0)
    return pl.pallas_call(
        functools.partial(_gla_kernel, seq=seq, has_state=has_state),
        grid=(bsz, GLA_HEADS),
        in_specs=in_specs,
        out_specs=[seq_spec(per_head), state_spec],
        out_shape=[jax.ShapeDtypeStruct((bsz, seq, BRANCH), F32),
                   jax.ShapeDtypeStruct((bsz, 2, GLA_HEADS, GLA_DK, GLA_DV), F32)],
        scratch_shapes=[pltpu.VMEM((seq, 128), F32)] * 3,
        compiler_params=_cparams("parallel", "parallel"),
        name="gla",
    )(*args)


def _s5_operators(lam_re, lam_im, log_dt, b_re, b_im, c_re, c_im, nlev):
    hp = dict(precision=HIGHEST)
    cs = S5_CHUNK
    dt = jnp.exp(log_dt)[..., None]
    er, ei = lam_re * dt, lam_im * dt
    lb_re, lb_im = jnp.exp(er) * jnp.cos(ei), jnp.exp(er) * jnp.sin(ei)
    den = lam_re * lam_re + lam_im * lam_im
    nr, ni = lb_re - 1.0, lb_im
    cr = (nr * lam_re + ni * lam_im) / den
    ci = (ni * lam_re - nr * lam_im) / den
    bb_re = cr[..., None] * b_re - ci[..., None] * b_im
    bb_im = cr[..., None] * b_im + ci[..., None] * b_re

    def powers(k):
        kk = k.astype(F32)[:, None, None, None]
        mag = jnp.exp(kk * er)
        return mag * jnp.cos(kk * ei), mag * jnp.sin(kk * ei)

    pr, pi = powers(jnp.arange(cs + 1))
    cl_re = c_re[None, None] * pr[:, :, :, None, :] - c_im[None, None] * pi[:, :, :, None, :]
    cl_im = c_re[None, None] * pi[:, :, :, None, :] + c_im[None, None] * pr[:, :, :, None, :]
    kern = (jnp.einsum('kdgjp,dgph->kdghj', cl_re[:cs], bb_re, **hp)
            - jnp.einsum('kdgjp,dgph->kdghj', cl_im[:cs], bb_im, **hp))
    s_in = jnp.arange(cs)[:, None]
    s_out = jnp.arange(cs)[None, :]
    lag_f = jnp.clip(s_out - s_in, 0, cs - 1)
    lag_b = jnp.clip(s_in - s_out, 0, cs - 1)
    m = (jnp.where((s_in <= s_out)[:, :, None, None, None], kern[:, 0][lag_f], 0.0)
         + jnp.where((s_in >= s_out)[:, :, None, None, None], kern[:, 1][lag_b], 0.0))
    m = m.transpose(2, 0, 3, 1, 4).reshape(S5_GROUPS, S5_ROW, S5_ROW)

    kf = jnp.arange(cs - 1, -1, -1)
    kb = jnp.arange(cs)
    pw_re = jnp.stack([pr[kf, 0], pr[kb, 1]], axis=1)
    pw_im = jnp.stack([pi[kf, 0], pi[kb, 1]], axis=1)
    win_re = pw_re[..., None] * bb_re[None] - pw_im[..., None] * bb_im[None]
    win_im = pw_re[..., None] * bb_im[None] + pw_im[..., None] * bb_re[None]

    def in_cols(w):
        return w.transpose(2, 0, 4, 1, 3).reshape(S5_GROUPS, S5_ROW, 2 * S5_STATE)

    win = jnp.concatenate([in_cols(win_re), in_cols(win_im)], axis=-1)

    kf = jnp.arange(1, cs + 1)
    kb = jnp.arange(cs, 0, -1)
    co_re = jnp.stack([cl_re[kf, 0], cl_re[kb, 1]], axis=1)
    co_im = jnp.stack([cl_im[kf, 0], cl_im[kb, 1]], axis=1)

    def out_rows(w):
        return w.transpose(2, 1, 4, 0, 3).reshape(S5_GROUPS, 2 * S5_STATE, S5_ROW)

    wout = jnp.concatenate([out_rows(co_re), out_rows(-co_im)], axis=1)

    lev = cs * (2 ** jnp.arange(nlev))
    ar, ai = powers(lev)
    lanes = lambda a: a.transpose(2, 0, 1, 3).reshape(S5_GROUPS, nlev, 2 * S5_STATE)
    return m.astype(BF16), win.astype(BF16), wout.astype(BF16), lanes(ar), lanes(ai)


def _gelu_tanh(x):
    return 0.5 * x * (1.0 + jnp.tanh(math.sqrt(2.0 / math.pi) * (x + 0.044715 * (x * x * x))))


def _s5_kernel(*refs, nseq, nrow, nlev, has_state):
    if has_state:
        (u_ref, m_ref, win_ref, wout_ref, ar_ref, ai_ref, d_ref, h0_ref, z_ref, last_ref, tmp_r, tmp_i) = refs
    else:
        (u_ref, m_ref, win_ref, wout_ref, ar_ref, ai_ref, d_ref, z_ref, last_ref, tmp_r, tmp_i) = refs
    P2 = 2 * S5_STATE
    R = nseq * nrow
    u = u_ref[0]
    ub = u.astype(BF16)
    x = _dot(ub, win_ref[0])
    xr, xi = x[:, :P2], x[:, P2:]
    n = lax.broadcasted_iota(jnp.int32, (R, P2), 0) % nrow
    fwd = lax.broadcasted_iota(jnp.int32, (R, P2), 1) < S5_STATE

    def shift(a, k):
        dn = pltpu.roll(a, k, 0)
        up = pltpu.roll(a, R - k, 0)
        return jnp.where(fwd, jnp.where(n >= k, dn, 0.0), jnp.where(n < nrow - k, up, 0.0))

    er, ei = shift(xr, 1), shift(xi, 1)
    if has_state:
        row = lax.broadcasted_iota(jnp.int32, (R, P2), 0)
        for b in range(nseq):
            first = row == jnp.where(fwd, b * nrow, b * nrow + nrow - 1)
            er = jnp.where(first, h0_ref[0, b:b + 1, :P2], er)
            ei = jnp.where(first, h0_ref[0, b:b + 1, P2:], ei)
    for j in range(nlev):
        k = 2 ** j
        ar = ar_ref[0, j:j + 1, :]
        ai = ai_ref[0, j:j + 1, :]
        sr, si = shift(er, k), shift(ei, k)
        er, ei = er + ar * sr - ai * si, ei + ar * si + ai * sr
    ecat = jnp.concatenate([er, ei], axis=1).astype(BF16)
    y = _dot(ub, m_ref[0]) + _dot(ecat, wout_ref[0]) + d_ref[0] * u
    z_ref[0] = _gelu_tanh(y)
    a1r = ar_ref[0, 0:1, :]
    a1i = ai_ref[0, 0:1, :]
    tmp_r[...] = a1r * er - a1i * ei + xr
    tmp_i[...] = a1r * ei + a1i * er + xi
    fwd_b = lax.broadcasted_iota(jnp.int32, (nseq, P2), 1) < S5_STATE
    ends = lambda t: jnp.where(fwd_b, t[pl.ds(nrow - 1, nseq, stride=nrow), :], t[pl.ds(0, nseq, stride=nrow), :])
    last_ref[0, :, :P2] = ends(tmp_r)
    last_ref[0, :, P2:] = ends(tmp_i)


def _s5_call(ugm, ops, dgm, h0, nseq, nrow):
    m, win, wout, ar, ai = ops
    nlev = ar.shape[1]
    has_state = h0 is not None
    R = nseq * nrow
    g3 = lambda g: (g, 0, 0)
    in_specs = [
        pl.BlockSpec((1, R, S5_ROW), g3),
        pl.BlockSpec((1, S5_ROW, S5_ROW), g3),
        pl.BlockSpec((1, S5_ROW, S5_ROW), g3),
        pl.BlockSpec((1, S5_ROW, S5_ROW), g3),
        pl.BlockSpec((1, nlev, 2 * S5_STATE), g3),
        pl.BlockSpec((1, nlev, 2 * S5_STATE), g3),
        pl.BlockSpec((1, 1, S5_ROW), g3),
    ]
    args = [ugm, m, win, wout, ar, ai, dgm]
    if has_state:
        in_specs.append(pl.BlockSpec((1, nseq, 4 * S5_STATE), g3))
        args.append(h0)
    return pl.pallas_call(
        functools.partial(_s5_kernel, nseq=nseq, nrow=nrow, nlev=nlev, has_state=has_state),
        grid=(S5_GROUPS,),
        in_specs=in_specs,
        out_specs=[pl.BlockSpec((1, R, S5_ROW), g3), pl.BlockSpec((1, nseq, 4 * S5_STATE), g3)],
        out_shape=[jax.ShapeDtypeStruct((S5_GROUPS, R, S5_ROW), F32),
                   jax.ShapeDtypeStruct((S5_GROUPS, nseq, 4 * S5_STATE), F32)],
        scratch_shapes=[pltpu.VMEM((R, 2 * S5_STATE), F32)] * 2,
        compiler_params=_cparams("parallel"),
        name="s5",
    )(*args)


def _to_group_major(u):
    b, l, _ = u.shape
    t = u.reshape(b, l // S5_CHUNK, S5_CHUNK, S5_GROUPS, S5_GROUP).transpose(3, 0, 1, 2, 4)
    return t.reshape(S5_GROUPS, b * l // S5_CHUNK, S5_ROW)


def _from_group_major(z, b, l):
    t = z.reshape(S5_GROUPS, b, l // S5_CHUNK, S5_CHUNK, S5_GROUP).transpose(1, 2, 3, 0, 4)
    return t.reshape(b, l, BRANCH)


def _seg_rmsnorm(x, w, nseg):
    seg = lax.broadcasted_iota(jnp.int32, x.shape, 1) // ATT_HD
    x2 = x * x
    rs = jnp.ones_like(x)
    for s in range(nseg):
        ms = jnp.sum(jnp.where(seg == s, x2, 0.0), axis=-1, keepdims=True) * (1.0 / ATT_HD)
        rs = jnp.where(seg == s, lax.rsqrt(ms + EPS), rs)
    return x * rs * w


def _rope(x, cos, sin):
    width = x.shape[1]
    lane = lax.broadcasted_iota(jnp.int32, x.shape, 1)
    partner = jnp.where((lane & 16) == 0, pltpu.roll(x, width - 16, 1), pltpu.roll(x, 16, 1))
    return x * cos + partner * sin


def _softmax_pv(scores, values, sink):
    m = sink
    for s in scores:
        m = jnp.maximum(m, jnp.max(s, axis=-1, keepdims=True))
    den = jnp.exp(sink - m)
    out = None
    for s, v in zip(scores, values):
        p = jnp.exp(s - m)
        den = den + jnp.sum(p, axis=-1, keepdims=True)
        pv = _dot(p.astype(BF16), v)
        out = pv if out is None else out + pv
    return out / den


def _att_ctx_kernel(q_ref, kv_ref, sa_ref, qn_ref, kn_ref, sink_ref, y_ref, kc_ref, vc_ref):
    q = _seg_rmsnorm(q_ref[0], qn_ref[...], ATT_GROUPS) * (ATT_HD ** -0.5)
    kv = _seg_rmsnorm(kv_ref[0], kn_ref[...], 1)
    k = kv[:, :ATT_HD]
    v = kv[:, ATT_HD:]
    kc_ref[0, 0] = k
    vc_ref[0, 0] = v
    kb = k.astype(BF16)
    vb = v.astype(BF16)
    outs = []
    for g in range(ATT_GROUPS):
        qg = q[:, g * ATT_HD:(g + 1) * ATT_HD].astype(BF16)
        s = _dot_nt(qg, kb)
        outs.append(_softmax_pv([s], [vb], sink_ref[0, g:g + 1, 0:1]))
    y_ref[0] = jnp.concatenate(outs, axis=1) * sa_ref[0]


def _att_ctx_call(q, kv, sa, qnw, knw, sink):
    bsz, seq, _ = q.shape
    W = ATT_GROUPS * ATT_HD
    per_kv = lambda b, k: (b, 0, k)
    cache_spec = pl.BlockSpec((1, 1, seq, ATT_HD), lambda b, k: (b, k, 0, 0))
    return pl.pallas_call(
        _att_ctx_kernel,
        grid=(bsz, ATT_KV_HEADS),
        in_specs=[
            pl.BlockSpec((1, seq, W), per_kv),
            pl.BlockSpec((1, seq, 2 * ATT_HD), per_kv),
            pl.BlockSpec((1, seq, W), per_kv),
            pl.BlockSpec((1, W), lambda b, k: (0, 0)),
            pl.BlockSpec((1, 2 * ATT_HD), lambda b, k: (0, 0)),
            pl.BlockSpec((1, 8, 128), lambda b, k: (k, 0, 0)),
        ],
        out_specs=[pl.BlockSpec((1, seq, W), per_kv), cache_spec, cache_spec],
        out_shape=[jax.ShapeDtypeStruct((bsz, seq, BRANCH), F32),
                   jax.ShapeDtypeStruct((bsz, ATT_KV_HEADS, seq, ATT_HD), F32),
                   jax.ShapeDtypeStruct((bsz, ATT_KV_HEADS, seq, ATT_HD), F32)],
        compiler_params=_cparams("parallel", "parallel"),
        name="att_ctx",
    )(q, kv, sa, qnw, knw, sink)


def _att_lat_kernel(q_ref, kv_ref, sa_ref, qn_ref, kn_ref, sink_ref, cq_ref, sq_ref, ck_ref, sk_ref,
                    kc_ref, vc_ref, y_ref, kr_scr, *, seq):
    nb = seq // QBLK
    KW = 3 * QBLK

    def k_body(i, carry):
        r = pl.ds(pl.multiple_of(i * QBLK, QBLK), QBLK)
        kv = _seg_rmsnorm(kv_ref[0, r, :], kn_ref[...], 1)
        kr_scr[r, :] = _rope(kv, ck_ref[r, :], sk_ref[r, :]).astype(BF16)
        return carry

    lax.fori_loop(0, nb, k_body, 0)
    kctx = kc_ref[0, 0, 0].astype(BF16)
    vctx = vc_ref[0, 0, 0].astype(BF16)
    qi = lax.broadcasted_iota(jnp.int32, (QBLK, KW), 0)
    kj = lax.broadcasted_iota(jnp.int32, (QBLK, KW), 1)

    def q_body(i, carry):
        r = pl.ds(pl.multiple_of(i * QBLK, QBLK), QBLK)
        q = _seg_rmsnorm(q_ref[0, r, :], qn_ref[...], ATT_GROUPS)
        q = _rope(q, cq_ref[r, :], sq_ref[r, :]) * (ATT_HD ** -0.5)
        ws = pl.multiple_of(jnp.clip(i * QBLK - QBLK, 0, seq - KW), QBLK)
        kvw = kr_scr[pl.ds(ws, KW), :]
        kw = kvw[:, :ATT_HD]
        vw = kvw[:, ATT_HD:]
        band = jnp.abs((i * QBLK + qi) - (ws + kj)) <= WINDOW
        outs = []
        for g in range(ATT_GROUPS):
            qg = q[:, g * ATT_HD:(g + 1) * ATT_HD].astype(BF16)
            s_win = jnp.where(band, _dot_nt(qg, kw), NEG)
            s_ctx = _dot_nt(qg, kctx)
            outs.append(_softmax_pv([s_win, s_ctx], [vw, vctx], sink_ref[0, g:g + 1, 0:1]))
        y_ref[0, r, :] = jnp.concatenate(outs, axis=1) * sa_ref[0, r, :]
        return carry

    lax.fori_loop(0, nb, q_body, 0)


def _rope_tables(seq, heads, pad_heads):
    pos = jnp.arange(seq)
    row = (pos // GRID_W).astype(F32)[:, None]
    col = (pos % GRID_W).astype(F32)[:, None]
    nf = ATT_HD // 4
    freq = ROPE_BASE ** (-jnp.arange(nf, dtype=F32) / nf)
    ar, ac = row * freq, col * freq
    cos = jnp.concatenate([jnp.cos(ar), jnp.cos(ar), jnp.cos(ac), jnp.cos(ac)], axis=1)
    sin = jnp.concatenate([-jnp.sin(ar), jnp.sin(ar), -jnp.sin(ac), jnp.sin(ac)], axis=1)
    cos = jnp.concatenate([jnp.tile(cos, (1, heads)), jnp.ones((seq, pad_heads * ATT_HD), F32)], axis=1)
    sin = jnp.concatenate([jnp.tile(sin, (1, heads)), jnp.zeros((seq, pad_heads * ATT_HD), F32)], axis=1)
    return cos, sin


def _att_lat_call(q, kv, sa, qnw, knw, sink, cache_k, cache_v, e):
    bsz, seq, _ = q.shape
    past = cache_k.shape[3]
    W = ATT_GROUPS * ATT_HD
    cq, sq = _rope_tables(seq, ATT_GROUPS, 0)
    ck, sk = _rope_tables(seq, 1, 1)
    per_kv = lambda b, k: (b, 0, k)
    const = lambda b, k: (0, 0)
    cache_spec = pl.BlockSpec((1, 1, 1, past, ATT_HD), lambda b, k: (b, e, k, 0, 0))
    return pl.pallas_call(
        functools.partial(_att_lat_kernel, seq=seq),
        grid=(bsz, ATT_KV_HEADS),
        in_specs=[
            pl.BlockSpec((1, seq, W), per_kv),
            pl.BlockSpec((1, seq, 2 * ATT_HD), per_kv),
            pl.BlockSpec((1, seq, W), per_kv),
            pl.BlockSpec((1, W), const),
            pl.BlockSpec((1, 2 * ATT_HD), const),
            pl.BlockSpec((1, 8, 128), lambda b, k: (k, 0, 0)),
            pl.BlockSpec((seq, W), const),
            pl.BlockSpec((seq, W), const),
            pl.BlockSpec((seq, 2 * ATT_HD), const),
            pl.BlockSpec((seq, 2 * ATT_HD), const),
            cache_spec,
            cache_spec,
        ],
        out_specs=pl.BlockSpec((1, seq, W), per_kv),
        out_shape=jax.ShapeDtypeStruct((bsz, seq, BRANCH), F32),
        scratch_shapes=[pltpu.VMEM((seq, 2 * ATT_HD), BF16)],
        compiler_params=_cparams("parallel", "parallel"),
        name="att_lat",
    )(q, kv, sa, qnw, knw, sink, cq, sq, ck, sk, cache_k, cache_v)


def _even_out_kernel(x_ref, mod_ref, yg_ref, z_ref, ss_ref, wg_ref, bg_ref, wo_ref, o_ref):
    z = z_ref[0]
    glu = z * jax.nn.sigmoid(_dot(z.astype(BF16), wg_ref[...]) + bg_ref[...])
    ys = (glu * ss_ref[0]).astype(BF16)
    yg = yg_ref[0].astype(BF16)
    out = _dot(yg, wo_ref[:BRANCH, :]) + _dot(ys, wo_ref[BRANCH:, :])
    o_ref[0] = x_ref[0] + mod_ref[0, 2:3, :] * out


def _even_out_call(x, mod, yg, z, ss, wglu, bglu, wout):
    bm, lm, _ = x.shape
    tm = min(ROW_TILE, lm)
    tile = lambda n: pl.BlockSpec((1, tm, n), lambda b, i: (b, i, 0))
    const = lambda b, i: (0, 0)
    return pl.pallas_call(
        _even_out_kernel,
        grid=(bm, lm // tm),
        in_specs=[
            tile(D_MODEL),
            pl.BlockSpec((1, 3, D_MODEL), lambda b, i: (b, 0, 0)),
            tile(BRANCH), tile(BRANCH), tile(BRANCH),
            pl.BlockSpec((BRANCH, BRANCH), const),
            pl.BlockSpec((1, BRANCH), const),
            pl.BlockSpec((D_MODEL, D_MODEL), const),
        ],
        out_specs=tile(D_MODEL),
        out_shape=jax.ShapeDtypeStruct((bm, lm, D_MODEL), F32),
        compiler_params=_cparams("parallel", "parallel"),
        name="even_out",
    )(x, mod, yg, z, ss, wglu, bglu, wout)


def _odd_out_kernel(x_ref, mod_ref, ya_ref, p_ref, pprev_ref, pnext_ref, bgz_ref, cw_ref, cb_ref, wo_ref,
                    o_ref, *, seq, tm):
    p = p_ref[0]
    pos = (pl.program_id(1) * tm + lax.broadcasted_iota(jnp.int32, (tm, BRANCH), 0)) % seq
    rowi = lax.broadcasted_iota(jnp.int32, (tm, BRANCH), 0)
    prev = jnp.where(rowi == 0, pprev_ref[0, 7:8, :], pltpu.roll(p, 1, 0))
    nxt = jnp.where(rowi == tm - 1, pnext_ref[0, 0:1, :], pltpu.roll(p, tm - 1, 0))
    prev = jnp.where(pos == 0, 0.0, prev)
    nxt = jnp.where(pos == seq - 1, 0.0, nxt)
    conv = prev * cw_ref[0:1, :] + p * cw_ref[1:2, :] + nxt * cw_ref[2:3, :] + cb_ref[...]
    yc = (bgz_ref[0] * conv).astype(BF16)
    ya = ya_ref[0].astype(BF16)
    out = _dot(ya, wo_ref[:BRANCH, :]) + _dot(yc, wo_ref[BRANCH:, :])
    o_ref[0] = x_ref[0] + mod_ref[0, 2:3, :] * out


def _odd_out_call(x, mod, ya, p, bgz, convw, convb, wout, seq):
    bm, lm, _ = x.shape
    tm = min(ROW_TILE, lm)
    nt8 = lm // 8
    tile = lambda n: pl.BlockSpec((1, tm, n), lambda b, i: (b, i, 0))
    const = lambda b, i: (0, 0)
    prev_spec = pl.BlockSpec((1, 8, BRANCH), lambda b, i: (b, jnp.maximum(i * (tm // 8) - 1, 0), 0))
    next_spec = pl.BlockSpec((1, 8, BRANCH), lambda b, i: (b, jnp.minimum((i + 1) * (tm // 8), nt8 - 1), 0))
    return pl.pallas_call(
        functools.partial(_odd_out_kernel, seq=seq, tm=tm),
        grid=(bm, lm // tm),
        in_specs=[
            tile(D_MODEL),
            pl.BlockSpec((1, 3, D_MODEL), lambda b, i: (b, 0, 0)),
            tile(BRANCH), tile(BRANCH), prev_spec, next_spec, tile(BRANCH),
            pl.BlockSpec((CONV_W, BRANCH), const),
            pl.BlockSpec((1, BRANCH), const),
            pl.BlockSpec((D_MODEL, D_MODEL), const),
        ],
        out_specs=tile(D_MODEL),
        out_shape=jax.ShapeDtypeStruct((bm, lm, D_MODEL), F32),
        compiler_params=_cparams("parallel", "parallel"),
        name="odd_out",
    )(x, mod, ya, p, p, p, bgz, convw, convb, wout)


def _even_in_weight(w):
    dk = GLA_HEADS * GLA_DK
    q, k = w[:, :dk], w[:, dk:2 * dk]
    v = w[:, 2 * dk:2 * dk + BRANCH]
    o = 2 * dk + BRANCH
    lr = w[:, o:o + 2 * GLA_RANK]
    rest = w[:, o + 2 * GLA_RANK:]
    qk = jnp.concatenate([jnp.concatenate([q[:, h * GLA_DK:(h + 1) * GLA_DK], k[:, h * GLA_DK:(h + 1) * GLA_DK]],
                                          axis=1) for h in range(GLA_HEADS)], axis=1)
    lr = jnp.pad(lr, ((0, 0), (0, 128 - 2 * GLA_RANK)))
    return jnp.concatenate([qk, v, lr, rest], axis=1).astype(BF16)


def _odd_in_weight(w):
    q = w[:, :BRANCH]
    k = w[:, BRANCH:BRANCH + ATT_KV_HEADS * ATT_HD]
    v = w[:, BRANCH + ATT_KV_HEADS * ATT_HD:BRANCH + 2 * ATT_KV_HEADS * ATT_HD]
    rest = w[:, BRANCH + 2 * ATT_KV_HEADS * ATT_HD:]
    kv = jnp.concatenate([jnp.concatenate([k[:, h * ATT_HD:(h + 1) * ATT_HD], v[:, h * ATT_HD:(h + 1) * ATT_HD]],
                                          axis=1) for h in range(ATT_KV_HEADS)], axis=1)
    return jnp.concatenate([q, kv, rest], axis=1).astype(BF16)


def _gate_weights(w2, b2):
    ws, bs = [], []
    for h in range(GLA_HEADS):
        sl = slice(h * GLA_DK, (h + 1) * GLA_DK)
        wh = jnp.zeros((128, 128), F32)
        wh = wh.at[:GLA_RANK, :GLA_DK].set(w2[0][:, sl])
        wh = wh.at[GLA_RANK:2 * GLA_RANK, GLA_DK:].set(w2[1][:, sl])
        ws.append(wh)
        bs.append(jnp.concatenate([b2[0][sl], b2[1][sl]])[None, :])
    return jnp.stack(ws), jnp.stack(bs)


def _even_layer(x, mod, nw, w_in, seq, gla_w, s5_ops, dgm, wglu, bglu, wout, onorm, gla_s0, s5_h0):
    bm, lm, _ = x.shape
    nseq = bm * lm // seq
    qk, v, lr, sg, u, ss = _in_call(_even_in_kernel, "even_in", x, mod, nw, w_in,
                                    (BRANCH, BRANCH, 128, BRANCH, BRANCH, BRANCH))
    per_seq = lambda a: a.reshape(nseq, seq, a.shape[-1])
    yg, s_new = _gla_call(per_seq(qk), per_seq(v), per_seq(lr), gla_w[0], gla_w[1], per_seq(sg), onorm, gla_s0)
    nrow = seq // S5_CHUNK
    z_gm, last = _s5_call(_to_group_major(per_seq(u)), s5_ops, dgm, s5_h0, nseq, nrow)
    z = _from_group_major(z_gm, nseq, seq)
    x_new = _even_out_call(x, mod, yg.reshape(bm, lm, BRANCH), z.reshape(bm, lm, BRANCH), ss, wglu, bglu, wout)
    return x_new, s_new, last


def _odd_layer(x, mod, nw, w_in, seq, qnw, knw, sink, convw, convb, wout, cache_k, cache_v, e):
    bm, lm, _ = x.shape
    nseq = bm * lm // seq
    q, kv, sa, p, bgz = _in_call(_odd_in_kernel, "odd_in", x, mod, nw, w_in,
                                 (BRANCH, 2 * ATT_KV_HEADS * ATT_HD, BRANCH, BRANCH, BRANCH))
    per_seq = lambda a: a.reshape(nseq, seq, a.shape[-1])
    if cache_k is None:
        ya, kc, vc = _att_ctx_call(per_seq(q), per_seq(kv), per_seq(sa), qnw, knw, sink)
    else:
        ya = _att_lat_call(per_seq(q), per_seq(kv), per_seq(sa), qnw, knw, sink, cache_k, cache_v, e)
        kc = vc = None
    x_new = _odd_out_call(x, mod, ya.reshape(bm, lm, BRANCH), p, bgz, convw, convb, wout, seq)
    return x_new, kc, vc


def kernel(x_prompt, x_sample, c, state_gla, state_s5_re, state_s5_im, cache_k, cache_v, c_ctx, norm_w, w_ada, b_ada, w_in_e, w_out_e, gla_w2, gla_b2, gla_onorm, s5_lam_re, s5_lam_im, s5_log_dt, s5_b_re, s5_b_im, s5_c_re, s5_c_im, s5_d, s5_w_glu, s5_b_glu, w_in_o, w_out_o, q_norm_w, k_norm_w, sink, conv_w, conv_b):
    depth = norm_w.shape[0]
    bp, lp, _ = x_prompt.shape
    bs, ls, _ = x_sample.shape

    rows = 8 * ((1 + bs + 7) // 8)
    cs = jnp.zeros((rows, D_MODEL), F32).at[0].set(c_ctx).at[1:1 + bs].set(c)
    mods = _ada_call(cs, w_ada, b_ada)

    xp = x_prompt.reshape(1, bp * lp, D_MODEL)
    xs = x_sample
    new_gla, new_re, new_im, new_k, new_v = [], [], [], [], []
    for l in range(depth):
        e = l // 2
        mod_p = mods[l, 0:1].reshape(1, 3, D_MODEL)
        mod_s = mods[l, 1:1 + bs].reshape(bs, 3, D_MODEL)
        nw = norm_w[l][None, :]
        if l % 2 == 0:
            w_in = _even_in_weight(w_in_e[e])
            wout = w_out_e[e].astype(BF16)
            gla_w = _gate_weights(gla_w2[e], gla_b2[e])
            onorm = gla_onorm[e][None, :]
            dgm = jnp.tile(s5_d[e].reshape(S5_GROUPS, 1, S5_GROUP), (1, S5_CHUNK, 1)).reshape(S5_GROUPS, 1, S5_ROW)
            wglu = s5_w_glu[e].astype(BF16)
            bglu = s5_b_glu[e][None, :]
            s5p = (s5_lam_re[e], s5_lam_im[e], s5_log_dt[e], s5_b_re[e], s5_b_im[e], s5_c_re[e], s5_c_im[e])
            ops_p = _s5_operators(*s5p, nlev=int(math.log2(lp // S5_CHUNK)))
            ops_s = _s5_operators(*s5p, nlev=int(math.log2(ls // S5_CHUNK)))
            h0 = jnp.concatenate([state_s5_re[:, e], state_s5_im[:, e]], axis=1)
            h0 = h0.transpose(2, 0, 1, 3).reshape(S5_GROUPS, bs, 4 * S5_STATE)
            xp, sg, last = _even_layer(xp, mod_p, nw, w_in, lp, gla_w, ops_p, dgm, wglu, bglu, wout, onorm,
                                       None, None)
            xs, _, _ = _even_layer(xs, mod_s, nw, w_in, ls, gla_w, ops_s, dgm, wglu, bglu, wout, onorm,
                                   state_gla[:, e], h0)
            new_gla.append(sg)
            last = last.reshape(S5_GROUPS, bp, 2, 2, S5_STATE).transpose(2, 1, 3, 0, 4)
            new_re.append(last[0])
            new_im.append(last[1])
        else:
            w_in = _odd_in_weight(w_in_o[e])
            wout = w_out_o[e].astype(BF16)
            qnw = jnp.tile(q_norm_w[e], ATT_GROUPS)[None, :]
            knw = jnp.concatenate([k_norm_w[e], jnp.ones((ATT_HD,), F32)])[None, :]
            sk = jnp.broadcast_to(sink[e].reshape(ATT_KV_HEADS, ATT_GROUPS, 1), (ATT_KV_HEADS, ATT_GROUPS, 128))
            sk = jnp.concatenate([sk, jnp.zeros((ATT_KV_HEADS, 8 - ATT_GROUPS, 128), F32)], axis=1)
            cb = conv_b[e][None, :]
            xp, kc, vc = _odd_layer(xp, mod_p, nw, w_in, lp, qnw, knw, sk, conv_w[e], cb, wout, None, None, e)
            xs, _, _ = _odd_layer(xs, mod_s, nw, w_in, ls, qnw, knw, sk, conv_w[e], cb, wout, cache_k, cache_v, e)
            new_k.append(kc)
            new_v.append(vc)
    return (xp.reshape(bp, lp, D_MODEL), xs,
            jnp.stack(new_gla, axis=1), jnp.stack(new_re, axis=1), jnp.stack(new_im, axis=1),
            jnp.stack(new_k, axis=1), jnp.stack(new_v, axis=1))
```

```python
import functools
import math

import jax
import jax.numpy as jnp
from jax import lax
from jax.experimental import pallas as pl
from jax.experimental.pallas import tpu as pltpu

F32 = jnp.float32
BF16 = jnp.bfloat16
HIGHEST = lax.Precision.HIGHEST

D_MODEL = 1024
BRANCH = D_MODEL // 2
GLA_HEADS = 4
GLA_DK = 64
GLA_DV = 128
GLA_RANK = 16
GLA_GATE_NORM = 16.0
GLA_CHUNK = 64
GLA_BLOCK = 256
S5_GROUP = 16
S5_GROUPS = BRANCH // S5_GROUP
S5_STATE = 64
S5_CHUNK = 16
S5_ROW = S5_CHUNK * S5_GROUP
ATT_HEADS = 8
ATT_KV_HEADS = 2
ATT_GROUPS = ATT_HEADS // ATT_KV_HEADS
ATT_HD = 64
WINDOW = 128
QBLK = 128
GRID_W = 64
ROPE_BASE = 10000.0
CONV_W = 3
EPS = 1e-6
NEG = -1e30

ROW_TILE = 512
VMEM_LIMIT = 48 * 1024 * 1024


def _cparams(*sem):
    return pltpu.CompilerParams(dimension_semantics=sem, vmem_limit_bytes=VMEM_LIMIT)


def _silu(x):
    return x * jax.nn.sigmoid(x)


def _dot(a, b):
    return jnp.dot(a, b, preferred_element_type=F32)


def _dot_nt(a, b):
    return lax.dot_general(a, b, (((1,), (1,)), ((), ())), preferred_element_type=F32)


def _ada_kernel(c_ref, w_ref, b_ref, o_ref):
    c = c_ref[...]
    o_ref[0] = jnp.dot(_silu(c), w_ref[0], precision=HIGHEST, preferred_element_type=F32) + b_ref[0]


def _ada_call(cs, w_ada, b_ada):
    depth = w_ada.shape[0]
    rows = cs.shape[0]
    nt = 3
    return pl.pallas_call(
        _ada_kernel,
        grid=(depth, nt),
        in_specs=[
            pl.BlockSpec((rows, D_MODEL), lambda l, j: (0, 0)),
            pl.BlockSpec((1, D_MODEL, D_MODEL), lambda l, j: (l, 0, j)),
            pl.BlockSpec((1, 1, D_MODEL), lambda l, j: (l, 0, j)),
        ],
        out_specs=pl.BlockSpec((1, rows, D_MODEL), lambda l, j: (l, 0, j)),
        out_shape=jax.ShapeDtypeStruct((depth, rows, 3 * D_MODEL), F32),
        compiler_params=_cparams("arbitrary", "arbitrary"),
        name="adaln",
    )(cs, w_ada, b_ada.reshape(depth, 1, 3 * D_MODEL))


def _modulated(x_ref, mod_ref, nw_ref):
    x = x_ref[0]
    ms = jnp.mean(x * x, axis=-1, keepdims=True)
    y = x * lax.rsqrt(ms + EPS) * nw_ref[...]
    shift = mod_ref[0, 0:1, :]
    scale = mod_ref[0, 1:2, :]
    return (y * (1.0 + scale) + shift).astype(BF16)


E_COLS = (0, 512, 1024, 1152, 1664, 2176, 2688)


def _log_sigmoid(x):
    return jnp.minimum(x, 0.0) - jnp.log(1.0 + jnp.exp(-jnp.abs(x)))


def _chunk_cumsum(g):
    rows = g.shape[0]
    pos = lax.broadcasted_iota(jnp.int32, g.shape, 0) % GLA_CHUNK
    fwd = (lax.broadcasted_iota(jnp.int32, g.shape, 1) % (2 * GLA_DK)) < GLA_DK
    k = 1
    while k < GLA_CHUNK:
        dn = jnp.where(pos >= k, pltpu.roll(g, k, 0), 0.0)
        up = jnp.where(pos < GLA_CHUNK - k, pltpu.roll(g, rows - k, 0), 0.0)
        g = g + jnp.where(fwd, dn, up)
        k *= 2
    return g


def _even_in_kernel(x_ref, mod_ref, nw_ref, w_ref, w2h_ref, w2l_ref, b2_ref,
                    qk_ref, v_ref, bc_ref, sg_ref, u_ref, ss_ref):
    h = _modulated(x_ref, mod_ref, nw_ref)
    c = E_COLS
    qk_ref[0] = _dot(h, w_ref[:, c[0]:c[1]])
    v_ref[0] = _dot(h, w_ref[:, c[1]:c[2]])
    lr = _dot(h, w_ref[:, c[2]:c[3]])
    lr_hi = lr.astype(BF16)
    lr_lo = (lr - lr_hi.astype(F32)).astype(BF16)
    pre = _dot(lr_hi, w2h_ref[...]) + (_dot(lr_lo, w2h_ref[...]) + _dot(lr_hi, w2l_ref[...])) + b2_ref[...]
    bc_ref[0] = _chunk_cumsum(_log_sigmoid(pre) * (1.0 / GLA_GATE_NORM))
    sg_ref[0] = _silu(_dot(h, w_ref[:, c[3]:c[4]]))
    u_ref[0] = _dot(h, w_ref[:, c[4]:c[5]])
    ss_ref[0] = _silu(_dot(h, w_ref[:, c[5]:c[6]]))


O_COLS = (0, 512, 768, 1280, 1792, 2304, 2816, 3328)


def _odd_in_kernel(x_ref, mod_ref, nw_ref, w_ref, q_ref, kv_ref, sa_ref, p_ref, bgz_ref):
    h = _modulated(x_ref, mod_ref, nw_ref)
    c = O_COLS
    q_ref[0] = _dot(h, w_ref[:, c[0]:c[1]])
    kv_ref[0] = _dot(h, w_ref[:, c[1]:c[2]])
    sa_ref[0] = _silu(_dot(h, w_ref[:, c[2]:c[3]]))
    xc = _dot(h, w_ref[:, c[3]:c[4]])
    bg = _dot(h, w_ref[:, c[4]:c[5]])
    cg = _dot(h, w_ref[:, c[5]:c[6]])
    zc = _dot(h, w_ref[:, c[6]:c[7]])
    p_ref[0] = cg * xc
    bgz_ref[0] = bg * _silu(zc)


def _in_call(body, name, x, mod, nw, w, consts, out_widths):
    bm, lm, _ = x.shape
    tm = min(ROW_TILE, lm)
    whole = lambda a: pl.BlockSpec(a.shape, lambda b, i: (0, 0))
    return pl.pallas_call(
        body,
        grid=(bm, lm // tm),
        in_specs=[
            pl.BlockSpec((1, tm, D_MODEL), lambda b, i: (b, i, 0)),
            pl.BlockSpec((1, 3, D_MODEL), lambda b, i: (b, 0, 0)),
            whole(nw),
            whole(w),
        ] + [whole(a) for a in consts],
        out_specs=[pl.BlockSpec((1, tm, n), lambda b, i: (b, i, 0)) for n in out_widths],
        out_shape=[jax.ShapeDtypeStruct((bm, lm, n), F32) for n in out_widths],
        compiler_params=_cparams("parallel", "parallel"),
        name=name,
    )(x, mod, nw, w, *consts)


def _gla_kernel(*refs, seq, has_state):
    if has_state:
        (qk_ref, v_ref, bc_ref, sg_ref, on_ref, s0_ref, y_ref, sn_ref,
         ut_scr, sf_scr, sb_scr, qd_scr, o_scr) = refs
    else:
        (qk_ref, v_ref, bc_ref, sg_ref, on_ref, y_ref, sn_ref,
         ut_scr, sf_scr, sb_scr, qd_scr, o_scr) = refs
        s0_ref = None
    C = GLA_CHUNK
    W = 2 * GLA_DK
    R = GLA_BLOCK
    CB = R // C
    nc = seq // C
    nb = seq // R
    ri = lax.broadcasted_iota(jnp.int32, (R, R), 0)
    ci = lax.broadcasted_iota(jnp.int32, (R, R), 1)
    same = (ri // C) == (ci // C)
    tril = same & (ri >= ci)
    triu = same & (ri <= ci)
    fwd = lax.broadcasted_iota(jnp.int32, (R, W), 1) < GLA_DK
    fwd_row = lax.broadcasted_iota(jnp.int32, (1, W), 1) < GLA_DK
    fwd_st = lax.broadcasted_iota(jnp.int32, (GLA_DV, W), 1) < GLA_DK
    row_chunk = lax.broadcasted_iota(jnp.int32, (R, W), 0) // C
    scale = GLA_DK ** -0.5

    def rows(c):
        return pl.ds(pl.multiple_of(c * C, C), C)

    def block_rows(j):
        return pl.ds(pl.multiple_of(j * R, R), R)

    def state_rows(c):
        return pl.ds(pl.multiple_of(c * GLA_DV, GLA_DV), GLA_DV)

    def intra(j, carry):
        r = block_rows(j)
        qk = qk_ref[0, r, :]
        bc = bc_ref[0, r, :]
        v = v_ref[0, r, :]
        sw = pltpu.roll(qk, GLA_DK, 1)
        q2 = jnp.where(fwd, qk, sw) * scale
        k2 = jnp.where(fwd, sw, qk)
        b_mid = bc[C // 2:C // 2 + 1, :]
        b_end = jnp.where(fwd_row, bc[C - 1:C, :], bc[0:1, :])
        for c in range(1, CB):
            o0 = c * C
            b_mid = jnp.where(row_chunk == c, bc[o0 + C // 2:o0 + C // 2 + 1, :], b_mid)
            b_end = jnp.where(row_chunk == c, jnp.where(fwd_row, bc[o0 + C - 1:o0 + C, :], bc[o0:o0 + 1, :]), b_end)
        qd = (q2 * jnp.exp(bc)).astype(BF16)
        qh = q2 * jnp.exp(bc - b_mid)
        kh = (k2 * jnp.exp(b_mid - bc)).astype(BF16)
        att = (jnp.where(tril, _dot_nt(jnp.where(fwd, qh, 0.0).astype(BF16), kh), 0.0)
               + jnp.where(triu, _dot_nt(jnp.where(fwd, 0.0, qh).astype(BF16), kh), 0.0))
        o = _dot(att.astype(BF16), v.astype(BF16))
        kd = k2 * jnp.exp(b_end - bc)
        kd_wide = jnp.concatenate([jnp.where(row_chunk == c, kd, 0.0) for c in range(CB)], axis=1)
        ut = _dot(v.T.astype(BF16), kd_wide.astype(BF16))
        qd_scr[r, :] = qd
        o_scr[r, :] = o
        ut_scr[j] = ut
        return carry

    lax.fori_loop(0, nb, intra, 0)

    def scan(j, st):
        jb = nb - 1 - j
        bcf = bc_ref[0, block_rows(j), :]
        bcb = bc_ref[0, block_rows(jb), :]
        utf = ut_scr[j]
        utb = ut_scr[jb]
        for c in range(CB):
            cb = CB - 1 - c
            stb = st.astype(BF16)
            sf_scr[j, :, c * W:(c + 1) * W] = stb
            sb_scr[jb, :, cb * W:(cb + 1) * W] = stb
            b_end = jnp.where(fwd_row, bcf[c * C + C - 1:c * C + C, :], bcb[cb * C:cb * C + 1, :])
            inc = jnp.where(fwd_st, utf[:, c * W:(c + 1) * W], utb[:, cb * W:(cb + 1) * W])
            st = st * jnp.exp(b_end) + inc
        return st

    if has_state:
        init = jnp.concatenate([s0_ref[0, 0, 0].T, s0_ref[0, 1, 0].T], axis=1)
    else:
        init = jnp.zeros((GLA_DV, W), F32)
    st = lax.fori_loop(0, nb, scan, init)
    sn_ref[0, 0, 0] = st[:, :GLA_DK].T
    sn_ref[0, 1, 0] = st[:, GLA_DK:].T

    fwd_wide = (lax.broadcasted_iota(jnp.int32, (GLA_DV, CB * W), 1) % W) < GLA_DK

    def finish(j, carry):
        r = block_rows(j)
        qd = qd_scr[r, :]
        zero = jnp.zeros((), BF16)
        qd_wide = jnp.concatenate([jnp.where(row_chunk == c, qd, zero) for c in range(CB)], axis=1)
        s_in = jnp.where(fwd_wide, sf_scr[j], sb_scr[j])
        o = o_scr[r, :] + _dot_nt(qd_wide, s_in)
        ms = jnp.mean(o * o, axis=-1, keepdims=True)
        y_ref[0, r, :] = o * lax.rsqrt(ms + EPS) * on_ref[...] * sg_ref[0, r, :]
        return carry

    lax.fori_loop(0, nb, finish, 0, unroll=2)


def _gla_call(qk, v, bc, sg, onorm, s0):
    bsz, seq, _ = qk.shape
    has_state = s0 is not None
    nb = seq // GLA_BLOCK
    wide = (GLA_BLOCK // GLA_CHUNK) * 2 * GLA_DK
    per_head =pl.BlockSpec((1, seq, 128), lambda b, h: (b, 0, h))
    in_specs = [per_head, per_head, per_head, per_head, pl.BlockSpec((1, GLA_DV), lambda b, h: (0, 0))]
    args = [qk, v, bc, sg, onorm]
    state_spec = pl.BlockSpec((1, 2, 1, GLA_DK, GLA_DV), lambda b, h: (b, 0, h, 0, 0))
    if has_state:
        in_specs.append(state_spec)
        args.append(s0)
    return pl.pallas_call(
        functools.partial(_gla_kernel, seq=seq, has_state=has_state),
        grid=(bsz, GLA_HEADS),
        in_specs=in_specs,
        out_specs=[per_head, state_spec],
        out_shape=[jax.ShapeDtypeStruct((bsz, seq, BRANCH), F32),
                   jax.ShapeDtypeStruct((bsz, 2, GLA_HEADS, GLA_DK, GLA_DV), F32)],
        scratch_shapes=[pltpu.VMEM((nb, GLA_DV, wide), F32),
                        pltpu.VMEM((nb, GLA_DV, wide), BF16),
                        pltpu.VMEM((nb, GLA_DV, wide), BF16),
                        pltpu.VMEM((seq, 2 * GLA_DK), BF16),
                        pltpu.VMEM((seq, GLA_DV), F32)],
        compiler_params=_cparams("parallel", "parallel"),
        name="gla",
    )(*args)


def _s5_operators(lam_re, lam_im, log_dt, b_re, b_im, c_re, c_im, nlev):
    hp = dict(precision=HIGHEST)
    cs = S5_CHUNK
    dt = jnp.exp(log_dt)[..., None]
    er, ei = lam_re * dt, lam_im * dt
    lb_re, lb_im = jnp.exp(er) * jnp.cos(ei), jnp.exp(er) * jnp.sin(ei)
    den = lam_re * lam_re + lam_im * lam_im
    nr, ni = lb_re - 1.0, lb_im
    cr = (nr * lam_re + ni * lam_im) / den
    ci = (ni * lam_re - nr * lam_im) / den
    bb_re = cr[..., None] * b_re - ci[..., None] * b_im
    bb_im = cr[..., None] * b_im + ci[..., None] * b_re

    def powers(k):
        kk = k.astype(F32)[:, None, None, None]
        mag = jnp.exp(kk * er)
        return mag * jnp.cos(kk * ei), mag * jnp.sin(kk * ei)

    pr, pi = powers(jnp.arange(cs + 1))
    cl_re = c_re[None, None] * pr[:, :, :, None, :] - c_im[None, None] * pi[:, :, :, None, :]
    cl_im = c_re[None, None] * pi[:, :, :, None, :] + c_im[None, None] * pr[:, :, :, None, :]
    kern = (jnp.einsum('kdgjp,dgph->kdghj', cl_re[:cs], bb_re, **hp)
            - jnp.einsum('kdgjp,dgph->kdghj', cl_im[:cs], bb_im, **hp))
    s_in = jnp.arange(cs)[:, None]
    s_out = jnp.arange(cs)[None, :]
    lag_f = jnp.clip(s_out - s_in, 0, cs - 1)
    lag_b = jnp.clip(s_in - s_out, 0, cs - 1)
    m = (jnp.where((s_in <= s_out)[:, :, None, None, None], kern[:, 0][lag_f], 0.0)
         + jnp.where((s_in >= s_out)[:, :, None, None, None], kern[:, 1][lag_b], 0.0))
    m = m.transpose(2, 0, 3, 1, 4).reshape(S5_GROUPS, S5_ROW, S5_ROW)

    kf = jnp.arange(cs - 1, -1, -1)
    kb = jnp.arange(cs)
    pw_re = jnp.stack([pr[kf, 0], pr[kb, 1]], axis=1)
    pw_im = jnp.stack([pi[kf, 0], pi[kb, 1]], axis=1)
    win_re = pw_re[..., None] * bb_re[None] - pw_im[..., None] * bb_im[None]
    win_im = pw_re[..., None] * bb_im[None] + pw_im[..., None] * bb_re[None]

    def in_cols(w):
        return w.transpose(2, 0, 4, 1, 3).reshape(S5_GROUPS, S5_ROW, 2 * S5_STATE)

    win = jnp.concatenate([in_cols(win_re), in_cols(win_im)], axis=-1)

    kf = jnp.arange(1, cs + 1)
    kb = jnp.arange(cs, 0, -1)
    co_re = jnp.stack([cl_re[kf, 0], cl_re[kb, 1]], axis=1)
    co_im = jnp.stack([cl_im[kf, 0], cl_im[kb, 1]], axis=1)

    def out_rows(w):
        return w.transpose(2, 1, 4, 0, 3).reshape(S5_GROUPS, 2 * S5_STATE, S5_ROW)

    wout = jnp.concatenate([out_rows(co_re), out_rows(-co_im)], axis=1)

    lev = cs * (2 ** jnp.arange(nlev))
    ar, ai = powers(lev)
    lanes = lambda a: a.transpose(2, 0, 1, 3).reshape(S5_GROUPS, nlev, 2 * S5_STATE)
    return m.astype(BF16), win.astype(BF16), wout.astype(BF16), lanes(ar), lanes(ai)


def _gelu_tanh(x):
    return 0.5 * x * (1.0 + jnp.tanh(math.sqrt(2.0 / math.pi) * (x + 0.044715 * (x * x * x))))


def _s5_kernel(*refs, nseq, nrow, nlev, has_state):
    if has_state:
        (u_ref, m_ref, win_ref, wout_ref, ar_ref, ai_ref, d_ref, h0_ref, z_ref, last_ref, tmp_r, tmp_i) = refs
    else:
        (u_ref, m_ref, win_ref, wout_ref, ar_ref, ai_ref, d_ref, z_ref, last_ref, tmp_r, tmp_i) = refs
    P2 = 2 * S5_STATE
    R = nseq * nrow
    u = u_ref[0]
    ub = u.astype(BF16)
    x = _dot(ub, win_ref[0])
    xr, xi = x[:, :P2], x[:, P2:]
    n = lax.broadcasted_iota(jnp.int32, (R, P2), 0) % nrow
    fwd = lax.broadcasted_iota(jnp.int32, (R, P2), 1) < S5_STATE

    def shift(a, k):
        dn = pltpu.roll(a, k, 0)
        up = pltpu.roll(a, R - k, 0)
        return jnp.where(fwd, jnp.where(n >= k, dn, 0.0), jnp.where(n < nrow - k, up, 0.0))

    er, ei = shift(xr, 1), shift(xi, 1)
    if has_state:
        row = lax.broadcasted_iota(jnp.int32, (R, P2), 0)
        for b in range(nseq):
            first = row == jnp.where(fwd, b * nrow, b * nrow + nrow - 1)
            er = jnp.where(first, h0_ref[0, b:b + 1, :P2], er)
            ei = jnp.where(first, h0_ref[0, b:b + 1, P2:], ei)
    for j in range(nlev):
        k = 2 ** j
        ar = ar_ref[0, j:j + 1, :]
        ai = ai_ref[0, j:j + 1, :]
        sr, si = shift(er, k), shift(ei, k)
        er, ei = er + ar * sr - ai * si, ei + ar * si + ai * sr
    ecat = jnp.concatenate([er, ei], axis=1).astype(BF16)
    y = _dot(ub, m_ref[0]) + _dot(ecat, wout_ref[0]) + d_ref[0] * u
    z_ref[0] = _gelu_tanh(y)
    a1r = ar_ref[0, 0:1, :]
    a1i = ai_ref[0, 0:1, :]
    tmp_r[...] = a1r * er - a1i * ei + xr
    tmp_i[...] = a1r * ei + a1i * er + xi
    fwd_b = lax.broadcasted_iota(jnp.int32, (nseq, P2), 1) < S5_STATE
    ends = lambda t: jnp.where(fwd_b, t[pl.ds(nrow - 1, nseq, stride=nrow), :], t[pl.ds(0, nseq, stride=nrow), :])
    last_ref[0, :, :P2] = ends(tmp_r)
    last_ref[0, :, P2:] = ends(tmp_i)


def _s5_call(ugm, ops, dgm, h0, nseq, nrow):
    m, win, wout, ar, ai = ops
    nlev = ar.shape[1]
    has_state = h0 is not None
    R = nseq * nrow
    g3 = lambda g: (g, 0, 0)
    in_specs = [
        pl.BlockSpec((1, R, S5_ROW), g3),
        pl.BlockSpec((1, S5_ROW, S5_ROW), g3),
        pl.BlockSpec((1, S5_ROW, S5_ROW), g3),
        pl.BlockSpec((1, S5_ROW, S5_ROW), g3),
        pl.BlockSpec((1, nlev, 2 * S5_STATE), g3),
        pl.BlockSpec((1, nlev, 2 * S5_STATE), g3),
        pl.BlockSpec((1, 1, S5_ROW), g3),
    ]
    args = [ugm, m, win, wout, ar, ai, dgm]
    if has_state:
        in_specs.append(pl.BlockSpec((1, nseq, 4 * S5_STATE), g3))
        args.append(h0)
    return pl.pallas_call(
        functools.partial(_s5_kernel, nseq=nseq, nrow=nrow, nlev=nlev, has_state=has_state),
        grid=(S5_GROUPS,),
        in_specs=in_specs,
        out_specs=[pl.BlockSpec((1, R, S5_ROW), g3), pl.BlockSpec((1, nseq, 4 * S5_STATE), g3)],
        out_shape=[jax.ShapeDtypeStruct((S5_GROUPS, R, S5_ROW), F32),
                   jax.ShapeDtypeStruct((S5_GROUPS, nseq, 4 * S5_STATE), F32)],
        scratch_shapes=[pltpu.VMEM((R, 2 * S5_STATE), F32)] * 2,
        compiler_params=_cparams("parallel"),
        name="s5",
    )(*args)


def _to_group_major(u):
    b, l, _ = u.shape
    t = u.reshape(b, l // S5_CHUNK, S5_CHUNK, S5_GROUPS, S5_GROUP).transpose(3, 0, 1, 2, 4)
    return t.reshape(S5_GROUPS, b * l // S5_CHUNK, S5_ROW)


def _from_group_major(z, b, l):
    t = z.reshape(S5_GROUPS, b, l // S5_CHUNK, S5_CHUNK, S5_GROUP).transpose(1, 2, 3, 0, 4)
    return t.reshape(b, l, BRANCH)


def _seg_rmsnorm(x, w, nseg):
    seg = lax.broadcasted_iota(jnp.int32, x.shape, 1) // ATT_HD
    x2 = x * x
    rs = jnp.ones_like(x)
    for s in range(nseg):
        ms = jnp.sum(jnp.where(seg == s, x2, 0.0), axis=-1, keepdims=True) * (1.0 / ATT_HD)
        rs = jnp.where(seg == s, lax.rsqrt(ms + EPS), rs)
    return x * rs * w


def _rope(x, cos, sin):
    width = x.shape[1]
    lane = lax.broadcasted_iota(jnp.int32, x.shape, 1)
    partner = jnp.where((lane & 16) == 0, pltpu.roll(x, width - 16, 1), pltpu.roll(x, 16, 1))
    return x * cos + partner * sin


def _softmax_pv(scores, values, sink):
    m = sink
    for s in scores:
        m = jnp.maximum(m, jnp.max(s, axis=-1, keepdims=True))
    den = jnp.exp(sink - m)
    out = None
    for s, v in zip(scores, values):
        p = jnp.exp(s - m)
        den = den + jnp.sum(p, axis=-1, keepdims=True)
        pv = _dot(p.astype(BF16), v)
        out = pv if out is None else out + pv
    return out / den


def _att_ctx_kernel(q_ref, kv_ref, sa_ref, qn_ref, kn_ref, sink_ref, y_ref, kc_ref, vc_ref):
    q = _seg_rmsnorm(q_ref[0], qn_ref[...], ATT_GROUPS) * (ATT_HD ** -0.5)
    kv = _seg_rmsnorm(kv_ref[0], kn_ref[...], 1)
    k = kv[:, :ATT_HD]
    v = kv[:, ATT_HD:]
    kc_ref[0, 0] = k
    vc_ref[0, 0] = v
    kb = k.astype(BF16)
    vb = v.astype(BF16)
    outs = []
    for g in range(ATT_GROUPS):
        qg = q[:, g * ATT_HD:(g + 1) * ATT_HD].astype(BF16)
        s = _dot_nt(qg, kb)
        outs.append(_softmax_pv([s], [vb], sink_ref[0, g:g + 1, 0:1]))
    y_ref[0] = jnp.concatenate(outs, axis=1) * sa_ref[0]


def _att_ctx_call(q, kv, sa, qnw, knw, sink):
    bsz, seq, _ = q.shape
    W = ATT_GROUPS * ATT_HD
    per_kv = lambda b, k: (b, 0, k)
    cache_spec = pl.BlockSpec((1, 1, seq, ATT_HD), lambda b, k: (b, k, 0, 0))
    return pl.pallas_call(
        _att_ctx_kernel,
        grid=(bsz, ATT_KV_HEADS),
        in_specs=[
            pl.BlockSpec((1, seq, W), per_kv),
            pl.BlockSpec((1, seq, 2 * ATT_HD), per_kv),
            pl.BlockSpec((1, seq, W), per_kv),
            pl.BlockSpec((1, W), lambda b, k: (0, 0)),
            pl.BlockSpec((1, 2 * ATT_HD), lambda b, k: (0, 0)),
            pl.BlockSpec((1, 8, 128), lambda b, k: (k, 0, 0)),
        ],
        out_specs=[pl.BlockSpec((1, seq, W), per_kv), cache_spec, cache_spec],
        out_shape=[jax.ShapeDtypeStruct((bsz, seq, BRANCH), F32),
                   jax.ShapeDtypeStruct((bsz, ATT_KV_HEADS, seq, ATT_HD), F32),
                   jax.ShapeDtypeStruct((bsz, ATT_KV_HEADS, seq, ATT_HD), F32)],
        compiler_params=_cparams("parallel", "parallel"),
        name="att_ctx",
    )(q, kv, sa, qnw, knw, sink)


def _att_lat_kernel(q_ref, kv_ref, sa_ref, qn_ref, kn_ref, sink_ref, cq_ref, sq_ref, ck_ref, sk_ref,
                    kc_ref, vc_ref, y_ref, kr_scr, *, seq):
    nb = seq // QBLK
    KW = 3 * QBLK

    def k_body(i, carry):
        r = pl.ds(pl.multiple_of(i * QBLK, QBLK), QBLK)
        kv = _seg_rmsnorm(kv_ref[0, r, :], kn_ref[...], 1)
        kr_scr[r, :] = _rope(kv, ck_ref[r, :], sk_ref[r, :]).astype(BF16)
        return carry

    lax.fori_loop(0, nb, k_body, 0)
    kctx = kc_ref[0, 0, 0].astype(BF16)
    vctx = vc_ref[0, 0, 0].astype(BF16)
    qi = lax.broadcasted_iota(jnp.int32, (QBLK, KW), 0)
    kj = lax.broadcasted_iota(jnp.int32, (QBLK, KW), 1)

    def q_body(i, carry):
        r = pl.ds(pl.multiple_of(i * QBLK, QBLK), QBLK)
        q = _seg_rmsnorm(q_ref[0, r, :], qn_ref[...], ATT_GROUPS)
        q = _rope(q, cq_ref[r, :], sq_ref[r, :]) * (ATT_HD ** -0.5)
        ws = pl.multiple_of(jnp.clip(i * QBLK - QBLK, 0, seq - KW), QBLK)
        kvw = kr_scr[pl.ds(ws, KW), :]
        kw = kvw[:, :ATT_HD]
        vw = kvw[:, ATT_HD:]
        band = jnp.abs((i * QBLK + qi) - (ws + kj)) <= WINDOW
        outs = []
        for g in range(ATT_GROUPS):
            qg = q[:, g * ATT_HD:(g + 1) * ATT_HD].astype(BF16)
            s_win = jnp.where(band, _dot_nt(qg, kw), NEG)
            s_ctx = _dot_nt(qg, kctx)
            outs.append(_softmax_pv([s_win, s_ctx], [vw, vctx], sink_ref[0, g:g + 1, 0:1]))
        y_ref[0, r, :] = jnp.concatenate(outs, axis=1) * sa_ref[0, r, :]
        return carry

    lax.fori_loop(0, nb, q_body, 0)


def _rope_tables(seq, heads, pad_heads):
    pos = jnp.arange(seq)
    row = (pos // GRID_W).astype(F32)[:, None]
    col = (pos % GRID_W).astype(F32)[:, None]
    nf = ATT_HD // 4
    freq = ROPE_BASE ** (-jnp.arange(nf, dtype=F32) / nf)
    ar, ac = row * freq, col * freq
    cos = jnp.concatenate([jnp.cos(ar), jnp.cos(ar), jnp.cos(ac), jnp.cos(ac)], axis=1)
    sin = jnp.concatenate([-jnp.sin(ar), jnp.sin(ar), -jnp.sin(ac), jnp.sin(ac)], axis=1)
    cos = jnp.concatenate([jnp.tile(cos, (1, heads)), jnp.ones((seq, pad_heads * ATT_HD), F32)], axis=1)
    sin = jnp.concatenate([jnp.tile(sin, (1, heads)), jnp.zeros((seq, pad_heads * ATT_HD), F32)], axis=1)
    return cos, sin


def _att_lat_call(q, kv, sa, qnw, knw, sink, cache_k, cache_v, e):
    bsz, seq, _ = q.shape
    past = cache_k.shape[3]
    W = ATT_GROUPS * ATT_HD
    cq, sq = _rope_tables(seq, ATT_GROUPS, 0)
    ck, sk = _rope_tables(seq, 1, 1)
    per_kv = lambda b, k: (b, 0, k)
    const = lambda b, k: (0, 0)
    cache_spec = pl.BlockSpec((1, 1, 1, past, ATT_HD), lambda b, k: (b, e, k, 0, 0))
    return pl.pallas_call(
        functools.partial(_att_lat_kernel, seq=seq),
        grid=(bsz, ATT_KV_HEADS),
        in_specs=[
            pl.BlockSpec((1, seq, W), per_kv),
            pl.BlockSpec((1, seq, 2 * ATT_HD), per_kv),
            pl.BlockSpec((1, seq, W), per_kv),
            pl.BlockSpec((1, W), const),
            pl.BlockSpec((1, 2 * ATT_HD), const),
            pl.BlockSpec((1, 8, 128), lambda b, k: (k, 0, 0)),
            pl.BlockSpec((seq, W), const),
            pl.BlockSpec((seq, W), const),
            pl.BlockSpec((seq, 2 * ATT_HD), const),
            pl.BlockSpec((seq, 2 * ATT_HD), const),
            cache_spec,
            cache_spec,
        ],
        out_specs=pl.BlockSpec((1, seq, W), per_kv),
        out_shape=jax.ShapeDtypeStruct((bsz, seq, BRANCH), F32),
        scratch_shapes=[pltpu.VMEM((seq, 2 * ATT_HD), BF16)],
        compiler_params=_cparams("parallel", "parallel"),
        name="att_lat",
    )(q, kv, sa, qnw, knw, sink, cq, sq, ck, sk, cache_k, cache_v)


def _even_out_kernel(x_ref, mod_ref, yg_ref, z_ref, ss_ref, wg_ref, bg_ref, wo_ref, o_ref):
    z = z_ref[0]
    glu = z * jax.nn.sigmoid(_dot(z.astype(BF16), wg_ref[...]) + bg_ref[...])
    ys = (glu * ss_ref[0]).astype(BF16)
    yg = yg_ref[0].astype(BF16)
    out = _dot(yg, wo_ref[:BRANCH, :]) + _dot(ys, wo_ref[BRANCH:, :])
    o_ref[0] = x_ref[0] + mod_ref[0, 2:3, :] * out


def _even_out_call(x, mod, yg, z, ss, wglu, bglu, wout):
    bm, lm, _ = x.shape
    tm = min(ROW_TILE, lm)
    tile = lambda n: pl.BlockSpec((1, tm, n), lambda b, i: (b, i, 0))
    const = lambda b, i: (0, 0)
    return pl.pallas_call(
        _even_out_kernel,
        grid=(bm, lm // tm),
        in_specs=[
            tile(D_MODEL),
            pl.BlockSpec((1, 3, D_MODEL), lambda b, i: (b, 0, 0)),
            tile(BRANCH), tile(BRANCH), tile(BRANCH),
            pl.BlockSpec((BRANCH, BRANCH), const),
            pl.BlockSpec((1, BRANCH), const),
            pl.BlockSpec((D_MODEL, D_MODEL), const),
        ],
        out_specs=tile(D_MODEL),
        out_shape=jax.ShapeDtypeStruct((bm, lm, D_MODEL), F32),
        compiler_params=_cparams("parallel", "parallel"),
        name="even_out",
    )(x, mod, yg, z, ss, wglu, bglu, wout)


def _odd_out_kernel(x_ref, mod_ref, ya_ref, p_ref, pprev_ref, pnext_ref, bgz_ref, cw_ref, cb_ref, wo_ref,
                    o_ref, *, seq, tm):
    p = p_ref[0]
    pos = (pl.program_id(1) * tm + lax.broadcasted_iota(jnp.int32, (tm, BRANCH), 0)) % seq
    rowi = lax.broadcasted_iota(jnp.int32, (tm, BRANCH), 0)
    prev = jnp.where(rowi == 0, pprev_ref[0, 7:8, :], pltpu.roll(p, 1, 0))
    nxt = jnp.where(rowi == tm - 1, pnext_ref[0, 0:1, :], pltpu.roll(p, tm - 1, 0))
    prev = jnp.where(pos == 0, 0.0, prev)
    nxt = jnp.where(pos == seq - 1, 0.0, nxt)
    conv = prev * cw_ref[0:1, :] + p * cw_ref[1:2, :] + nxt * cw_ref[2:3, :] + cb_ref[...]
    yc = (bgz_ref[0] * conv).astype(BF16)
    ya = ya_ref[0].astype(BF16)
    out = _dot(ya, wo_ref[:BRANCH, :]) + _dot(yc, wo_ref[BRANCH:, :])
    o_ref[0] = x_ref[0] + mod_ref[0, 2:3, :] * out


def _odd_out_call(x, mod, ya, p, bgz, convw, convb, wout, seq):
    bm, lm, _ = x.shape
    tm = min(ROW_TILE, lm)
    nt8 = lm // 8
    tile = lambda n: pl.BlockSpec((1, tm, n), lambda b, i: (b, i, 0))
    const = lambda b, i: (0, 0)
    prev_spec = pl.BlockSpec((1, 8, BRANCH), lambda b, i: (b, jnp.maximum(i * (tm // 8) - 1, 0), 0))
    next_spec = pl.BlockSpec((1, 8, BRANCH), lambda b, i: (b, jnp.minimum((i + 1) * (tm // 8), nt8 - 1), 0))
    return pl.pallas_call(
        functools.partial(_odd_out_kernel, seq=seq, tm=tm),
        grid=(bm, lm // tm),
        in_specs=[
            tile(D_MODEL),
            pl.BlockSpec((1, 3, D_MODEL), lambda b, i: (b, 0, 0)),
            tile(BRANCH), tile(BRANCH), prev_spec, next_spec, tile(BRANCH),
            pl.BlockSpec((CONV_W, BRANCH), const),
            pl.BlockSpec((1, BRANCH), const),
            pl.BlockSpec((D_MODEL, D_MODEL), const),
        ],
        out_specs=tile(D_MODEL),
        out_shape=jax.ShapeDtypeStruct((bm, lm, D_MODEL), F32),
        compiler_params=_cparams("parallel", "parallel"),
        name="odd_out",
    )(x, mod, ya, p, p, p, bgz, convw, convb, wout)


def _even_in_weight(w):
    dk = GLA_HEADS * GLA_DK
    q, k = w[:, :dk], w[:, dk:2 * dk]
    v = w[:, 2 * dk:2 * dk + BRANCH]
    o = 2 * dk + BRANCH
    lr = w[:, o:o + 2 * GLA_RANK]
    rest = w[:, o + 2 * GLA_RANK:]
    qk = jnp.concatenate([jnp.concatenate([q[:, h * GLA_DK:(h + 1) * GLA_DK], k[:, h * GLA_DK:(h + 1) * GLA_DK]],
                                          axis=1) for h in range(GLA_HEADS)], axis=1)
    lr = jnp.pad(lr, ((0, 0), (0, 128 - 2 * GLA_RANK)))
    return jnp.concatenate([qk, v, lr, rest], axis=1).astype(BF16)


def _odd_in_weight(w):
    q = w[:, :BRANCH]
    k = w[:, BRANCH:BRANCH + ATT_KV_HEADS * ATT_HD]
    v = w[:, BRANCH + ATT_KV_HEADS * ATT_HD:BRANCH + 2 * ATT_KV_HEADS * ATT_HD]
    rest = w[:, BRANCH + 2 * ATT_KV_HEADS * ATT_HD:]
    kv = jnp.concatenate([jnp.concatenate([k[:, h * ATT_HD:(h + 1) * ATT_HD], v[:, h * ATT_HD:(h + 1) * ATT_HD]],
                                          axis=1) for h in range(ATT_KV_HEADS)], axis=1)
    return jnp.concatenate([q, kv, rest], axis=1).astype(BF16)


def _gate_weights(w2, b2):
    zf = jnp.zeros((GLA_RANK, GLA_DK), F32)
    cols, bias = [], []
    for h in range(GLA_HEADS):
        sl = slice(h * GLA_DK, (h + 1) * GLA_DK)
        cols.append(jnp.concatenate([jnp.concatenate([w2[0][:, sl], zf], axis=1),
                                     jnp.concatenate([zf, w2[1][:, sl]], axis=1)], axis=0))
        bias += [b2[0][sl], b2[1][sl]]
    w = jnp.pad(jnp.concatenate(cols, axis=1), ((0, 128 - 2 * GLA_RANK), (0, 0)))
    w_hi = w.astype(BF16)
    w_lo = (w - w_hi.astype(F32)).astype(BF16)
    return w_hi, w_lo, jnp.concatenate(bias)[None, :]


def _even_layer(x, mod, nw, w_in, seq, gla_w, s5_ops, dgm, wglu, bglu, wout, onorm, gla_s0, s5_h0):
    bm, lm, _ = x.shape
    nseq = bm * lm // seq
    qk, v, bc, sg, u, ss = _in_call(_even_in_kernel, "even_in", x, mod, nw, w_in, gla_w,
                                    (BRANCH,) * 6)
    per_seq = lambda a: a.reshape(nseq, seq, a.shape[-1])
    yg, s_new = _gla_call(per_seq(qk), per_seq(v), per_seq(bc), per_seq(sg), onorm, gla_s0)
    nrow = seq // S5_CHUNK
    z_gm, last = _s5_call(_to_group_major(per_seq(u)), s5_ops, dgm, s5_h0, nseq, nrow)
    z = _from_group_major(z_gm, nseq, seq)
    x_new = _even_out_call(x, mod, yg.reshape(bm, lm, BRANCH), z.reshape(bm, lm, BRANCH), ss, wglu, bglu, wout)
    return x_new, s_new, last


def _odd_layer(x, mod, nw, w_in, seq, qnw, knw, sink, convw, convb, wout, cache_k, cache_v, e):
    bm, lm, _ = x.shape
    nseq = bm * lm // seq
    q, kv, sa, p, bgz = _in_call(_odd_in_kernel, "odd_in", x, mod, nw, w_in, (),
                                 (BRANCH, 2 * ATT_KV_HEADS * ATT_HD, BRANCH, BRANCH, BRANCH))
    per_seq = lambda a: a.reshape(nseq, seq, a.shape[-1])
    if cache_k is None:
        ya, kc, vc = _att_ctx_call(per_seq(q), per_seq(kv), per_seq(sa), qnw, knw, sink)
    else:
        ya = _att_lat_call(per_seq(q), per_seq(kv), per_seq(sa), qnw, knw, sink, cache_k, cache_v, e)
        kc = vc = None
    x_new = _odd_out_call(x, mod, ya.reshape(bm, lm, BRANCH), p, bgz, convw, convb, wout, seq)
    return x_new, kc, vc


def kernel(x_prompt, x_sample, c, state_gla, state_s5_re, state_s5_im, cache_k, cache_v, c_ctx, norm_w, w_ada, b_ada, w_in_e, w_out_e, gla_w2, gla_b2, gla_onorm, s5_lam_re, s5_lam_im, s5_log_dt, s5_b_re, s5_b_im, s5_c_re, s5_c_im, s5_d, s5_w_glu, s5_b_glu, w_in_o, w_out_o, q_norm_w, k_norm_w, sink, conv_w, conv_b):
    depth = norm_w.shape[0]
    bp, lp, _ = x_prompt.shape
    bs, ls, _ = x_sample.shape

    rows = 8 * ((1 + bs + 7) // 8)
    cs = jnp.zeros((rows, D_MODEL), F32).at[0].set(c_ctx).at[1:1 + bs].set(c)
    mods = _ada_call(cs, w_ada, b_ada)

    xp = x_prompt.reshape(1, bp * lp, D_MODEL)
    xs = x_sample
    new_gla, new_re, new_im, new_k, new_v = [], [], [], [], []
    for l in range(depth):
        e = l // 2
        mod_p = mods[l, 0:1].reshape(1, 3, D_MODEL)
        mod_s = mods[l, 1:1 + bs].reshape(bs, 3, D_MODEL)
        nw = norm_w[l][None, :]
        if l % 2 == 0:
            w_in = _even_in_weight(w_in_e[e])
            wout = w_out_e[e].astype(BF16)
            gla_w = _gate_weights(gla_w2[e], gla_b2[e])
            onorm = gla_onorm[e][None, :]
            dgm = jnp.tile(s5_d[e].reshape(S5_GROUPS, 1, S5_GROUP), (1, S5_CHUNK, 1)).reshape(S5_GROUPS, 1, S5_ROW)
            wglu = s5_w_glu[e].astype(BF16)
            bglu = s5_b_glu[e][None, :]
            s5p = (s5_lam_re[e], s5_lam_im[e], s5_log_dt[e], s5_b_re[e], s5_b_im[e], s5_c_re[e], s5_c_im[e])
            ops_p = _s5_operators(*s5p, nlev=int(math.log2(lp // S5_CHUNK)))
            ops_s = _s5_operators(*s5p, nlev=int(math.log2(ls // S5_CHUNK)))
            h0 = jnp.concatenate([state_s5_re[:, e], state_s5_im[:, e]], axis=1)
            h0 = h0.transpose(2, 0, 1, 3).reshape(S5_GROUPS, bs, 4 * S5_STATE)
            xp, sg, last = _even_layer(xp, mod_p, nw, w_in, lp, gla_w, ops_p, dgm, wglu, bglu, wout, onorm,
                                       None, None)
            xs, _, _ = _even_layer(xs, mod_s, nw, w_in, ls, gla_w, ops_s, dgm, wglu, bglu, wout, onorm,
                                   state_gla[:, e], h0)
            new_gla.append(sg)
            last = last.reshape(S5_GROUPS, bp, 2, 2, S5_STATE).transpose(2, 1, 3, 0, 4)
            new_re.append(last[0])
            new_im.append(last[1])
        else:
            w_in = _odd_in_weight(w_in_o[e])
            wout = w_out_o[e].astype(BF16)
            qnw = jnp.tile(q_norm_w[e], ATT_GROUPS)[None, :]
            knw = jnp.concatenate([k_norm_w[e], jnp.ones((ATT_HD,), F32)])[None, :]
            sk = jnp.broadcast_to(sink[e].reshape(ATT_KV_HEADS, ATT_GROUPS, 1), (ATT_KV_HEADS, ATT_GROUPS, 128))
            sk = jnp.concatenate([sk, jnp.zeros((ATT_KV_HEADS, 8 - ATT_GROUPS, 128), F32)], axis=1)
            cb = conv_b[e][None, :]
            xp, kc, vc = _odd_layer(xp, mod_p, nw, w_in, lp, qnw, knw, sk, conv_w[e], cb, wout, None, None, e)
            xs, _, _ = _odd_layer(xs, mod_s, nw, w_in, ls, qnw, knw, sk, conv_w[e], cb, wout, cache_k, cache_v, e)
            new_k.append(kc)
            new_v.append(vc)
    return (xp.reshape(bp, lp, D_MODEL), xs,
            jnp.stack(new_gla, axis=1), jnp.stack(new_re, axis=1), jnp.stack(new_im, axis=1),
            jnp.stack(new_k, axis=1), jnp.stack(new_v, axis=1))
```

```python
import functools
import math

import jax
import jax.numpy as jnp
from jax import lax
from jax.experimental import pallas as pl
from jax.experimental.pallas import tpu as pltpu

F32 = jnp.float32
BF16 = jnp.bfloat16
HIGHEST = lax.Precision.HIGHEST

D_MODEL = 1024
BRANCH = D_MODEL // 2
GLA_HEADS = 4
GLA_DK = 64
GLA_DV = 128
GLA_RANK = 16
GLA_GATE_NORM = 16.0
GLA_CHUNK = 64
GLA_BLOCK = 256
S5_GROUP = 16
S5_GROUPS = BRANCH // S5_GROUP
S5_STATE = 64
S5_CHUNK = 16
S5_ROW = S5_CHUNK * S5_GROUP
S5_SLAB_GROUPS = 128 // S5_GROUP
S5_BLOCK_ROWS = 256
ATT_HEADS = 8
ATT_KV_HEADS = 2
ATT_GROUPS = ATT_HEADS // ATT_KV_HEADS
ATT_HD = 64
WINDOW = 128
QBLK = 128
GRID_W = 64
ROPE_BASE = 10000.0
CONV_W = 3
EPS = 1e-6
NEG = -1e30

ROW_TILE = 512
VMEM_LIMIT = 48 * 1024 * 1024


def _cparams(*sem):
    return pltpu.CompilerParams(dimension_semantics=sem, vmem_limit_bytes=VMEM_LIMIT)


def _silu(x):
    return x * jax.nn.sigmoid(x)


def _dot(a, b):
    return jnp.dot(a, b, preferred_element_type=F32)


def _dot_nt(a, b):
    return lax.dot_general(a, b, (((1,), (1,)), ((), ())), preferred_element_type=F32)


def _ada_kernel(c_ref, w_ref, b_ref, o_ref):
    c = c_ref[...]
    o_ref[0] = jnp.dot(_silu(c), w_ref[0], precision=HIGHEST, preferred_element_type=F32) + b_ref[0]


def _ada_call(cs, w_ada, b_ada):
    depth = w_ada.shape[0]
    rows = cs.shape[0]
    nt = 3
    return pl.pallas_call(
        _ada_kernel,
        grid=(depth, nt),
        in_specs=[
            pl.BlockSpec((rows, D_MODEL), lambda l, j: (0, 0)),
            pl.BlockSpec((1, D_MODEL, D_MODEL), lambda l, j: (l, 0, j)),
            pl.BlockSpec((1, 1, D_MODEL), lambda l, j: (l, 0, j)),
        ],
        out_specs=pl.BlockSpec((1, rows, D_MODEL), lambda l, j: (l, 0, j)),
        out_shape=jax.ShapeDtypeStruct((depth, rows, 3 * D_MODEL), F32),
        compiler_params=_cparams("arbitrary", "arbitrary"),
        name="adaln",
    )(cs, w_ada, b_ada.reshape(depth, 1, 3 * D_MODEL))


def _modulated(x_ref, mod_ref, nw_ref):
    x = x_ref[0]
    ms = jnp.mean(x * x, axis=-1, keepdims=True)
    y = x * lax.rsqrt(ms + EPS) * nw_ref[...]
    shift = mod_ref[0, 0:1, :]
    scale = mod_ref[0, 1:2, :]
    return (y * (1.0 + scale) + shift).astype(BF16)


E_COLS = (0, 512, 1024, 1152, 1664, 2176, 2688)


def _log_sigmoid(x):
    return jnp.minimum(x, 0.0) - jnp.log(1.0 + jnp.exp(-jnp.abs(x)))


def _chunk_cumsum(g):
    rows = g.shape[0]
    pos = lax.broadcasted_iota(jnp.int32, g.shape, 0) % GLA_CHUNK
    fwd = (lax.broadcasted_iota(jnp.int32, g.shape, 1) % (2 * GLA_DK)) < GLA_DK
    k = 1
    while k < GLA_CHUNK:
        dn = jnp.where(pos >= k, pltpu.roll(g, k, 0), 0.0)
        up = jnp.where(pos < GLA_CHUNK - k, pltpu.roll(g, rows - k, 0), 0.0)
        g = g + jnp.where(fwd, dn, up)
        k *= 2
    return g


def _even_in_kernel(x_ref, mod_ref, nw_ref, w_ref, w2h_ref, w2l_ref, b2_ref,
                    qk_ref, v_ref, bc_ref, sg_ref, u_ref, ss_ref):
    h = _modulated(x_ref, mod_ref, nw_ref)
    c = E_COLS
    qk_ref[0] = _dot(h, w_ref[:, c[0]:c[1]])
    v_ref[0] = _dot(h, w_ref[:, c[1]:c[2]])
    lr = _dot(h, w_ref[:, c[2]:c[3]])
    lr_hi = lr.astype(BF16)
    lr_lo = (lr - lr_hi.astype(F32)).astype(BF16)
    pre = _dot(lr_hi, w2h_ref[...]) + (_dot(lr_lo, w2h_ref[...]) + _dot(lr_hi, w2l_ref[...])) + b2_ref[...]
    bc_ref[0] = _chunk_cumsum(_log_sigmoid(pre) * (1.0 / GLA_GATE_NORM))
    sg_ref[0] = _silu(_dot(h, w_ref[:, c[3]:c[4]]))
    u_ref[0] = _dot(h, w_ref[:, c[4]:c[5]])
    ss_ref[0] = _silu(_dot(h, w_ref[:, c[5]:c[6]]))


O_COLS = (0, 512, 768, 1280, 1792, 2304, 2816, 3328)


def _odd_in_kernel(x_ref, mod_ref, nw_ref, w_ref, q_ref, kv_ref, sa_ref, p_ref, bgz_ref):
    h = _modulated(x_ref, mod_ref, nw_ref)
    c = O_COLS
    q_ref[0] = _dot(h, w_ref[:, c[0]:c[1]])
    kv_ref[0] = _dot(h, w_ref[:, c[1]:c[2]])
    sa_ref[0] = _silu(_dot(h, w_ref[:, c[2]:c[3]]))
    xc = _dot(h, w_ref[:, c[3]:c[4]])
    bg = _dot(h, w_ref[:, c[4]:c[5]])
    cg = _dot(h, w_ref[:, c[5]:c[6]])
    zc = _dot(h, w_ref[:, c[6]:c[7]])
    p_ref[0] = cg * xc
    bgz_ref[0] = bg * _silu(zc)


def _in_call(body, name, x, mod, nw, w, consts, out_widths):
    bm, lm, _ = x.shape
    tm = min(ROW_TILE, lm)
    whole = lambda a: pl.BlockSpec(a.shape, lambda b, i: (0, 0))
    return pl.pallas_call(
        body,
        grid=(bm, lm // tm),
        in_specs=[
            pl.BlockSpec((1, tm, D_MODEL), lambda b, i: (b, i, 0)),
            pl.BlockSpec((1, 3, D_MODEL), lambda b, i: (b, 0, 0)),
            whole(nw),
            whole(w),
        ] + [whole(a) for a in consts],
        out_specs=[pl.BlockSpec((1, tm, n), lambda b, i: (b, i, 0)) for n in out_widths],
        out_shape=[jax.ShapeDtypeStruct((bm, lm, n), F32) for n in out_widths],
        compiler_params=_cparams("parallel", "parallel"),
        name=name,
    )(x, mod, nw, w, *consts)


def _gla_kernel(*refs, seq, has_state):
    if has_state:
        (qk_ref, v_ref, bc_ref, sg_ref, on_ref, s0_ref, y_ref, sn_ref,
         ut_scr, sf_scr, sb_scr, qd_scr, o_scr) = refs
    else:
        (qk_ref, v_ref, bc_ref, sg_ref, on_ref, y_ref, sn_ref,
         ut_scr, sf_scr, sb_scr, qd_scr, o_scr) = refs
        s0_ref = None
    C = GLA_CHUNK
    W = 2 * GLA_DK
    R = GLA_BLOCK
    CB = R // C
    nc = seq // C
    nb = seq // R
    ri = lax.broadcasted_iota(jnp.int32, (R, R), 0)
    ci = lax.broadcasted_iota(jnp.int32, (R, R), 1)
    same = (ri // C) == (ci // C)
    tril = same & (ri >= ci)
    triu = same & (ri <= ci)
    fwd = lax.broadcasted_iota(jnp.int32, (R, W), 1) < GLA_DK
    fwd_row = lax.broadcasted_iota(jnp.int32, (1, W), 1) < GLA_DK
    fwd_st = lax.broadcasted_iota(jnp.int32, (GLA_DV, W), 1) < GLA_DK
    row_chunk = lax.broadcasted_iota(jnp.int32, (R, W), 0) // C
    scale = GLA_DK ** -0.5

    def rows(c):
        return pl.ds(pl.multiple_of(c * C, C), C)

    def block_rows(j):
        return pl.ds(pl.multiple_of(j * R, R), R)

    def state_rows(c):
        return pl.ds(pl.multiple_of(c * GLA_DV, GLA_DV), GLA_DV)

    def intra(j, carry):
        r = block_rows(j)
        qk = qk_ref[0, r, :]
        bc = bc_ref[0, r, :]
        v = v_ref[0, r, :]
        sw = pltpu.roll(qk, GLA_DK, 1)
        q2 = jnp.where(fwd, qk, sw) * scale
        k2 = jnp.where(fwd, sw, qk)
        b_mid = bc[C // 2:C // 2 + 1, :]
        b_end = jnp.where(fwd_row, bc[C - 1:C, :], bc[0:1, :])
        for c in range(1, CB):
            o0 = c * C
            b_mid = jnp.where(row_chunk == c, bc[o0 + C // 2:o0 + C // 2 + 1, :], b_mid)
            b_end = jnp.where(row_chunk == c, jnp.where(fwd_row, bc[o0 + C - 1:o0 + C, :], bc[o0:o0 + 1, :]), b_end)
        qd = (q2 * jnp.exp(bc)).astype(BF16)
        qh = q2 * jnp.exp(bc - b_mid)
        kh = (k2 * jnp.exp(b_mid - bc)).astype(BF16)
        att = (jnp.where(tril, _dot_nt(jnp.where(fwd, qh, 0.0).astype(BF16), kh), 0.0)
               + jnp.where(triu, _dot_nt(jnp.where(fwd, 0.0, qh).astype(BF16), kh), 0.0))
        o = _dot(att.astype(BF16), v.astype(BF16))
        kd = k2 * jnp.exp(b_end - bc)
        kd_wide = jnp.concatenate([jnp.where(row_chunk == c, kd, 0.0) for c in range(CB)], axis=1)
        ut = _dot(v.T.astype(BF16), kd_wide.astype(BF16))
        qd_scr[r, :] = qd
        o_scr[r, :] = o
        ut_scr[j] = ut
        return carry

    lax.fori_loop(0, nb, intra, 0)

    def scan(j, st):
        jb = nb - 1 - j
        bcf = bc_ref[0, block_rows(j), :]
        bcb = bc_ref[0, block_rows(jb), :]
        utf = ut_scr[j]
        utb = ut_scr[jb]
        for c in range(CB):
            cb = CB - 1 - c
            stb = st.astype(BF16)
            sf_scr[j, :, c * W:(c + 1) * W] = stb
            sb_scr[jb, :, cb * W:(cb + 1) * W] = stb
            b_end = jnp.where(fwd_row, bcf[c * C + C - 1:c * C + C, :], bcb[cb * C:cb * C + 1, :])
            inc = jnp.where(fwd_st, utf[:, c * W:(c + 1) * W], utb[:, cb * W:(cb + 1) * W])
            st = st * jnp.exp(b_end) + inc
        return st

    if has_state:
        init = jnp.concatenate([s0_ref[0, 0, 0].T, s0_ref[0, 1, 0].T], axis=1)
    else:
        init = jnp.zeros((GLA_DV, W), F32)
    st = lax.fori_loop(0, nb, scan, init)
    sn_ref[0, 0, 0] = st[:, :GLA_DK].T
    sn_ref[0, 1, 0] = st[:, GLA_DK:].T

    fwd_wide = (lax.broadcasted_iota(jnp.int32, (GLA_DV, CB * W), 1) % W) < GLA_DK

    def finish(j, carry):
        r = block_rows(j)
        qd = qd_scr[r, :]
        zero = jnp.zeros((), BF16)
        qd_wide = jnp.concatenate([jnp.where(row_chunk == c, qd, zero) for c in range(CB)], axis=1)
        s_in = jnp.where(fwd_wide, sf_scr[j], sb_scr[j])
        o = o_scr[r, :] + _dot_nt(qd_wide, s_in)
        ms = jnp.mean(o * o, axis=-1, keepdims=True)
        y_ref[0, r, :] = o * lax.rsqrt(ms + EPS) * on_ref[...] * sg_ref[0, r, :]
        return carry

    lax.fori_loop(0, nb, finish, 0, unroll=2)


def _gla_call(qk, v, bc, sg, onorm, s0):
    bsz, seq, _ = qk.shape
    has_state = s0 is not None
    nb = seq // GLA_BLOCK
    wide = (GLA_BLOCK // GLA_CHUNK) * 2 * GLA_DK
    per_head =pl.BlockSpec((1, seq, 128), lambda b, h: (b, 0, h))
    in_specs = [per_head, per_head, per_head, per_head, pl.BlockSpec((1, GLA_DV), lambda b, h: (0, 0))]
    args = [qk, v, bc, sg, onorm]
    state_spec = pl.BlockSpec((1, 2, 1, GLA_DK, GLA_DV), lambda b, h: (b, 0, h, 0, 0))
    if has_state:
        in_specs.append(state_spec)
        args.append(s0)
    return pl.pallas_call(
        functools.partial(_gla_kernel, seq=seq, has_state=has_state),
        grid=(bsz, GLA_HEADS),
        in_specs=in_specs,
        out_specs=[per_head, state_spec],
        out_shape=[jax.ShapeDtypeStruct((bsz, seq, BRANCH), F32),
                   jax.ShapeDtypeStruct((bsz, 2, GLA_HEADS, GLA_DK, GLA_DV), F32)],
        scratch_shapes=[pltpu.VMEM((nb, GLA_DV, wide), F32),
                        pltpu.VMEM((nb, GLA_DV, wide), BF16),
                        pltpu.VMEM((nb, GLA_DV, wide), BF16),
                        pltpu.VMEM((seq, 2 * GLA_DK), BF16),
                        pltpu.VMEM((seq, GLA_DV), F32)],
        compiler_params=_cparams("parallel", "parallel"),
        name="gla",
    )(*args)


def _s5_prep_kernel(lr_ref, li_ref, ld_ref, br_ref, bi_ref, cr_ref, ci_ref,
                    m_ref, win_ref, wout_ref, ar_ref, ai_ref, *, nlev):
    cs = S5_CHUNK
    P2 = 2 * S5_STATE
    lam_re, lam_im = lr_ref[0], li_ref[0]
    dt = jnp.exp(ld_ref[0])
    er, ei = lam_re * dt, lam_im * dt
    mag = jnp.exp(er)
    nr, ni = mag * jnp.cos(ei) - 1.0, mag * jnp.sin(ei)
    den = lam_re * lam_re + lam_im * lam_im
    cr = (nr * lam_re + ni * lam_im) / den
    ci = (ni * lam_re - nr * lam_im) / den
    b_re, b_im = br_ref[0], bi_ref[0]
    tile = lambda a: jnp.tile(a, (cs, 1))
    bb_re = tile(cr * b_re - ci * b_im)
    bb_im = tile(cr * b_im + ci * b_re)
    c_re, c_im = tile(cr_ref[0]), tile(ci_ref[0])

    def power(k):
        m = jnp.exp(k * er)
        return m * jnp.cos(k * ei), m * jnp.sin(k * ei)

    def cmul(ar, ai, br, bi):
        return ar * br - ai * bi, ar * bi + ai * br

    s_row = (lax.broadcasted_iota(jnp.int32, (S5_ROW, P2), 0) // S5_GROUP).astype(F32)
    fwd = lax.broadcasted_iota(jnp.int32, (S5_ROW, P2), 1) < S5_STATE
    half = float(cs // 2)
    k_in = jnp.where(fwd, half - s_row, s_row - half)
    l_re, l_im = cmul(*power(k_in), bb_re, bb_im)
    r_re, r_im = cmul(*power(-k_in), c_re, c_im)
    lcat = jnp.concatenate([l_re, l_im], axis=1)
    rcat = jnp.concatenate([r_re, -r_im], axis=1)
    fwd2 = (lax.broadcasted_iota(jnp.int32, (S5_ROW, S5_ROW), 1) % P2) < S5_STATE
    nt = (((1,), (1,)), ((), ()))
    m_f = lax.dot_general(jnp.where(fwd2, lcat, 0.0), rcat, nt, precision=HIGHEST, preferred_element_type=F32)
    m_b = lax.dot_general(jnp.where(fwd2, 0.0, lcat), rcat, nt, precision=HIGHEST, preferred_element_type=F32)
    s_in = lax.broadcasted_iota(jnp.int32, (S5_ROW, S5_ROW), 0) // S5_GROUP
    s_out = lax.broadcasted_iota(jnp.int32, (S5_ROW, S5_ROW), 1) // S5_GROUP
    m_ref[0] = (jnp.where(s_in <= s_out, m_f, 0.0) + jnp.where(s_in >= s_out, m_b, 0.0)).astype(BF16)
    w_re, w_im = cmul(*power(jnp.where(fwd, (cs - 1.0) - s_row, s_row)), bb_re, bb_im)
    win_ref[0] = jnp.concatenate([w_re, w_im], axis=1).astype(BF16)
    o_re, o_im = cmul(*power(jnp.where(fwd, s_row + 1.0, cs - s_row)), c_re, c_im)
    wout_ref[0] = jnp.concatenate([o_re, -o_im], axis=1).astype(BF16)
    lev = lax.shift_left(jnp.int32(cs), lax.broadcasted_iota(jnp.int32, (nlev, P2), 0)).astype(F32)
    ar_ref[0], ai_ref[0] = power(lev)


def _s5_operators(lam_re, lam_im, log_dt, b_re, b_im, c_re, c_im, nlev):
    P2 = 2 * S5_STATE
    lanes = lambda a: a.transpose(1, 0, 2).reshape(S5_GROUPS, 1, P2)
    both = lambda a: jnp.concatenate([a, a], axis=-1)
    args = (lanes(lam_re), lanes(lam_im),
            lanes(jnp.broadcast_to(log_dt[..., None], lam_re.shape)),
            both(b_re.transpose(0, 2, 1)), both(b_im.transpose(0, 2, 1)), both(c_re), both(c_im))
    g3 = lambda g: (g, 0, 0)
    vec = pl.BlockSpec((1, 1, P2), g3)
    mat = pl.BlockSpec((1, S5_GROUP, P2), g3)
    op = pl.BlockSpec((1, S5_ROW, S5_ROW), g3)
    lev = pl.BlockSpec((1, nlev, P2), g3)
    return pl.pallas_call(
        functools.partial(_s5_prep_kernel, nlev=nlev),
        grid=(S5_GROUPS,),
        in_specs=[vec, vec, vec, mat, mat, mat, mat],
        out_specs=[op, op, op, lev, lev],
        out_shape=[jax.ShapeDtypeStruct((S5_GROUPS, S5_ROW, S5_ROW), BF16)] * 3
        + [jax.ShapeDtypeStruct((S5_GROUPS, nlev, P2), F32)] * 2,
        compiler_params=_cparams("parallel"),
        name="s5_prep",
    )(*args)


def _gelu_tanh(x):
    return 0.5 * x * (1.0 + jnp.tanh(math.sqrt(2.0 / math.pi) * (x + 0.044715 * (x * x * x))))


def _s5_kernel(*refs, nseq, nrow, nlev, has_state):
    if has_state:
        (u_ref, m_ref, win_ref, wout_ref, ar_ref, ai_ref, d_ref, h0_ref,
         z_ref, last_ref, t_scr, y_scr, tmp_r, tmp_i) = refs
    else:
        (u_ref, m_ref, win_ref, wout_ref, ar_ref, ai_ref, d_ref,
         z_ref, last_ref, t_scr, y_scr, tmp_r, tmp_i) = refs
    P2 = 2 * S5_STATE
    R = nseq * nrow
    GP = S5_SLAB_GROUPS
    TR = 16
    slot = lax.broadcasted_iota(jnp.int32, (TR, 128), 1) // S5_GROUP

    def token_rows(nt, s):
        return pl.ds(pl.multiple_of(nt * (TR * S5_CHUNK), TR * S5_CHUNK) + s, TR, stride=S5_CHUNK)

    def gather(nt, carry):
        for j in range(2):
            src = [u_ref[0, token_rows(nt, 8 * j + s8), :] for s8 in range(8)]
            for gl in range(GP):
                acc = None
                for s8 in range(8):
                    sh = ((s8 - gl) * S5_GROUP) % 128
                    piece = pltpu.roll(src[s8], sh, 1) if sh else src[s8]
                    acc = piece if acc is None else jnp.where(slot == s8, piece, acc)
                t_scr[gl, pl.ds(pl.multiple_of(nt * TR, TR), TR), j * 128:(j + 1) * 128] = acc.astype(BF16)
        return carry

    lax.fori_loop(0, R // TR, gather, 0)

    n = lax.broadcasted_iota(jnp.int32, (R, P2), 0) % nrow
    row = lax.broadcasted_iota(jnp.int32, (R, P2), 0)
    fwd = lax.broadcasted_iota(jnp.int32, (R, P2), 1) < S5_STATE
    fwd_b = lax.broadcasted_iota(jnp.int32, (nseq, P2), 1) < S5_STATE

    def shift(a, k):
        dn = pltpu.roll(a, k, 0)
        up = pltpu.roll(a, R - k, 0)
        return jnp.where(fwd, jnp.where(n >= k, dn, 0.0), jnp.where(n < nrow - k, up, 0.0))

    def ends(t):
        if nseq == 1:
            return jnp.where(fwd_b, t[nrow - 1:nrow, :], t[0:1, :])
        return jnp.where(fwd_b, t[pl.ds(nrow - 1, nseq, stride=nrow), :], t[pl.ds(0, nseq, stride=nrow), :])

    def per_group(gl, carry):
        ub = t_scr[gl]
        x = _dot(ub, win_ref[gl])
        xr, xi = x[:, :P2], x[:, P2:]
        er, ei = shift(xr, 1), shift(xi, 1)
        if has_state:
            h0 = h0_ref[0, gl]
            for b in range(nseq):
                first = row == jnp.where(fwd, b * nrow, b * nrow + nrow - 1)
                er = jnp.where(first, h0[b:b + 1, :P2], er)
                ei = jnp.where(first, h0[b:b + 1, P2:], ei)
        for j in range(nlev):
            k = 2 ** j
            ar = ar_ref[gl, j:j + 1, :]
            ai = ai_ref[gl, j:j + 1, :]
            sr, si = shift(er, k), shift(ei, k)
            er, ei = er + ar * sr - ai * si, ei + ar * si + ai * sr
        ecat = jnp.concatenate([er, ei], axis=1).astype(BF16)
        y_scr[gl] = _dot(ub, m_ref[gl]) + _dot_nt(ecat, wout_ref[gl])
        a1r = ar_ref[gl, 0:1, :]
        a1i = ai_ref[gl, 0:1, :]
        tmp_r[...] = a1r * er - a1i * ei + xr
        tmp_i[...] = a1r * ei + a1i * er + xi
        last_ref[0, gl, :, :P2] = ends(tmp_r)
        last_ref[0, gl, :, P2:] = ends(tmp_i)
        return carry

    lax.fori_loop(0, GP, per_group, 0)

    def scatter(nt, carry):
        for j in range(2):
            ys = [y_scr[gl, pl.ds(pl.multiple_of(nt * TR, TR), TR), j * 128:(j + 1) * 128] for gl in range(GP)]
            for s8 in range(8):
                acc = None
                for gl in range(GP):
                    sh = ((gl - s8) * S5_GROUP) % 128
                    piece = pltpu.roll(ys[gl], sh, 1) if sh else ys[gl]
                    acc = piece if acc is None else jnp.where(slot == gl, piece, acc)
                rows = token_rows(nt, 8 * j + s8)
                z_ref[0, rows, :] = _gelu_tanh(acc + d_ref[...] * u_ref[0, rows, :])
        return carry

    lax.fori_loop(0, R // TR, scatter, 0)


def _s5_call(u, ops, d, h0, seq):
    m, win, wout, ar, ai = ops
    bm, lm, _ = u.shape
    nrow = seq // S5_CHUNK
    nlev = int(math.log2(nrow))
    total = bm * lm // seq
    nseq = max(1, min(total, S5_BLOCK_ROWS // nrow))
    nblk = total // nseq
    R = nseq * nrow
    tok = R * S5_CHUNK
    GP = S5_SLAB_GROUPS
    ns = BRANCH // 128
    has_state = h0 is not None
    slab = pl.BlockSpec((1, tok, 128), lambda t, i: (i, 0, t))
    op = pl.BlockSpec((GP, S5_ROW, S5_ROW), lambda t, i: (t, 0, 0))
    lev = pl.BlockSpec((GP, ar.shape[1], 2 * S5_STATE), lambda t, i: (t, 0, 0))
    state = pl.BlockSpec((1, GP, nseq, 4 * S5_STATE), lambda t, i: (i, t, 0, 0))
    in_specs = [slab, op, op, op, lev, lev, pl.BlockSpec((1, 128), lambda t, i: (0, t))]
    args = [u.reshape(nblk, tok, BRANCH), m, win, wout, ar, ai, d]
    if has_state:
        in_specs.append(state)
        args.append(h0.reshape(nblk, nseq, S5_GROUPS, 4 * S5_STATE).transpose(0, 2, 1, 3))
    z, last = pl.pallas_call(
        functools.partial(_s5_kernel, nseq=nseq, nrow=nrow, nlev=nlev, has_state=has_state),
        grid=(ns, nblk),
        in_specs=in_specs,
        out_specs=[slab, state],
        out_shape=[jax.ShapeDtypeStruct((nblk, tok, BRANCH), F32),
                   jax.ShapeDtypeStruct((nblk, S5_GROUPS, nseq, 4 * S5_STATE), F32)],
        scratch_shapes=[pltpu.VMEM((GP, R, S5_ROW), BF16), pltpu.VMEM((GP, R, S5_ROW), F32),
                        pltpu.VMEM((R, 2 * S5_STATE), F32), pltpu.VMEM((R, 2 * S5_STATE), F32)],
        compiler_params=_cparams("parallel", "parallel"),
        name="s5",
    )(*args)
    last = last.transpose(0, 2, 1, 3).reshape(total, S5_GROUPS, 4 * S5_STATE)
    return z.reshape(bm, lm, BRANCH), last


def _seg_rmsnorm(x, w, nseg):
    seg = lax.broadcasted_iota(jnp.int32, x.shape, 1) // ATT_HD
    x2 = x * x
    rs = jnp.ones_like(x)
    for s in range(nseg):
        ms = jnp.sum(jnp.where(seg == s, x2, 0.0), axis=-1, keepdims=True) * (1.0 / ATT_HD)
        rs = jnp.where(seg == s, lax.rsqrt(ms + EPS), rs)
    return x * rs * w


def _rope(x, cos, sin):
    width = x.shape[1]
    lane = lax.broadcasted_iota(jnp.int32, x.shape, 1)
    partner = jnp.where((lane & 16) == 0, pltpu.roll(x, width - 16, 1), pltpu.roll(x, 16, 1))
    return x * cos + partner * sin


def _softmax_pv(scores, values, sink):
    m = sink
    for s in scores:
        m = jnp.maximum(m, jnp.max(s, axis=-1, keepdims=True))
    den = jnp.exp(sink - m)
    out = None
    for s, v in zip(scores, values):
        p = jnp.exp(s - m)
        den = den + jnp.sum(p, axis=-1, keepdims=True)
        pv = _dot(p.astype(BF16), v)
        out = pv if out is None else out + pv
    return out / den


def _att_ctx_kernel(q_ref, kv_ref, sa_ref, qn_ref, kn_ref, sink_ref, y_ref, kc_ref, vc_ref):
    q = _seg_rmsnorm(q_ref[0], qn_ref[...], ATT_GROUPS) * (ATT_HD ** -0.5)
    kv = _seg_rmsnorm(kv_ref[0], kn_ref[...], 1)
    k = kv[:, :ATT_HD]
    v = kv[:, ATT_HD:]
    kc_ref[0, 0] = k
    vc_ref[0, 0] = v
    kb = k.astype(BF16)
    vb = v.astype(BF16)
    outs = []
    for g in range(ATT_GROUPS):
        qg = q[:, g * ATT_HD:(g + 1) * ATT_HD].astype(BF16)
        s = _dot_nt(qg, kb)
        outs.append(_softmax_pv([s], [vb], sink_ref[0, g:g + 1, 0:1]))
    y_ref[0] = jnp.concatenate(outs, axis=1) * sa_ref[0]


def _att_ctx_call(q, kv, sa, qnw, knw, sink):
    bsz, seq, _ = q.shape
    W = ATT_GROUPS * ATT_HD
    per_kv = lambda b, k: (b, 0, k)
    cache_spec = pl.BlockSpec((1, 1, seq, ATT_HD), lambda b, k: (b, k, 0, 0))
    return pl.pallas_call(
        _att_ctx_kernel,
        grid=(bsz, ATT_KV_HEADS),
        in_specs=[
            pl.BlockSpec((1, seq, W), per_kv),
            pl.BlockSpec((1, seq, 2 * ATT_HD), per_kv),
            pl.BlockSpec((1, seq, W), per_kv),
            pl.BlockSpec((1, W), lambda b, k: (0, 0)),
            pl.BlockSpec((1, 2 * ATT_HD), lambda b, k: (0, 0)),
            pl.BlockSpec((1, 8, 128), lambda b, k: (k, 0, 0)),
        ],
        out_specs=[pl.BlockSpec((1, seq, W), per_kv), cache_spec, cache_spec],
        out_shape=[jax.ShapeDtypeStruct((bsz, seq, BRANCH), F32),
                   jax.ShapeDtypeStruct((bsz, ATT_KV_HEADS, seq, ATT_HD), F32),
                   jax.ShapeDtypeStruct((bsz, ATT_KV_HEADS, seq, ATT_HD), F32)],
        compiler_params=_cparams("parallel", "parallel"),
        name="att_ctx",
    )(q, kv, sa, qnw, knw, sink)


def _att_lat_kernel(q_ref, kv_ref, sa_ref, qn_ref, kn_ref, sink_ref, cq_ref, sq_ref, ck_ref, sk_ref,
                    kc_ref, vc_ref, y_ref, kr_scr, *, seq):
    nb = seq // QBLK
    KW = 3 * QBLK

    def k_body(i, carry):
        r = pl.ds(pl.multiple_of(i * QBLK, QBLK), QBLK)
        kv = _seg_rmsnorm(kv_ref[0, r, :], kn_ref[...], 1)
        kr_scr[r, :] = _rope(kv, ck_ref[r, :], sk_ref[r, :]).astype(BF16)
        return carry

    lax.fori_loop(0, nb, k_body, 0)
    kctx = kc_ref[0, 0, 0].astype(BF16)
    vctx = vc_ref[0, 0, 0].astype(BF16)
    qi = lax.broadcasted_iota(jnp.int32, (QBLK, KW), 0)
    kj = lax.broadcasted_iota(jnp.int32, (QBLK, KW), 1)

    def q_body(i, carry):
        r = pl.ds(pl.multiple_of(i * QBLK, QBLK), QBLK)
        q = _seg_rmsnorm(q_ref[0, r, :], qn_ref[...], ATT_GROUPS)
        q = _rope(q, cq_ref[r, :], sq_ref[r, :]) * (ATT_HD ** -0.5)
        ws = pl.multiple_of(jnp.clip(i * QBLK - QBLK, 0, seq - KW), QBLK)
        kvw = kr_scr[pl.ds(ws, KW), :]
        kw = kvw[:, :ATT_HD]
        vw = kvw[:, ATT_HD:]
        band = jnp.abs((i * QBLK + qi) - (ws + kj)) <= WINDOW
        outs = []
        for g in range(ATT_GROUPS):
            qg = q[:, g * ATT_HD:(g + 1) * ATT_HD].astype(BF16)
            s_win = jnp.where(band, _dot_nt(qg, kw), NEG)
            s_ctx = _dot_nt(qg, kctx)
            outs.append(_softmax_pv([s_win, s_ctx], [vw, vctx], sink_ref[0, g:g + 1, 0:1]))
        y_ref[0, r, :] = jnp.concatenate(outs, axis=1) * sa_ref[0, r, :]
        return carry

    lax.fori_loop(0, nb, q_body, 0)


def _rope_tables(seq, heads, pad_heads):
    pos = jnp.arange(seq)
    row = (pos // GRID_W).astype(F32)[:, None]
    col = (pos % GRID_W).astype(F32)[:, None]
    nf = ATT_HD // 4
    freq = ROPE_BASE ** (-jnp.arange(nf, dtype=F32) / nf)
    ar, ac = row * freq, col * freq
    cos = jnp.concatenate([jnp.cos(ar), jnp.cos(ar), jnp.cos(ac), jnp.cos(ac)], axis=1)
    sin = jnp.concatenate([-jnp.sin(ar), jnp.sin(ar), -jnp.sin(ac), jnp.sin(ac)], axis=1)
    cos = jnp.concatenate([jnp.tile(cos, (1, heads)), jnp.ones((seq, pad_heads * ATT_HD), F32)], axis=1)
    sin = jnp.concatenate([jnp.tile(sin, (1, heads)), jnp.zeros((seq, pad_heads * ATT_HD), F32)], axis=1)
    return cos, sin


def _att_lat_call(q, kv, sa, qnw, knw, sink, cache_k, cache_v, e):
    bsz, seq, _ = q.shape
    past = cache_k.shape[3]
    W = ATT_GROUPS * ATT_HD
    cq, sq = _rope_tables(seq, ATT_GROUPS, 0)
    ck, sk = _rope_tables(seq, 1, 1)
    per_kv = lambda b, k: (b, 0, k)
    const = lambda b, k: (0, 0)
    cache_spec = pl.BlockSpec((1, 1, 1, past, ATT_HD), lambda b, k: (b, e, k, 0, 0))
    return pl.pallas_call(
        functools.partial(_att_lat_kernel, seq=seq),
        grid=(bsz, ATT_KV_HEADS),
        in_specs=[
            pl.BlockSpec((1, seq, W), per_kv),
            pl.BlockSpec((1, seq, 2 * ATT_HD), per_kv),
            pl.BlockSpec((1, seq, W), per_kv),
            pl.BlockSpec((1, W), const),
            pl.BlockSpec((1, 2 * ATT_HD), const),
            pl.BlockSpec((1, 8, 128), lambda b, k: (k, 0, 0)),
            pl.BlockSpec((seq, W), const),
            pl.BlockSpec((seq, W), const),
            pl.BlockSpec((seq, 2 * ATT_HD), const),
            pl.BlockSpec((seq, 2 * ATT_HD), const),
            cache_spec,
            cache_spec,
        ],
        out_specs=pl.BlockSpec((1, seq, W), per_kv),
        out_shape=jax.ShapeDtypeStruct((bsz, seq, BRANCH), F32),
        scratch_shapes=[pltpu.VMEM((seq, 2 * ATT_HD), BF16)],
        compiler_params=_cparams("parallel", "parallel"),
        name="att_lat",
    )(q, kv, sa, qnw, knw, sink, cq, sq, ck, sk, cache_k, cache_v)


def _even_out_kernel(x_ref, mod_ref, yg_ref, z_ref, ss_ref, wg_ref, bg_ref, wo_ref, o_ref):
    z = z_ref[0]
    glu = z * jax.nn.sigmoid(_dot(z.astype(BF16), wg_ref[...]) + bg_ref[...])
    ys = (glu * ss_ref[0]).astype(BF16)
    yg = yg_ref[0].astype(BF16)
    out = _dot(yg, wo_ref[:BRANCH, :]) + _dot(ys, wo_ref[BRANCH:, :])
    o_ref[0] = x_ref[0] + mod_ref[0, 2:3, :] * out


def _even_out_call(x, mod, yg, z, ss, wglu, bglu, wout):
    bm, lm, _ = x.shape
    tm = min(ROW_TILE, lm)
    tile = lambda n: pl.BlockSpec((1, tm, n), lambda b, i: (b, i, 0))
    const = lambda b, i: (0, 0)
    return pl.pallas_call(
        _even_out_kernel,
        grid=(bm, lm // tm),
        in_specs=[
            tile(D_MODEL),
            pl.BlockSpec((1, 3, D_MODEL), lambda b, i: (b, 0, 0)),
            tile(BRANCH), tile(BRANCH), tile(BRANCH),
            pl.BlockSpec((BRANCH, BRANCH), const),
            pl.BlockSpec((1, BRANCH), const),
            pl.BlockSpec((D_MODEL, D_MODEL), const),
        ],
        out_specs=tile(D_MODEL),
        out_shape=jax.ShapeDtypeStruct((bm, lm, D_MODEL), F32),
        compiler_params=_cparams("parallel", "parallel"),
        name="even_out",
    )(x, mod, yg, z, ss, wglu, bglu, wout)


def _odd_out_kernel(x_ref, mod_ref, ya_ref, p_ref, pprev_ref, pnext_ref, bgz_ref, cw_ref, cb_ref, wo_ref,
                    o_ref, *, seq, tm):
    p = p_ref[0]
    pos = (pl.program_id(1) * tm + lax.broadcasted_iota(jnp.int32, (tm, BRANCH), 0)) % seq
    rowi = lax.broadcasted_iota(jnp.int32, (tm, BRANCH), 0)
    prev = jnp.where(rowi == 0, pprev_ref[0, 7:8, :], pltpu.roll(p, 1, 0))
    nxt = jnp.where(rowi == tm - 1, pnext_ref[0, 0:1, :], pltpu.roll(p, tm - 1, 0))
    prev = jnp.where(pos == 0, 0.0, prev)
    nxt = jnp.where(pos == seq - 1, 0.0, nxt)
    conv = prev * cw_ref[0:1, :] + p * cw_ref[1:2, :] + nxt * cw_ref[2:3, :] + cb_ref[...]
    yc = (bgz_ref[0] * conv).astype(BF16)
    ya = ya_ref[0].astype(BF16)
    out = _dot(ya, wo_ref[:BRANCH, :]) + _dot(yc, wo_ref[BRANCH:, :])
    o_ref[0] = x_ref[0] + mod_ref[0, 2:3, :] * out


def _odd_out_call(x, mod, ya, p, bgz, convw, convb, wout, seq):
    bm, lm, _ = x.shape
    tm = min(ROW_TILE, lm)
    nt8 = lm // 8
    tile = lambda n: pl.BlockSpec((1, tm, n), lambda b, i: (b, i, 0))
    const = lambda b, i: (0, 0)
    prev_spec = pl.BlockSpec((1, 8, BRANCH), lambda b, i: (b, jnp.maximum(i * (tm // 8) - 1, 0), 0))
    next_spec = pl.BlockSpec((1, 8, BRANCH), lambda b, i: (b, jnp.minimum((i + 1) * (tm // 8), nt8 - 1), 0))
    return pl.pallas_call(
        functools.partial(_odd_out_kernel, seq=seq, tm=tm),
        grid=(bm, lm // tm),
        in_specs=[
            tile(D_MODEL),
            pl.BlockSpec((1, 3, D_MODEL), lambda b, i: (b, 0, 0)),
            tile(BRANCH), tile(BRANCH), prev_spec, next_spec, tile(BRANCH),
            pl.BlockSpec((CONV_W, BRANCH), const),
            pl.BlockSpec((1, BRANCH), const),
            pl.BlockSpec((D_MODEL, D_MODEL), const),
        ],
        out_specs=tile(D_MODEL),
        out_shape=jax.ShapeDtypeStruct((bm, lm, D_MODEL), F32),
        compiler_params=_cparams("parallel", "parallel"),
        name="odd_out",
    )(x, mod, ya, p, p, p, bgz, convw, convb, wout)


def _even_in_weight(w):
    dk = GLA_HEADS * GLA_DK
    q, k = w[:, :dk], w[:, dk:2 * dk]
    v = w[:, 2 * dk:2 * dk + BRANCH]
    o = 2 * dk + BRANCH
    lr = w[:, o:o + 2 * GLA_RANK]
    rest = w[:, o + 2 * GLA_RANK:]
    qk = jnp.concatenate([jnp.concatenate([q[:, h * GLA_DK:(h + 1) * GLA_DK], k[:, h * GLA_DK:(h + 1) * GLA_DK]],
                                          axis=1) for h in range(GLA_HEADS)], axis=1)
    lr = jnp.pad(lr, ((0, 0), (0, 128 - 2 * GLA_RANK)))
    return jnp.concatenate([qk, v, lr, rest], axis=1).astype(BF16)


def _odd_in_weight(w):
    q = w[:, :BRANCH]
    k = w[:, BRANCH:BRANCH + ATT_KV_HEADS * ATT_HD]
    v = w[:, BRANCH + ATT_KV_HEADS * ATT_HD:BRANCH + 2 * ATT_KV_HEADS * ATT_HD]
    rest = w[:, BRANCH + 2 * ATT_KV_HEADS * ATT_HD:]
    kv = jnp.concatenate([jnp.concatenate([k[:, h * ATT_HD:(h + 1) * ATT_HD], v[:, h * ATT_HD:(h + 1) * ATT_HD]],
                                          axis=1) for h in range(ATT_KV_HEADS)], axis=1)
    return jnp.concatenate([q, kv, rest], axis=1).astype(BF16)


def _gate_weights(w2, b2):
    zf = jnp.zeros((GLA_RANK, GLA_DK), F32)
    cols, bias = [], []
    for h in range(GLA_HEADS):
        sl = slice(h * GLA_DK, (h + 1) * GLA_DK)
        cols.append(jnp.concatenate([jnp.concatenate([w2[0][:, sl], zf], axis=1),
                                     jnp.concatenate([zf, w2[1][:, sl]], axis=1)], axis=0))
        bias += [b2[0][sl], b2[1][sl]]
    w = jnp.pad(jnp.concatenate(cols, axis=1), ((0, 128 - 2 * GLA_RANK), (0, 0)))
    w_hi = w.astype(BF16)
    w_lo = (w - w_hi.astype(F32)).astype(BF16)
    return w_hi, w_lo, jnp.concatenate(bias)[None, :]


def _even_layer(x, mod, nw, w_in, seq, gla_w, s5_ops, dgm, wglu, bglu, wout, onorm, gla_s0, s5_h0):
    bm, lm, _ = x.shape
    nseq = bm * lm // seq
    qk, v, bc, sg, u, ss = _in_call(_even_in_kernel, "even_in", x, mod, nw, w_in, gla_w,
                                    (BRANCH,) * 6)
    per_seq = lambda a: a.reshape(nseq, seq, a.shape[-1])
    yg, s_new = _gla_call(per_seq(qk), per_seq(v), per_seq(bc), per_seq(sg), onorm, gla_s0)
    z, last = _s5_call(u, s5_ops, dgm, s5_h0, seq)
    x_new = _even_out_call(x, mod, yg.reshape(bm, lm, BRANCH), z, ss, wglu, bglu, wout)
    return x_new, s_new, last


def _odd_layer(x, mod, nw, w_in, seq, qnw, knw, sink, convw, convb, wout, cache_k, cache_v, e):
    bm, lm, _ = x.shape
    nseq = bm * lm // seq
    q, kv, sa, p, bgz = _in_call(_odd_in_kernel, "odd_in", x, mod, nw, w_in, (),
                                 (BRANCH, 2 * ATT_KV_HEADS * ATT_HD, BRANCH, BRANCH, BRANCH))
    per_seq = lambda a: a.reshape(nseq, seq, a.shape[-1])
    if cache_k is None:
        ya, kc, vc = _att_ctx_call(per_seq(q), per_seq(kv), per_seq(sa), qnw, knw, sink)
    else:
        ya = _att_lat_call(per_seq(q), per_seq(kv), per_seq(sa), qnw, knw, sink, cache_k, cache_v, e)
        kc = vc = None
    x_new = _odd_out_call(x, mod, ya.reshape(bm, lm, BRANCH), p, bgz, convw, convb, wout, seq)
    return x_new, kc, vc


def kernel(x_prompt, x_sample, c, state_gla, state_s5_re, state_s5_im, cache_k, cache_v, c_ctx, norm_w, w_ada, b_ada, w_in_e, w_out_e, gla_w2, gla_b2, gla_onorm, s5_lam_re, s5_lam_im, s5_log_dt, s5_b_re, s5_b_im, s5_c_re, s5_c_im, s5_d, s5_w_glu, s5_b_glu, w_in_o, w_out_o, q_norm_w, k_norm_w, sink, conv_w, conv_b):
    depth = norm_w.shape[0]
    bp, lp, _ = x_prompt.shape
    bs, ls, _ = x_sample.shape

    rows = 8 * ((1 + bs + 7) // 8)
    cs = jnp.zeros((rows, D_MODEL), F32).at[0].set(c_ctx).at[1:1 + bs].set(c)
    mods = _ada_call(cs, w_ada, b_ada)

    xp = x_prompt.reshape(1, bp * lp, D_MODEL)
    xs = x_sample
    new_gla, new_re, new_im, new_k, new_v = [], [], [], [], []
    for l in range(depth):
        e = l // 2
        mod_p = mods[l, 0:1].reshape(1, 3, D_MODEL)
        mod_s = mods[l, 1:1 + bs].reshape(bs, 3, D_MODEL)
        nw = norm_w[l][None, :]
        if l % 2 == 0:
            w_in = _even_in_weight(w_in_e[e])
            wout = w_out_e[e].astype(BF16)
            gla_w = _gate_weights(gla_w2[e], gla_b2[e])
            onorm = gla_onorm[e][None, :]
            dgm = s5_d[e][None, :]
            wglu = s5_w_glu[e].astype(BF16)
            bglu = s5_b_glu[e][None, :]
            s5_ops = _s5_operators(s5_lam_re[e], s5_lam_im[e], s5_log_dt[e], s5_b_re[e], s5_b_im[e],
                                   s5_c_re[e], s5_c_im[e], nlev=int(math.log2(max(lp, ls) // S5_CHUNK)))
            h0 = jnp.concatenate([state_s5_re[:, e], state_s5_im[:, e]], axis=1)
            h0 = h0.transpose(0, 2, 1, 3).reshape(bs, S5_GROUPS, 4 * S5_STATE)
            xp, sg, last = _even_layer(xp, mod_p, nw, w_in, lp, gla_w, s5_ops, dgm, wglu, bglu, wout, onorm,
                                       None, None)
            xs, _, _ = _even_layer(xs, mod_s, nw, w_in, ls, gla_w, s5_ops, dgm, wglu, bglu, wout, onorm,
                                   state_gla[:, e], h0)
            new_gla.append(sg)
            last = last.reshape(bp, S5_GROUPS, 2, 2, S5_STATE).transpose(2, 0, 3, 1, 4)
            new_re.append(last[0])
            new_im.append(last[1])
        else:
            w_in = _odd_in_weight(w_in_o[e])
            wout = w_out_o[e].astype(BF16)
            qnw = jnp.tile(q_norm_w[e], ATT_GROUPS)[None, :]
            knw = jnp.concatenate([k_norm_w[e], jnp.ones((ATT_HD,), F32)])[None, :]
            sk = jnp.broadcast_to(sink[e].reshape(ATT_KV_HEADS, ATT_GROUPS, 1), (ATT_KV_HEADS, ATT_GROUPS, 128))
            sk = jnp.concatenate([sk, jnp.zeros((ATT_KV_HEADS, 8 - ATT_GROUPS, 128), F32)], axis=1)
            cb = conv_b[e][None, :]
            xp, kc, vc = _odd_layer(xp, mod_p, nw, w_in, lp, qnw, knw, sk, conv_w[e], cb, wout, None, None, e)
            xs, _, _ = _odd_layer(xs, mod_s, nw, w_in, ls, qnw, knw, sk, conv_w[e], cb, wout, cache_k, cache_v, e)
            new_k.append(kc)
            new_v.append(vc)
    return (xp.reshape(bp, lp, D_MODEL), xs,
            jnp.stack(new_gla, axis=1), jnp.stack(new_re, axis=1), jnp.stack(new_im, axis=1),
            jnp.stack(new_k, axis=1), jnp.stack(new_v, axis=1))
```

```python
import functools
import math

import jax
import jax.numpy as jnp
from jax import lax
from jax.experimental import pallas as pl
from jax.experimental.pallas import tpu as pltpu

F32 = jnp.float32
BF16 = jnp.bfloat16
HIGHEST = lax.Precision.HIGHEST

D_MODEL = 1024
BRANCH = D_MODEL // 2
GLA_HEADS = 4
GLA_DK = 64
GLA_DV = 128
GLA_RANK = 16
GLA_GATE_NORM = 16.0
GLA_CHUNK = 64
GLA_BLOCK = 256
S5_GROUP = 16
S5_GROUPS = BRANCH // S5_GROUP
S5_STATE = 64
S5_CHUNK = 16
S5_ROW = S5_CHUNK * S5_GROUP
S5_SLAB_GROUPS = 128 // S5_GROUP
S5_BLOCK_ROWS = 256
ATT_HEADS = 8
ATT_KV_HEADS = 2
ATT_GROUPS = ATT_HEADS // ATT_KV_HEADS
ATT_HD = 64
WINDOW = 128
QBLK = 128
GRID_W = 64
ROPE_BASE = 10000.0
CONV_W = 3
EPS = 1e-6
NEG = -1e30

ROW_TILE = 512
VMEM_LIMIT = 48 * 1024 * 1024


def _cparams(*sem):
    return pltpu.CompilerParams(dimension_semantics=sem, vmem_limit_bytes=VMEM_LIMIT)


def _silu(x):
    return x * jax.nn.sigmoid(x)


def _dot(a, b):
    return jnp.dot(a, b, preferred_element_type=F32)


def _dot_nt(a, b):
    return lax.dot_general(a, b, (((1,), (1,)), ((), ())), preferred_element_type=F32)


def _ada_kernel(c_ref, w_ref, b_ref, o_ref):
    c = c_ref[...]
    o_ref[0] = jnp.dot(_silu(c), w_ref[0], precision=HIGHEST, preferred_element_type=F32) + b_ref[0]


def _ada_call(cs, w_ada, b_ada):
    depth = w_ada.shape[0]
    rows = cs.shape[0]
    nt = 3
    return pl.pallas_call(
        _ada_kernel,
        grid=(depth, nt),
        in_specs=[
            pl.BlockSpec((rows, D_MODEL), lambda l, j: (0, 0)),
            pl.BlockSpec((1, D_MODEL, D_MODEL), lambda l, j: (l, 0, j)),
            pl.BlockSpec((1, 1, D_MODEL), lambda l, j: (l, 0, j)),
        ],
        out_specs=pl.BlockSpec((1, rows, D_MODEL), lambda l, j: (l, 0, j)),
        out_shape=jax.ShapeDtypeStruct((depth, rows, 3 * D_MODEL), F32),
        compiler_params=_cparams("arbitrary", "arbitrary"),
        name="adaln",
    )(cs, w_ada, b_ada.reshape(depth, 1, 3 * D_MODEL))


def _modulated(x_ref, mod_ref, nw_ref):
    x = x_ref[0]
    ms = jnp.mean(x * x, axis=-1, keepdims=True)
    y = x * lax.rsqrt(ms + EPS) * nw_ref[...]
    shift = mod_ref[0, 0:1, :]
    scale = mod_ref[0, 1:2, :]
    return (y * (1.0 + scale) + shift).astype(BF16)


E_COLS = (0, 512, 1024, 1152, 1664, 2176, 2688)


def _log_sigmoid(x):
    return jnp.minimum(x, 0.0) - jnp.log(1.0 + jnp.exp(-jnp.abs(x)))


def _chunk_cumsum(g):
    rows = g.shape[0]
    pos = lax.broadcasted_iota(jnp.int32, g.shape, 0) % GLA_CHUNK
    fwd = (lax.broadcasted_iota(jnp.int32, g.shape, 1) % (2 * GLA_DK)) < GLA_DK
    k = 1
    while k < GLA_CHUNK:
        dn = jnp.where(pos >= k, pltpu.roll(g, k, 0), 0.0)
        up = jnp.where(pos < GLA_CHUNK - k, pltpu.roll(g, rows - k, 0), 0.0)
        g = g + jnp.where(fwd, dn, up)
        k *= 2
    return g


def _even_in_kernel(x_ref, mod_ref, nw_ref, w_ref, w2h_ref, w2l_ref, b2_ref,
                    qk_ref, v_ref, bc_ref, sg_ref, u_ref, ss_ref):
    h = _modulated(x_ref, mod_ref, nw_ref)
    c = E_COLS
    qk_ref[0] = _dot(h, w_ref[:, c[0]:c[1]])
    v_ref[0] = _dot(h, w_ref[:, c[1]:c[2]])
    lr = _dot(h, w_ref[:, c[2]:c[3]])
    lr_hi = lr.astype(BF16)
    lr_lo = (lr - lr_hi.astype(F32)).astype(BF16)
    pre = _dot(lr_hi, w2h_ref[...]) + (_dot(lr_lo, w2h_ref[...]) + _dot(lr_hi, w2l_ref[...])) + b2_ref[...]
    bc_ref[0] = _chunk_cumsum(_log_sigmoid(pre) * (1.0 / GLA_GATE_NORM))
    sg_ref[0] = _silu(_dot(h, w_ref[:, c[3]:c[4]]))
    u_ref[0] = _dot(h, w_ref[:, c[4]:c[5]])
    ss_ref[0] = _silu(_dot(h, w_ref[:, c[5]:c[6]]))


O_COLS = (0, 512, 768, 1280, 1792, 2304, 2816, 3328)


def _odd_in_kernel(x_ref, mod_ref, nw_ref, w_ref, q_ref, kv_ref, sa_ref, p_ref, bgz_ref):
    h = _modulated(x_ref, mod_ref, nw_ref)
    c = O_COLS
    q_ref[0] = _dot(h, w_ref[:, c[0]:c[1]])
    kv_ref[0] = _dot(h, w_ref[:, c[1]:c[2]])
    sa_ref[0] = _silu(_dot(h, w_ref[:, c[2]:c[3]]))
    xc = _dot(h, w_ref[:, c[3]:c[4]])
    bg = _dot(h, w_ref[:, c[4]:c[5]])
    cg = _dot(h, w_ref[:, c[5]:c[6]])
    zc = _dot(h, w_ref[:, c[6]:c[7]])
    p_ref[0] = cg * xc
    bgz_ref[0] = bg * _silu(zc)


def _in_call(body, name, x, mod, nw, w, consts, out_widths):
    bm, lm, _ = x.shape
    tm = min(ROW_TILE, lm)
    whole = lambda a: pl.BlockSpec(a.shape, lambda b, i: (0, 0))
    return pl.pallas_call(
        body,
        grid=(bm, lm // tm),
        in_specs=[
            pl.BlockSpec((1, tm, D_MODEL), lambda b, i: (b, i, 0)),
            pl.BlockSpec((1, 3, D_MODEL), lambda b, i: (b, 0, 0)),
            whole(nw),
            whole(w),
        ] + [whole(a) for a in consts],
        out_specs=[pl.BlockSpec((1, tm, n), lambda b, i: (b, i, 0)) for n in out_widths],
        out_shape=[jax.ShapeDtypeStruct((bm, lm, n), F32) for n in out_widths],
        compiler_params=_cparams("parallel", "parallel"),
        name=name,
    )(x, mod, nw, w, *consts)


def _gla_kernel(*refs, seq, has_state):
    if has_state:
        (qk_ref, v_ref, bc_ref, sg_ref, on_ref, s0_ref, y_ref, sn_ref,
         ut_scr, sf_scr, sb_scr, qd_scr, o_scr) = refs
    else:
        (qk_ref, v_ref, bc_ref, sg_ref, on_ref, y_ref, sn_ref,
         ut_scr, sf_scr, sb_scr, qd_scr, o_scr) = refs
        s0_ref = None
    C = GLA_CHUNK
    W = 2 * GLA_DK
    R = GLA_BLOCK
    CB = R // C
    nc = seq // C
    nb = seq // R
    ri = lax.broadcasted_iota(jnp.int32, (R, R), 0)
    ci = lax.broadcasted_iota(jnp.int32, (R, R), 1)
    same = (ri // C) == (ci // C)
    tril = same & (ri >= ci)
    triu = same & (ri <= ci)
    fwd = lax.broadcasted_iota(jnp.int32, (R, W), 1) < GLA_DK
    fwd_row = lax.broadcasted_iota(jnp.int32, (1, W), 1) < GLA_DK
    fwd_st = lax.broadcasted_iota(jnp.int32, (GLA_DV, W), 1) < GLA_DK
    row_chunk = lax.broadcasted_iota(jnp.int32, (R, W), 0) // C
    scale = GLA_DK ** -0.5

    def rows(c):
        return pl.ds(pl.multiple_of(c * C, C), C)

    def block_rows(j):
        return pl.ds(pl.multiple_of(j * R, R), R)

    def state_rows(c):
        return pl.ds(pl.multiple_of(c * GLA_DV, GLA_DV), GLA_DV)

    def intra(j, carry):
        r = block_rows(j)
        qk = qk_ref[0, r, :]
        bc = bc_ref[0, r, :]
        v = v_ref[0, r, :]
        sw = pltpu.roll(qk, GLA_DK, 1)
        q2 = jnp.where(fwd, qk, sw) * scale
        k2 = jnp.where(fwd, sw, qk)
        b_mid = bc[C // 2:C // 2 + 1, :]
        b_end = jnp.where(fwd_row, bc[C - 1:C, :], bc[0:1, :])
        for c in range(1, CB):
            o0 = c * C
            b_mid = jnp.where(row_chunk == c, bc[o0 + C // 2:o0 + C // 2 + 1, :], b_mid)
            b_end = jnp.where(row_chunk == c, jnp.where(fwd_row, bc[o0 + C - 1:o0 + C, :], bc[o0:o0 + 1, :]), b_end)
        qd = (q2 * jnp.exp(bc)).astype(BF16)
        qh = q2 * jnp.exp(bc - b_mid)
        kh = (k2 * jnp.exp(b_mid - bc)).astype(BF16)
        att = (jnp.where(tril, _dot_nt(jnp.where(fwd, qh, 0.0).astype(BF16), kh), 0.0)
               + jnp.where(triu, _dot_nt(jnp.where(fwd, 0.0, qh).astype(BF16), kh), 0.0))
        o = _dot(att.astype(BF16), v.astype(BF16))
        kd = k2 * jnp.exp(b_end - bc)
        kd_wide = jnp.concatenate([jnp.where(row_chunk == c, kd, 0.0) for c in range(CB)], axis=1)
        ut = _dot(v.T.astype(BF16), kd_wide.astype(BF16))
        qd_scr[r, :] = qd
        o_scr[r, :] = o
        ut_scr[j] = ut
        return carry

    lax.fori_loop(0, nb, intra, 0)

    def scan(j, st):
        jb = nb - 1 - j
        bcf = bc_ref[0, block_rows(j), :]
        bcb = bc_ref[0, block_rows(jb), :]
        utf = ut_scr[j]
        utb = ut_scr[jb]
        for c in range(CB):
            cb = CB - 1 - c
            stb = st.astype(BF16)
            sf_scr[j, :, c * W:(c + 1) * W] = stb
            sb_scr[jb, :, cb * W:(cb + 1) * W] = stb
            b_end = jnp.where(fwd_row, bcf[c * C + C - 1:c * C + C, :], bcb[cb * C:cb * C + 1, :])
            inc = jnp.where(fwd_st, utf[:, c * W:(c + 1) * W], utb[:, cb * W:(cb + 1) * W])
            st = st * jnp.exp(b_end) + inc
        return st

    if has_state:
        init = jnp.concatenate([s0_ref[0, 0, 0].T, s0_ref[0, 1, 0].T], axis=1)
    else:
        init = jnp.zeros((GLA_DV, W), F32)
    st = lax.fori_loop(0, nb, scan, init)
    sn_ref[0, 0, 0] = st[:, :GLA_DK].T
    sn_ref[0, 1, 0] = st[:, GLA_DK:].T

    fwd_wide = (lax.broadcasted_iota(jnp.int32, (GLA_DV, CB * W), 1) % W) < GLA_DK

    def finish(j, carry):
        r = block_rows(j)
        qd = qd_scr[r, :]
        zero = jnp.zeros((), BF16)
        qd_wide = jnp.concatenate([jnp.where(row_chunk == c, qd, zero) for c in range(CB)], axis=1)
        s_in = jnp.where(fwd_wide, sf_scr[j], sb_scr[j])
        o = o_scr[r, :] + _dot_nt(qd_wide, s_in)
        ms = jnp.mean(o * o, axis=-1, keepdims=True)
        y_ref[0, r, :] = o * lax.rsqrt(ms + EPS) * on_ref[...] * sg_ref[0, r, :]
        return carry

    lax.fori_loop(0, nb, finish, 0, unroll=2)


def _gla_call(qk, v, bc, sg, onorm, s0):
    bsz, seq, _ = qk.shape
    has_state = s0 is not None
    nb = seq // GLA_BLOCK
    wide = (GLA_BLOCK // GLA_CHUNK) * 2 * GLA_DK
    per_head =pl.BlockSpec((1, seq, 128), lambda b, h: (b, 0, h))
    in_specs = [per_head, per_head, per_head, per_head, pl.BlockSpec((1, GLA_DV), lambda b, h: (0, 0))]
    args = [qk, v, bc, sg, onorm]
    state_spec = pl.BlockSpec((1, 2, 1, GLA_DK, GLA_DV), lambda b, h: (b, 0, h, 0, 0))
    if has_state:
        in_specs.append(state_spec)
        args.append(s0)
    return pl.pallas_call(
        functools.partial(_gla_kernel, seq=seq, has_state=has_state),
        grid=(bsz, GLA_HEADS),
        in_specs=in_specs,
        out_specs=[per_head, state_spec],
        out_shape=[jax.ShapeDtypeStruct((bsz, seq, BRANCH), F32),
                   jax.ShapeDtypeStruct((bsz, 2, GLA_HEADS, GLA_DK, GLA_DV), F32)],
        scratch_shapes=[pltpu.VMEM((nb, GLA_DV, wide), F32),
                        pltpu.VMEM((nb, GLA_DV, wide), BF16),
                        pltpu.VMEM((nb, GLA_DV, wide), BF16),
                        pltpu.VMEM((seq, 2 * GLA_DK), BF16),
                        pltpu.VMEM((seq, GLA_DV), F32)],
        compiler_params=_cparams("parallel", "parallel"),
        name="gla",
    )(*args)


def _s5_prep_kernel(lr_ref, li_ref, ld_ref, br_ref, bi_ref, cr_ref, ci_ref,
                    m_ref, win_ref, wout_ref, ar_ref, ai_ref, *, nlev):
    cs = S5_CHUNK
    P2 = 2 * S5_STATE
    lam_re, lam_im = lr_ref[0], li_ref[0]
    dt = jnp.exp(ld_ref[0])
    er, ei = lam_re * dt, lam_im * dt
    mag = jnp.exp(er)
    nr, ni = mag * jnp.cos(ei) - 1.0, mag * jnp.sin(ei)
    den = lam_re * lam_re + lam_im * lam_im
    cr = (nr * lam_re + ni * lam_im) / den
    ci = (ni * lam_re - nr * lam_im) / den
    b_re, b_im = br_ref[0], bi_ref[0]
    bb_re = cr * b_re - ci * b_im
    bb_im = cr * b_im + ci * b_re
    c_re, c_im = cr_ref[0], ci_ref[0]

    def power(k):
        m = jnp.exp(k * er)
        return m * jnp.cos(k * ei), m * jnp.sin(k * ei)

    def outer(a, b):
        return (a[:, None, :] * b[None, :, :]).reshape(S5_ROW, P2)

    def cmul(ar, ai, br, bi):
        return outer(ar, br) - outer(ai, bi), outer(ar, bi) + outer(ai, br)

    s_row = lax.broadcasted_iota(jnp.int32, (cs, P2), 0).astype(F32)
    fwd = lax.broadcasted_iota(jnp.int32, (cs, P2), 1) < S5_STATE
    half = float(cs // 2)
    k_in = jnp.where(fwd, half - s_row, s_row - half)
    l_re, l_im = cmul(*power(k_in), bb_re, bb_im)
    r_re, r_im = cmul(*power(-k_in), c_re, c_im)
    lcat = jnp.concatenate([l_re, l_im], axis=1)
    rcat = jnp.concatenate([r_re, -r_im], axis=1)
    fwd2 = (lax.broadcasted_iota(jnp.int32, (S5_ROW, S5_ROW), 1) % P2) < S5_STATE
    nt = (((1,), (1,)), ((), ()))
    m_f = lax.dot_general(jnp.where(fwd2, lcat, 0.0), rcat, nt, precision=HIGHEST, preferred_element_type=F32)
    m_b = lax.dot_general(jnp.where(fwd2, 0.0, lcat), rcat, nt, precision=HIGHEST, preferred_element_type=F32)
    s_in = lax.broadcasted_iota(jnp.int32, (S5_ROW, S5_ROW), 0) // S5_GROUP
    s_out = lax.broadcasted_iota(jnp.int32, (S5_ROW, S5_ROW), 1) // S5_GROUP
    m_ref[0] = (jnp.where(s_in <= s_out, m_f, 0.0) + jnp.where(s_in >= s_out, m_b, 0.0)).astype(BF16)
    w_re, w_im = cmul(*power(jnp.where(fwd, (cs - 1.0) - s_row, s_row)), bb_re, bb_im)
    win_ref[0] = jnp.concatenate([w_re, w_im], axis=1).astype(BF16)
    o_re, o_im = cmul(*power(jnp.where(fwd, s_row + 1.0, cs - s_row)), c_re, c_im)
    wout_ref[0] = jnp.concatenate([o_re, -o_im], axis=1).astype(BF16)
    lev = lax.shift_left(jnp.int32(cs), lax.broadcasted_iota(jnp.int32, (nlev, P2), 0)).astype(F32)
    ar_ref[0], ai_ref[0] = power(lev)


def _s5_operators(lam_re, lam_im, log_dt, b_re, b_im, c_re, c_im, nlev):
    P2 = 2 * S5_STATE
    lanes = lambda a: a.transpose(1, 0, 2).reshape(S5_GROUPS, 1, P2)
    both = lambda a: jnp.concatenate([a, a], axis=-1)
    args = (lanes(lam_re), lanes(lam_im),
            lanes(jnp.broadcast_to(log_dt[..., None], lam_re.shape)),
            both(b_re.transpose(0, 2, 1)), both(b_im.transpose(0, 2, 1)), both(c_re), both(c_im))
    g3 = lambda g: (g, 0, 0)
    vec = pl.BlockSpec((1, 1, P2), g3)
    mat = pl.BlockSpec((1, S5_GROUP, P2), g3)
    op = pl.BlockSpec((1, S5_ROW, S5_ROW), g3)
    lev = pl.BlockSpec((1, nlev, P2), g3)
    return pl.pallas_call(
        functools.partial(_s5_prep_kernel, nlev=nlev),
        grid=(S5_GROUPS,),
        in_specs=[vec, vec, vec, mat, mat, mat, mat],
        out_specs=[op, op, op, lev, lev],
        out_shape=[jax.ShapeDtypeStruct((S5_GROUPS, S5_ROW, S5_ROW), BF16)] * 3
        + [jax.ShapeDtypeStruct((S5_GROUPS, nlev, P2), F32)] * 2,
        compiler_params=_cparams("parallel"),
        name="s5_prep",
    )(*args)


def _gelu_tanh(x):
    return 0.5 * x * (1.0 + jnp.tanh(math.sqrt(2.0 / math.pi) * (x + 0.044715 * (x * x * x))))


def _s5_kernel(*refs, nseq, nrow, nlev, has_state):
    if has_state:
        (u_ref, m_ref, win_ref, wout_ref, ar_ref, ai_ref, d_ref, h0_ref,
         z_ref, last_ref, t_scr, y_scr, tmp_r, tmp_i) = refs
    else:
        (u_ref, m_ref, win_ref, wout_ref, ar_ref, ai_ref, d_ref,
         z_ref, last_ref, t_scr, y_scr, tmp_r, tmp_i) = refs
    P2 = 2 * S5_STATE
    R = nseq * nrow
    GP = S5_SLAB_GROUPS
    TR = 16
    slot = lax.broadcasted_iota(jnp.int32, (TR, 128), 1) // S5_GROUP

    def token_rows(nt, s):
        return pl.ds(pl.multiple_of(nt * (TR * S5_CHUNK), TR * S5_CHUNK) + s, TR, stride=S5_CHUNK)

    def gather(nt, carry):
        for j in range(2):
            src = [u_ref[0, token_rows(nt, 8 * j + s8), :] for s8 in range(8)]
            for gl in range(GP):
                acc = None
                for s8 in range(8):
                    sh = ((s8 - gl) * S5_GROUP) % 128
                    piece = pltpu.roll(src[s8], sh, 1) if sh else src[s8]
                    acc = piece if acc is None else jnp.where(slot == s8, piece, acc)
                t_scr[gl, pl.ds(pl.multiple_of(nt * TR, TR), TR), j * 128:(j + 1) * 128] = acc.astype(BF16)
        return carry

    lax.fori_loop(0, R // TR, gather, 0)

    n = lax.broadcasted_iota(jnp.int32, (R, P2), 0) % nrow
    row = lax.broadcasted_iota(jnp.int32, (R, P2), 0)
    fwd = lax.broadcasted_iota(jnp.int32, (R, P2), 1) < S5_STATE
    fwd_b = lax.broadcasted_iota(jnp.int32, (nseq, P2), 1) < S5_STATE

    def shift(a, k):
        dn = pltpu.roll(a, k, 0)
        up = pltpu.roll(a, R - k, 0)
        return jnp.where(fwd, jnp.where(n >= k, dn, 0.0), jnp.where(n < nrow - k, up, 0.0))

    def ends(t):
        if nseq == 1:
            return jnp.where(fwd_b, t[nrow - 1:nrow, :], t[0:1, :])
        return jnp.where(fwd_b, t[pl.ds(nrow - 1, nseq, stride=nrow), :], t[pl.ds(0, nseq, stride=nrow), :])

    def per_group(gl, carry):
        ub = t_scr[gl]
        x = _dot(ub, win_ref[gl])
        xr, xi = x[:, :P2], x[:, P2:]
        er, ei = shift(xr, 1), shift(xi, 1)
        if has_state:
            h0 = h0_ref[0, gl]
            for b in range(nseq):
                first = row == jnp.where(fwd, b * nrow, b * nrow + nrow - 1)
                er = jnp.where(first, h0[b:b + 1, :P2], er)
                ei = jnp.where(first, h0[b:b + 1, P2:], ei)
        for j in range(nlev):
            k = 2 ** j
            ar = ar_ref[gl, j:j + 1, :]
            ai = ai_ref[gl, j:j + 1, :]
            sr, si = shift(er, k), shift(ei, k)
            er, ei = er + ar * sr - ai * si, ei + ar * si + ai * sr
        ecat = jnp.concatenate([er, ei], axis=1).astype(BF16)
        y_scr[gl] = _dot(ub, m_ref[gl]) + _dot_nt(ecat, wout_ref[gl])
        a1r = ar_ref[gl, 0:1, :]
        a1i = ai_ref[gl, 0:1, :]
        tmp_r[...] = a1r * er - a1i * ei + xr
        tmp_i[...] = a1r * ei + a1i * er + xi
        last_ref[0, gl, :, :P2] = ends(tmp_r)
        last_ref[0, gl, :, P2:] = ends(tmp_i)
        return carry

    lax.fori_loop(0, GP, per_group, 0)

    def scatter(nt, carry):
        for j in range(2):
            ys = [y_scr[gl, pl.ds(pl.multiple_of(nt * TR, TR), TR), j * 128:(j + 1) * 128] for gl in range(GP)]
            for s8 in range(8):
                acc = None
                for gl in range(GP):
                    sh = ((gl - s8) * S5_GROUP) % 128
                    piece = pltpu.roll(ys[gl], sh, 1) if sh else ys[gl]
                    acc = piece if acc is None else jnp.where(slot == gl, piece, acc)
                rows = token_rows(nt, 8 * j + s8)
                z_ref[0, rows, :] = _gelu_tanh(acc + d_ref[...] * u_ref[0, rows, :])
        return carry

    lax.fori_loop(0, R // TR, scatter, 0)


def _s5_call(u, ops, d, h0, seq):
    m, win, wout, ar, ai = ops
    bm, lm, _ = u.shape
    nrow = seq // S5_CHUNK
    nlev = int(math.log2(nrow))
    total = bm * lm // seq
    nseq = max(1, min(total, S5_BLOCK_ROWS // nrow))
    nblk = total // nseq
    R = nseq * nrow
    tok = R * S5_CHUNK
    GP = S5_SLAB_GROUPS
    ns = BRANCH // 128
    has_state = h0 is not None
    slab = pl.BlockSpec((1, tok, 128), lambda t, i: (i, 0, t))
    op = pl.BlockSpec((GP, S5_ROW, S5_ROW), lambda t, i: (t, 0, 0))
    lev = pl.BlockSpec((GP, ar.shape[1], 2 * S5_STATE), lambda t, i: (t, 0, 0))
    state = pl.BlockSpec((1, GP, nseq, 4 * S5_STATE), lambda t, i: (i, t, 0, 0))
    in_specs = [slab, op, op, op, lev, lev, pl.BlockSpec((1, 128), lambda t, i: (0, t))]
    args = [u.reshape(nblk, tok, BRANCH), m, win, wout, ar, ai, d]
    if has_state:
        in_specs.append(state)
        args.append(h0.reshape(nblk, nseq, S5_GROUPS, 4 * S5_STATE).transpose(0, 2, 1, 3))
    z, last = pl.pallas_call(
        functools.partial(_s5_kernel, nseq=nseq, nrow=nrow, nlev=nlev, has_state=has_state),
        grid=(ns, nblk),
        in_specs=in_specs,
        out_specs=[slab, state],
        out_shape=[jax.ShapeDtypeStruct((nblk, tok, BRANCH), F32),
                   jax.ShapeDtypeStruct((nblk, S5_GROUPS, nseq, 4 * S5_STATE), F32)],
        scratch_shapes=[pltpu.VMEM((GP, R, S5_ROW), BF16), pltpu.VMEM((GP, R, S5_ROW), F32),
                        pltpu.VMEM((R, 2 * S5_STATE), F32), pltpu.VMEM((R, 2 * S5_STATE), F32)],
        compiler_params=_cparams("parallel", "parallel"),
        name="s5",
    )(*args)
    last = last.transpose(0, 2, 1, 3).reshape(total, S5_GROUPS, 4 * S5_STATE)
    return z.reshape(bm, lm, BRANCH), last


def _seg_mean_matrix(width, nseg):
    r = lax.broadcasted_iota(jnp.int32, (width, width), 0) // ATT_HD
    c = lax.broadcasted_iota(jnp.int32, (width, width), 1) // ATT_HD
    return jnp.where((r == c) & (r < nseg), 1.0 / ATT_HD, 0.0).astype(BF16)


def _seg_rmsnorm(x, w, nseg, mean_mat):
    seg = lax.broadcasted_iota(jnp.int32, x.shape, 1) // ATT_HD
    x2 = x * x
    hi = x2.astype(BF16)
    lo = (x2 - hi.astype(F32)).astype(BF16)
    ms = _dot(hi, mean_mat) + _dot(lo, mean_mat)
    return x * jnp.where(seg < nseg, lax.rsqrt(ms + EPS), 1.0) * w


def _rope(x, cos, sin):
    width = x.shape[1]
    lane = lax.broadcasted_iota(jnp.int32, x.shape, 1)
    partner = jnp.where((lane & 16) == 0, pltpu.roll(x, width - 16, 1), pltpu.roll(x, 16, 1))
    return x * cos + partner * sin


HEAD_ORDER = (0, 2, 1, 3)


def _stack_heads(q):
    head = lax.broadcasted_iota(jnp.int32, q.shape, 1) // ATT_HD
    qb = q.astype(BF16)
    zero = jnp.zeros((), BF16)
    return jnp.concatenate([jnp.where(head == h, qb, zero) for h in HEAD_ORDER], axis=0)


def _key_value_operands(kv):
    low = lax.broadcasted_iota(jnp.int32, kv.shape, 1) < ATT_HD
    sw = pltpu.roll(kv, ATT_HD, 1)
    kk = jnp.where(low, kv, sw).astype(BF16)
    return (jnp.concatenate([kk, kk], axis=1),
            jnp.where(low, sw, 1.0).astype(BF16), jnp.where(low, 1.0, kv).astype(BF16))


LOG2E = 1.4426950408889634
Q_SCALE = (ATT_HD ** -0.5) * LOG2E


def _sink_column(sink_ref, rows):
    col = jnp.concatenate([jnp.broadcast_to(sink_ref[0, h:h + 1, 0:1], (rows, 1)) for h in HEAD_ORDER], axis=0)
    return col * LOG2E


def _softmax_pv(scores, values_even, values_odd, sink_col, rows):
    t = None
    for s in scores:
        for c in range(0, s.shape[1], 128):
            t = s[:, c:c + 128] if t is None else jnp.maximum(t, s[:, c:c + 128])
    m = jnp.maximum(jnp.max(t, axis=-1, keepdims=True), sink_col)
    e_sink = jnp.exp2(sink_col - m)
    low = lax.broadcasted_iota(jnp.int32, (2 * rows, 2 * ATT_HD), 1) < ATT_HD
    o_even = jnp.where(low, 0.0, e_sink[:2 * rows])
    o_odd = jnp.where(low, e_sink[2 * rows:], 0.0)
    for s, ve, vo in zip(scores, values_even, values_odd):
        p = jnp.exp2(s - m).astype(BF16)
        o_even = o_even + _dot(p[:2 * rows], ve)
        o_odd = o_odd + _dot(p[2 * rows:], vo)
    o_even = o_even / pltpu.roll(o_even, ATT_HD, 1)
    o_odd = o_odd / pltpu.roll(o_odd, ATT_HD, 1)
    pair = jnp.where(low, o_even, o_odd)
    return jnp.concatenate([pair[:rows], pair[rows:]], axis=1)


def _att_ctx_kernel(q_ref, kv_ref, sa_ref, qn_ref, kn_ref, sink_ref, y_ref, kc_ref, vc_ref):
    seq = q_ref.shape[1]
    q = _seg_rmsnorm(q_ref[0], qn_ref[...], ATT_GROUPS, _seg_mean_matrix(ATT_GROUPS * ATT_HD, ATT_GROUPS)) * Q_SCALE
    kv = _seg_rmsnorm(kv_ref[0], kn_ref[...], 1, _seg_mean_matrix(2 * ATT_HD, 1))
    kc_ref[0, 0] = kv[:, :ATT_HD]
    vc_ref[0, 0] = kv[:, ATT_HD:]
    k4, va, vb = _key_value_operands(kv)
    s = _dot_nt(_stack_heads(q), k4)
    y_ref[0] = _softmax_pv([s], [va], [vb], _sink_column(sink_ref, seq), seq) * sa_ref[0]


def _att_ctx_call(q, kv, sa, qnw, knw, sink):
    bsz, seq, _ = q.shape
    W = ATT_GROUPS * ATT_HD
    per_kv = lambda b, k: (b, 0, k)
    cache_spec = pl.BlockSpec((1, 1, seq, ATT_HD), lambda b, k: (b, k, 0, 0))
    return pl.pallas_call(
        _att_ctx_kernel,
        grid=(bsz, ATT_KV_HEADS),
        in_specs=[
            pl.BlockSpec((1, seq, W), per_kv),
            pl.BlockSpec((1, seq, 2 * ATT_HD), per_kv),
            pl.BlockSpec((1, seq, W), per_kv),
            pl.BlockSpec((1, W), lambda b, k: (0, 0)),
            pl.BlockSpec((1, 2 * ATT_HD), lambda b, k: (0, 0)),
            pl.BlockSpec((1, 8, 128), lambda b, k: (k, 0, 0)),
        ],
        out_specs=[pl.BlockSpec((1, seq, W), per_kv), cache_spec, cache_spec],
        out_shape=[jax.ShapeDtypeStruct((bsz, seq, BRANCH), F32),
                   jax.ShapeDtypeStruct((bsz, ATT_KV_HEADS, seq, ATT_HD), F32),
                   jax.ShapeDtypeStruct((bsz, ATT_KV_HEADS, seq, ATT_HD), F32)],
        compiler_params=_cparams("parallel", "parallel"),
        name="att_ctx",
    )(q, kv, sa, qnw, knw, sink)


def _att_lat_kernel(q_ref, kv_ref, sa_ref, qn_ref, kn_ref, sink_ref, cq_ref, sq_ref, ck_ref, sk_ref,
                    kc_ref, vc_ref, y_ref, k4_scr, va_scr, vb_scr, *, seq):
    nb = seq // QBLK
    KW = 3 * QBLK
    q_mean = _seg_mean_matrix(ATT_GROUPS * ATT_HD, ATT_GROUPS)
    k_mean = _seg_mean_matrix(2 * ATT_HD, 1)

    def k_body(i, carry):
        r = pl.ds(pl.multiple_of(i * QBLK, QBLK), QBLK)
        kv = _seg_rmsnorm(kv_ref[0, r, :], kn_ref[...], 1, k_mean)
        k4, va, vb = _key_value_operands(_rope(kv, ck_ref[r, :], sk_ref[r, :]))
        k4_scr[r, :] = k4
        va_scr[r, :] = va
        vb_scr[r, :] = vb
        return carry

    lax.fori_loop(0, nb, k_body, 0)
    kc = kc_ref[0, 0, 0]
    vc = vc_ref[0, 0, 0]
    ones = jnp.ones_like(vc)
    kc4 = jnp.concatenate([kc] * ATT_GROUPS, axis=1).astype(BF16)
    vca = jnp.concatenate([vc, ones], axis=1).astype(BF16)
    vcb = jnp.concatenate([ones, vc], axis=1).astype(BF16)
    sink_col = _sink_column(sink_ref, QBLK)
    qi = lax.broadcasted_iota(jnp.int32, (QBLK, KW), 0)
    kj = lax.broadcasted_iota(jnp.int32, (QBLK, KW), 1)

    def q_body(i, carry):
        r = pl.ds(pl.multiple_of(i * QBLK, QBLK), QBLK)
        q = _seg_rmsnorm(q_ref[0, r, :], qn_ref[...], ATT_GROUPS, q_mean)
        q = _rope(q, cq_ref[r, :], sq_ref[r, :]) * Q_SCALE
        ws = pl.multiple_of(jnp.clip(i * QBLK - QBLK, 0, seq - KW), QBLK)
        win = pl.ds(ws, KW)
        bias = jnp.where(jnp.abs((i * QBLK + qi) - (ws + kj)) <= WINDOW, 0.0, NEG)
        qs = _stack_heads(q)
        s_win = _dot_nt(qs, k4_scr[win, :]) + jnp.concatenate([bias] * ATT_GROUPS, axis=0)
        s_ctx = _dot_nt(qs, kc4)
        out = _softmax_pv([s_win, s_ctx], [va_scr[win, :], vca], [vb_scr[win, :], vcb], sink_col, QBLK)
        y_ref[0, r, :] = out * sa_ref[0, r, :]
        return carry

    lax.fori_loop(0, nb, q_body, 0, unroll=2)


def _rope_tables(seq, heads, pad_heads):
    pos = jnp.arange(seq)
    row = (pos // GRID_W).astype(F32)[:, None]
    col = (pos % GRID_W).astype(F32)[:, None]
    nf = ATT_HD // 4
    freq = ROPE_BASE ** (-jnp.arange(nf, dtype=F32) / nf)
    ar, ac = row * freq, col * freq
    cos = jnp.concatenate([jnp.cos(ar), jnp.cos(ar), jnp.cos(ac), jnp.cos(ac)], axis=1)
    sin = jnp.concatenate([-jnp.sin(ar), jnp.sin(ar), -jnp.sin(ac), jnp.sin(ac)], axis=1)
    cos = jnp.concatenate([jnp.tile(cos, (1, heads)), jnp.ones((seq, pad_heads * ATT_HD), F32)], axis=1)
    sin = jnp.concatenate([jnp.tile(sin, (1, heads)), jnp.zeros((seq, pad_heads * ATT_HD), F32)], axis=1)
    return cos, sin


def _att_lat_call(q, kv, sa, qnw, knw, sink, cache_k, cache_v, e):
    bsz, seq, _ = q.shape
    past = cache_k.shape[3]
    W = ATT_GROUPS * ATT_HD
    cq, sq = _rope_tables(seq, ATT_GROUPS, 0)
    ck, sk = _rope_tables(seq, 1, 1)
    per_kv = lambda b, k: (b, 0, k)
    const = lambda b, k: (0, 0)
    cache_spec = pl.BlockSpec((1, 1, 1, past, ATT_HD), lambda b, k: (b, e, k, 0, 0))
    return pl.pallas_call(
        functools.partial(_att_lat_kernel, seq=seq),
        grid=(bsz, ATT_KV_HEADS),
        in_specs=[
            pl.BlockSpec((1, seq, W), per_kv),
            pl.BlockSpec((1, seq, 2 * ATT_HD), per_kv),
            pl.BlockSpec((1, seq, W), per_kv),
            pl.BlockSpec((1, W), const),
            pl.BlockSpec((1, 2 * ATT_HD), const),
            pl.BlockSpec((1, 8, 128), lambda b, k: (k, 0, 0)),
            pl.BlockSpec((seq, W), const),
            pl.BlockSpec((seq, W), const),
            pl.BlockSpec((seq, 2 * ATT_HD), const),
            pl.BlockSpec((seq, 2 * ATT_HD), const),
            cache_spec,
            cache_spec,
        ],
        out_specs=pl.BlockSpec((1, seq, W), per_kv),
        out_shape=jax.ShapeDtypeStruct((bsz, seq, BRANCH), F32),
        scratch_shapes=[pltpu.VMEM((seq, W), BF16), pltpu.VMEM((seq, 2 * ATT_HD), BF16),
                        pltpu.VMEM((seq, 2 * ATT_HD), BF16)],
        compiler_params=_cparams("parallel", "parallel"),
        name="att_lat",
    )(q, kv, sa, qnw, knw, sink, cq, sq, ck, sk, cache_k, cache_v)


def _even_out_kernel(x_ref, mod_ref, yg_ref, z_ref, ss_ref, wg_ref, bg_ref, wo_ref, o_ref):
    z = z_ref[0]
    glu = z * jax.nn.sigmoid(_dot(z.astype(BF16), wg_ref[...]) + bg_ref[...])
    ys = (glu * ss_ref[0]).astype(BF16)
    yg = yg_ref[0].astype(BF16)
    out = _dot(yg, wo_ref[:BRANCH, :]) + _dot(ys, wo_ref[BRANCH:, :])
    o_ref[0] = x_ref[0] + mod_ref[0, 2:3, :] * out


def _even_out_call(x, mod, yg, z, ss, wglu, bglu, wout):
    bm, lm, _ = x.shape
    tm = min(ROW_TILE, lm)
    tile = lambda n: pl.BlockSpec((1, tm, n), lambda b, i: (b, i, 0))
    const = lambda b, i: (0, 0)
    return pl.pallas_call(
        _even_out_kernel,
        grid=(bm, lm // tm),
        in_specs=[
            tile(D_MODEL),
            pl.BlockSpec((1, 3, D_MODEL), lambda b, i: (b, 0, 0)),
            tile(BRANCH), tile(BRANCH), tile(BRANCH),
            pl.BlockSpec((BRANCH, BRANCH), const),
            pl.BlockSpec((1, BRANCH), const),
            pl.BlockSpec((D_MODEL, D_MODEL), const),
        ],
        out_specs=tile(D_MODEL),
        out_shape=jax.ShapeDtypeStruct((bm, lm, D_MODEL), F32),
        compiler_params=_cparams("parallel", "parallel"),
        name="even_out",
    )(x, mod, yg, z, ss, wglu, bglu, wout)


def _odd_out_kernel(x_ref, mod_ref, ya_ref, p_ref, pprev_ref, pnext_ref, bgz_ref, cw_ref, cb_ref, wo_ref,
                    o_ref, *, seq, tm):
    p = p_ref[0]
    pos = (pl.program_id(1) * tm + lax.broadcasted_iota(jnp.int32, (tm, BRANCH), 0)) % seq
    rowi = lax.broadcasted_iota(jnp.int32, (tm, BRANCH), 0)
    prev = jnp.where(rowi == 0, pprev_ref[0, 7:8, :], pltpu.roll(p, 1, 0))
    nxt = jnp.where(rowi == tm - 1, pnext_ref[0, 0:1, :], pltpu.roll(p, tm - 1, 0))
    prev = jnp.where(pos == 0, 0.0, prev)
    nxt = jnp.where(pos == seq - 1, 0.0, nxt)
    conv = prev * cw_ref[0:1, :] + p * cw_ref[1:2, :] + nxt * cw_ref[2:3, :] + cb_ref[...]
    yc = (bgz_ref[0] * conv).astype(BF16)
    ya = ya_ref[0].astype(BF16)
    out = _dot(ya, wo_ref[:BRANCH, :]) + _dot(yc, wo_ref[BRANCH:, :])
    o_ref[0] = x_ref[0] + mod_ref[0, 2:3, :] * out


def _odd_out_call(x, mod, ya, p, bgz, convw, convb, wout, seq):
    bm, lm, _ = x.shape
    tm = min(ROW_TILE, lm)
    nt8 = lm // 8
    tile = lambda n: pl.BlockSpec((1, tm, n), lambda b, i: (b, i, 0))
    const = lambda b, i: (0, 0)
    prev_spec = pl.BlockSpec((1, 8, BRANCH), lambda b, i: (b, jnp.maximum(i * (tm // 8) - 1, 0), 0))
    next_spec = pl.BlockSpec((1, 8, BRANCH), lambda b, i: (b, jnp.minimum((i + 1) * (tm // 8), nt8 - 1), 0))
    return pl.pallas_call(
        functools.partial(_odd_out_kernel, seq=seq, tm=tm),
        grid=(bm, lm // tm),
        in_specs=[
            tile(D_MODEL),
            pl.BlockSpec((1, 3, D_MODEL), lambda b, i: (b, 0, 0)),
            tile(BRANCH), tile(BRANCH), prev_spec, next_spec, tile(BRANCH),
            pl.BlockSpec((CONV_W, BRANCH), const),
            pl.BlockSpec((1, BRANCH), const),
            pl.BlockSpec((D_MODEL, D_MODEL), const),
        ],
        out_specs=tile(D_MODEL),
        out_shape=jax.ShapeDtypeStruct((bm, lm, D_MODEL), F32),
        compiler_params=_cparams("parallel", "parallel"),
        name="odd_out",
    )(x, mod, ya, p, p, p, bgz, convw, convb, wout)


def _even_in_weight(w):
    dk = GLA_HEADS * GLA_DK
    q, k = w[:, :dk], w[:, dk:2 * dk]
    v = w[:, 2 * dk:2 * dk + BRANCH]
    o = 2 * dk + BRANCH
    lr = w[:, o:o + 2 * GLA_RANK]
    rest = w[:, o + 2 * GLA_RANK:]
    qk = jnp.concatenate([jnp.concatenate([q[:, h * GLA_DK:(h + 1) * GLA_DK], k[:, h * GLA_DK:(h + 1) * GLA_DK]],
                                          axis=1) for h in range(GLA_HEADS)], axis=1)
    lr = jnp.pad(lr, ((0, 0), (0, 128 - 2 * GLA_RANK)))
    return jnp.concatenate([qk, v, lr, rest], axis=1).astype(BF16)


def _odd_in_weight(w):
    q = w[:, :BRANCH]
    k = w[:, BRANCH:BRANCH + ATT_KV_HEADS * ATT_HD]
    v = w[:, BRANCH + ATT_KV_HEADS * ATT_HD:BRANCH + 2 * ATT_KV_HEADS * ATT_HD]
    rest = w[:, BRANCH + 2 * ATT_KV_HEADS * ATT_HD:]
    kv = jnp.concatenate([jnp.concatenate([k[:, h * ATT_HD:(h + 1) * ATT_HD], v[:, h * ATT_HD:(h + 1) * ATT_HD]],
                                          axis=1) for h in range(ATT_KV_HEADS)], axis=1)
    return jnp.concatenate([q, kv, rest], axis=1).astype(BF16)


def _gate_weights(w2, b2):
    zf = jnp.zeros((GLA_RANK, GLA_DK), F32)
    cols, bias = [], []
    for h in range(GLA_HEADS):
        sl = slice(h * GLA_DK, (h + 1) * GLA_DK)
        cols.append(jnp.concatenate([jnp.concatenate([w2[0][:, sl], zf], axis=1),
                                     jnp.concatenate([zf, w2[1][:, sl]], axis=1)], axis=0))
        bias += [b2[0][sl], b2[1][sl]]
    w = jnp.pad(jnp.concatenate(cols, axis=1), ((0, 128 - 2 * GLA_RANK), (0, 0)))
    w_hi = w.astype(BF16)
    w_lo = (w - w_hi.astype(F32)).astype(BF16)
    return w_hi, w_lo, jnp.concatenate(bias)[None, :]


def _even_layer(x, mod, nw, w_in, seq, gla_w, s5_ops, dgm, wglu, bglu, wout, onorm, gla_s0, s5_h0):
    bm, lm, _ = x.shape
    nseq = bm * lm // seq
    qk, v, bc, sg, u, ss = _in_call(_even_in_kernel, "even_in", x, mod, nw, w_in, gla_w,
                                    (BRANCH,) * 6)
    per_seq = lambda a: a.reshape(nseq, seq, a.shape[-1])
    yg, s_new = _gla_call(per_seq(qk), per_seq(v), per_seq(bc), per_seq(sg), onorm, gla_s0)
    z, last = _s5_call(u, s5_ops, dgm, s5_h0, seq)
    x_new = _even_out_call(x, mod, yg.reshape(bm, lm, BRANCH), z, ss, wglu, bglu, wout)
    return x_new, s_new, last


def _odd_layer(x, mod, nw, w_in, seq, qnw, knw, sink, convw, convb, wout, cache_k, cache_v, e):
    bm, lm, _ = x.shape
    nseq = bm * lm // seq
    q, kv, sa, p, bgz = _in_call(_odd_in_kernel, "odd_in", x, mod, nw, w_in, (),
                                 (BRANCH, 2 * ATT_KV_HEADS * ATT_HD, BRANCH, BRANCH, BRANCH))
    per_seq = lambda a: a.reshape(nseq, seq, a.shape[-1])
    if cache_k is None:
        ya, kc, vc = _att_ctx_call(per_seq(q), per_seq(kv), per_seq(sa), qnw, knw, sink)
    else:
        ya = _att_lat_call(per_seq(q), per_seq(kv), per_seq(sa), qnw, knw, sink, cache_k, cache_v, e)
        kc = vc = None
    x_new = _odd_out_call(x, mod, ya.reshape(bm, lm, BRANCH), p, bgz, convw, convb, wout, seq)
    return x_new, kc, vc


def kernel(x_prompt, x_sample, c, state_gla, state_s5_re, state_s5_im, cache_k, cache_v, c_ctx, norm_w, w_ada, b_ada, w_in_e, w_out_e, gla_w2, gla_b2, gla_onorm, s5_lam_re, s5_lam_im, s5_log_dt, s5_b_re, s5_b_im, s5_c_re, s5_c_im, s5_d, s5_w_glu, s5_b_glu, w_in_o, w_out_o, q_norm_w, k_norm_w, sink, conv_w, conv_b):
    depth = norm_w.shape[0]
    bp, lp, _ = x_prompt.shape
    bs, ls, _ = x_sample.shape

    rows = 8 * ((1 + bs + 7) // 8)
    cs = jnp.zeros((rows, D_MODEL), F32).at[0].set(c_ctx).at[1:1 + bs].set(c)
    mods = _ada_call(cs, w_ada, b_ada)

    xp = x_prompt.reshape(1, bp * lp, D_MODEL)
    xs = x_sample
    new_gla, new_re, new_im, new_k, new_v = [], [], [], [], []
    for l in range(depth):
        e = l // 2
        mod_p = mods[l, 0:1].reshape(1, 3, D_MODEL)
        mod_s = mods[l, 1:1 + bs].reshape(bs, 3, D_MODEL)
        nw = norm_w[l][None, :]
        if l % 2 == 0:
            w_in = _even_in_weight(w_in_e[e])
            wout = w_out_e[e].astype(BF16)
            gla_w = _gate_weights(gla_w2[e], gla_b2[e])
            onorm = gla_onorm[e][None, :]
            dgm = s5_d[e][None, :]
            wglu = s5_w_glu[e].astype(BF16)
            bglu = s5_b_glu[e][None, :]
            s5_ops = _s5_operators(s5_lam_re[e], s5_lam_im[e], s5_log_dt[e], s5_b_re[e], s5_b_im[e],
                                   s5_c_re[e], s5_c_im[e], nlev=int(math.log2(max(lp, ls) // S5_CHUNK)))
            h0 = jnp.concatenate([state_s5_re[:, e], state_s5_im[:, e]], axis=1)
            h0 = h0.transpose(0, 2, 1, 3).reshape(bs, S5_GROUPS, 4 * S5_STATE)
            xp, sg, last = _even_layer(xp, mod_p, nw, w_in, lp, gla_w, s5_ops, dgm, wglu, bglu, wout, onorm,
                                       None, None)
            xs, _, _ = _even_layer(xs, mod_s, nw, w_in, ls, gla_w, s5_ops, dgm, wglu, bglu, wout, onorm,
                                   state_gla[:, e], h0)
            new_gla.append(sg)
            last = last.reshape(bp, S5_GROUPS, 2, 2, S5_STATE).transpose(2, 0, 3, 1, 4)
            new_re.append(last[0])
            new_im.append(last[1])
        else:
            w_in = _odd_in_weight(w_in_o[e])
            wout = w_out_o[e].astype(BF16)
            qnw = jnp.tile(q_norm_w[e], ATT_GROUPS)[None, :]
            knw = jnp.concatenate([k_norm_w[e], jnp.ones((ATT_HD,), F32)])[None, :]
            sk = jnp.broadcast_to(sink[e].reshape(ATT_KV_HEADS, ATT_GROUPS, 1), (ATT_KV_HEADS, ATT_GROUPS, 128))
            sk = jnp.concatenate([sk, jnp.zeros((ATT_KV_HEADS, 8 - ATT_GROUPS, 128), F32)], axis=1)
            cb = conv_b[e][None, :]
            xp, kc, vc = _odd_layer(xp, mod_p, nw, w_in, lp, qnw, knw, sk, conv_w[e], cb, wout, None, None, e)
            xs, _, _ = _odd_layer(xs, mod_s, nw, w_in, ls, qnw, knw, sk, conv_w[e], cb, wout, cache_k, cache_v, e)
            new_k.append(kc)
            new_v.append(vc)
    return (xp.reshape(bp, lp, D_MODEL), xs,
            jnp.stack(new_gla, axis=1), jnp.stack(new_re, axis=1), jnp.stack(new_im, axis=1),
            jnp.stack(new_k, axis=1), jnp.stack(new_v, axis=1))
```

```python
import functools
import math

import jax
import jax.numpy as jnp
from jax import lax
from jax.experimental import pallas as pl
from jax.experimental.pallas import tpu as pltpu

F32 = jnp.float32
BF16 = jnp.bfloat16
HIGHEST = lax.Precision.HIGHEST

D_MODEL = 1024
BRANCH = D_MODEL // 2
GLA_HEADS = 4
GLA_DK = 64
GLA_DV = 128
GLA_RANK = 16
GLA_GATE_NORM = 16.0
GLA_CHUNK = 64
GLA_BLOCK = 256
S5_GROUP = 16
S5_GROUPS = BRANCH // S5_GROUP
S5_STATE = 64
S5_CHUNK = 16
S5_ROW = S5_CHUNK * S5_GROUP
S5_SLAB_GROUPS = 128 // S5_GROUP
S5_BLOCK_ROWS = 256
ATT_HEADS = 8
ATT_KV_HEADS = 2
ATT_GROUPS = ATT_HEADS // ATT_KV_HEADS
ATT_HD = 64
WINDOW = 128
QBLK = 128
GRID_W = 64
ROPE_BASE = 10000.0
CONV_W = 3
CONV_HALO_ROWS = 16
EPS = 1e-6
NEG = -1e30

ROW_TILE = 512
VMEM_LIMIT = 48 * 1024 * 1024


def _cparams(*sem):
    return pltpu.CompilerParams(dimension_semantics=sem, vmem_limit_bytes=VMEM_LIMIT)


def _silu(x):
    return x * jax.nn.sigmoid(x)


def _dot(a, b):
    return jnp.dot(a, b, preferred_element_type=F32)


def _dot_nt(a, b):
    return lax.dot_general(a, b, (((1,), (1,)), ((), ())), preferred_element_type=F32)


def _ada_kernel(c_ref, w_ref, b_ref, o_ref):
    c = c_ref[...]
    o_ref[0] = jnp.dot(_silu(c), w_ref[0], precision=HIGHEST, preferred_element_type=F32) + b_ref[0]


def _ada_call(cs, w_ada, b_ada):
    depth = w_ada.shape[0]
    rows = cs.shape[0]
    nt = 3
    return pl.pallas_call(
        _ada_kernel,
        grid=(depth, nt),
        in_specs=[
            pl.BlockSpec((rows, D_MODEL), lambda l, j: (0, 0)),
            pl.BlockSpec((1, D_MODEL, D_MODEL), lambda l, j: (l, 0, j)),
            pl.BlockSpec((1, 1, D_MODEL), lambda l, j: (l, 0, j)),
        ],
        out_specs=pl.BlockSpec((1, rows, D_MODEL), lambda l, j: (l, 0, j)),
        out_shape=jax.ShapeDtypeStruct((depth, rows, 3 * D_MODEL), F32),
        compiler_params=_cparams("arbitrary", "arbitrary"),
        name="adaln",
    )(cs, w_ada, b_ada.reshape(depth, 1, 3 * D_MODEL))


def _modulated(x_ref, mod_ref, nw_ref):
    x = x_ref[0]
    ms = jnp.mean(x * x, axis=-1, keepdims=True)
    y = x * lax.rsqrt(ms + EPS) * nw_ref[...]
    shift = mod_ref[0, 0:1, :]
    scale = mod_ref[0, 1:2, :]
    return (y * (1.0 + scale) + shift).astype(BF16)


E_COLS = (0, 512, 1024, 1152, 1664, 2176, 2688)


def _log_sigmoid(x):
    return jnp.minimum(x, 0.0) - jnp.log(1.0 + jnp.exp(-jnp.abs(x)))


def _chunk_cumsum(g):
    rows = g.shape[0]
    pos = lax.broadcasted_iota(jnp.int32, g.shape, 0) % GLA_CHUNK
    fwd = (lax.broadcasted_iota(jnp.int32, g.shape, 1) % (2 * GLA_DK)) < GLA_DK
    k = 1
    while k < GLA_CHUNK:
        dn = jnp.where(pos >= k, pltpu.roll(g, k, 0), 0.0)
        up = jnp.where(pos < GLA_CHUNK - k, pltpu.roll(g, rows - k, 0), 0.0)
        g = g + jnp.where(fwd, dn, up)
        k *= 2
    return g


def _even_in_kernel(x_ref, mod_ref, nw_ref, w_ref, w2h_ref, w2l_ref, b2_ref,
                    qk_ref, v_ref, bc_ref, sg_ref, u_ref, ss_ref):
    h = _modulated(x_ref, mod_ref, nw_ref)
    c = E_COLS
    qk_ref[0] = _dot(h, w_ref[:, c[0]:c[1]]).astype(BF16)
    v_ref[0] = _dot(h, w_ref[:, c[1]:c[2]]).astype(BF16)
    lr = _dot(h, w_ref[:, c[2]:c[3]])
    lr_hi = lr.astype(BF16)
    lr_lo = (lr - lr_hi.astype(F32)).astype(BF16)
    pre = _dot(lr_hi, w2h_ref[...]) + (_dot(lr_lo, w2h_ref[...]) + _dot(lr_hi, w2l_ref[...])) + b2_ref[...]
    bc_ref[0] = _chunk_cumsum(_log_sigmoid(pre) * (1.0 / GLA_GATE_NORM))
    sg_ref[0] = _silu(_dot(h, w_ref[:, c[3]:c[4]])).astype(BF16)
    u_ref[0] = _dot(h, w_ref[:, c[4]:c[5]])
    ss_ref[0] = _silu(_dot(h, w_ref[:, c[5]:c[6]])).astype(BF16)


O_COLS = (0, 512, 768, 1280, 1792, 2304, 2816, 3328)


def _rope(x, cos, sin):
    width = x.shape[1]
    lane = lax.broadcasted_iota(jnp.int32, x.shape, 1)
    partner = jnp.where((lane & 16) == 0, pltpu.roll(x, width - 16, 1), pltpu.roll(x, 16, 1))
    return x * cos + partner * sin


def _head_pair_norm(x, w, cos, sin, both):
    low = lax.broadcasted_iota(jnp.int32, x.shape, 1) < ATT_HD
    x2 = x * x
    ms = jnp.sum(jnp.where(low, x2, 0.0), axis=-1, keepdims=True) * (1.0 / ATT_HD)
    rs = lax.rsqrt(ms + EPS)
    if both:
        ms_hi = jnp.sum(jnp.where(low, 0.0, x2), axis=-1, keepdims=True) * (1.0 / ATT_HD)
        rs = jnp.where(low, rs, lax.rsqrt(ms_hi + EPS))
    else:
        rs = jnp.where(low, rs, 1.0)
    y = x * rs * w
    return y if cos is None else _rope(y, cos, sin)


def _odd_in_kernel(*refs, rope):
    if rope:
        (x_ref, mod_ref, nw_ref, w_ref, qn_ref, kn_ref, cq_ref, sq_ref, ck_ref, sk_ref,
         q_ref, kv_ref, sa_ref, p_ref, bgz_ref) = refs
        cq, sq, ck, sk = cq_ref[...], sq_ref[...], ck_ref[...], sk_ref[...]
    else:
        (x_ref, mod_ref, nw_ref, w_ref, qn_ref, kn_ref, q_ref, kv_ref, sa_ref, p_ref, bgz_ref) = refs
        cq = sq = ck = sk = None
    h = _modulated(x_ref, mod_ref, nw_ref)
    c = O_COLS
    q = _dot(h, w_ref[:, c[0]:c[1]])
    for t in range(BRANCH // 128):
        lanes = slice(t * 128, (t + 1) * 128)
        qt = _head_pair_norm(q[:, lanes], qn_ref[:, lanes], cq, sq, True)
        q_ref[0, :, lanes] = (qt * Q_SCALE).astype(BF16)
    kv = _dot(h, w_ref[:, c[1]:c[2]])
    for t in range(ATT_KV_HEADS):
        lanes = slice(t * 128, (t + 1) * 128)
        kv_ref[0, :, lanes] = _head_pair_norm(kv[:, lanes], kn_ref[:, lanes], ck, sk, False)
    sa_ref[0] = _silu(_dot(h, w_ref[:, c[2]:c[3]])).astype(BF16)
    xc = _dot(h, w_ref[:, c[3]:c[4]])
    bg = _dot(h, w_ref[:, c[4]:c[5]])
    cg = _dot(h, w_ref[:, c[5]:c[6]])
    zc = _dot(h, w_ref[:, c[6]:c[7]])
    p_ref[0] = (cg * xc).astype(BF16)
    bgz_ref[0] = (bg * _silu(zc)).astype(BF16)


def _in_call(body, name, x, mod, nw, w, consts, outs, row_tables=()):
    out_widths = [n for n, _ in outs]
    bm, lm, _ = x.shape
    tm = min(ROW_TILE, lm)
    whole = lambda a: pl.BlockSpec(a.shape, lambda b, i: (0, 0))
    by_row = lambda a: pl.BlockSpec((tm, a.shape[1]), lambda b, i: (i, 0))
    return pl.pallas_call(
        body,
        grid=(bm, lm // tm),
        in_specs=[
            pl.BlockSpec((1, tm, D_MODEL), lambda b, i: (b, i, 0)),
            pl.BlockSpec((1, 3, D_MODEL), lambda b, i: (b, 0, 0)),
            whole(nw),
            whole(w),
        ] + [whole(a) for a in consts] + [by_row(a) for a in row_tables],
        out_specs=[pl.BlockSpec((1, tm, n), lambda b, i: (b, i, 0)) for n in out_widths],
        out_shape=[jax.ShapeDtypeStruct((bm, lm, n), dt) for n, dt in outs],
        compiler_params=_cparams("parallel", "parallel"),
        name=name,
    )(x, mod, nw, w, *consts, *row_tables)


def _gla_kernel(*refs, seq, has_state):
    if has_state:
        (qk_ref, v_ref, bc_ref, sg_ref, on_ref, s0_ref, y_ref, sn_ref,
         ut_scr, sf_scr, sb_scr, qd_scr, o_scr) = refs
    else:
        (qk_ref, v_ref, bc_ref, sg_ref, on_ref, y_ref, sn_ref,
         ut_scr, sf_scr, sb_scr, qd_scr, o_scr) = refs
        s0_ref = None
    C = GLA_CHUNK
    W = 2 * GLA_DK
    R = GLA_BLOCK
    CB = R // C
    nc = seq // C
    nb = seq // R
    ri = lax.broadcasted_iota(jnp.int32, (R, R), 0)
    ci = lax.broadcasted_iota(jnp.int32, (R, R), 1)
    same = (ri // C) == (ci // C)
    tril = same & (ri >= ci)
    triu = same & (ri <= ci)
    fwd = lax.broadcasted_iota(jnp.int32, (R, W), 1) < GLA_DK
    fwd_row = lax.broadcasted_iota(jnp.int32, (1, W), 1) < GLA_DK
    fwd_st = lax.broadcasted_iota(jnp.int32, (GLA_DV, W), 1) < GLA_DK
    row_chunk = lax.broadcasted_iota(jnp.int32, (R, W), 0) // C
    scale = GLA_DK ** -0.5

    def rows(c):
        return pl.ds(pl.multiple_of(c * C, C), C)

    def block_rows(j):
        return pl.ds(pl.multiple_of(j * R, R), R)

    def state_rows(c):
        return pl.ds(pl.multiple_of(c * GLA_DV, GLA_DV), GLA_DV)

    def intra(j, carry):
        r = block_rows(j)
        qk = qk_ref[0, r, :].astype(F32)
        bc = bc_ref[0, r, :]
        v = v_ref[0, r, :].astype(F32)
        sw = pltpu.roll(qk, GLA_DK, 1)
        q2 = jnp.where(fwd, qk, sw) * scale
        k2 = jnp.where(fwd, sw, qk)
        b_mid = bc[C // 2:C // 2 + 1, :]
        b_end = jnp.where(fwd_row, bc[C - 1:C, :], bc[0:1, :])
        for c in range(1, CB):
            o0 = c * C
            b_mid = jnp.where(row_chunk == c, bc[o0 + C // 2:o0 + C // 2 + 1, :], b_mid)
            b_end = jnp.where(row_chunk == c, jnp.where(fwd_row, bc[o0 + C - 1:o0 + C, :], bc[o0:o0 + 1, :]), b_end)
        qd = (q2 * jnp.exp(bc)).astype(BF16)
        qh = q2 * jnp.exp(bc - b_mid)
        kh = (k2 * jnp.exp(b_mid - bc)).astype(BF16)
        att = (jnp.where(tril, _dot_nt(jnp.where(fwd, qh, 0.0).astype(BF16), kh), 0.0)
               + jnp.where(triu, _dot_nt(jnp.where(fwd, 0.0, qh).astype(BF16), kh), 0.0))
        o = _dot(att.astype(BF16), v.astype(BF16))
        kd = k2 * jnp.exp(b_end - bc)
        kd_wide = jnp.concatenate([jnp.where(row_chunk == c, kd, 0.0) for c in range(CB)], axis=1)
        ut = _dot(v.T.astype(BF16), kd_wide.astype(BF16))
        qd_scr[r, :] = qd
        o_scr[r, :] = o
        ut_scr[j] = ut
        return carry

    lax.fori_loop(0, nb, intra, 0, unroll=2)

    def scan(j, st):
        jb = nb - 1 - j
        bcf = bc_ref[0, block_rows(j), :]
        bcb = bc_ref[0, block_rows(jb), :]
        utf = ut_scr[j]
        utb = ut_scr[jb]
        for c in range(CB):
            cb = CB - 1 - c
            stb = st.astype(BF16)
            sf_scr[j, :, c * W:(c + 1) * W] = stb
            sb_scr[jb, :, cb * W:(cb + 1) * W] = stb
            b_end = jnp.where(fwd_row, bcf[c * C + C - 1:c * C + C, :], bcb[cb * C:cb * C + 1, :])
            inc = jnp.where(fwd_st, utf[:, c * W:(c + 1) * W], utb[:, cb * W:(cb + 1) * W])
            st = st * jnp.exp(b_end) + inc
        return st

    if has_state:
        init = jnp.concatenate([s0_ref[0, 0, 0].T, s0_ref[0, 1, 0].T], axis=1)
    else:
        init = jnp.zeros((GLA_DV, W), F32)
    st = lax.fori_loop(0, nb, scan, init)
    sn_ref[0, 0, 0] = st[:, :GLA_DK].T
    sn_ref[0, 1, 0] = st[:, GLA_DK:].T

    fwd_wide = (lax.broadcasted_iota(jnp.int32, (GLA_DV, CB * W), 1) % W) < GLA_DK

    def finish(j, carry):
        r = block_rows(j)
        qd = qd_scr[r, :]
        zero = jnp.zeros((), BF16)
        qd_wide = jnp.concatenate([jnp.where(row_chunk == c, qd, zero) for c in range(CB)], axis=1)
        s_in = jnp.where(fwd_wide, sf_scr[j], sb_scr[j])
        o = o_scr[r, :] + _dot_nt(qd_wide, s_in)
        ms = jnp.mean(o * o, axis=-1, keepdims=True)
        y_ref[0, r, :] = (o * lax.rsqrt(ms + EPS) * on_ref[...] * sg_ref[0, r, :].astype(F32)).astype(BF16)
        return carry

    lax.fori_loop(0, nb, finish, 0, unroll=2)


def _gla_call(qk, v, bc, sg, onorm, s0):
    bsz, seq, _ = qk.shape
    has_state = s0 is not None
    nb = seq // GLA_BLOCK
    wide = (GLA_BLOCK // GLA_CHUNK) * 2 * GLA_DK
    per_head =pl.BlockSpec((1, seq, 128), lambda b, h: (b, 0, h))
    in_specs = [per_head, per_head, per_head, per_head, pl.BlockSpec((1, GLA_DV), lambda b, h: (0, 0))]
    args = [qk, v, bc, sg, onorm]
    state_spec = pl.BlockSpec((1, 2, 1, GLA_DK, GLA_DV), lambda b, h: (b, 0, h, 0, 0))
    if has_state:
        in_specs.append(state_spec)
        args.append(s0)
    return pl.pallas_call(
        functools.partial(_gla_kernel, seq=seq, has_state=has_state),
        grid=(bsz, GLA_HEADS),
        in_specs=in_specs,
        out_specs=[per_head, state_spec],
        out_shape=[jax.ShapeDtypeStruct((bsz, seq, BRANCH), BF16),
                   jax.ShapeDtypeStruct((bsz, 2, GLA_HEADS, GLA_DK, GLA_DV), F32)],
        scratch_shapes=[pltpu.VMEM((nb, GLA_DV, wide), F32),
                        pltpu.VMEM((nb, GLA_DV, wide), BF16),
                        pltpu.VMEM((nb, GLA_DV, wide), BF16),
                        pltpu.VMEM((seq, 2 * GLA_DK), BF16),
                        pltpu.VMEM((seq, GLA_DV), F32)],
        compiler_params=_cparams("parallel", "parallel"),
        name="gla",
    )(*args)


def _s5_prep_kernel(lr_ref, li_ref, ld_ref, br_ref, bi_ref, cr_ref, ci_ref,
                    m_ref, win_ref, wout_ref, ar_ref, ai_ref, *, nlev):
    cs = S5_CHUNK
    P2 = 2 * S5_STATE
    lam_re, lam_im = lr_ref[0], li_ref[0]
    dt = jnp.exp(ld_ref[0])
    er, ei = lam_re * dt, lam_im * dt
    mag = jnp.exp(er)
    nr, ni = mag * jnp.cos(ei) - 1.0, mag * jnp.sin(ei)
    den = lam_re * lam_re + lam_im * lam_im
    cr = (nr * lam_re + ni * lam_im) / den
    ci = (ni * lam_re - nr * lam_im) / den
    b_re, b_im = br_ref[0], bi_ref[0]
    bb_re = cr * b_re - ci * b_im
    bb_im = cr * b_im + ci * b_re
    c_re, c_im = cr_ref[0], ci_ref[0]

    def power(k):
        m = jnp.exp(k * er)
        return m * jnp.cos(k * ei), m * jnp.sin(k * ei)

    def outer(a, b):
        return (a[:, None, :] * b[None, :, :]).reshape(S5_ROW, P2)

    def cmul(ar, ai, br, bi):
        return outer(ar, br) - outer(ai, bi), outer(ar, bi) + outer(ai, br)

    s_row = lax.broadcasted_iota(jnp.int32, (cs, P2), 0).astype(F32)
    fwd = lax.broadcasted_iota(jnp.int32, (cs, P2), 1) < S5_STATE
    half = float(cs // 2)
    k_in = jnp.where(fwd, half - s_row, s_row - half)
    l_re, l_im = cmul(*power(k_in), bb_re, bb_im)
    r_re, r_im = cmul(*power(-k_in), c_re, c_im)
    lcat = jnp.concatenate([l_re, l_im], axis=1)
    rcat = jnp.concatenate([r_re, -r_im], axis=1)
    fwd2 = (lax.broadcasted_iota(jnp.int32, (S5_ROW, S5_ROW), 1) % P2) < S5_STATE
    nt = (((1,), (1,)), ((), ()))
    m_f = lax.dot_general(jnp.where(fwd2, lcat, 0.0), rcat, nt, precision=HIGHEST, preferred_element_type=F32)
    m_b = lax.dot_general(jnp.where(fwd2, 0.0, lcat), rcat, nt, precision=HIGHEST, preferred_element_type=F32)
    s_in = lax.broadcasted_iota(jnp.int32, (S5_ROW, S5_ROW), 0) // S5_GROUP
    s_out = lax.broadcasted_iota(jnp.int32, (S5_ROW, S5_ROW), 1) // S5_GROUP
    m_ref[0] = (jnp.where(s_in <= s_out, m_f, 0.0) + jnp.where(s_in >= s_out, m_b, 0.0)).astype(BF16)
    w_re, w_im = cmul(*power(jnp.where(fwd, (cs - 1.0) - s_row, s_row)), bb_re, bb_im)
    win_ref[0] = jnp.concatenate([w_re, w_im], axis=1).astype(BF16)
    o_re, o_im = cmul(*power(jnp.where(fwd, s_row + 1.0, cs - s_row)), c_re, c_im)
    wout_ref[0] = jnp.concatenate([o_re, -o_im], axis=1).astype(BF16)
    lev = lax.shift_left(jnp.int32(cs), lax.broadcasted_iota(jnp.int32, (nlev, P2), 0)).astype(F32)
    ar_ref[0], ai_ref[0] = power(lev)


def _s5_operators(lam_re, lam_im, log_dt, b_re, b_im, c_re, c_im, nlev):
    P2 = 2 * S5_STATE
    lanes = lambda a: a.transpose(1, 0, 2).reshape(S5_GROUPS, 1, P2)
    both = lambda a: jnp.concatenate([a, a], axis=-1)
    args = (lanes(lam_re), lanes(lam_im),
            lanes(jnp.broadcast_to(log_dt[..., None], lam_re.shape)),
            both(b_re.transpose(0, 2, 1)), both(b_im.transpose(0, 2, 1)), both(c_re), both(c_im))
    g3 = lambda g: (g, 0, 0)
    vec = pl.BlockSpec((1, 1, P2), g3)
    mat = pl.BlockSpec((1, S5_GROUP, P2), g3)
    op = pl.BlockSpec((1, S5_ROW, S5_ROW), g3)
    lev = pl.BlockSpec((1, nlev, P2), g3)
    return pl.pallas_call(
        functools.partial(_s5_prep_kernel, nlev=nlev),
        grid=(S5_GROUPS,),
        in_specs=[vec, vec, vec, mat, mat, mat, mat],
        out_specs=[op, op, op, lev, lev],
        out_shape=[jax.ShapeDtypeStruct((S5_GROUPS, S5_ROW, S5_ROW), BF16)] * 3
        + [jax.ShapeDtypeStruct((S5_GROUPS, nlev, P2), F32)] * 2,
        compiler_params=_cparams("parallel"),
        name="s5_prep",
    )(*args)


def _gelu_tanh(x):
    return 0.5 * x * (1.0 + jnp.tanh(math.sqrt(2.0 / math.pi) * (x + 0.044715 * (x * x * x))))


def _s5_kernel(*refs, nseq, nrow, nlev, has_state):
    if has_state:
        (u_ref, m_ref, win_ref, wout_ref, ar_ref, ai_ref, d_ref, h0_ref,
         z_ref, last_ref, t_scr, y_scr, tmp_r, tmp_i) = refs
    else:
        (u_ref, m_ref, win_ref, wout_ref, ar_ref, ai_ref, d_ref,
         z_ref, last_ref, t_scr, y_scr, tmp_r, tmp_i) = refs
    P2 = 2 * S5_STATE
    R = nseq * nrow
    GP = S5_SLAB_GROUPS
    TR = 16
    slot = lax.broadcasted_iota(jnp.int32, (TR, 128), 1) // S5_GROUP

    def token_rows(nt, s):
        return pl.ds(pl.multiple_of(nt * (TR * S5_CHUNK), TR * S5_CHUNK) + s, TR, stride=S5_CHUNK)

    def gather(nt, carry):
        for j in range(2):
            src = [u_ref[0, token_rows(nt, 8 * j + s8), :] for s8 in range(8)]
            for gl in range(GP):
                acc = None
                for s8 in range(8):
                    sh = ((s8 - gl) * S5_GROUP) % 128
                    piece = pltpu.roll(src[s8], sh, 1) if sh else src[s8]
                    acc = piece if acc is None else jnp.where(slot == s8, piece, acc)
                t_scr[gl, pl.ds(pl.multiple_of(nt * TR, TR), TR), j * 128:(j + 1) * 128] = acc.astype(BF16)
        return carry

    lax.fori_loop(0, R // TR, gather, 0)

    n = lax.broadcasted_iota(jnp.int32, (R, P2), 0) % nrow
    row = lax.broadcasted_iota(jnp.int32, (R, P2), 0)
    fwd = lax.broadcasted_iota(jnp.int32, (R, P2), 1) < S5_STATE
    fwd_b = lax.broadcasted_iota(jnp.int32, (nseq, P2), 1) < S5_STATE

    def shift(a, k):
        dn = pltpu.roll(a, k, 0)
        up = pltpu.roll(a, R - k, 0)
        return jnp.where(fwd, jnp.where(n >= k, dn, 0.0), jnp.where(n < nrow - k, up, 0.0))

    def ends(t):
        if nseq == 1:
            return jnp.where(fwd_b, t[nrow - 1:nrow, :], t[0:1, :])
        return jnp.where(fwd_b, t[pl.ds(nrow - 1, nseq, stride=nrow), :], t[pl.ds(0, nseq, stride=nrow), :])

    def per_group(gl, carry):
        ub = t_scr[gl]
        x = _dot(ub, win_ref[gl])
        xr, xi = x[:, :P2], x[:, P2:]
        er, ei = shift(xr, 1), shift(xi, 1)
        if has_state:
            h0 = h0_ref[0, gl]
            for b in range(nseq):
                first = row == jnp.where(fwd, b * nrow, b * nrow + nrow - 1)
                er = jnp.where(first, h0[b:b + 1, :P2], er)
                ei = jnp.where(first, h0[b:b + 1, P2:], ei)
        for j in range(nlev):
            k = 2 ** j
            ar = ar_ref[gl, j:j + 1, :]
            ai = ai_ref[gl, j:j + 1, :]
            sr, si = shift(er, k), shift(ei, k)
            er, ei = er + ar * sr - ai * si, ei + ar * si + ai * sr
        ecat = jnp.concatenate([er, ei], axis=1).astype(BF16)
        y_scr[gl] = _dot(ub, m_ref[gl]) + _dot_nt(ecat, wout_ref[gl])
        a1r = ar_ref[gl, 0:1, :]
        a1i = ai_ref[gl, 0:1, :]
        tmp_r[...] = a1r * er - a1i * ei + xr
        tmp_i[...] = a1r * ei + a1i * er + xi
        last_ref[0, gl, :, :P2] = ends(tmp_r)
        last_ref[0, gl, :, P2:] = ends(tmp_i)
        return carry

    lax.fori_loop(0, GP, per_group, 0)

    def scatter(nt, carry):
        for j in range(2):
            ys = [y_scr[gl, pl.ds(pl.multiple_of(nt * TR, TR), TR), j * 128:(j + 1) * 128] for gl in range(GP)]
            for s8 in range(8):
                acc = None
                for gl in range(GP):
                    sh = ((gl - s8) * S5_GROUP) % 128
                    piece = pltpu.roll(ys[gl], sh, 1) if sh else ys[gl]
                    acc = piece if acc is None else jnp.where(slot == gl, piece, acc)
                rows = token_rows(nt, 8 * j + s8)
                z_ref[0, rows, :] = _gelu_tanh(acc + d_ref[...] * u_ref[0, rows, :])
        return carry

    lax.fori_loop(0, R // TR, scatter, 0)


def _s5_call(u, ops, d, h0, seq):
    m, win, wout, ar, ai = ops
    bm, lm, _ = u.shape
    nrow = seq // S5_CHUNK
    nlev = int(math.log2(nrow))
    total = bm * lm // seq
    nseq = max(1, min(total, S5_BLOCK_ROWS // nrow))
    nblk = total // nseq
    R = nseq * nrow
    tok = R * S5_CHUNK
    GP = S5_SLAB_GROUPS
    ns = BRANCH // 128
    has_state = h0 is not None
    slab = pl.BlockSpec((1, tok, 128), lambda t, i: (i, 0, t))
    op = pl.BlockSpec((GP, S5_ROW, S5_ROW), lambda t, i: (t, 0, 0))
    lev = pl.BlockSpec((GP, ar.shape[1], 2 * S5_STATE), lambda t, i: (t, 0, 0))
    state = pl.BlockSpec((1, GP, nseq, 4 * S5_STATE), lambda t, i: (i, t, 0, 0))
    in_specs = [slab, op, op, op, lev, lev, pl.BlockSpec((1, 128), lambda t, i: (0, t))]
    args = [u.reshape(nblk, tok, BRANCH), m, win, wout, ar, ai, d]
    if has_state:
        in_specs.append(state)
        args.append(h0.reshape(nblk, nseq, S5_GROUPS, 4 * S5_STATE).transpose(0, 2, 1, 3))
    z, last = pl.pallas_call(
        functools.partial(_s5_kernel, nseq=nseq, nrow=nrow, nlev=nlev, has_state=has_state),
        grid=(ns, nblk),
        in_specs=in_specs,
        out_specs=[slab, state],
        out_shape=[jax.ShapeDtypeStruct((nblk, tok, BRANCH), F32),
                   jax.ShapeDtypeStruct((nblk, S5_GROUPS, nseq, 4 * S5_STATE), F32)],
        scratch_shapes=[pltpu.VMEM((GP, R, S5_ROW), BF16), pltpu.VMEM((GP, R, S5_ROW), F32),
                        pltpu.VMEM((R, 2 * S5_STATE), F32), pltpu.VMEM((R, 2 * S5_STATE), F32)],
        compiler_params=_cparams("parallel", "parallel"),
        name="s5",
    )(*args)
    last = last.transpose(0, 2, 1, 3).reshape(total, S5_GROUPS, 4 * S5_STATE)
    return z.reshape(bm, lm, BRANCH), last


HEAD_ORDER = (0, 2, 1, 3)


def _stack_heads(q):
    head = lax.broadcasted_iota(jnp.int32, q.shape, 1) // ATT_HD
    qb = q.astype(BF16)
    zero = jnp.zeros((), BF16)
    return jnp.concatenate([jnp.where(head == h, qb, zero) for h in HEAD_ORDER], axis=0)


def _key_value_operands(kv):
    low = lax.broadcasted_iota(jnp.int32, kv.shape, 1) < ATT_HD
    sw = pltpu.roll(kv, ATT_HD, 1)
    kk = jnp.where(low, kv, sw).astype(BF16)
    return (jnp.concatenate([kk, kk], axis=1),
            jnp.where(low, sw, 1.0).astype(BF16), jnp.where(low, 1.0, kv).astype(BF16))


LOG2E = 1.4426950408889634
Q_SCALE = (ATT_HD ** -0.5) * LOG2E


def _sink_column(sink_ref, rows):
    col = jnp.concatenate([jnp.broadcast_to(sink_ref[0, h:h + 1, 0:1], (rows, 1)) for h in HEAD_ORDER], axis=0)
    return col * LOG2E


def _softmax_pv(scores, values_even, values_odd, sink_col, rows):
    t = None
    for s in scores:
        for c in range(0, s.shape[1], 128):
            t = s[:, c:c + 128] if t is None else jnp.maximum(t, s[:, c:c + 128])
    m = jnp.maximum(jnp.max(t, axis=-1, keepdims=True), sink_col)
    e_sink = jnp.exp2(sink_col - m)
    low = lax.broadcasted_iota(jnp.int32, (2 * rows, 2 * ATT_HD), 1) < ATT_HD
    o_even = jnp.where(low, 0.0, e_sink[:2 * rows])
    o_odd = jnp.where(low, e_sink[2 * rows:], 0.0)
    for s, ve, vo in zip(scores, values_even, values_odd):
        p = jnp.exp2(s - m).astype(BF16)
        o_even = o_even + _dot(p[:2 * rows], ve)
        o_odd = o_odd + _dot(p[2 * rows:], vo)
    o_even = o_even / pltpu.roll(o_even, ATT_HD, 1)
    o_odd = o_odd / pltpu.roll(o_odd, ATT_HD, 1)
    pair = jnp.where(low, o_even, o_odd)
    return jnp.concatenate([pair[:rows], pair[rows:]], axis=1)


def _att_ctx_kernel(q_ref, kv_ref, sa_ref, sink_ref, y_ref, kc_ref, vc_ref):
    seq = q_ref.shape[1]
    kv = kv_ref[0]
    kc_ref[0, 0] = kv[:, :ATT_HD]
    vc_ref[0, 0] = kv[:, ATT_HD:]
    k4, va, vb = _key_value_operands(kv)
    s = _dot_nt(_stack_heads(q_ref[0]), k4)
    out = _softmax_pv([s], [va], [vb], _sink_column(sink_ref, seq), seq)
    y_ref[0] = (out * sa_ref[0].astype(F32)).astype(BF16)


def _att_ctx_call(q, kv, sa, sink):
    bsz, seq, _ = q.shape
    W = ATT_GROUPS * ATT_HD
    per_kv = lambda b, k: (b, 0, k)
    cache_spec = pl.BlockSpec((1, 1, seq, ATT_HD), lambda b, k: (b, k, 0, 0))
    return pl.pallas_call(
        _att_ctx_kernel,
        grid=(bsz, ATT_KV_HEADS),
        in_specs=[
            pl.BlockSpec((1, seq, W), per_kv),
            pl.BlockSpec((1, seq, 2 * ATT_HD), per_kv),
            pl.BlockSpec((1, seq, W), per_kv),
            pl.BlockSpec((1, 8, 128), lambda b, k: (k, 0, 0)),
        ],
        out_specs=[pl.BlockSpec((1, seq, W), per_kv), cache_spec, cache_spec],
        out_shape=[jax.ShapeDtypeStruct((bsz, seq, BRANCH), BF16),
                   jax.ShapeDtypeStruct((bsz, ATT_KV_HEADS, seq, ATT_HD), F32),
                   jax.ShapeDtypeStruct((bsz, ATT_KV_HEADS, seq, ATT_HD), F32)],
        compiler_params=_cparams("parallel", "parallel"),
        name="att_ctx",
    )(q, kv, sa, sink)


def _att_lat_kernel(q_ref, kv_ref, sa_ref, sink_ref, kc_ref, vc_ref, y_ref, k4_scr, va_scr, vb_scr, *, seq):
    nb = seq // QBLK
    KW = 3 * QBLK

    def k_body(i, carry):
        r = pl.ds(pl.multiple_of(i * QBLK, QBLK), QBLK)
        k4, va, vb = _key_value_operands(kv_ref[0, r, :])
        k4_scr[r, :] = k4
        va_scr[r, :] = va
        vb_scr[r, :] = vb
        return carry

    lax.fori_loop(0, nb, k_body, 0)
    kc = kc_ref[0, 0, 0]
    vc = vc_ref[0, 0, 0]
    ones = jnp.ones_like(vc)
    kc4 = jnp.concatenate([kc] * ATT_GROUPS, axis=1).astype(BF16)
    vca = jnp.concatenate([vc, ones], axis=1).astype(BF16)
    vcb = jnp.concatenate([ones, vc], axis=1).astype(BF16)
    sink_col = _sink_column(sink_ref, QBLK)
    qi = lax.broadcasted_iota(jnp.int32, (QBLK, KW), 0)
    kj = lax.broadcasted_iota(jnp.int32, (QBLK, KW), 1)

    def q_body(i, carry):
        r = pl.ds(pl.multiple_of(i * QBLK, QBLK), QBLK)
        ws = pl.multiple_of(jnp.clip(i * QBLK - QBLK, 0, seq - KW), QBLK)
        win = pl.ds(ws, KW)
        bias = jnp.where(jnp.abs((i * QBLK + qi) - (ws + kj)) <= WINDOW, 0.0, NEG)
        qs = _stack_heads(q_ref[0, r, :])
        s_win = _dot_nt(qs, k4_scr[win, :]) + jnp.concatenate([bias] * ATT_GROUPS, axis=0)
        s_ctx = _dot_nt(qs, kc4)
        out = _softmax_pv([s_win, s_ctx], [va_scr[win, :], vca], [vb_scr[win, :], vcb], sink_col, QBLK)
        y_ref[0, r, :] = (out * sa_ref[0, r, :].astype(F32)).astype(BF16)
        return carry

    lax.fori_loop(0, nb, q_body, 0, unroll=2)


def _rope_tables(seq, heads, pad_heads):
    pos = jnp.arange(seq)
    row = (pos // GRID_W).astype(F32)[:, None]
    col = (pos % GRID_W).astype(F32)[:, None]
    nf = ATT_HD // 4
    freq = ROPE_BASE ** (-jnp.arange(nf, dtype=F32) / nf)
    ar, ac = row * freq, col * freq
    cos = jnp.concatenate([jnp.cos(ar), jnp.cos(ar), jnp.cos(ac), jnp.cos(ac)], axis=1)
    sin = jnp.concatenate([-jnp.sin(ar), jnp.sin(ar), -jnp.sin(ac), jnp.sin(ac)], axis=1)
    cos = jnp.concatenate([jnp.tile(cos, (1, heads)), jnp.ones((seq, pad_heads * ATT_HD), F32)], axis=1)
    sin = jnp.concatenate([jnp.tile(sin, (1, heads)), jnp.zeros((seq, pad_heads * ATT_HD), F32)], axis=1)
    return cos, sin


def _att_lat_call(q, kv, sa, sink, cache_k, cache_v, e):
    bsz, seq, _ = q.shape
    past = cache_k.shape[3]
    W = ATT_GROUPS * ATT_HD
    per_kv = lambda b, k: (b, 0, k)
    cache_spec = pl.BlockSpec((1, 1, 1, past, ATT_HD), lambda b, k: (b, e, k, 0, 0))
    return pl.pallas_call(
        functools.partial(_att_lat_kernel, seq=seq),
        grid=(bsz, ATT_KV_HEADS),
        in_specs=[
            pl.BlockSpec((1, seq, W), per_kv),
            pl.BlockSpec((1, seq, 2 * ATT_HD), per_kv),
            pl.BlockSpec((1, seq, W), per_kv),
            pl.BlockSpec((1, 8, 128), lambda b, k: (k, 0, 0)),
            cache_spec,
            cache_spec,
        ],
        out_specs=pl.BlockSpec((1, seq, W), per_kv),
        out_shape=jax.ShapeDtypeStruct((bsz, seq, BRANCH), BF16),
        scratch_shapes=[pltpu.VMEM((seq, W), BF16), pltpu.VMEM((seq, 2 * ATT_HD), BF16),
                        pltpu.VMEM((seq, 2 * ATT_HD), BF16)],
        compiler_params=_cparams("parallel", "parallel"),
        name="att_lat",
    )(q, kv, sa, sink, cache_k, cache_v)


def _even_out_kernel(x_ref, mod_ref, yg_ref, z_ref, ss_ref, wg_ref, bg_ref, wo_ref, o_ref):
    z = z_ref[0]
    glu = z * jax.nn.sigmoid(_dot(z.astype(BF16), wg_ref[...]) + bg_ref[...])
    ys = (glu * ss_ref[0].astype(F32)).astype(BF16)
    out = _dot(yg_ref[0], wo_ref[:BRANCH, :]) + _dot(ys, wo_ref[BRANCH:, :])
    o_ref[0] = x_ref[0] + mod_ref[0, 2:3, :] * out


def _even_out_call(x, mod, yg, z, ss, wglu, bglu, wout):
    bm, lm, _ = x.shape
    tm = min(ROW_TILE, lm)
    tile = lambda n: pl.BlockSpec((1, tm, n), lambda b, i: (b, i, 0))
    const = lambda b, i: (0, 0)
    return pl.pallas_call(
        _even_out_kernel,
        grid=(bm, lm // tm),
        in_specs=[
            tile(D_MODEL),
            pl.BlockSpec((1, 3, D_MODEL), lambda b, i: (b, 0, 0)),
            tile(BRANCH), tile(BRANCH), tile(BRANCH),
            pl.BlockSpec((BRANCH, BRANCH), const),
            pl.BlockSpec((1, BRANCH), const),
            pl.BlockSpec((D_MODEL, D_MODEL), const),
        ],
        out_specs=tile(D_MODEL),
        out_shape=jax.ShapeDtypeStruct((bm, lm, D_MODEL), F32),
        compiler_params=_cparams("parallel", "parallel"),
        name="even_out",
    )(x, mod, yg, z, ss, wglu, bglu, wout)


def _odd_out_kernel(x_ref, mod_ref, ya_ref, p_ref, pprev_ref, pnext_ref, bgz_ref, cw_ref, cb_ref, wo_ref,
                    o_ref, *, seq, tm):
    p = p_ref[0].astype(F32)
    pos = (pl.program_id(1) * tm + lax.broadcasted_iota(jnp.int32, (tm, BRANCH), 0)) % seq
    rowi = lax.broadcasted_iota(jnp.int32, (tm, BRANCH), 0)
    halo = CONV_HALO_ROWS
    p_before = pprev_ref[0].astype(F32)[halo - 1:halo, :]
    p_after = pnext_ref[0].astype(F32)[0:1, :]
    prev = jnp.where(rowi == 0, p_before, pltpu.roll(p, 1, 0))
    nxt = jnp.where(rowi == tm - 1, p_after, pltpu.roll(p, tm - 1, 0))
    prev = jnp.where(pos == 0, 0.0, prev)
    nxt = jnp.where(pos == seq - 1, 0.0, nxt)
    conv = prev * cw_ref[0:1, :] + p * cw_ref[1:2, :] + nxt * cw_ref[2:3, :] + cb_ref[...]
    yc = (bgz_ref[0].astype(F32) * conv).astype(BF16)
    out = _dot(ya_ref[0], wo_ref[:BRANCH, :]) + _dot(yc, wo_ref[BRANCH:, :])
    o_ref[0] = x_ref[0] + mod_ref[0, 2:3, :] * out


def _odd_out_call(x, mod, ya, p, bgz, convw, convb, wout, seq):
    bm, lm, _ = x.shape
    tm = min(ROW_TILE, lm)
    hr = CONV_HALO_ROWS
    nth = lm // hr
    tile = lambda n: pl.BlockSpec((1, tm, n), lambda b, i: (b, i, 0))
    const = lambda b, i: (0, 0)
    prev_spec = pl.BlockSpec((1, hr, BRANCH), lambda b, i: (b, jnp.maximum(i * (tm // hr) - 1, 0), 0))
    next_spec = pl.BlockSpec((1, hr, BRANCH), lambda b, i: (b, jnp.minimum((i + 1) * (tm // hr), nth - 1), 0))
    return pl.pallas_call(
        functools.partial(_odd_out_kernel, seq=seq, tm=tm),
        grid=(bm, lm // tm),
        in_specs=[
            tile(D_MODEL),
            pl.BlockSpec((1, 3, D_MODEL), lambda b, i: (b, 0, 0)),
            tile(BRANCH), tile(BRANCH), prev_spec, next_spec, tile(BRANCH),
            pl.BlockSpec((CONV_W, BRANCH), const),
            pl.BlockSpec((1, BRANCH), const),
            pl.BlockSpec((D_MODEL, D_MODEL), const),
        ],
        out_specs=tile(D_MODEL),
        out_shape=jax.ShapeDtypeStruct((bm, lm, D_MODEL), F32),
        compiler_params=_cparams("parallel", "parallel"),
        name="odd_out",
    )(x, mod, ya, p, p, p, bgz, convw, convb, wout)


def _even_in_weight(w):
    dk = GLA_HEADS * GLA_DK
    q, k = w[:, :dk], w[:, dk:2 * dk]
    v = w[:, 2 * dk:2 * dk + BRANCH]
    o = 2 * dk + BRANCH
    lr = w[:, o:o + 2 * GLA_RANK]
    rest = w[:, o + 2 * GLA_RANK:]
    qk = jnp.concatenate([jnp.concatenate([q[:, h * GLA_DK:(h + 1) * GLA_DK], k[:, h * GLA_DK:(h + 1) * GLA_DK]],
                                          axis=1) for h in range(GLA_HEADS)], axis=1)
    lr = jnp.pad(lr, ((0, 0), (0, 128 - 2 * GLA_RANK)))
    return jnp.concatenate([qk, v, lr, rest], axis=1).astype(BF16)


def _odd_in_weight(w):
    q = w[:, :BRANCH]
    k = w[:, BRANCH:BRANCH + ATT_KV_HEADS * ATT_HD]
    v = w[:, BRANCH + ATT_KV_HEADS * ATT_HD:BRANCH + 2 * ATT_KV_HEADS * ATT_HD]
    rest = w[:, BRANCH + 2 * ATT_KV_HEADS * ATT_HD:]
    kv = jnp.concatenate([jnp.concatenate([k[:, h * ATT_HD:(h + 1) * ATT_HD], v[:, h * ATT_HD:(h + 1) * ATT_HD]],
                                          axis=1) for h in range(ATT_KV_HEADS)], axis=1)
    return jnp.concatenate([q, kv, rest], axis=1).astype(BF16)


def _gate_weights(w2, b2):
    zf = jnp.zeros((GLA_RANK, GLA_DK), F32)
    cols, bias = [], []
    for h in range(GLA_HEADS):
        sl = slice(h * GLA_DK, (h + 1) * GLA_DK)
        cols.append(jnp.concatenate([jnp.concatenate([w2[0][:, sl], zf], axis=1),
                                     jnp.concatenate([zf, w2[1][:, sl]], axis=1)], axis=0))
        bias += [b2[0][sl], b2[1][sl]]
    w = jnp.pad(jnp.concatenate(cols, axis=1), ((0, 128 - 2 * GLA_RANK), (0, 0)))
    w_hi = w.astype(BF16)
    w_lo = (w - w_hi.astype(F32)).astype(BF16)
    return w_hi, w_lo, jnp.concatenate(bias)[None, :]


def _even_layer(x, mod, nw, w_in, seq, gla_w, s5_ops, dgm, wglu, bglu, wout, onorm, gla_s0, s5_h0):
    bm, lm, _ = x.shape
    nseq = bm * lm // seq
    qk, v, bc, sg, u, ss = _in_call(_even_in_kernel, "even_in", x, mod, nw, w_in, gla_w,
                                    [(BRANCH, dt) for dt in (BF16, BF16, F32, BF16, F32, BF16)])
    per_seq = lambda a: a.reshape(nseq, seq, a.shape[-1])
    yg, s_new = _gla_call(per_seq(qk), per_seq(v), per_seq(bc), per_seq(sg), onorm, gla_s0)
    z, last = _s5_call(u, s5_ops, dgm, s5_h0, seq)
    x_new = _even_out_call(x, mod, yg.reshape(bm, lm, BRANCH), z, ss, wglu, bglu, wout)
    return x_new, s_new, last


def _odd_layer(x, mod, nw, w_in, seq, qnw, knw, sink, convw, convb, wout, cache_k, cache_v, e):
    bm, lm, _ = x.shape
    nseq = bm * lm // seq
    latent = cache_k is not None
    tables = ()
    if latent:
        tables = _rope_tables(seq, 2, 0) + _rope_tables(seq, 1, 1)
    q, kv, sa, p, bgz = _in_call(functools.partial(_odd_in_kernel, rope=latent), "odd_in", x, mod, nw, w_in,
                                 (qnw, knw),
                                 [(BRANCH, BF16), (2 * ATT_KV_HEADS * ATT_HD, F32), (BRANCH, BF16),
                                  (BRANCH, BF16), (BRANCH, BF16)], row_tables=tables)
    per_seq = lambda a: a.reshape(nseq, seq, a.shape[-1])
    if latent:
        ya = _att_lat_call(per_seq(q), per_seq(kv), per_seq(sa), sink, cache_k, cache_v, e)
        kc = vc = None
    else:
        ya, kc, vc = _att_ctx_call(per_seq(q), per_seq(kv), per_seq(sa), sink)
    x_new = _odd_out_call(x, mod, ya.reshape(bm, lm, BRANCH), p, bgz, convw, convb, wout, seq)
    return x_new, kc, vc


def kernel(x_prompt, x_sample, c, state_gla, state_s5_re, state_s5_im, cache_k, cache_v, c_ctx, norm_w, w_ada, b_ada, w_in_e, w_out_e, gla_w2, gla_b2, gla_onorm, s5_lam_re, s5_lam_im, s5_log_dt, s5_b_re, s5_b_im, s5_c_re, s5_c_im, s5_d, s5_w_glu, s5_b_glu, w_in_o, w_out_o, q_norm_w, k_norm_w, sink, conv_w, conv_b):
    depth = norm_w.shape[0]
    bp, lp, _ = x_prompt.shape
    bs, ls, _ = x_sample.shape

    rows = 8 * ((1 + bs + 7) // 8)
    cs = jnp.zeros((rows, D_MODEL), F32).at[0].set(c_ctx).at[1:1 + bs].set(c)
    mods = _ada_call(cs, w_ada, b_ada)

    xp = x_prompt.reshape(1, bp * lp, D_MODEL)
    xs = x_sample
    new_gla, new_re, new_im, new_k, new_v = [], [], [], [], []
    for l in range(depth):
        e = l // 2
        mod_p = mods[l, 0:1].reshape(1, 3, D_MODEL)
        mod_s = mods[l, 1:1 + bs].reshape(bs, 3, D_MODEL)
        nw = norm_w[l][None, :]
        if l % 2 == 0:
            w_in = _even_in_weight(w_in_e[e])
            wout = w_out_e[e].astype(BF16)
            gla_w = _gate_weights(gla_w2[e], gla_b2[e])
            onorm = gla_onorm[e][None, :]
            dgm = s5_d[e][None, :]
            wglu = s5_w_glu[e].astype(BF16)
            bglu = s5_b_glu[e][None, :]
            s5_ops = _s5_operators(s5_lam_re[e], s5_lam_im[e], s5_log_dt[e], s5_b_re[e], s5_b_im[e],
                                   s5_c_re[e], s5_c_im[e], nlev=int(math.log2(max(lp, ls) // S5_CHUNK)))
            h0 = jnp.concatenate([state_s5_re[:, e], state_s5_im[:, e]], axis=1)
            h0 = h0.transpose(0, 2, 1, 3).reshape(bs, S5_GROUPS, 4 * S5_STATE)
            xp, sg, last = _even_layer(xp, mod_p, nw, w_in, lp, gla_w, s5_ops, dgm, wglu, bglu, wout, onorm,
                                       None, None)
            xs, _, _ = _even_layer(xs, mod_s, nw, w_in, ls, gla_w, s5_ops, dgm, wglu, bglu, wout, onorm,
                                   state_gla[:, e], h0)
            new_gla.append(sg)
            last = last.reshape(bp, S5_GROUPS, 2, 2, S5_STATE).transpose(2, 0, 3, 1, 4)
            new_re.append(last[0])
            new_im.append(last[1])
        else:
            w_in = _odd_in_weight(w_in_o[e])
            wout = w_out_o[e].astype(BF16)
            qnw = jnp.tile(q_norm_w[e], ATT_HEADS)[None, :]
            knw = jnp.tile(jnp.concatenate([k_norm_w[e], jnp.ones((ATT_HD,), F32)]), ATT_KV_HEADS)[None, :]
            sk = jnp.broadcast_to(sink[e].reshape(ATT_KV_HEADS, ATT_GROUPS, 1), (ATT_KV_HEADS, ATT_GROUPS, 128))
            sk = jnp.concatenate([sk, jnp.zeros((ATT_KV_HEADS, 8 - ATT_GROUPS, 128), F32)], axis=1)
            cb = conv_b[e][None, :]
            xp, kc, vc = _odd_layer(xp, mod_p, nw, w_in, lp, qnw, knw, sk, conv_w[e], cb, wout, None, None, e)
            xs, _, _ = _odd_layer(xs, mod_s, nw, w_in, ls, qnw, knw, sk, conv_w[e], cb, wout, cache_k, cache_v, e)
            new_k.append(kc)
            new_v.append(vc)
    return (xp.reshape(bp, lp, D_MODEL), xs,
            jnp.stack(new_gla, axis=1), jnp.stack(new_re, axis=1), jnp.stack(new_im, axis=1),
            jnp.stack(new_k, axis=1), jnp.stack(new_v, axis=1))
```

```python
import functools
import math

import jax
import jax.numpy as jnp
from jax import lax
from jax.experimental import pallas as pl
from jax.experimental.pallas import tpu as pltpu

F32 = jnp.float32
BF16 = jnp.bfloat16
HIGHEST = lax.Precision.HIGHEST

D_MODEL = 1024
BRANCH = D_MODEL // 2
GLA_HEADS = 4
GLA_DK = 64
GLA_DV = 128
GLA_RANK = 16
GLA_GATE_NORM = 16.0
GLA_CHUNK = 64
GLA_BLOCK = 256
S5_GROUP = 16
S5_GROUPS = BRANCH // S5_GROUP
S5_STATE = 64
S5_CHUNK = 16
S5_ROW = S5_CHUNK * S5_GROUP
S5_SLAB_GROUPS = 128 // S5_GROUP
S5_BLOCK_ROWS = 256
ATT_HEADS = 8
ATT_KV_HEADS = 2
ATT_GROUPS = ATT_HEADS // ATT_KV_HEADS
ATT_HD = 64
WINDOW = 128
QBLK = 128
GRID_W = 64
ROPE_BASE = 10000.0
CONV_W = 3
CONV_HALO_ROWS = 16
EPS = 1e-6
NEG = -1e30

ROW_TILE = 512
VMEM_LIMIT = 48 * 1024 * 1024


def _cparams(*sem):
    return pltpu.CompilerParams(dimension_semantics=sem, vmem_limit_bytes=VMEM_LIMIT)


def _silu(x):
    return x * jax.nn.sigmoid(x)


def _dot(a, b):
    return jnp.dot(a, b, preferred_element_type=F32)


def _dot_nt(a, b):
    return lax.dot_general(a, b, (((1,), (1,)), ((), ())), preferred_element_type=F32)


def _ada_kernel(c_ref, w_ref, b_ref, o_ref):
    c = c_ref[...]
    o_ref[0] = jnp.dot(_silu(c), w_ref[0], precision=HIGHEST, preferred_element_type=F32) + b_ref[0]


def _ada_call(cs, w_ada, b_ada):
    depth = w_ada.shape[0]
    rows = cs.shape[0]
    nt = 3
    return pl.pallas_call(
        _ada_kernel,
        grid=(depth, nt),
        in_specs=[
            pl.BlockSpec((rows, D_MODEL), lambda l, j: (0, 0)),
            pl.BlockSpec((1, D_MODEL, D_MODEL), lambda l, j: (l, 0, j)),
            pl.BlockSpec((1, 1, D_MODEL), lambda l, j: (l, 0, j)),
        ],
        out_specs=pl.BlockSpec((1, rows, D_MODEL), lambda l, j: (l, 0, j)),
        out_shape=jax.ShapeDtypeStruct((depth, rows, 3 * D_MODEL), F32),
        compiler_params=_cparams("arbitrary", "arbitrary"),
        name="adaln",
    )(cs, w_ada, b_ada.reshape(depth, 1, 3 * D_MODEL))


def _modulated(x, mod_ref, nw_ref):
    ms = jnp.mean(x * x, axis=-1, keepdims=True)
    y = x * lax.rsqrt(ms + EPS) * nw_ref[...]
    shift = mod_ref[0, 0:1, :]
    scale = mod_ref[0, 1:2, :]
    return (y * (1.0 + scale) + shift).astype(BF16)


E_COLS = (0, 512, 1024, 1152, 1664, 2176, 2688)


def _log_sigmoid(x):
    return jnp.minimum(x, 0.0) - jnp.log(1.0 + jnp.exp(-jnp.abs(x)))


def _chunk_cumsum(g):
    rows, width = g.shape
    pos = lax.broadcasted_iota(jnp.int32, g.shape, 0) % GLA_CHUNK
    fwd = (lax.broadcasted_iota(jnp.int32, g.shape, 1) % (2 * GLA_DK)) < GLA_DK
    c = g
    k = 1
    while k < GLA_CHUNK:
        c = c + jnp.where(pos >= k, pltpu.roll(c, k, 0), 0.0)
        k *= 2
    total = jnp.concatenate([jnp.broadcast_to(c[r + GLA_CHUNK - 1:r + GLA_CHUNK, :], (GLA_CHUNK, width))
                             for r in range(0, rows, GLA_CHUNK)], axis=0)
    return jnp.where(fwd, c, total - c + g)


def _even_in_compute(x, refs, outs):
    mod_ref, nw_ref, w_ref, w2h_ref, w2l_ref, b2_ref = refs
    qk_ref, v_ref, bc_ref, sg_ref, u_ref, ss_ref = outs
    h = _modulated(x, mod_ref, nw_ref)
    c = E_COLS
    lr = _dot(h, w_ref[:, c[2]:c[3]])
    lr_hi = lr.astype(BF16)
    lr_lo = (lr - lr_hi.astype(F32)).astype(BF16)
    pre = _dot(lr_hi, w2h_ref[...]) + (_dot(lr_lo, w2h_ref[...]) + _dot(lr_hi, w2l_ref[...])) + b2_ref[...]
    bc_ref[0] = _chunk_cumsum(_log_sigmoid(pre) * (1.0 / GLA_GATE_NORM))
    qk_ref[0] = _dot(h, w_ref[:, c[0]:c[1]]).astype(BF16)
    v_ref[0] = _dot(h, w_ref[:, c[1]:c[2]]).astype(BF16)
    sg_ref[0] = _silu(_dot(h, w_ref[:, c[3]:c[4]])).astype(BF16)
    u_ref[0] = _dot(h, w_ref[:, c[4]:c[5]])
    ss_ref[0] = _silu(_dot(h, w_ref[:, c[5]:c[6]])).astype(BF16)


O_COLS = (0, 512, 768, 1280, 1792, 2304, 2816, 3328)


def _rope(x, cos, sin):
    width = x.shape[1]
    lane = lax.broadcasted_iota(jnp.int32, x.shape, 1)
    partner = jnp.where((lane & 16) == 0, pltpu.roll(x, width - 16, 1), pltpu.roll(x, 16, 1))
    return x * cos + partner * sin


def _head_pair_norm(x, w, cos, sin, both):
    low = lax.broadcasted_iota(jnp.int32, x.shape, 1) < ATT_HD
    x2 = x * x
    ms = jnp.sum(jnp.where(low, x2, 0.0), axis=-1, keepdims=True) * (1.0 / ATT_HD)
    rs = lax.rsqrt(ms + EPS)
    if both:
        ms_hi = jnp.sum(jnp.where(low, 0.0, x2), axis=-1, keepdims=True) * (1.0 / ATT_HD)
        rs = jnp.where(low, rs, lax.rsqrt(ms_hi + EPS))
    else:
        rs = jnp.where(low, rs, 1.0)
    y = x * rs * w
    return y if cos is None else _rope(y, cos, sin)


def _odd_in_compute(x, refs, outs, rope):
    if rope:
        mod_ref, nw_ref, w_ref, qn_ref, kn_ref, cq_ref, sq_ref, ck_ref, sk_ref = refs
        cq, sq, ck, sk = cq_ref[...], sq_ref[...], ck_ref[...], sk_ref[...]
    else:
        mod_ref, nw_ref, w_ref, qn_ref, kn_ref = refs
        cq = sq = ck = sk = None
    q_ref, kv_ref, sa_ref, p_ref, bgz_ref = outs
    h = _modulated(x, mod_ref, nw_ref)
    c = O_COLS
    q = _dot(h, w_ref[:, c[0]:c[1]])
    for t in range(BRANCH // 128):
        lanes = slice(t * 128, (t + 1) * 128)
        qt = _head_pair_norm(q[:, lanes], qn_ref[:, lanes], cq, sq, True)
        q_ref[0, :, lanes] = (qt * Q_SCALE).astype(BF16)
    kv = _dot(h, w_ref[:, c[1]:c[2]])
    for t in range(ATT_KV_HEADS):
        lanes = slice(t * 128, (t + 1) * 128)
        kv_ref[0, :, lanes] = _head_pair_norm(kv[:, lanes], kn_ref[:, lanes], ck, sk, False)
    sa_ref[0] = _silu(_dot(h, w_ref[:, c[2]:c[3]])).astype(BF16)
    xc = _dot(h, w_ref[:, c[3]:c[4]])
    bg = _dot(h, w_ref[:, c[4]:c[5]])
    cg = _dot(h, w_ref[:, c[5]:c[6]])
    zc = _dot(h, w_ref[:, c[6]:c[7]])
    p_ref[0] = (cg * xc).astype(BF16)
    bgz_ref[0] = (bg * _silu(zc)).astype(BF16)


def _tile_spec(tm, n):
    return pl.BlockSpec((1, tm, n), lambda b, i: (b, i, 0))


def _whole_spec(a):
    return pl.BlockSpec(a.shape, lambda b, i: (0,) * a.ndim)


def _mod_spec():
    return pl.BlockSpec((1, 3, D_MODEL), lambda b, i: (b, 0, 0))


def _even_in_part(tm, mod, nw, w_in, gla_w):
    args = [mod, nw, w_in, *gla_w]
    specs = [_mod_spec()] + [_whole_spec(a) for a in args[1:]]
    outs = [(BRANCH, dt) for dt in (BF16, BF16, F32, BF16, F32, BF16)]
    return _even_in_compute, args, specs, outs


def _odd_in_part(tm, mod, nw, w_in, qnw, knw, tables):
    args = [mod, nw, w_in, qnw, knw, *tables]
    specs = ([_mod_spec()] + [_whole_spec(a) for a in args[1:5]]
             + [pl.BlockSpec((tm, a.shape[1]), lambda b, i: (i, 0)) for a in tables])
    outs = [(BRANCH, BF16), (2 * ATT_KV_HEADS * ATT_HD, F32), (BRANCH, BF16), (BRANCH, BF16), (BRANCH, BF16)]
    return functools.partial(_odd_in_compute, rope=bool(tables)), args, specs, outs


def _stage_kernel(*refs, n_out_args, n_in_args, out_fn, in_fn):
    x_ref = refs[0]
    out_args = refs[1:1 + n_out_args]
    in_args = refs[1 + n_out_args:1 + n_out_args + n_in_args]
    outs = refs[1 + n_out_args + n_in_args:]
    x = x_ref[0]
    if out_fn is not None:
        x = out_fn(x, out_args)
        outs[0][0] = x
        outs = outs[1:]
    if in_fn is not None:
        in_fn(x, in_args, outs)


def _stage_call(name, x, out_part, in_part):
    bm, lm, _ = x.shape
    tm = min(ROW_TILE, lm)
    args, specs, out_defs = [x], [_tile_spec(tm, D_MODEL)], []
    out_fn = in_fn = None
    n_out_args = n_in_args = 0
    if out_part is not None:
        out_fn, a, s = out_part
        args, specs, n_out_args = args + a, specs + s, len(a)
        out_defs.append((D_MODEL, F32))
    if in_part is not None:
        in_fn, a, s, o = in_part
        args, specs, n_in_args = args + a, specs + s, len(a)
        out_defs += o
    res = pl.pallas_call(
        functools.partial(_stage_kernel, n_out_args=n_out_args, n_in_args=n_in_args, out_fn=out_fn, in_fn=in_fn),
        grid=(bm, lm // tm),
        in_specs=specs,
        out_specs=[_tile_spec(tm, n) for n, _ in out_defs],
        out_shape=[jax.ShapeDtypeStruct((bm, lm, n), dt) for n, dt in out_defs],
        compiler_params=_cparams("parallel", "parallel"),
        name=name,
    )(*args)
    return res


def _gla_kernel(*refs, seq, has_state):
    if has_state:
        (qk_ref, v_ref, bc_ref, sg_ref, on_ref, s0_ref, y_ref, sn_ref,
         ut_scr, sf_scr, sb_scr, qd_scr, o_scr) = refs
    else:
        (qk_ref, v_ref, bc_ref, sg_ref, on_ref, y_ref, sn_ref,
         ut_scr, sf_scr, sb_scr, qd_scr, o_scr) = refs
        s0_ref = None
    C = GLA_CHUNK
    W = 2 * GLA_DK
    R = GLA_BLOCK
    CB = R // C
    nc = seq // C
    nb = seq // R
    ri = lax.broadcasted_iota(jnp.int32, (R, R), 0)
    ci = lax.broadcasted_iota(jnp.int32, (R, R), 1)
    same = (ri // C) == (ci // C)
    tril = same & (ri >= ci)
    triu = same & (ri <= ci)
    fwd = lax.broadcasted_iota(jnp.int32, (R, W), 1) < GLA_DK
    fwd_row = lax.broadcasted_iota(jnp.int32, (1, W), 1) < GLA_DK
    fwd_st = lax.broadcasted_iota(jnp.int32, (GLA_DV, W), 1) < GLA_DK
    row_chunk = lax.broadcasted_iota(jnp.int32, (R, W), 0) // C
    scale = GLA_DK ** -0.5

    def rows(c):
        return pl.ds(pl.multiple_of(c * C, C), C)

    def block_rows(j):
        return pl.ds(pl.multiple_of(j * R, R), R)

    def state_rows(c):
        return pl.ds(pl.multiple_of(c * GLA_DV, GLA_DV), GLA_DV)

    def intra(j, carry):
        r = block_rows(j)
        qk = qk_ref[0, r, :].astype(F32)
        bc = bc_ref[0, r, :]
        v = v_ref[0, r, :].astype(F32)
        sw = pltpu.roll(qk, GLA_DK, 1)
        q2 = jnp.where(fwd, qk, sw) * scale
        k2 = jnp.where(fwd, sw, qk)
        b_mid = bc[C // 2:C // 2 + 1, :]
        b_end = jnp.where(fwd_row, bc[C - 1:C, :], bc[0:1, :])
        for c in range(1, CB):
            o0 = c * C
            b_mid = jnp.where(row_chunk == c, bc[o0 + C // 2:o0 + C // 2 + 1, :], b_mid)
            b_end = jnp.where(row_chunk == c, jnp.where(fwd_row, bc[o0 + C - 1:o0 + C, :], bc[o0:o0 + 1, :]), b_end)
        qd = (q2 * jnp.exp(bc)).astype(BF16)
        qh = q2 * jnp.exp(bc - b_mid)
        kh = (k2 * jnp.exp(b_mid - bc)).astype(BF16)
        att = (jnp.where(tril, _dot_nt(jnp.where(fwd, qh, 0.0).astype(BF16), kh), 0.0)
               + jnp.where(triu, _dot_nt(jnp.where(fwd, 0.0, qh).astype(BF16), kh), 0.0))
        o = _dot(att.astype(BF16), v.astype(BF16))
        kd = k2 * jnp.exp(b_end - bc)
        kd_wide = jnp.concatenate([jnp.where(row_chunk == c, kd, 0.0) for c in range(CB)], axis=1)
        ut = _dot(v.T.astype(BF16), kd_wide.astype(BF16))
        qd_scr[r, :] = qd
        o_scr[r, :] = o
        ut_scr[j] = ut
        return carry

    lax.fori_loop(0, nb, intra, 0, unroll=2)

    def scan(j, st):
        jb = nb - 1 - j
        bcf = bc_ref[0, block_rows(j), :]
        bcb = bc_ref[0, block_rows(jb), :]
        utf = ut_scr[j]
        utb = ut_scr[jb]
        for c in range(CB):
            cb = CB - 1 - c
            stb = st.astype(BF16)
            sf_scr[j, :, c * W:(c + 1) * W] = stb
            sb_scr[jb, :, cb * W:(cb + 1) * W] = stb
            b_end = jnp.where(fwd_row, bcf[c * C + C - 1:c * C + C, :], bcb[cb * C:cb * C + 1, :])
            inc = jnp.where(fwd_st, utf[:, c * W:(c + 1) * W], utb[:, cb * W:(cb + 1) * W])
            st = st * jnp.exp(b_end) + inc
        return st

    if has_state:
        init = jnp.concatenate([s0_ref[0, 0, 0].T, s0_ref[0, 1, 0].T], axis=1)
    else:
        init = jnp.zeros((GLA_DV, W), F32)
    st = lax.fori_loop(0, nb, scan, init)
    sn_ref[0, 0, 0] = st[:, :GLA_DK].T
    sn_ref[0, 1, 0] = st[:, GLA_DK:].T

    fwd_wide = (lax.broadcasted_iota(jnp.int32, (GLA_DV, CB * W), 1) % W) < GLA_DK

    def finish(j, carry):
        r = block_rows(j)
        qd = qd_scr[r, :]
        zero = jnp.zeros((), BF16)
        qd_wide = jnp.concatenate([jnp.where(row_chunk == c, qd, zero) for c in range(CB)], axis=1)
        s_in = jnp.where(fwd_wide, sf_scr[j], sb_scr[j])
        o = o_scr[r, :] + _dot_nt(qd_wide, s_in)
        ms = jnp.mean(o * o, axis=-1, keepdims=True)
        y_ref[0, r, :] = (o * lax.rsqrt(ms + EPS) * on_ref[...] * sg_ref[0, r, :].astype(F32)).astype(BF16)
        return carry

    lax.fori_loop(0, nb, finish, 0, unroll=2)


def _gla_call(qk, v, bc, sg, onorm, s0):
    bsz, seq, _ = qk.shape
    has_state = s0 is not None
    nb = seq // GLA_BLOCK
    wide = (GLA_BLOCK // GLA_CHUNK) * 2 * GLA_DK
    per_head =pl.BlockSpec((1, seq, 128), lambda b, h: (b, 0, h))
    in_specs = [per_head, per_head, per_head, per_head, pl.BlockSpec((1, GLA_DV), lambda b, h: (0, 0))]
    args = [qk, v, bc, sg, onorm]
    state_spec = pl.BlockSpec((1, 2, 1, GLA_DK, GLA_DV), lambda b, h: (b, 0, h, 0, 0))
    if has_state:
        in_specs.append(state_spec)
        args.append(s0)
    return pl.pallas_call(
        functools.partial(_gla_kernel, seq=seq, has_state=has_state),
        grid=(bsz, GLA_HEADS),
        in_specs=in_specs,
        out_specs=[per_head, state_spec],
        out_shape=[jax.ShapeDtypeStruct((bsz, seq, BRANCH), BF16),
                   jax.ShapeDtypeStruct((bsz, 2, GLA_HEADS, GLA_DK, GLA_DV), F32)],
        scratch_shapes=[pltpu.VMEM((nb, GLA_DV, wide), F32),
                        pltpu.VMEM((nb, GLA_DV, wide), BF16),
                        pltpu.VMEM((nb, GLA_DV, wide), BF16),
                        pltpu.VMEM((seq, 2 * GLA_DK), BF16),
                        pltpu.VMEM((seq, GLA_DV), F32)],
        compiler_params=_cparams("parallel", "parallel"),
        name="gla",
    )(*args)


def _s5_prep_kernel(lr_ref, li_ref, ld_ref, br_ref, bi_ref, cr_ref, ci_ref,
                    m_ref, win_ref, wout_ref, ar_ref, ai_ref, *, nlev):
    cs = S5_CHUNK
    P2 = 2 * S5_STATE
    lam_re, lam_im = lr_ref[0], li_ref[0]
    dt = jnp.exp(ld_ref[0])
    er, ei = lam_re * dt, lam_im * dt
    mag = jnp.exp(er)
    nr, ni = mag * jnp.cos(ei) - 1.0, mag * jnp.sin(ei)
    den = lam_re * lam_re + lam_im * lam_im
    cr = (nr * lam_re + ni * lam_im) / den
    ci = (ni * lam_re - nr * lam_im) / den
    b_re, b_im = br_ref[0], bi_ref[0]
    bb_re = cr * b_re - ci * b_im
    bb_im = cr * b_im + ci * b_re
    c_re, c_im = cr_ref[0], ci_ref[0]

    def power(k):
        m = jnp.exp(k * er)
        return m * jnp.cos(k * ei), m * jnp.sin(k * ei)

    def outer(a, b):
        return (a[:, None, :] * b[None, :, :]).reshape(S5_ROW, P2)

    def cmul(ar, ai, br, bi):
        return outer(ar, br) - outer(ai, bi), outer(ar, bi) + outer(ai, br)

    s_row = lax.broadcasted_iota(jnp.int32, (cs, P2), 0).astype(F32)
    fwd = lax.broadcasted_iota(jnp.int32, (cs, P2), 1) < S5_STATE
    half = float(cs // 2)
    k_in = jnp.where(fwd, half - s_row, s_row - half)
    l_re, l_im = cmul(*power(k_in), bb_re, bb_im)
    r_re, r_im = cmul(*power(-k_in), c_re, c_im)
    lcat = jnp.concatenate([l_re, l_im], axis=1)
    rcat = jnp.concatenate([r_re, -r_im], axis=1)
    fwd2 = (lax.broadcasted_iota(jnp.int32, (S5_ROW, S5_ROW), 1) % P2) < S5_STATE
    nt = (((1,), (1,)), ((), ()))
    m_f = lax.dot_general(jnp.where(fwd2, lcat, 0.0), rcat, nt, precision=HIGHEST, preferred_element_type=F32)
    m_b = lax.dot_general(jnp.where(fwd2, 0.0, lcat), rcat, nt, precision=HIGHEST, preferred_element_type=F32)
    s_in = lax.broadcasted_iota(jnp.int32, (S5_ROW, S5_ROW), 0) // S5_GROUP
    s_out = lax.broadcasted_iota(jnp.int32, (S5_ROW, S5_ROW), 1) // S5_GROUP
    m_ref[0] = (jnp.where(s_in <= s_out, m_f, 0.0) + jnp.where(s_in >= s_out, m_b, 0.0)).astype(BF16)
    w_re, w_im = cmul(*power(jnp.where(fwd, (cs - 1.0) - s_row, s_row)), bb_re, bb_im)
    win_ref[0] = jnp.concatenate([w_re, w_im], axis=1).astype(BF16)
    o_re, o_im = cmul(*power(jnp.where(fwd, s_row + 1.0, cs - s_row)), c_re, c_im)
    wout_ref[0] = jnp.concatenate([o_re, -o_im], axis=1).astype(BF16)
    lev = lax.shift_left(jnp.int32(cs), lax.broadcasted_iota(jnp.int32, (nlev, P2), 0)).astype(F32)
    ar_ref[0], ai_ref[0] = power(lev)


def _s5_operators(lam_re, lam_im, log_dt, b_re, b_im, c_re, c_im, nlev):
    P2 = 2 * S5_STATE
    lanes = lambda a: a.transpose(1, 0, 2).reshape(S5_GROUPS, 1, P2)
    both = lambda a: jnp.concatenate([a, a], axis=-1)
    args = (lanes(lam_re), lanes(lam_im),
            lanes(jnp.broadcast_to(log_dt[..., None], lam_re.shape)),
            both(b_re.transpose(0, 2, 1)), both(b_im.transpose(0, 2, 1)), both(c_re), both(c_im))
    g3 = lambda g: (g, 0, 0)
    vec = pl.BlockSpec((1, 1, P2), g3)
    mat = pl.BlockSpec((1, S5_GROUP, P2), g3)
    op = pl.BlockSpec((1, S5_ROW, S5_ROW), g3)
    lev = pl.BlockSpec((1, nlev, P2), g3)
    return pl.pallas_call(
        functools.partial(_s5_prep_kernel, nlev=nlev),
        grid=(S5_GROUPS,),
        in_specs=[vec, vec, vec, mat, mat, mat, mat],
        out_specs=[op, op, op, lev, lev],
        out_shape=[jax.ShapeDtypeStruct((S5_GROUPS, S5_ROW, S5_ROW), BF16)] * 3
        + [jax.ShapeDtypeStruct((S5_GROUPS, nlev, P2), F32)] * 2,
        compiler_params=_cparams("parallel"),
        name="s5_prep",
    )(*args)


def _gelu_tanh(x):
    return 0.5 * x * (1.0 + jnp.tanh(math.sqrt(2.0 / math.pi) * (x + 0.044715 * (x * x * x))))


def _s5_kernel(*refs, nseq, nrow, nlev, has_state):
    if has_state:
        (u_ref, m_ref, win_ref, wout_ref, ar_ref, ai_ref, d_ref, h0_ref,
         z_ref, last_ref, t_scr, y_scr, tmp_r, tmp_i) = refs
    else:
        (u_ref, m_ref, win_ref, wout_ref, ar_ref, ai_ref, d_ref,
         z_ref, last_ref, t_scr, y_scr, tmp_r, tmp_i) = refs
    P2 = 2 * S5_STATE
    R = nseq * nrow
    GP = S5_SLAB_GROUPS
    TR = 16
    slot = lax.broadcasted_iota(jnp.int32, (TR, 128), 1) // S5_GROUP

    def token_rows(nt, s):
        return pl.ds(pl.multiple_of(nt * (TR * S5_CHUNK), TR * S5_CHUNK) + s, TR, stride=S5_CHUNK)

    def gather(nt, carry):
        for j in range(2):
            src = [u_ref[0, token_rows(nt, 8 * j + s8), :] for s8 in range(8)]
            for gl in range(GP):
                acc = None
                for s8 in range(8):
                    sh = ((s8 - gl) * S5_GROUP) % 128
                    piece = pltpu.roll(src[s8], sh, 1) if sh else src[s8]
                    acc = piece if acc is None else jnp.where(slot == s8, piece, acc)
                t_scr[gl, pl.ds(pl.multiple_of(nt * TR, TR), TR), j * 128:(j + 1) * 128] = acc.astype(BF16)
        return carry

    lax.fori_loop(0, R // TR, gather, 0)

    n = lax.broadcasted_iota(jnp.int32, (R, P2), 0) % nrow
    row = lax.broadcasted_iota(jnp.int32, (R, P2), 0)
    fwd = lax.broadcasted_iota(jnp.int32, (R, P2), 1) < S5_STATE
    fwd_b = lax.broadcasted_iota(jnp.int32, (nseq, P2), 1) < S5_STATE

    def shift(a, k):
        dn = pltpu.roll(a, k, 0)
        up = pltpu.roll(a, R - k, 0)
        return jnp.where(fwd, jnp.where(n >= k, dn, 0.0), jnp.where(n < nrow - k, up, 0.0))

    def ends(t):
        if nseq == 1:
            return jnp.where(fwd_b, t[nrow - 1:nrow, :], t[0:1, :])
        return jnp.where(fwd_b, t[pl.ds(nrow - 1, nseq, stride=nrow), :], t[pl.ds(0, nseq, stride=nrow), :])

    def per_group(gl, carry):
        ub = t_scr[gl]
        x = _dot(ub, win_ref[gl])
        xr, xi = x[:, :P2], x[:, P2:]
        er, ei = shift(xr, 1), shift(xi, 1)
        if has_state:
            h0 = h0_ref[0, gl]
            for b in range(nseq):
                first = row == jnp.where(fwd, b * nrow, b * nrow + nrow - 1)
                er = jnp.where(first, h0[b:b + 1, :P2], er)
                ei = jnp.where(first, h0[b:b + 1, P2:], ei)
        for j in range(nlev):
            k = 2 ** j
            ar = ar_ref[gl, j:j + 1, :]
            ai = ai_ref[gl, j:j + 1, :]
            sr, si = shift(er, k), shift(ei, k)
            er, ei = er + ar * sr - ai * si, ei + ar * si + ai * sr
        ecat = jnp.concatenate([er, ei], axis=1).astype(BF16)
        y_scr[gl] = _dot(ub, m_ref[gl]) + _dot_nt(ecat, wout_ref[gl])
        a1r = ar_ref[gl, 0:1, :]
        a1i = ai_ref[gl, 0:1, :]
        tmp_r[...] = a1r * er - a1i * ei + xr
        tmp_i[...] = a1r * ei + a1i * er + xi
        last_ref[0, gl, :, :P2] = ends(tmp_r)
        last_ref[0, gl, :, P2:] = ends(tmp_i)
        return carry

    lax.fori_loop(0, GP, per_group, 0)

    def scatter(nt, carry):
        for j in range(2):
            ys = [y_scr[gl, pl.ds(pl.multiple_of(nt * TR, TR), TR), j * 128:(j + 1) * 128] for gl in range(GP)]
            for s8 in range(8):
                acc = None
                for gl in range(GP):
                    sh = ((gl - s8) * S5_GROUP) % 128
                    piece = pltpu.roll(ys[gl], sh, 1) if sh else ys[gl]
                    acc = piece if acc is None else jnp.where(slot == gl, piece, acc)
                rows = token_rows(nt, 8 * j + s8)
                z_ref[0, rows, :] = _gelu_tanh(acc + d_ref[...] * u_ref[0, rows, :])
        return carry

    lax.fori_loop(0, R // TR, scatter, 0)


def _s5_call(u, ops, d, h0, seq):
    m, win, wout, ar, ai = ops
    bm, lm, _ = u.shape
    nrow = seq // S5_CHUNK
    nlev = int(math.log2(nrow))
    total = bm * lm // seq
    nseq = max(1, min(total, S5_BLOCK_ROWS // nrow))
    nblk = total // nseq
    R = nseq * nrow
    tok = R * S5_CHUNK
    GP = S5_SLAB_GROUPS
    ns = BRANCH // 128
    has_state = h0 is not None
    slab = pl.BlockSpec((1, tok, 128), lambda t, i: (i, 0, t))
    op = pl.BlockSpec((GP, S5_ROW, S5_ROW), lambda t, i: (t, 0, 0))
    lev = pl.BlockSpec((GP, ar.shape[1], 2 * S5_STATE), lambda t, i: (t, 0, 0))
    state = pl.BlockSpec((1, GP, nseq, 4 * S5_STATE), lambda t, i: (i, t, 0, 0))
    in_specs = [slab, op, op, op, lev, lev, pl.BlockSpec((1, 128), lambda t, i: (0, t))]
    args = [u.reshape(nblk, tok, BRANCH), m, win, wout, ar, ai, d]
    if has_state:
        in_specs.append(state)
        args.append(h0.reshape(nblk, nseq, S5_GROUPS, 4 * S5_STATE).transpose(0, 2, 1, 3))
    z, last = pl.pallas_call(
        functools.partial(_s5_kernel, nseq=nseq, nrow=nrow, nlev=nlev, has_state=has_state),
        grid=(ns, nblk),
        in_specs=in_specs,
        out_specs=[slab, state],
        out_shape=[jax.ShapeDtypeStruct((nblk, tok, BRANCH), F32),
                   jax.ShapeDtypeStruct((nblk, S5_GROUPS, nseq, 4 * S5_STATE), F32)],
        scratch_shapes=[pltpu.VMEM((GP, R, S5_ROW), BF16), pltpu.VMEM((GP, R, S5_ROW), F32),
                        pltpu.VMEM((R, 2 * S5_STATE), F32), pltpu.VMEM((R, 2 * S5_STATE), F32)],
        compiler_params=_cparams("parallel", "parallel"),
        name="s5",
    )(*args)
    last = last.transpose(0, 2, 1, 3).reshape(total, S5_GROUPS, 4 * S5_STATE)
    return z.reshape(bm, lm, BRANCH), last


HEAD_ORDER = (0, 2, 1, 3)


def _stack_heads(q, heads):
    head = lax.broadcasted_iota(jnp.int32, q.shape, 1) // ATT_HD
    zero = jnp.zeros((), q.dtype)
    return jnp.concatenate([jnp.where(head == h, q, zero) for h in heads], axis=0)


def _key_value_operands(kv):
    low = lax.broadcasted_iota(jnp.int32, kv.shape, 1) < ATT_HD
    sw = pltpu.roll(kv, ATT_HD, 1)
    kk = jnp.where(low, kv, sw).astype(BF16)
    return (jnp.concatenate([kk, kk], axis=1),
            jnp.where(low, sw, 1.0).astype(BF16), jnp.where(low, 1.0, kv).astype(BF16))


LOG2E = 1.4426950408889634
Q_SCALE = (ATT_HD ** -0.5) * LOG2E


def _attend_pair(q, parity, key_sets, sink_ref):
    rows = q.shape[0]
    heads = (parity, parity + 2)
    qs = _stack_heads(q, heads)
    sink = jnp.concatenate([jnp.broadcast_to(sink_ref[0, h:h + 1, 0:1], (rows, 1)) for h in heads], axis=0) * LOG2E
    scores = []
    t = None
    for k4, _, _, bias in key_sets:
        s = _dot_nt(qs, k4)
        if bias is not None:
            s = s + jnp.concatenate([bias, bias], axis=0)
        scores.append(s)
        for c in range(0, s.shape[1], 128):
            t = s[:, c:c + 128] if t is None else jnp.maximum(t, s[:, c:c + 128])
    m = jnp.maximum(jnp.max(t, axis=-1, keepdims=True), sink)
    lane = lax.broadcasted_iota(jnp.int32, (2 * rows, 2 * ATT_HD), 1)
    den_lanes = (lane >= ATT_HD) if parity == 0 else (lane < ATT_HD)
    o = jnp.where(den_lanes, jnp.exp2(sink - m), 0.0)
    for s, (_, va, vb, _) in zip(scores, key_sets):
        o = o + _dot(jnp.exp2(s - m).astype(BF16), va if parity == 0 else vb)
    return o / pltpu.roll(o, ATT_HD, 1)


def _attend_stacked(q, key_sets, sink_ref):
    rows = q.shape[0]
    qs = _stack_heads(q, HEAD_ORDER)
    sink = jnp.concatenate([jnp.broadcast_to(sink_ref[0, h:h + 1, 0:1], (rows, 1)) for h in HEAD_ORDER],
                           axis=0) * LOG2E
    scores = []
    t = None
    for k4, _, _, bias in key_sets:
        s = _dot_nt(qs, k4)
        if bias is not None:
            s = s + jnp.concatenate([bias] * ATT_GROUPS, axis=0)
        scores.append(s)
        for c in range(0, s.shape[1], 128):
            t = s[:, c:c + 128] if t is None else jnp.maximum(t, s[:, c:c + 128])
    m = jnp.maximum(jnp.max(t, axis=-1, keepdims=True), sink)
    e_sink = jnp.exp2(sink - m)
    low = lax.broadcasted_iota(jnp.int32, (2 * rows, 2 * ATT_HD), 1) < ATT_HD
    o_even = jnp.where(low, 0.0, e_sink[:2 * rows])
    o_odd = jnp.where(low, e_sink[2 * rows:], 0.0)
    for s, (_, va, vb, _) in zip(scores, key_sets):
        p = jnp.exp2(s - m).astype(BF16)
        o_even = o_even + _dot(p[:2 * rows], va)
        o_odd = o_odd + _dot(p[2 * rows:], vb)
    return o_even / pltpu.roll(o_even, ATT_HD, 1), o_odd / pltpu.roll(o_odd, ATT_HD, 1)


def _attend(q, key_sets, sink_ref, split):
    rows = q.shape[0]
    if split:
        o_even = _attend_pair(q, 0, key_sets, sink_ref)
        o_odd = _attend_pair(q, 1, key_sets, sink_ref)
    else:
        o_even, o_odd = _attend_stacked(q, key_sets, sink_ref)
    low = lax.broadcasted_iota(jnp.int32, o_even.shape, 1) < ATT_HD
    pair = jnp.where(low, o_even, o_odd)
    return jnp.concatenate([pair[:rows], pair[rows:]], axis=1)


def _att_ctx_kernel(q_ref, kv_ref, sa_ref, sink_ref, y_ref, kc_ref, vc_ref):
    kv = kv_ref[0]
    kc_ref[0, 0] = kv[:, :ATT_HD]
    vc_ref[0, 0] = kv[:, ATT_HD:]
    out = _attend(q_ref[0], [_key_value_operands(kv) + (None,)], sink_ref, split=True)
    y_ref[0] = (out * sa_ref[0].astype(F32)).astype(BF16)


def _att_ctx_call(q, kv, sa, sink):
    bsz, seq, _ = q.shape
    W = ATT_GROUPS * ATT_HD
    per_kv = lambda b, k: (b, 0, k)
    cache_spec = pl.BlockSpec((1, 1, seq, ATT_HD), lambda b, k: (b, k, 0, 0))
    return pl.pallas_call(
        _att_ctx_kernel,
        grid=(bsz, ATT_KV_HEADS),
        in_specs=[
            pl.BlockSpec((1, seq, W), per_kv),
            pl.BlockSpec((1, seq, 2 * ATT_HD), per_kv),
            pl.BlockSpec((1, seq, W), per_kv),
            pl.BlockSpec((1, 8, 128), lambda b, k: (k, 0, 0)),
        ],
        out_specs=[pl.BlockSpec((1, seq, W), per_kv), cache_spec, cache_spec],
        out_shape=[jax.ShapeDtypeStruct((bsz, seq, BRANCH), BF16),
                   jax.ShapeDtypeStruct((bsz, ATT_KV_HEADS, seq, ATT_HD), F32),
                   jax.ShapeDtypeStruct((bsz, ATT_KV_HEADS, seq, ATT_HD), F32)],
        compiler_params=_cparams("parallel", "parallel"),
        name="att_ctx",
    )(q, kv, sa, sink)


def _att_lat_kernel(q_ref, kv_ref, sa_ref, sink_ref, kc_ref, vc_ref, y_ref, k4_scr, va_scr, vb_scr, *, seq):
    nb = seq // QBLK
    KW = 3 * QBLK

    def k_body(i, carry):
        r = pl.ds(pl.multiple_of(i * QBLK, QBLK), QBLK)
        k4, va, vb = _key_value_operands(kv_ref[0, r, :])
        k4_scr[r, :] = k4
        va_scr[r, :] = va
        vb_scr[r, :] = vb
        return carry

    lax.fori_loop(0, nb, k_body, 0)
    kc = kc_ref[0, 0, 0]
    vc = vc_ref[0, 0, 0]
    ones = jnp.ones_like(vc)
    kc4 = jnp.concatenate([kc] * ATT_GROUPS, axis=1).astype(BF16)
    vca = jnp.concatenate([vc, ones], axis=1).astype(BF16)
    vcb = jnp.concatenate([ones, vc], axis=1).astype(BF16)
    qi = lax.broadcasted_iota(jnp.int32, (QBLK, KW), 0)
    kj = lax.broadcasted_iota(jnp.int32, (QBLK, KW), 1)

    def q_body(i, carry):
        r = pl.ds(pl.multiple_of(i * QBLK, QBLK), QBLK)
        ws = pl.multiple_of(jnp.clip(i * QBLK - QBLK, 0, seq - KW), QBLK)
        win = pl.ds(ws, KW)
        bias = jnp.where(jnp.abs((i * QBLK + qi) - (ws + kj)) <= WINDOW, 0.0, NEG)
        key_sets = [(k4_scr[win, :], va_scr[win, :], vb_scr[win, :], bias), (kc4, vca, vcb, None)]
        out = _attend(q_ref[0, r, :], key_sets, sink_ref, split=False)
        y_ref[0, r, :] = (out * sa_ref[0, r, :].astype(F32)).astype(BF16)
        return carry

    lax.fori_loop(0, nb, q_body, 0, unroll=2)


def _rope_tables(seq, heads, pad_heads):
    pos = jnp.arange(seq)
    row = (pos // GRID_W).astype(F32)[:, None]
    col = (pos % GRID_W).astype(F32)[:, None]
    nf = ATT_HD // 4
    freq = ROPE_BASE ** (-jnp.arange(nf, dtype=F32) / nf)
    ar, ac = row * freq, col * freq
    cos = jnp.concatenate([jnp.cos(ar), jnp.cos(ar), jnp.cos(ac), jnp.cos(ac)], axis=1)
    sin = jnp.concatenate([-jnp.sin(ar), jnp.sin(ar), -jnp.sin(ac), jnp.sin(ac)], axis=1)
    cos = jnp.concatenate([jnp.tile(cos, (1, heads)), jnp.ones((seq, pad_heads * ATT_HD), F32)], axis=1)
    sin = jnp.concatenate([jnp.tile(sin, (1, heads)), jnp.zeros((seq, pad_heads * ATT_HD), F32)], axis=1)
    return cos, sin


def _att_lat_call(q, kv, sa, sink, cache_k, cache_v, e):
    bsz, seq, _ = q.shape
    past = cache_k.shape[3]
    W = ATT_GROUPS * ATT_HD
    per_kv = lambda b, k: (b, 0, k)
    cache_spec = pl.BlockSpec((1, 1, 1, past, ATT_HD), lambda b, k: (b, e, k, 0, 0))
    return pl.pallas_call(
        functools.partial(_att_lat_kernel, seq=seq),
        grid=(bsz, ATT_KV_HEADS),
        in_specs=[
            pl.BlockSpec((1, seq, W), per_kv),
            pl.BlockSpec((1, seq, 2 * ATT_HD), per_kv),
            pl.BlockSpec((1, seq, W), per_kv),
            pl.BlockSpec((1, 8, 128), lambda b, k: (k, 0, 0)),
            cache_spec,
            cache_spec,
        ],
        out_specs=pl.BlockSpec((1, seq, W), per_kv),
        out_shape=jax.ShapeDtypeStruct((bsz, seq, BRANCH), BF16),
        scratch_shapes=[pltpu.VMEM((seq, W), BF16), pltpu.VMEM((seq, 2 * ATT_HD), BF16),
                        pltpu.VMEM((seq, 2 * ATT_HD), BF16)],
        compiler_params=_cparams("parallel", "parallel"),
        name="att_lat",
    )(q, kv, sa, sink, cache_k, cache_v)


def _even_out_compute(x, refs):
    mod_ref, yg_ref, z_ref, ss_ref, wg_ref, bg_ref, wo_ref = refs
    z = z_ref[0]
    glu = z * jax.nn.sigmoid(_dot(z.astype(BF16), wg_ref[...]) + bg_ref[...])
    ys = (glu * ss_ref[0].astype(F32)).astype(BF16)
    out = _dot(yg_ref[0], wo_ref[:BRANCH, :]) + _dot(ys, wo_ref[BRANCH:, :])
    return x + mod_ref[0, 2:3, :] * out


def _even_out_part(tm, mod, yg, z, ss, wglu, bglu, wout):
    args = [mod, yg, z, ss, wglu, bglu, wout]
    specs = [_mod_spec(), _tile_spec(tm, BRANCH), _tile_spec(tm, BRANCH), _tile_spec(tm, BRANCH),
             _whole_spec(wglu), _whole_spec(bglu), _whole_spec(wout)]
    return _even_out_compute, args, specs


def _odd_out_compute(x, refs, seq, tm):
    mod_ref, ya_ref, p_ref, pprev_ref, pnext_ref, bgz_ref, cw_ref, cb_ref, wo_ref = refs
    p = p_ref[0].astype(F32)
    pos = (pl.program_id(1) * tm + lax.broadcasted_iota(jnp.int32, (tm, BRANCH), 0)) % seq
    rowi = lax.broadcasted_iota(jnp.int32, (tm, BRANCH), 0)
    halo = CONV_HALO_ROWS
    p_before = pprev_ref[0].astype(F32)[halo - 1:halo, :]
    p_after = pnext_ref[0].astype(F32)[0:1, :]
    prev = jnp.where(rowi == 0, p_before, pltpu.roll(p, 1, 0))
    nxt = jnp.where(rowi == tm - 1, p_after, pltpu.roll(p, tm - 1, 0))
    prev = jnp.where(pos == 0, 0.0, prev)
    nxt = jnp.where(pos == seq - 1, 0.0, nxt)
    conv = prev * cw_ref[0:1, :] + p * cw_ref[1:2, :] + nxt * cw_ref[2:3, :] + cb_ref[...]
    yc = (bgz_ref[0].astype(F32) * conv).astype(BF16)
    out = _dot(ya_ref[0], wo_ref[:BRANCH, :]) + _dot(yc, wo_ref[BRANCH:, :])
    return x + mod_ref[0, 2:3, :] * out


def _odd_out_part(tm, lm, seq, mod, ya, p, bgz, convw, convb, wout):
    hr = CONV_HALO_ROWS
    nth = lm // hr
    prev_spec = pl.BlockSpec((1, hr, BRANCH), lambda b, i: (b, jnp.maximum(i * (tm // hr) - 1, 0), 0))
    next_spec = pl.BlockSpec((1, hr, BRANCH), lambda b, i: (b, jnp.minimum((i + 1) * (tm // hr), nth - 1), 0))
    args = [mod, ya, p, p, p, bgz, convw, convb, wout]
    specs = [_mod_spec(), _tile_spec(tm, BRANCH), _tile_spec(tm, BRANCH), prev_spec, next_spec,
             _tile_spec(tm, BRANCH), _whole_spec(convw), _whole_spec(convb), _whole_spec(wout)]
    return functools.partial(_odd_out_compute, seq=seq, tm=tm), args, specs


def _even_in_weight(w):
    dk = GLA_HEADS * GLA_DK
    q, k = w[:, :dk], w[:, dk:2 * dk]
    v = w[:, 2 * dk:2 * dk + BRANCH]
    o = 2 * dk + BRANCH
    lr = w[:, o:o + 2 * GLA_RANK]
    rest = w[:, o + 2 * GLA_RANK:]
    qk = jnp.concatenate([jnp.concatenate([q[:, h * GLA_DK:(h + 1) * GLA_DK], k[:, h * GLA_DK:(h + 1) * GLA_DK]],
                                          axis=1) for h in range(GLA_HEADS)], axis=1)
    lr = jnp.pad(lr, ((0, 0), (0, 128 - 2 * GLA_RANK)))
    return jnp.concatenate([qk, v, lr, rest], axis=1).astype(BF16)


def _odd_in_weight(w):
    q = w[:, :BRANCH]
    k = w[:, BRANCH:BRANCH + ATT_KV_HEADS * ATT_HD]
    v = w[:, BRANCH + ATT_KV_HEADS * ATT_HD:BRANCH + 2 * ATT_KV_HEADS * ATT_HD]
    rest = w[:, BRANCH + 2 * ATT_KV_HEADS * ATT_HD:]
    kv = jnp.concatenate([jnp.concatenate([k[:, h * ATT_HD:(h + 1) * ATT_HD], v[:, h * ATT_HD:(h + 1) * ATT_HD]],
                                          axis=1) for h in range(ATT_KV_HEADS)], axis=1)
    return jnp.concatenate([q, kv, rest], axis=1).astype(BF16)


def _gate_weights(w2, b2):
    zf = jnp.zeros((GLA_RANK, GLA_DK), F32)
    cols, bias = [], []
    for h in range(GLA_HEADS):
        sl = slice(h * GLA_DK, (h + 1) * GLA_DK)
        cols.append(jnp.concatenate([jnp.concatenate([w2[0][:, sl], zf], axis=1),
                                     jnp.concatenate([zf, w2[1][:, sl]], axis=1)], axis=0))
        bias += [b2[0][sl], b2[1][sl]]
    w = jnp.pad(jnp.concatenate(cols, axis=1), ((0, 128 - 2 * GLA_RANK), (0, 0)))
    w_hi = w.astype(BF16)
    w_lo = (w - w_hi.astype(F32)).astype(BF16)
    return w_hi, w_lo, jnp.concatenate(bias)[None, :]


def _run_stream(x, mods, seq, params, gla_s0, s5_h0, cache_k, cache_v):
    depth = len(params)
    bm, lm, _ = x.shape
    tm = min(ROW_TILE, lm)
    nseq = bm * lm // seq
    per_seq = lambda a: a.reshape(nseq, seq, a.shape[-1])
    flat = lambda a: a.reshape(bm, lm, a.shape[-1])
    latent = cache_k is not None
    tables = _rope_tables(seq, 2, 0) + _rope_tables(seq, 1, 1) if latent else ()

    def in_part(l):
        p = params[l]
        if l % 2 == 0:
            return _even_in_part(tm, mods[l], p["nw"], p["w_in"], p["gla_w"])
        return _odd_in_part(tm, mods[l], p["nw"], p["w_in"], p["qnw"], p["knw"], tables)

    collected = {"gla": [], "s5": [], "k": [], "v": []}
    out_part = None
    for l in range(depth + 1):
        res = _stage_call("stage%d" % l, x, out_part, in_part(l) if l < depth else None)
        if out_part is not None:
            x, res = res[0], res[1:]
        if l == depth:
            break
        p, e = params[l], l // 2
        if l % 2 == 0:
            qk, v, bc, sg, u, ss = res
            yg, s_new = _gla_call(per_seq(qk), per_seq(v), per_seq(bc), per_seq(sg), p["onorm"],
                                  None if gla_s0 is None else gla_s0[:, e])
            z, last = _s5_call(u, p["s5_ops"], p["d"], None if s5_h0 is None else s5_h0[e], seq)
            collected["gla"].append(s_new)
            collected["s5"].append(last)
            out_part = _even_out_part(tm, mods[l], flat(yg), z, ss, p["wglu"], p["bglu"], p["wout"])
        else:
            q, kv, sa, pc, bgz = res
            if latent:
                ya = _att_lat_call(per_seq(q), per_seq(kv), per_seq(sa), p["sink"], cache_k, cache_v, e)
            else:
                ya, kc, vc = _att_ctx_call(per_seq(q), per_seq(kv), per_seq(sa), p["sink"])
                collected["k"].append(kc)
                collected["v"].append(vc)
            out_part = _odd_out_part(tm, lm, seq, mods[l], flat(ya), pc, bgz, p["convw"], p["convb"], p["wout"])
    return x, collected


def kernel(x_prompt, x_sample, c, state_gla, state_s5_re, state_s5_im, cache_k, cache_v, c_ctx, norm_w, w_ada, b_ada, w_in_e, w_out_e, gla_w2, gla_b2, gla_onorm, s5_lam_re, s5_lam_im, s5_log_dt, s5_b_re, s5_b_im, s5_c_re, s5_c_im, s5_d, s5_w_glu, s5_b_glu, w_in_o, w_out_o, q_norm_w, k_norm_w, sink, conv_w, conv_b):
    depth = norm_w.shape[0]
    bp, lp, _ = x_prompt.shape
    bs, ls, _ = x_sample.shape

    rows = 8 * ((1 + bs + 7) // 8)
    cs = jnp.zeros((rows, D_MODEL), F32).at[0].set(c_ctx).at[1:1 + bs].set(c)
    mods = _ada_call(cs, w_ada, b_ada)

    params, s5_h0 = [], []
    for l in range(depth):
        e = l // 2
        p = {"nw": norm_w[l][None, :]}
        if l % 2 == 0:
            p.update(
                w_in=_even_in_weight(w_in_e[e]), wout=w_out_e[e].astype(BF16),
                gla_w=_gate_weights(gla_w2[e], gla_b2[e]), onorm=gla_onorm[e][None, :],
                d=s5_d[e][None, :], wglu=s5_w_glu[e].astype(BF16), bglu=s5_b_glu[e][None, :],
                s5_ops=_s5_operators(s5_lam_re[e], s5_lam_im[e], s5_log_dt[e], s5_b_re[e], s5_b_im[e],
                                     s5_c_re[e], s5_c_im[e], nlev=int(math.log2(max(lp, ls) // S5_CHUNK))))
            h0 = jnp.concatenate([state_s5_re[:, e], state_s5_im[:, e]], axis=1)
            s5_h0.append(h0.transpose(0, 2, 1, 3).reshape(bs, S5_GROUPS, 4 * S5_STATE))
        else:
            sk = jnp.broadcast_to(sink[e].reshape(ATT_KV_HEADS, ATT_GROUPS, 1), (ATT_KV_HEADS, ATT_GROUPS, 128))
            p.update(
                w_in=_odd_in_weight(w_in_o[e]), wout=w_out_o[e].astype(BF16),
                qnw=jnp.tile(q_norm_w[e], ATT_HEADS)[None, :],
                knw=jnp.tile(jnp.concatenate([k_norm_w[e], jnp.ones((ATT_HD,), F32)]), ATT_KV_HEADS)[None, :],
                sink=jnp.concatenate([sk, jnp.zeros((ATT_KV_HEADS, 8 - ATT_GROUPS, 128), F32)], axis=1),
                convw=conv_w[e], convb=conv_b[e][None, :])
        params.append(p)

    mods_p = [mods[l, 0:1].reshape(1, 3, D_MODEL) for l in range(depth)]
    mods_s = [mods[l, 1:1 + bs].reshape(bs, 3, D_MODEL) for l in range(depth)]
    xp, got = _run_stream(x_prompt.reshape(1, bp * lp, D_MODEL), mods_p, lp, params, None, None, None, None)
    xs, _ = _run_stream(x_sample, mods_s, ls, params, state_gla, s5_h0, cache_k, cache_v)
    last = [t.reshape(bp, S5_GROUPS, 2, 2, S5_STATE).transpose(2, 0, 3, 1, 4) for t in got["s5"]]
    return (xp.reshape(bp, lp, D_MODEL), xs,
            jnp.stack(got["gla"], axis=1),
            jnp.stack([t[0] for t in last], axis=1), jnp.stack([t[1] for t in last], axis=1),
            jnp.stack(got["k"], axis=1), jnp.stack(got["v"], axis=1))
```

```python
import functools
import math

import jax
import jax.numpy as jnp
from jax import lax
from jax.experimental import pallas as pl
from jax.experimental.pallas import tpu as pltpu

F32 = jnp.float32
BF16 = jnp.bfloat16
HIGHEST = lax.Precision.HIGHEST

D_MODEL = 1024
BRANCH = D_MODEL // 2
GLA_HEADS = 4
GLA_DK = 64
GLA_DV = 128
GLA_RANK = 16
GLA_GATE_NORM = 16.0
GLA_CHUNK = 64
GLA_BLOCK = 256
S5_GROUP = 16
S5_GROUPS = BRANCH // S5_GROUP
S5_STATE = 64
S5_CHUNK = 16
S5_ROW = S5_CHUNK * S5_GROUP
S5_SLAB_GROUPS = 128 // S5_GROUP
S5_BLOCK_ROWS = 256
ATT_HEADS = 8
ATT_KV_HEADS = 2
ATT_GROUPS = ATT_HEADS // ATT_KV_HEADS
ATT_HD = 64
WINDOW = 128
QBLK = 128
GRID_W = 64
ROPE_BASE = 10000.0
CONV_W = 3
CONV_HALO_ROWS = 16
EPS = 1e-6
NEG = -1e30

ROW_TILE = 512
VMEM_LIMIT = 48 * 1024 * 1024


def _cparams(*sem):
    return pltpu.CompilerParams(dimension_semantics=sem, vmem_limit_bytes=VMEM_LIMIT)


def _silu(x):
    return x * jax.nn.sigmoid(x)


def _dot(a, b):
    return jnp.dot(a, b, preferred_element_type=F32)


def _dot_nt(a, b):
    return lax.dot_general(a, b, (((1,), (1,)), ((), ())), preferred_element_type=F32)


def _ada_kernel(c_ref, w_ref, b_ref, o_ref):
    c = c_ref[...]
    o_ref[0] = jnp.dot(_silu(c), w_ref[0], precision=HIGHEST, preferred_element_type=F32) + b_ref[0]


def _ada_call(cs, w_ada, b_ada):
    depth = w_ada.shape[0]
    rows = cs.shape[0]
    nt = 3
    return pl.pallas_call(
        _ada_kernel,
        grid=(depth, nt),
        in_specs=[
            pl.BlockSpec((rows, D_MODEL), lambda l, j: (0, 0)),
            pl.BlockSpec((1, D_MODEL, D_MODEL), lambda l, j: (l, 0, j)),
            pl.BlockSpec((1, 1, D_MODEL), lambda l, j: (l, 0, j)),
        ],
        out_specs=pl.BlockSpec((1, rows, D_MODEL), lambda l, j: (l, 0, j)),
        out_shape=jax.ShapeDtypeStruct((depth, rows, 3 * D_MODEL), F32),
        compiler_params=_cparams("arbitrary", "arbitrary"),
        name="adaln",
    )(cs, w_ada, b_ada.reshape(depth, 1, 3 * D_MODEL))


def _modulated(x, mod_ref, nw_ref):
    ms = jnp.mean(x * x, axis=-1, keepdims=True)
    y = x * lax.rsqrt(ms + EPS) * nw_ref[...]
    shift = mod_ref[0, 0:1, :]
    scale = mod_ref[0, 1:2, :]
    return (y * (1.0 + scale) + shift).astype(BF16)


E_COLS = (0, 512, 1024, 1152, 1664, 2176, 2688)


def _log_sigmoid(x):
    return jnp.minimum(x, 0.0) - jnp.log(1.0 + jnp.exp(-jnp.abs(x)))


def _chunk_cumsum(g):
    rows, width = g.shape
    pos = lax.broadcasted_iota(jnp.int32, g.shape, 0) % GLA_CHUNK
    fwd = (lax.broadcasted_iota(jnp.int32, g.shape, 1) % (2 * GLA_DK)) < GLA_DK
    c = g
    k = 1
    while k < GLA_CHUNK:
        c = c + jnp.where(pos >= k, pltpu.roll(c, k, 0), 0.0)
        k *= 2
    total = jnp.concatenate([jnp.broadcast_to(c[r + GLA_CHUNK - 1:r + GLA_CHUNK, :], (GLA_CHUNK, width))
                             for r in range(0, rows, GLA_CHUNK)], axis=0)
    return jnp.where(fwd, c, total - c + g)


def _even_in_compute(x, refs, outs):
    mod_ref, nw_ref, w_ref, w2h_ref, w2l_ref, b2_ref = refs
    qk_ref, v_ref, bc_ref, sg_ref, u_ref, ss_ref = outs
    h = _modulated(x, mod_ref, nw_ref)
    c = E_COLS
    lr = _dot(h, w_ref[:, c[2]:c[3]])
    lr_hi = lr.astype(BF16)
    lr_lo = (lr - lr_hi.astype(F32)).astype(BF16)
    pre = _dot(lr_hi, w2h_ref[...]) + (_dot(lr_lo, w2h_ref[...]) + _dot(lr_hi, w2l_ref[...])) + b2_ref[...]
    bc_ref[0] = _chunk_cumsum(_log_sigmoid(pre) * (1.0 / GLA_GATE_NORM))
    qk_ref[0] = _dot(h, w_ref[:, c[0]:c[1]]).astype(BF16)
    v_ref[0] = _dot(h, w_ref[:, c[1]:c[2]]).astype(BF16)
    sg_ref[0] = _silu(_dot(h, w_ref[:, c[3]:c[4]])).astype(BF16)
    u_ref[0] = _dot(h, w_ref[:, c[4]:c[5]])
    ss_ref[0] = _silu(_dot(h, w_ref[:, c[5]:c[6]])).astype(BF16)


O_COLS = (0, 512, 768, 1280, 1792, 2304, 2816, 3328)


def _rope(x, cos, sin):
    width = x.shape[1]
    lane = lax.broadcasted_iota(jnp.int32, x.shape, 1)
    partner = jnp.where((lane & 16) == 0, pltpu.roll(x, width - 16, 1), pltpu.roll(x, 16, 1))
    return x * cos + partner * sin


def _head_pair_norm(x, w, cos, sin, both):
    low = lax.broadcasted_iota(jnp.int32, x.shape, 1) < ATT_HD
    x2 = x * x
    ms = jnp.sum(jnp.where(low, x2, 0.0), axis=-1, keepdims=True) * (1.0 / ATT_HD)
    rs = lax.rsqrt(ms + EPS)
    if both:
        ms_hi = jnp.sum(jnp.where(low, 0.0, x2), axis=-1, keepdims=True) * (1.0 / ATT_HD)
        rs = jnp.where(low, rs, lax.rsqrt(ms_hi + EPS))
    else:
        rs = jnp.where(low, rs, 1.0)
    y = x * rs * w
    return y if cos is None else _rope(y, cos, sin)


def _odd_in_compute(x, refs, outs, rope):
    if rope:
        mod_ref, nw_ref, w_ref, qn_ref, kn_ref, cq_ref, sq_ref, ck_ref, sk_ref = refs
        cq, sq, ck, sk = cq_ref[...], sq_ref[...], ck_ref[...], sk_ref[...]
    else:
        mod_ref, nw_ref, w_ref, qn_ref, kn_ref = refs
        cq = sq = ck = sk = None
    q_ref, kv_ref, sa_ref, p_ref, bgz_ref = outs
    h = _modulated(x, mod_ref, nw_ref)
    c = O_COLS
    q = _dot(h, w_ref[:, c[0]:c[1]])
    for t in range(BRANCH // 128):
        lanes = slice(t * 128, (t + 1) * 128)
        qt = _head_pair_norm(q[:, lanes], qn_ref[:, lanes], cq, sq, True)
        q_ref[0, :, lanes] = (qt * Q_SCALE).astype(BF16)
    kv = _dot(h, w_ref[:, c[1]:c[2]])
    for t in range(ATT_KV_HEADS):
        lanes = slice(t * 128, (t + 1) * 128)
        kv_ref[0, :, lanes] = _head_pair_norm(kv[:, lanes], kn_ref[:, lanes], ck, sk, False)
    sa_ref[0] = _silu(_dot(h, w_ref[:, c[2]:c[3]])).astype(BF16)
    xc = _dot(h, w_ref[:, c[3]:c[4]])
    bg = _dot(h, w_ref[:, c[4]:c[5]])
    cg = _dot(h, w_ref[:, c[5]:c[6]])
    zc = _dot(h, w_ref[:, c[6]:c[7]])
    p_ref[0] = (cg * xc).astype(BF16)
    bgz_ref[0] = (bg * _silu(zc)).astype(BF16)


def _tile_spec(tm, n):
    return pl.BlockSpec((1, tm, n), lambda b, i: (b, i, 0))


def _whole_spec(a):
    return pl.BlockSpec(a.shape, lambda b, i: (0,) * a.ndim)


def _mod_spec():
    return pl.BlockSpec((1, 3, D_MODEL), lambda b, i: (b, 0, 0))


def _even_in_part(tm, mod, nw, w_in, gla_w):
    args = [mod, nw, w_in, *gla_w]
    specs = [_mod_spec()] + [_whole_spec(a) for a in args[1:]]
    outs = [(BRANCH, dt) for dt in (BF16, BF16, F32, BF16, F32, BF16)]
    return _even_in_compute, args, specs, outs


def _odd_in_part(tm, mod, nw, w_in, qnw, knw, tables):
    args = [mod, nw, w_in, qnw, knw, *tables]
    specs = ([_mod_spec()] + [_whole_spec(a) for a in args[1:5]]
             + [pl.BlockSpec((tm, a.shape[1]), lambda b, i: (i, 0)) for a in tables])
    outs = [(BRANCH, BF16), (2 * ATT_KV_HEADS * ATT_HD, F32), (BRANCH, BF16), (BRANCH, BF16), (BRANCH, BF16)]
    return functools.partial(_odd_in_compute, rope=bool(tables)), args, specs, outs


def _stage_kernel(*refs, n_out_args, n_in_args, out_fn, in_fn):
    x_ref = refs[0]
    out_args = refs[1:1 + n_out_args]
    in_args = refs[1 + n_out_args:1 + n_out_args + n_in_args]
    outs = refs[1 + n_out_args + n_in_args:]
    x = x_ref[0]
    if out_fn is not None:
        x = out_fn(x, out_args)
        outs[0][0] = x
        outs = outs[1:]
    if in_fn is not None:
        in_fn(x, in_args, outs)


def _stage_call(name, x, out_part, in_part):
    bm, lm, _ = x.shape
    tm = min(ROW_TILE, lm)
    args, specs, out_defs = [x], [_tile_spec(tm, D_MODEL)], []
    out_fn = in_fn = None
    n_out_args = n_in_args = 0
    if out_part is not None:
        out_fn, a, s = out_part
        args, specs, n_out_args = args + a, specs + s, len(a)
        out_defs.append((D_MODEL, F32))
    if in_part is not None:
        in_fn, a, s, o = in_part
        args, specs, n_in_args = args + a, specs + s, len(a)
        out_defs += o
    res = pl.pallas_call(
        functools.partial(_stage_kernel, n_out_args=n_out_args, n_in_args=n_in_args, out_fn=out_fn, in_fn=in_fn),
        grid=(bm, lm // tm),
        in_specs=specs,
        out_specs=[_tile_spec(tm, n) for n, _ in out_defs],
        out_shape=[jax.ShapeDtypeStruct((bm, lm, n), dt) for n, dt in out_defs],
        compiler_params=_cparams("parallel", "parallel"),
        name=name,
    )(*args)
    return res


def _loop(n, body, init, unroll):
    if n <= unroll:
        for i in range(n):
            init = body(i, init)
        return init
    return lax.fori_loop(0, n, body, init, unroll=unroll)


def _gla_kernel(*refs, seq, has_state, heads):
    if has_state:
        (qk_ref, v_ref, bc_ref, sg_ref, on_ref, s0_ref, y_ref, sn_ref,
         ut_scr, sf_scr, sb_scr, qd_scr, o_scr) = refs
    else:
        (qk_ref, v_ref, bc_ref, sg_ref, on_ref, y_ref, sn_ref,
         ut_scr, sf_scr, sb_scr, qd_scr, o_scr) = refs
        s0_ref = None
    C = GLA_CHUNK
    W = 2 * GLA_DK
    R = GLA_BLOCK
    CB = R // C
    nc = seq // C
    nb = seq // R
    ri = lax.broadcasted_iota(jnp.int32, (R, R), 0)
    ci = lax.broadcasted_iota(jnp.int32, (R, R), 1)
    same = (ri // C) == (ci // C)
    tril = same & (ri >= ci)
    triu = same & (ri <= ci)
    fwd = lax.broadcasted_iota(jnp.int32, (R, W), 1) < GLA_DK
    fwd_row = lax.broadcasted_iota(jnp.int32, (1, W), 1) < GLA_DK
    fwd_st = lax.broadcasted_iota(jnp.int32, (GLA_DV, W), 1) < GLA_DK
    row_chunk = lax.broadcasted_iota(jnp.int32, (R, W), 0) // C
    scale = GLA_DK ** -0.5

    def rows(c):
        return pl.ds(pl.multiple_of(c * C, C), C)

    def block_rows(j):
        return pl.ds(pl.multiple_of(j * R, R), R)

    def state_rows(c):
        return pl.ds(pl.multiple_of(c * GLA_DV, GLA_DV), GLA_DV)

    def intra(h, j, carry):
        r = block_rows(j)
        hl = slice(h * W, (h + 1) * W)
        qk = qk_ref[0, r, hl].astype(F32)
        bc = bc_ref[0, r, hl]
        v = v_ref[0, r, hl].astype(F32)
        sw = pltpu.roll(qk, GLA_DK, 1)
        q2 = jnp.where(fwd, qk, sw) * scale
        k2 = jnp.where(fwd, sw, qk)
        b_mid = bc[C // 2:C // 2 + 1, :]
        b_end = jnp.where(fwd_row, bc[C - 1:C, :], bc[0:1, :])
        for c in range(1, CB):
            o0 = c * C
            b_mid = jnp.where(row_chunk == c, bc[o0 + C // 2:o0 + C // 2 + 1, :], b_mid)
            b_end = jnp.where(row_chunk == c, jnp.where(fwd_row, bc[o0 + C - 1:o0 + C, :], bc[o0:o0 + 1, :]), b_end)
        qd = (q2 * jnp.exp(bc)).astype(BF16)
        qh = q2 * jnp.exp(bc - b_mid)
        kh = (k2 * jnp.exp(b_mid - bc)).astype(BF16)
        att = (jnp.where(tril, _dot_nt(jnp.where(fwd, qh, 0.0).astype(BF16), kh), 0.0)
               + jnp.where(triu, _dot_nt(jnp.where(fwd, 0.0, qh).astype(BF16), kh), 0.0))
        o = _dot(att.astype(BF16), v.astype(BF16))
        kd = k2 * jnp.exp(b_end - bc)
        kd_wide = jnp.concatenate([jnp.where(row_chunk == c, kd, 0.0) for c in range(CB)], axis=1)
        ut = _dot(v.T.astype(BF16), kd_wide.astype(BF16))
        qd_scr[h, r, :] = qd
        o_scr[h, r, :] = o
        ut_scr[h, j] = ut
        return carry

    for h in range(heads):
        _loop(nb, functools.partial(intra, h), 0, 2)

    def scan(h, j, st):
        jb = nb - 1 - j
        hl = slice(h * W, (h + 1) * W)
        bcf = bc_ref[0, block_rows(j), hl]
        bcb = bc_ref[0, block_rows(jb), hl]
        utf = ut_scr[h, j]
        utb = ut_scr[h, jb]
        for c in range(CB):
            cb = CB - 1 - c
            stb = st.astype(BF16)
            sf_scr[h, j, :, c * W:(c + 1) * W] = stb
            sb_scr[h, jb, :, cb * W:(cb + 1) * W] = stb
            b_end = jnp.where(fwd_row, bcf[c * C + C - 1:c * C + C, :], bcb[cb * C:cb * C + 1, :])
            inc = jnp.where(fwd_st, utf[:, c * W:(c + 1) * W], utb[:, cb * W:(cb + 1) * W])
            st = st * jnp.exp(b_end) + inc
        return st

    for h in range(heads):
        if has_state:
            init = jnp.concatenate([s0_ref[0, 0, h].T, s0_ref[0, 1, h].T], axis=1)
        else:
            init = jnp.zeros((GLA_DV, W), F32)
        st = _loop(nb, functools.partial(scan, h), init, 1)
        sn_ref[0, 0, h] = st[:, :GLA_DK].T
        sn_ref[0, 1, h] = st[:, GLA_DK:].T

    fwd_wide = (lax.broadcasted_iota(jnp.int32, (GLA_DV, CB * W), 1) % W) < GLA_DK

    def finish(h, j, carry):
        r = block_rows(j)
        hl = slice(h * W, (h + 1) * W)
        qd = qd_scr[h, r, :]
        zero = jnp.zeros((), BF16)
        qd_wide = jnp.concatenate([jnp.where(row_chunk == c, qd, zero) for c in range(CB)], axis=1)
        s_in = jnp.where(fwd_wide, sf_scr[h, j], sb_scr[h, j])
        o = o_scr[h, r, :] + _dot_nt(qd_wide, s_in)
        ms = jnp.mean(o * o, axis=-1, keepdims=True)
        y_ref[0, r, hl] = (o * lax.rsqrt(ms + EPS) * on_ref[...] * sg_ref[0, r, hl].astype(F32)).astype(BF16)
        return carry

    for h in range(heads):
        _loop(nb, functools.partial(finish, h), 0, 2)


def _gla_call(qk, v, bc, sg, onorm, s0):
    bsz, seq, _ = qk.shape
    has_state = s0 is not None
    nb = seq // GLA_BLOCK
    wide = (GLA_BLOCK // GLA_CHUNK) * 2 * GLA_DK
    hps = GLA_HEADS if nb == 1 else 1
    per_head = pl.BlockSpec((1, seq, 128 * hps), lambda b, h: (b, 0, h))
    in_specs = [per_head, per_head, per_head, per_head, pl.BlockSpec((1, GLA_DV), lambda b, h: (0, 0))]
    args = [qk, v, bc, sg, onorm]
    state_spec = pl.BlockSpec((1, 2, hps, GLA_DK, GLA_DV), lambda b, h: (b, 0, h, 0, 0))
    if has_state:
        in_specs.append(state_spec)
        args.append(s0)
    return pl.pallas_call(
        functools.partial(_gla_kernel, seq=seq, has_state=has_state, heads=hps),
        grid=(bsz, GLA_HEADS // hps),
        in_specs=in_specs,
        out_specs=[per_head, state_spec],
        out_shape=[jax.ShapeDtypeStruct((bsz, seq, BRANCH), BF16),
                   jax.ShapeDtypeStruct((bsz, 2, GLA_HEADS, GLA_DK, GLA_DV), F32)],
        scratch_shapes=[pltpu.VMEM((hps, nb, GLA_DV, wide), F32),
                        pltpu.VMEM((hps, nb, GLA_DV, wide), BF16),
                        pltpu.VMEM((hps, nb, GLA_DV, wide), BF16),
                        pltpu.VMEM((hps, seq, 2 * GLA_DK), BF16),
                        pltpu.VMEM((hps, seq, GLA_DV), F32)],
        compiler_params=_cparams("parallel", "parallel"),
        name="gla",
    )(*args)


def _s5_prep_kernel(lr_ref, li_ref, ld_ref, br_ref, bi_ref, cr_ref, ci_ref,
                    m_ref, win_ref, wout_ref, ar_ref, ai_ref, *, nlev):
    cs = S5_CHUNK
    P2 = 2 * S5_STATE
    lam_re, lam_im = lr_ref[0], li_ref[0]
    dt = jnp.exp(ld_ref[0])
    er, ei = lam_re * dt, lam_im * dt
    mag = jnp.exp(er)
    nr, ni = mag * jnp.cos(ei) - 1.0, mag * jnp.sin(ei)
    den = lam_re * lam_re + lam_im * lam_im
    cr = (nr * lam_re + ni * lam_im) / den
    ci = (ni * lam_re - nr * lam_im) / den
    b_re, b_im = br_ref[0], bi_ref[0]
    bb_re = cr * b_re - ci * b_im
    bb_im = cr * b_im + ci * b_re
    c_re, c_im = cr_ref[0], ci_ref[0]

    def power(k):
        m = jnp.exp(k * er)
        return m * jnp.cos(k * ei), m * jnp.sin(k * ei)

    def outer(a, b):
        return (a[:, None, :] * b[None, :, :]).reshape(S5_ROW, P2)

    def cmul(ar, ai, br, bi):
        return outer(ar, br) - outer(ai, bi), outer(ar, bi) + outer(ai, br)

    s_row = lax.broadcasted_iota(jnp.int32, (cs, P2), 0).astype(F32)
    fwd = lax.broadcasted_iota(jnp.int32, (cs, P2), 1) < S5_STATE
    half = float(cs // 2)
    k_in = jnp.where(fwd, half - s_row, s_row - half)
    l_re, l_im = cmul(*power(k_in), bb_re, bb_im)
    r_re, r_im = cmul(*power(-k_in), c_re, c_im)
    lcat = jnp.concatenate([l_re, l_im], axis=1)
    rcat = jnp.concatenate([r_re, -r_im], axis=1)
    fwd2 = (lax.broadcasted_iota(jnp.int32, (S5_ROW, S5_ROW), 1) % P2) < S5_STATE
    nt = (((1,), (1,)), ((), ()))
    m_f = lax.dot_general(jnp.where(fwd2, lcat, 0.0), rcat, nt, precision=HIGHEST, preferred_element_type=F32)
    m_b = lax.dot_general(jnp.where(fwd2, 0.0, lcat), rcat, nt, precision=HIGHEST, preferred_element_type=F32)
    s_in = lax.broadcasted_iota(jnp.int32, (S5_ROW, S5_ROW), 0) // S5_GROUP
    s_out = lax.broadcasted_iota(jnp.int32, (S5_ROW, S5_ROW), 1) // S5_GROUP
    m_ref[0] = (jnp.where(s_in <= s_out, m_f, 0.0) + jnp.where(s_in >= s_out, m_b, 0.0)).astype(BF16)
    w_re, w_im = cmul(*power(jnp.where(fwd, (cs - 1.0) - s_row, s_row)), bb_re, bb_im)
    win_ref[0] = jnp.concatenate([w_re, w_im], axis=1).astype(BF16)
    o_re, o_im = cmul(*power(jnp.where(fwd, s_row + 1.0, cs - s_row)), c_re, c_im)
    wout_ref[0] = jnp.concatenate([o_re, -o_im], axis=1).astype(BF16)
    lev = lax.shift_left(jnp.int32(cs), lax.broadcasted_iota(jnp.int32, (nlev, P2), 0)).astype(F32)
    ar_ref[0], ai_ref[0] = power(lev)


def _s5_operators(lam_re, lam_im, log_dt, b_re, b_im, c_re, c_im, nlev):
    P2 = 2 * S5_STATE
    lanes = lambda a: a.transpose(1, 0, 2).reshape(S5_GROUPS, 1, P2)
    both = lambda a: jnp.concatenate([a, a], axis=-1)
    args = (lanes(lam_re), lanes(lam_im),
            lanes(jnp.broadcast_to(log_dt[..., None], lam_re.shape)),
            both(b_re.transpose(0, 2, 1)), both(b_im.transpose(0, 2, 1)), both(c_re), both(c_im))
    g3 = lambda g: (g, 0, 0)
    vec = pl.BlockSpec((1, 1, P2), g3)
    mat = pl.BlockSpec((1, S5_GROUP, P2), g3)
    op = pl.BlockSpec((1, S5_ROW, S5_ROW), g3)
    lev = pl.BlockSpec((1, nlev, P2), g3)
    return pl.pallas_call(
        functools.partial(_s5_prep_kernel, nlev=nlev),
        grid=(S5_GROUPS,),
        in_specs=[vec, vec, vec, mat, mat, mat, mat],
        out_specs=[op, op, op, lev, lev],
        out_shape=[jax.ShapeDtypeStruct((S5_GROUPS, S5_ROW, S5_ROW), BF16)] * 3
        + [jax.ShapeDtypeStruct((S5_GROUPS, nlev, P2), F32)] * 2,
        compiler_params=_cparams("parallel"),
        name="s5_prep",
    )(*args)


def _gelu_tanh(x):
    return 0.5 * x * (1.0 + jnp.tanh(math.sqrt(2.0 / math.pi) * (x + 0.044715 * (x * x * x))))


def _slot_transpose(v, slot):
    v = list(v)
    d = len(v) // 2
    while d >= 1:
        take_own = (slot & d) == 0
        for i in range(len(v)):
            if i & d == 0:
                a, b = v[i], v[i + d]
                v[i] = jnp.where(take_own, a, pltpu.roll(b, d * S5_GROUP, 1))
                v[i + d] = jnp.where(take_own, pltpu.roll(a, 128 - d * S5_GROUP, 1), b)
        d //= 2
    return v


def _s5_kernel(*refs, nseq, nrow, nlev, has_state):
    if has_state:
        (u_ref, m_ref, win_ref, wout_ref, ar_ref, ai_ref, d_ref, h0_ref,
         z_ref, last_ref, t_scr, y_scr, tmp_r, tmp_i) = refs
    else:
        (u_ref, m_ref, win_ref, wout_ref, ar_ref, ai_ref, d_ref,
         z_ref, last_ref, t_scr, y_scr, tmp_r, tmp_i) = refs
    P2 = 2 * S5_STATE
    R = nseq * nrow
    GP = S5_SLAB_GROUPS
    TR = 16
    slot = lax.broadcasted_iota(jnp.int32, (TR, 128), 1) // S5_GROUP

    def token_rows(nt, s):
        return pl.ds(pl.multiple_of(nt * (TR * S5_CHUNK), TR * S5_CHUNK) + s, TR, stride=S5_CHUNK)

    def gather(nt, carry):
        for j in range(2):
            src = [u_ref[0, token_rows(nt, 8 * j + s8), :] for s8 in range(8)]
            by_group = _slot_transpose(src, slot)
            for gl in range(GP):
                t_scr[gl, pl.ds(pl.multiple_of(nt * TR, TR), TR), j * 128:(j + 1) * 128] = by_group[gl].astype(BF16)
        return carry

    lax.fori_loop(0, R // TR, gather, 0, unroll=2)

    n = lax.broadcasted_iota(jnp.int32, (R, P2), 0) % nrow
    row = lax.broadcasted_iota(jnp.int32, (R, P2), 0)
    fwd = lax.broadcasted_iota(jnp.int32, (R, P2), 1) < S5_STATE
    fwd_b = lax.broadcasted_iota(jnp.int32, (nseq, P2), 1) < S5_STATE

    def shift(a, k):
        dn = pltpu.roll(a, k, 0)
        up = pltpu.roll(a, R - k, 0)
        return jnp.where(fwd, jnp.where(n >= k, dn, 0.0), jnp.where(n < nrow - k, up, 0.0))

    def ends(t):
        if nseq == 1:
            return jnp.where(fwd_b, t[nrow - 1:nrow, :], t[0:1, :])
        return jnp.where(fwd_b, t[pl.ds(nrow - 1, nseq, stride=nrow), :], t[pl.ds(0, nseq, stride=nrow), :])

    def per_group(gl, carry):
        ub = t_scr[gl]
        x = _dot(ub, win_ref[gl])
        xr, xi = x[:, :P2], x[:, P2:]
        er, ei = shift(xr, 1), shift(xi, 1)
        if has_state:
            h0 = h0_ref[0, gl]
            for b in range(nseq):
                first = row == jnp.where(fwd, b * nrow, b * nrow + nrow - 1)
                er = jnp.where(first, h0[b:b + 1, :P2], er)
                ei = jnp.where(first, h0[b:b + 1, P2:], ei)
        for j in range(nlev):
            k = 2 ** j
            ar = ar_ref[gl, j:j + 1, :]
            ai = ai_ref[gl, j:j + 1, :]
            sr, si = shift(er, k), shift(ei, k)
            er, ei = er + ar * sr - ai * si, ei + ar * si + ai * sr
        ecat = jnp.concatenate([er, ei], axis=1).astype(BF16)
        y_scr[gl] = _dot(ub, m_ref[gl]) + _dot_nt(ecat, wout_ref[gl])
        a1r = ar_ref[gl, 0:1, :]
        a1i = ai_ref[gl, 0:1, :]
        tmp_r[...] = a1r * er - a1i * ei + xr
        tmp_i[...] = a1r * ei + a1i * er + xi
        last_ref[0, gl, :, :P2] = ends(tmp_r)
        last_ref[0, gl, :, P2:] = ends(tmp_i)
        return carry

    lax.fori_loop(0, GP, per_group, 0)

    def scatter(nt, carry):
        for j in range(2):
            ys = [y_scr[gl, pl.ds(pl.multiple_of(nt * TR, TR), TR), j * 128:(j + 1) * 128] for gl in range(GP)]
            by_token = _slot_transpose(ys, slot)
            for s8 in range(8):
                rows = token_rows(nt, 8 * j + s8)
                z_ref[0, rows, :] = _gelu_tanh(by_token[s8] + d_ref[...] * u_ref[0, rows, :])
        return carry

    lax.fori_loop(0, R // TR, scatter, 0, unroll=2)


def _s5_call(u, ops, d, h0, seq):
    m, win, wout, ar, ai = ops
    bm, lm, _ = u.shape
    nrow = seq // S5_CHUNK
    nlev = int(math.log2(nrow))
    total = bm * lm // seq
    nseq = max(1, min(total, S5_BLOCK_ROWS // nrow))
    nblk = total // nseq
    R = nseq * nrow
    tok = R * S5_CHUNK
    GP = S5_SLAB_GROUPS
    ns = BRANCH // 128
    has_state = h0 is not None
    slab = pl.BlockSpec((1, tok, 128), lambda t, i: (i, 0, t))
    op = pl.BlockSpec((GP, S5_ROW, S5_ROW), lambda t, i: (t, 0, 0))
    lev = pl.BlockSpec((GP, ar.shape[1], 2 * S5_STATE), lambda t, i: (t, 0, 0))
    state = pl.BlockSpec((1, GP, nseq, 4 * S5_STATE), lambda t, i: (i, t, 0, 0))
    in_specs = [slab, op, op, op, lev, lev, pl.BlockSpec((1, 128), lambda t, i: (0, t))]
    args = [u.reshape(nblk, tok, BRANCH), m, win, wout, ar, ai, d]
    if has_state:
        in_specs.append(state)
        args.append(h0.reshape(nblk, nseq, S5_GROUPS, 4 * S5_STATE).transpose(0, 2, 1, 3))
    z, last = pl.pallas_call(
        functools.partial(_s5_kernel, nseq=nseq, nrow=nrow, nlev=nlev, has_state=has_state),
        grid=(ns, nblk),
        in_specs=in_specs,
        out_specs=[slab, state],
        out_shape=[jax.ShapeDtypeStruct((nblk, tok, BRANCH), F32),
                   jax.ShapeDtypeStruct((nblk, S5_GROUPS, nseq, 4 * S5_STATE), F32)],
        scratch_shapes=[pltpu.VMEM((GP, R, S5_ROW), BF16), pltpu.VMEM((GP, R, S5_ROW), F32),
                        pltpu.VMEM((R, 2 * S5_STATE), F32), pltpu.VMEM((R, 2 * S5_STATE), F32)],
        compiler_params=_cparams("parallel", "parallel"),
        name="s5",
    )(*args)
    last = last.transpose(0, 2, 1, 3).reshape(total, S5_GROUPS, 4 * S5_STATE)
    return z.reshape(bm, lm, BRANCH), last


HEAD_ORDER = (0, 2, 1, 3)


def _stack_heads(q, heads):
    head = lax.broadcasted_iota(jnp.int32, q.shape, 1) // ATT_HD
    zero = jnp.zeros((), q.dtype)
    return jnp.concatenate([jnp.where(head == h, q, zero) for h in heads], axis=0)


def _key_value_operands(kv):
    low = lax.broadcasted_iota(jnp.int32, kv.shape, 1) < ATT_HD
    sw = pltpu.roll(kv, ATT_HD, 1)
    kk = jnp.where(low, kv, sw).astype(BF16)
    return (jnp.concatenate([kk, kk], axis=1),
            jnp.where(low, sw, 1.0).astype(BF16), jnp.where(low, 1.0, kv).astype(BF16))


LOG2E = 1.4426950408889634
Q_SCALE = (ATT_HD ** -0.5) * LOG2E


def _attend_pair(q, parity, key_sets, sink_ref):
    rows = q.shape[0]
    heads = (parity, parity + 2)
    qs = _stack_heads(q, heads)
    sink = jnp.concatenate([jnp.broadcast_to(sink_ref[0, h:h + 1, 0:1], (rows, 1)) for h in heads], axis=0) * LOG2E
    scores = []
    t = None
    for k4, _, _, bias in key_sets:
        s = _dot_nt(qs, k4)
        if bias is not None:
            s = s + jnp.concatenate([bias, bias], axis=0)
        scores.append(s)
        for c in range(0, s.shape[1], 128):
            t = s[:, c:c + 128] if t is None else jnp.maximum(t, s[:, c:c + 128])
    m = jnp.maximum(jnp.max(t, axis=-1, keepdims=True), sink)
    lane = lax.broadcasted_iota(jnp.int32, (2 * rows, 2 * ATT_HD), 1)
    den_lanes = (lane >= ATT_HD) if parity == 0 else (lane < ATT_HD)
    o = jnp.where(den_lanes, jnp.exp2(sink - m), 0.0)
    for s, (_, va, vb, _) in zip(scores, key_sets):
        o = o + _dot(jnp.exp2(s - m).astype(BF16), va if parity == 0 else vb)
    return o / pltpu.roll(o, ATT_HD, 1)


def _attend_stacked(q, key_sets, sink_ref):
    rows = q.shape[0]
    qs = _stack_heads(q, HEAD_ORDER)
    sink = jnp.concatenate([jnp.broadcast_to(sink_ref[0, h:h + 1, 0:1], (rows, 1)) for h in HEAD_ORDER],
                           axis=0) * LOG2E
    scores = []
    t = None
    for k4, _, _, bias in key_sets:
        s = _dot_nt(qs, k4)
        if bias is not None:
            s = s + jnp.concatenate([bias] * ATT_GROUPS, axis=0)
        scores.append(s)
        for c in range(0, s.shape[1], 128):
            t = s[:, c:c + 128] if t is None else jnp.maximum(t, s[:, c:c + 128])
    m = jnp.maximum(jnp.max(t, axis=-1, keepdims=True), sink)
    e_sink = jnp.exp2(sink - m)
    low = lax.broadcasted_iota(jnp.int32, (2 * rows, 2 * ATT_HD), 1) < ATT_HD
    o_even = jnp.where(low, 0.0, e_sink[:2 * rows])
    o_odd = jnp.where(low, e_sink[2 * rows:], 0.0)
    for s, (_, va, vb, _) in zip(scores, key_sets):
        p = jnp.exp2(s - m).astype(BF16)
        o_even = o_even + _dot(p[:2 * rows], va)
        o_odd = o_odd + _dot(p[2 * rows:], vb)
    return o_even / pltpu.roll(o_even, ATT_HD, 1), o_odd / pltpu.roll(o_odd, ATT_HD, 1)


def _attend(q, key_sets, sink_ref, split):
    rows = q.shape[0]
    if split:
        o_even = _attend_pair(q, 0, key_sets, sink_ref)
        o_odd = _attend_pair(q, 1, key_sets, sink_ref)
    else:
        o_even, o_odd = _attend_stacked(q, key_sets, sink_ref)
    low = lax.broadcasted_iota(jnp.int32, o_even.shape, 1) < ATT_HD
    pair = jnp.where(low, o_even, o_odd)
    return jnp.concatenate([pair[:rows], pair[rows:]], axis=1)


def _att_ctx_kernel(q_ref, kv_ref, sa_ref, sink_ref, y_ref, kc_ref, vc_ref):
    kv = kv_ref[0]
    kc_ref[0, 0] = kv[:, :ATT_HD]
    vc_ref[0, 0] = kv[:, ATT_HD:]
    out = _attend(q_ref[0], [_key_value_operands(kv) + (None,)], sink_ref, split=True)
    y_ref[0] = (out * sa_ref[0].astype(F32)).astype(BF16)


def _att_ctx_call(q, kv, sa, sink):
    bsz, seq, _ = q.shape
    W = ATT_GROUPS * ATT_HD
    per_kv = lambda b, k: (b, 0, k)
    cache_spec = pl.BlockSpec((1, 1, seq, ATT_HD), lambda b, k: (b, k, 0, 0))
    return pl.pallas_call(
        _att_ctx_kernel,
        grid=(bsz, ATT_KV_HEADS),
        in_specs=[
            pl.BlockSpec((1, seq, W), per_kv),
            pl.BlockSpec((1, seq, 2 * ATT_HD), per_kv),
            pl.BlockSpec((1, seq, W), per_kv),
            pl.BlockSpec((1, 8, 128), lambda b, k: (k, 0, 0)),
        ],
        out_specs=[pl.BlockSpec((1, seq, W), per_kv), cache_spec, cache_spec],
        out_shape=[jax.ShapeDtypeStruct((bsz, seq, BRANCH), BF16),
                   jax.ShapeDtypeStruct((bsz, ATT_KV_HEADS, seq, ATT_HD), F32),
                   jax.ShapeDtypeStruct((bsz, ATT_KV_HEADS, seq, ATT_HD), F32)],
        compiler_params=_cparams("parallel", "parallel"),
        name="att_ctx",
    )(q, kv, sa, sink)


def _att_lat_kernel(q_ref, kv_ref, sa_ref, sink_ref, kc_ref, vc_ref, y_ref, k4_scr, va_scr, vb_scr, *, seq):
    nb = seq // QBLK
    KW = 3 * QBLK

    def k_body(i, carry):
        r = pl.ds(pl.multiple_of(i * QBLK, QBLK), QBLK)
        k4, va, vb = _key_value_operands(kv_ref[0, r, :])
        k4_scr[r, :] = k4
        va_scr[r, :] = va
        vb_scr[r, :] = vb
        return carry

    lax.fori_loop(0, nb, k_body, 0)
    kc = kc_ref[0, 0, 0]
    vc = vc_ref[0, 0, 0]
    ones = jnp.ones_like(vc)
    kc4 = jnp.concatenate([kc] * ATT_GROUPS, axis=1).astype(BF16)
    vca = jnp.concatenate([vc, ones], axis=1).astype(BF16)
    vcb = jnp.concatenate([ones, vc], axis=1).astype(BF16)
    qi = lax.broadcasted_iota(jnp.int32, (QBLK, KW), 0)
    kj = lax.broadcasted_iota(jnp.int32, (QBLK, KW), 1)

    def q_body(i, carry):
        r = pl.ds(pl.multiple_of(i * QBLK, QBLK), QBLK)
        ws = pl.multiple_of(jnp.clip(i * QBLK - QBLK, 0, seq - KW), QBLK)
        win = pl.ds(ws, KW)
        bias = jnp.where(jnp.abs((i * QBLK + qi) - (ws + kj)) <= WINDOW, 0.0, NEG)
        key_sets = [(k4_scr[win, :], va_scr[win, :], vb_scr[win, :], bias), (kc4, vca, vcb, None)]
        out = _attend(q_ref[0, r, :], key_sets, sink_ref, split=False)
        y_ref[0, r, :] = (out * sa_ref[0, r, :].astype(F32)).astype(BF16)
        return carry

    lax.fori_loop(0, nb, q_body, 0, unroll=2)


def _rope_tables(seq, heads, pad_heads):
    pos = jnp.arange(seq)
    row = (pos // GRID_W).astype(F32)[:, None]
    col = (pos % GRID_W).astype(F32)[:, None]
    nf = ATT_HD // 4
    freq = ROPE_BASE ** (-jnp.arange(nf, dtype=F32) / nf)
    ar, ac = row * freq, col * freq
    cos = jnp.concatenate([jnp.cos(ar), jnp.cos(ar), jnp.cos(ac), jnp.cos(ac)], axis=1)
    sin = jnp.concatenate([-jnp.sin(ar), jnp.sin(ar), -jnp.sin(ac), jnp.sin(ac)], axis=1)
    cos = jnp.concatenate([jnp.tile(cos, (1, heads)), jnp.ones((seq, pad_heads * ATT_HD), F32)], axis=1)
    sin = jnp.concatenate([jnp.tile(sin, (1, heads)), jnp.zeros((seq, pad_heads * ATT_HD), F32)], axis=1)
    return cos, sin


def _att_lat_call(q, kv, sa, sink, cache_k, cache_v, e):
    bsz, seq, _ = q.shape
    past = cache_k.shape[3]
    W = ATT_GROUPS * ATT_HD
    per_kv = lambda b, k: (b, 0, k)
    cache_spec = pl.BlockSpec((1, 1, 1, past, ATT_HD), lambda b, k: (b, e, k, 0, 0))
    return pl.pallas_call(
        functools.partial(_att_lat_kernel, seq=seq),
        grid=(bsz, ATT_KV_HEADS),
        in_specs=[
            pl.BlockSpec((1, seq, W), per_kv),
            pl.BlockSpec((1, seq, 2 * ATT_HD), per_kv),
            pl.BlockSpec((1, seq, W), per_kv),
            pl.BlockSpec((1, 8, 128), lambda b, k: (k, 0, 0)),
            cache_spec,
            cache_spec,
        ],
        out_specs=pl.BlockSpec((1, seq, W), per_kv),
        out_shape=jax.ShapeDtypeStruct((bsz, seq, BRANCH), BF16),
        scratch_shapes=[pltpu.VMEM((seq, W), BF16), pltpu.VMEM((seq, 2 * ATT_HD), BF16),
                        pltpu.VMEM((seq, 2 * ATT_HD), BF16)],
        compiler_params=_cparams("parallel", "parallel"),
        name="att_lat",
    )(q, kv, sa, sink, cache_k, cache_v)


def _even_out_compute(x, refs):
    mod_ref, yg_ref, z_ref, ss_ref, wg_ref, bg_ref, wo_ref = refs
    z = z_ref[0]
    glu = z * jax.nn.sigmoid(_dot(z.astype(BF16), wg_ref[...]) + bg_ref[...])
    ys = (glu * ss_ref[0].astype(F32)).astype(BF16)
    out = _dot(yg_ref[0], wo_ref[:BRANCH, :]) + _dot(ys, wo_ref[BRANCH:, :])
    return x + mod_ref[0, 2:3, :] * out


def _even_out_part(tm, mod, yg, z, ss, wglu, bglu, wout):
    args = [mod, yg, z, ss, wglu, bglu, wout]
    specs = [_mod_spec(), _tile_spec(tm, BRANCH), _tile_spec(tm, BRANCH), _tile_spec(tm, BRANCH),
             _whole_spec(wglu), _whole_spec(bglu), _whole_spec(wout)]
    return _even_out_compute, args, specs


def _odd_out_compute(x, refs, seq, tm):
    mod_ref, ya_ref, p_ref, pprev_ref, pnext_ref, bgz_ref, cw_ref, cb_ref, wo_ref = refs
    p = p_ref[0].astype(F32)
    pos = (pl.program_id(1) * tm + lax.broadcasted_iota(jnp.int32, (tm, BRANCH), 0)) % seq
    rowi = lax.broadcasted_iota(jnp.int32, (tm, BRANCH), 0)
    halo = CONV_HALO_ROWS
    p_before = pprev_ref[0].astype(F32)[halo - 1:halo, :]
    p_after = pnext_ref[0].astype(F32)[0:1, :]
    prev = jnp.where(rowi == 0, p_before, pltpu.roll(p, 1, 0))
    nxt = jnp.where(rowi == tm - 1, p_after, pltpu.roll(p, tm - 1, 0))
    prev = jnp.where(pos == 0, 0.0, prev)
    nxt = jnp.where(pos == seq - 1, 0.0, nxt)
    conv = prev * cw_ref[0:1, :] + p * cw_ref[1:2, :] + nxt * cw_ref[2:3, :] + cb_ref[...]
    yc = (bgz_ref[0].astype(F32) * conv).astype(BF16)
    out = _dot(ya_ref[0], wo_ref[:BRANCH, :]) + _dot(yc, wo_ref[BRANCH:, :])
    return x + mod_ref[0, 2:3, :] * out


def _odd_out_part(tm, lm, seq, mod, ya, p, bgz, convw, convb, wout):
    hr = CONV_HALO_ROWS
    nth = lm // hr
    prev_spec = pl.BlockSpec((1, hr, BRANCH), lambda b, i: (b, jnp.maximum(i * (tm // hr) - 1, 0), 0))
    next_spec = pl.BlockSpec((1, hr, BRANCH), lambda b, i: (b, jnp.minimum((i + 1) * (tm // hr), nth - 1), 0))
    args = [mod, ya, p, p, p, bgz, convw, convb, wout]
    specs = [_mod_spec(), _tile_spec(tm, BRANCH), _tile_spec(tm, BRANCH), prev_spec, next_spec,
             _tile_spec(tm, BRANCH), _whole_spec(convw), _whole_spec(convb), _whole_spec(wout)]
    return functools.partial(_odd_out_compute, seq=seq, tm=tm), args, specs


def _even_in_weight(w):
    dk = GLA_HEADS * GLA_DK
    q, k = w[:, :dk], w[:, dk:2 * dk]
    v = w[:, 2 * dk:2 * dk + BRANCH]
    o = 2 * dk + BRANCH
    lr = w[:, o:o + 2 * GLA_RANK]
    rest = w[:, o + 2 * GLA_RANK:]
    qk = jnp.concatenate([jnp.concatenate([q[:, h * GLA_DK:(h + 1) * GLA_DK], k[:, h * GLA_DK:(h + 1) * GLA_DK]],
                                          axis=1) for h in range(GLA_HEADS)], axis=1)
    lr = jnp.pad(lr, ((0, 0), (0, 128 - 2 * GLA_RANK)))
    return jnp.concatenate([qk, v, lr, rest], axis=1).astype(BF16)


def _odd_in_weight(w):
    q = w[:, :BRANCH]
    k = w[:, BRANCH:BRANCH + ATT_KV_HEADS * ATT_HD]
    v = w[:, BRANCH + ATT_KV_HEADS * ATT_HD:BRANCH + 2 * ATT_KV_HEADS * ATT_HD]
    rest = w[:, BRANCH + 2 * ATT_KV_HEADS * ATT_HD:]
    kv = jnp.concatenate([jnp.concatenate([k[:, h * ATT_HD:(h + 1) * ATT_HD], v[:, h * ATT_HD:(h + 1) * ATT_HD]],
                                          axis=1) for h in range(ATT_KV_HEADS)], axis=1)
    return jnp.concatenate([q, kv, rest], axis=1).astype(BF16)


def _gate_weights(w2, b2):
    zf = jnp.zeros((GLA_RANK, GLA_DK), F32)
    cols, bias = [], []
    for h in range(GLA_HEADS):
        sl = slice(h * GLA_DK, (h + 1) * GLA_DK)
        cols.append(jnp.concatenate([jnp.concatenate([w2[0][:, sl], zf], axis=1),
                                     jnp.concatenate([zf, w2[1][:, sl]], axis=1)], axis=0))
        bias += [b2[0][sl], b2[1][sl]]
    w = jnp.pad(jnp.concatenate(cols, axis=1), ((0, 128 - 2 * GLA_RANK), (0, 0)))
    w_hi = w.astype(BF16)
    w_lo = (w - w_hi.astype(F32)).astype(BF16)
    return w_hi, w_lo, jnp.concatenate(bias)[None, :]


def _run_stream(x, mods, seq, params, gla_s0, s5_h0, cache_k, cache_v):
    depth = len(params)
    bm, lm, _ = x.shape
    tm = min(ROW_TILE, lm)
    nseq = bm * lm // seq
    per_seq = lambda a: a.reshape(nseq, seq, a.shape[-1])
    flat = lambda a: a.reshape(bm, lm, a.shape[-1])
    latent = cache_k is not None
    tables = _rope_tables(seq, 2, 0) + _rope_tables(seq, 1, 1) if latent else ()

    def in_part(l):
        p = params[l]
        if l % 2 == 0:
            return _even_in_part(tm, mods[l], p["nw"], p["w_in"], p["gla_w"])
        return _odd_in_part(tm, mods[l], p["nw"], p["w_in"], p["qnw"], p["knw"], tables)

    collected = {"gla": [], "s5": [], "k": [], "v": []}
    out_part = None
    for l in range(depth + 1):
        res = _stage_call("stage%d" % l, x, out_part, in_part(l) if l < depth else None)
        if out_part is not None:
            x, res = res[0], res[1:]
        if l == depth:
            break
        p, e = params[l], l // 2
        if l % 2 == 0:
            qk, v, bc, sg, u, ss = res
            yg, s_new = _gla_call(per_seq(qk), per_seq(v), per_seq(bc), per_seq(sg), p["onorm"],
                                  None if gla_s0 is None else gla_s0[:, e])
            z, last = _s5_call(u, p["s5_ops"], p["d"], None if s5_h0 is None else s5_h0[e], seq)
            collected["gla"].append(s_new)
            collected["s5"].append(last)
            out_part = _even_out_part(tm, mods[l], flat(yg), z, ss, p["wglu"], p["bglu"], p["wout"])
        else:
            q, kv, sa, pc, bgz = res
            if latent:
                ya = _att_lat_call(per_seq(q), per_seq(kv), per_seq(sa), p["sink"], cache_k, cache_v, e)
            else:
                ya, kc, vc = _att_ctx_call(per_seq(q), per_seq(kv), per_seq(sa), p["sink"])
                collected["k"].append(kc)
                collected["v"].append(vc)
            out_part = _odd_out_part(tm, lm, seq, mods[l], flat(ya), pc, bgz, p["convw"], p["convb"], p["wout"])
    return x, collected


def kernel(x_prompt, x_sample, c, state_gla, state_s5_re, state_s5_im, cache_k, cache_v, c_ctx, norm_w, w_ada, b_ada, w_in_e, w_out_e, gla_w2, gla_b2, gla_onorm, s5_lam_re, s5_lam_im, s5_log_dt, s5_b_re, s5_b_im, s5_c_re, s5_c_im, s5_d, s5_w_glu, s5_b_glu, w_in_o, w_out_o, q_norm_w, k_norm_w, sink, conv_w, conv_b):
    depth = norm_w.shape[0]
    bp, lp, _ = x_prompt.shape
    bs, ls, _ = x_sample.shape

    rows = 8 * ((1 + bs + 7) // 8)
    cs = jnp.zeros((rows, D_MODEL), F32).at[0].set(c_ctx).at[1:1 + bs].set(c)
    mods = _ada_call(cs, w_ada, b_ada)

    params, s5_h0 = [], []
    for l in range(depth):
        e = l // 2
        p = {"nw": norm_w[l][None, :]}
        if l % 2 == 0:
            p.update(
                w_in=_even_in_weight(w_in_e[e]), wout=w_out_e[e].astype(BF16),
                gla_w=_gate_weights(gla_w2[e], gla_b2[e]), onorm=gla_onorm[e][None, :],
                d=s5_d[e][None, :], wglu=s5_w_glu[e].astype(BF16), bglu=s5_b_glu[e][None, :],
                s5_ops=_s5_operators(s5_lam_re[e], s5_lam_im[e], s5_log_dt[e], s5_b_re[e], s5_b_im[e],
                                     s5_c_re[e], s5_c_im[e], nlev=int(math.log2(max(lp, ls) // S5_CHUNK))))
            h0 = jnp.concatenate([state_s5_re[:, e], state_s5_im[:, e]], axis=1)
            s5_h0.append(h0.transpose(0, 2, 1, 3).reshape(bs, S5_GROUPS, 4 * S5_STATE))
        else:
            sk = jnp.broadcast_to(sink[e].reshape(ATT_KV_HEADS, ATT_GROUPS, 1), (ATT_KV_HEADS, ATT_GROUPS, 128))
            p.update(
                w_in=_odd_in_weight(w_in_o[e]), wout=w_out_o[e].astype(BF16),
                qnw=jnp.tile(q_norm_w[e], ATT_HEADS)[None, :],
                knw=jnp.tile(jnp.concatenate([k_norm_w[e], jnp.ones((ATT_HD,), F32)]), ATT_KV_HEADS)[None, :],
                sink=jnp.concatenate([sk, jnp.zeros((ATT_KV_HEADS, 8 - ATT_GROUPS, 128), F32)], axis=1),
                convw=conv_w[e], convb=conv_b[e][None, :])
        params.append(p)

    mods_p = [mods[l, 0:1].reshape(1, 3, D_MODEL) for l in range(depth)]
    mods_s = [mods[l, 1:1 + bs].reshape(bs, 3, D_MODEL) for l in range(depth)]
    xp, got = _run_stream(x_prompt.reshape(1, bp * lp, D_MODEL), mods_p, lp, params, None, None, None, None)
    xs, _ = _run_stream(x_sample, mods_s, ls, params, state_gla, s5_h0, cache_k, cache_v)
    last = [t.reshape(bp, S5_GROUPS, 2, 2, S5_STATE).transpose(2, 0, 3, 1, 4) for t in got["s5"]]
    return (xp.reshape(bp, lp, D_MODEL), xs,
            jnp.stack(got["gla"], axis=1),
            jnp.stack([t[0] for t in last], axis=1), jnp.stack([t[1] for t in last], axis=1),
            jnp.stack(got["k"], axis=1), jnp.stack(got["v"], axis=1))
```

```python
import functools
import math

import jax
import jax.numpy as jnp
from jax import lax
from jax.experimental import pallas as pl
from jax.experimental.pallas import tpu as pltpu

F32 = jnp.float32
BF16 = jnp.bfloat16
HIGHEST = lax.Precision.HIGHEST

D_MODEL = 1024
BRANCH = D_MODEL // 2
GLA_HEADS = 4
GLA_DK = 64
GLA_DV = 128
GLA_RANK = 16
GLA_GATE_NORM = 16.0
GLA_CHUNK = 64
GLA_BLOCK = 256
S5_GROUP = 16
S5_GROUPS = BRANCH // S5_GROUP
S5_STATE = 64
S5_CHUNK = 16
S5_ROW = S5_CHUNK * S5_GROUP
S5_SLAB_GROUPS = 128 // S5_GROUP
S5_BLOCK_ROWS = 256
ATT_HEADS = 8
ATT_KV_HEADS = 2
ATT_GROUPS = ATT_HEADS // ATT_KV_HEADS
ATT_HD = 64
WINDOW = 128
QBLK = 128
GRID_W = 64
ROPE_BASE = 10000.0
CONV_W = 3
CONV_HALO_ROWS = 16
EPS = 1e-6
NEG = -1e30

ROW_TILE = 512
VMEM_LIMIT = 48 * 1024 * 1024


def _cparams(*sem):
    return pltpu.CompilerParams(dimension_semantics=sem, vmem_limit_bytes=VMEM_LIMIT)


def _silu(x):
    return x * jax.nn.sigmoid(x)


def _dot(a, b):
    return jnp.dot(a, b, preferred_element_type=F32)


def _dot_nt(a, b):
    return lax.dot_general(a, b, (((1,), (1,)), ((), ())), preferred_element_type=F32)


def _ada_kernel(c_ref, w_ref, b_ref, o_ref):
    c = c_ref[...]
    o_ref[0] = jnp.dot(_silu(c), w_ref[0], precision=HIGHEST, preferred_element_type=F32) + b_ref[0]


def _ada_call(cs, w_ada, b_ada):
    depth = w_ada.shape[0]
    rows = cs.shape[0]
    nt = 3
    return pl.pallas_call(
        _ada_kernel,
        grid=(depth, nt),
        in_specs=[
            pl.BlockSpec((rows, D_MODEL), lambda l, j: (0, 0)),
            pl.BlockSpec((1, D_MODEL, D_MODEL), lambda l, j: (l, 0, j)),
            pl.BlockSpec((1, 1, D_MODEL), lambda l, j: (l, 0, j)),
        ],
        out_specs=pl.BlockSpec((1, rows, D_MODEL), lambda l, j: (l, 0, j)),
        out_shape=jax.ShapeDtypeStruct((depth, rows, 3 * D_MODEL), F32),
        compiler_params=_cparams("arbitrary", "arbitrary"),
        name="adaln",
    )(cs, w_ada, b_ada.reshape(depth, 1, 3 * D_MODEL))


def _modulated(x, mod_ref, nw_ref):
    ms = jnp.mean(x * x, axis=-1, keepdims=True)
    y = x * lax.rsqrt(ms + EPS) * nw_ref[...]
    shift = mod_ref[0, 0:1, :]
    scale = mod_ref[0, 1:2, :]
    return (y * (1.0 + scale) + shift).astype(BF16)


E_COLS = (0, 512, 1024, 1152, 1664, 2176, 2688)


def _log_sigmoid(x):
    return jnp.minimum(x, 0.0) - jnp.log(1.0 + jnp.exp(-jnp.abs(x)))


def _chunk_cumsum(g):
    rows, width = g.shape
    pos = lax.broadcasted_iota(jnp.int32, g.shape, 0) % GLA_CHUNK
    fwd = (lax.broadcasted_iota(jnp.int32, g.shape, 1) % (2 * GLA_DK)) < GLA_DK
    c = g
    k = 1
    while k < GLA_CHUNK:
        c = c + jnp.where(pos >= k, pltpu.roll(c, k, 0), 0.0)
        k *= 2
    total = jnp.concatenate([jnp.broadcast_to(c[r + GLA_CHUNK - 1:r + GLA_CHUNK, :], (GLA_CHUNK, width))
                             for r in range(0, rows, GLA_CHUNK)], axis=0)
    return jnp.where(fwd, c, total - c + g)


def _even_in_compute(x, refs, outs):
    mod_ref, nw_ref, w_ref, w2h_ref, w2l_ref, b2_ref = refs
    qk_ref, v_ref, bc_ref, sg_ref, u_ref, ss_ref = outs
    h = _modulated(x, mod_ref, nw_ref)
    c = E_COLS
    lr = _dot(h, w_ref[:, c[2]:c[3]])
    lr_hi = lr.astype(BF16)
    lr_lo = (lr - lr_hi.astype(F32)).astype(BF16)
    pre = _dot(lr_hi, w2h_ref[...]) + (_dot(lr_lo, w2h_ref[...]) + _dot(lr_hi, w2l_ref[...])) + b2_ref[...]
    bc_ref[0] = _chunk_cumsum(_log_sigmoid(pre) * (1.0 / GLA_GATE_NORM))
    qk_ref[0] = _dot(h, w_ref[:, c[0]:c[1]]).astype(BF16)
    v_ref[0] = _dot(h, w_ref[:, c[1]:c[2]]).astype(BF16)
    sg_ref[0] = _silu(_dot(h, w_ref[:, c[3]:c[4]])).astype(BF16)
    u_ref[0] = _dot(h, w_ref[:, c[4]:c[5]])
    ss_ref[0] = _silu(_dot(h, w_ref[:, c[5]:c[6]])).astype(BF16)


O_COLS = (0, 512, 768, 1280, 1792, 2304, 2816, 3328)


def _rope(x, cos, sin):
    width = x.shape[1]
    lane = lax.broadcasted_iota(jnp.int32, x.shape, 1)
    partner = jnp.where((lane & 16) == 0, pltpu.roll(x, width - 16, 1), pltpu.roll(x, 16, 1))
    return x * cos + partner * sin


def _head_pair_norm(x, w, cos, sin, both):
    low = lax.broadcasted_iota(jnp.int32, x.shape, 1) < ATT_HD
    x2 = x * x
    ms = jnp.sum(jnp.where(low, x2, 0.0), axis=-1, keepdims=True) * (1.0 / ATT_HD)
    rs = lax.rsqrt(ms + EPS)
    if both:
        ms_hi = jnp.sum(jnp.where(low, 0.0, x2), axis=-1, keepdims=True) * (1.0 / ATT_HD)
        rs = jnp.where(low, rs, lax.rsqrt(ms_hi + EPS))
    else:
        rs = jnp.where(low, rs, 1.0)
    y = x * rs * w
    return y if cos is None else _rope(y, cos, sin)


def _odd_in_compute(x, refs, outs, rope):
    if rope:
        mod_ref, nw_ref, w_ref, qn_ref, kn_ref, cq_ref, sq_ref, ck_ref, sk_ref = refs
        cq, sq, ck, sk = cq_ref[...], sq_ref[...], ck_ref[...], sk_ref[...]
    else:
        mod_ref, nw_ref, w_ref, qn_ref, kn_ref = refs
        cq = sq = ck = sk = None
    q_ref, kv_ref, sa_ref, p_ref, bgz_ref = outs
    h = _modulated(x, mod_ref, nw_ref)
    c = O_COLS
    q = _dot(h, w_ref[:, c[0]:c[1]])
    for t in range(BRANCH // 128):
        lanes = slice(t * 128, (t + 1) * 128)
        qt = _head_pair_norm(q[:, lanes], qn_ref[:, lanes], cq, sq, True)
        q_ref[0, :, lanes] = (qt * Q_SCALE).astype(BF16)
    kv = _dot(h, w_ref[:, c[1]:c[2]])
    for t in range(ATT_KV_HEADS):
        lanes = slice(t * 128, (t + 1) * 128)
        kv_ref[0, :, lanes] = _head_pair_norm(kv[:, lanes], kn_ref[:, lanes], ck, sk, False)
    sa_ref[0] = _silu(_dot(h, w_ref[:, c[2]:c[3]])).astype(BF16)
    xc = _dot(h, w_ref[:, c[3]:c[4]])
    bg = _dot(h, w_ref[:, c[4]:c[5]])
    cg = _dot(h, w_ref[:, c[5]:c[6]])
    zc = _dot(h, w_ref[:, c[6]:c[7]])
    p_ref[0] = (cg * xc).astype(BF16)
    bgz_ref[0] = (bg * _silu(zc)).astype(BF16)


def _tile_spec(tm, n):
    return pl.BlockSpec((1, tm, n), lambda b, i: (b, i, 0))


def _whole_spec(a):
    return pl.BlockSpec(a.shape, lambda b, i: (0,) * a.ndim)


def _mod_spec():
    return pl.BlockSpec((1, 3, D_MODEL), lambda b, i: (b, 0, 0))


def _even_in_part(tm, mod, nw, w_in, gla_w):
    args = [mod, nw, w_in, *gla_w]
    specs = [_mod_spec()] + [_whole_spec(a) for a in args[1:]]
    outs = [(BRANCH, dt) for dt in (BF16, BF16, F32, BF16, F32, BF16)]
    return _even_in_compute, args, specs, outs


def _odd_in_part(tm, mod, nw, w_in, qnw, knw, tables):
    args = [mod, nw, w_in, qnw, knw, *tables]
    specs = ([_mod_spec()] + [_whole_spec(a) for a in args[1:5]]
             + [pl.BlockSpec((tm, a.shape[1]), lambda b, i: (i, 0)) for a in tables])
    outs = [(BRANCH, BF16), (2 * ATT_KV_HEADS * ATT_HD, F32), (BRANCH, BF16), (BRANCH, BF16), (BRANCH, BF16)]
    return functools.partial(_odd_in_compute, rope=bool(tables)), args, specs, outs


def _stage_kernel(*refs, n_out_args, n_in_args, out_fn, in_fn):
    x_ref = refs[0]
    out_args = refs[1:1 + n_out_args]
    in_args = refs[1 + n_out_args:1 + n_out_args + n_in_args]
    outs = refs[1 + n_out_args + n_in_args:]
    x = x_ref[0]
    if out_fn is not None:
        x = out_fn(x, out_args)
        outs[0][0] = x
        outs = outs[1:]
    if in_fn is not None:
        in_fn(x, in_args, outs)


def _stage_call(name, x, out_part, in_part):
    bm, lm, _ = x.shape
    tm = min(ROW_TILE, lm)
    args, specs, out_defs = [x], [_tile_spec(tm, D_MODEL)], []
    out_fn = in_fn = None
    n_out_args = n_in_args = 0
    if out_part is not None:
        out_fn, a, s = out_part
        args, specs, n_out_args = args + a, specs + s, len(a)
        out_defs.append((D_MODEL, F32))
    if in_part is not None:
        in_fn, a, s, o = in_part
        args, specs, n_in_args = args + a, specs + s, len(a)
        out_defs += o
    res = pl.pallas_call(
        functools.partial(_stage_kernel, n_out_args=n_out_args, n_in_args=n_in_args, out_fn=out_fn, in_fn=in_fn),
        grid=(bm, lm // tm),
        in_specs=specs,
        out_specs=[_tile_spec(tm, n) for n, _ in out_defs],
        out_shape=[jax.ShapeDtypeStruct((bm, lm, n), dt) for n, dt in out_defs],
        compiler_params=_cparams("parallel", "parallel"),
        name=name,
    )(*args)
    return res


def _loop(n, body, init, unroll):
    if n <= unroll:
        for i in range(n):
            init = body(i, init)
        return init
    return lax.fori_loop(0, n, body, init, unroll=unroll)


def _gla_kernel(*refs, seq, has_state, heads):
    if has_state:
        (qk_ref, v_ref, bc_ref, sg_ref, on_ref, _, s0_ref, y_ref, sn_ref,
         ut_scr, sf_scr, sb_scr, qd_scr, o_scr) = refs
    else:
        (qk_ref, v_ref, bc_ref, sg_ref, on_ref, _, y_ref, sn_ref,
         ut_scr, sf_scr, sb_scr, qd_scr, o_scr) = refs
        s0_ref = None
    C = GLA_CHUNK
    W = 2 * GLA_DK
    R = GLA_BLOCK
    CB = R // C
    nc = seq // C
    nb = seq // R
    ri = lax.broadcasted_iota(jnp.int32, (R, R), 0)
    ci = lax.broadcasted_iota(jnp.int32, (R, R), 1)
    same = (ri // C) == (ci // C)
    tril = same & (ri >= ci)
    triu = same & (ri <= ci)
    fwd = lax.broadcasted_iota(jnp.int32, (R, W), 1) < GLA_DK
    fwd_row = lax.broadcasted_iota(jnp.int32, (1, W), 1) < GLA_DK
    fwd_st = lax.broadcasted_iota(jnp.int32, (GLA_DV, W), 1) < GLA_DK
    row_chunk = lax.broadcasted_iota(jnp.int32, (R, W), 0) // C
    scale = GLA_DK ** -0.5

    def rows(c):
        return pl.ds(pl.multiple_of(c * C, C), C)

    def block_rows(j):
        return pl.ds(pl.multiple_of(j * R, R), R)

    def state_rows(c):
        return pl.ds(pl.multiple_of(c * GLA_DV, GLA_DV), GLA_DV)

    def intra(h, j, carry):
        r = block_rows(j)
        hl = slice(h * W, (h + 1) * W)
        qk = qk_ref[0, r, hl].astype(F32)
        bc = bc_ref[0, r, hl]
        v = v_ref[0, r, hl].astype(F32)
        sw = pltpu.roll(qk, GLA_DK, 1)
        q2 = jnp.where(fwd, qk, sw) * scale
        k2 = jnp.where(fwd, sw, qk)
        b_mid = bc[C // 2:C // 2 + 1, :]
        b_end = jnp.where(fwd_row, bc[C - 1:C, :], bc[0:1, :])
        for c in range(1, CB):
            o0 = c * C
            b_mid = jnp.where(row_chunk == c, bc[o0 + C // 2:o0 + C // 2 + 1, :], b_mid)
            b_end = jnp.where(row_chunk == c, jnp.where(fwd_row, bc[o0 + C - 1:o0 + C, :], bc[o0:o0 + 1, :]), b_end)
        qd = (q2 * jnp.exp(bc)).astype(BF16)
        qh = q2 * jnp.exp(bc - b_mid)
        kh = (k2 * jnp.exp(b_mid - bc)).astype(BF16)
        att = (jnp.where(tril, _dot_nt(jnp.where(fwd, qh, 0.0).astype(BF16), kh), 0.0)
               + jnp.where(triu, _dot_nt(jnp.where(fwd, 0.0, qh).astype(BF16), kh), 0.0))
        o = _dot(att.astype(BF16), v.astype(BF16))
        kd = k2 * jnp.exp(b_end - bc)
        kd_wide = jnp.concatenate([jnp.where(row_chunk == c, kd, 0.0) for c in range(CB)], axis=1)
        ut = _dot(v.T.astype(BF16), kd_wide.astype(BF16))
        qd_scr[h, r, :] = qd
        o_scr[h, r, :] = o
        ut_scr[h, j] = ut
        return carry

    for h in range(heads):
        _loop(nb, functools.partial(intra, h), 0, 2)

    def scan(h, j, st):
        jb = nb - 1 - j
        hl = slice(h * W, (h + 1) * W)
        bcf = bc_ref[0, block_rows(j), hl]
        bcb = bc_ref[0, block_rows(jb), hl]
        utf = ut_scr[h, j]
        utb = ut_scr[h, jb]
        for c in range(CB):
            cb = CB - 1 - c
            stb = st.astype(BF16)
            sf_scr[h, j, :, c * W:(c + 1) * W] = stb
            sb_scr[h, jb, :, cb * W:(cb + 1) * W] = stb
            b_end = jnp.where(fwd_row, bcf[c * C + C - 1:c * C + C, :], bcb[cb * C:cb * C + 1, :])
            inc = jnp.where(fwd_st, utf[:, c * W:(c + 1) * W], utb[:, cb * W:(cb + 1) * W])
            st = st * jnp.exp(b_end) + inc
        return st

    for h in range(heads):
        if has_state:
            init = jnp.concatenate([s0_ref[0, 0, h].T, s0_ref[0, 1, h].T], axis=1)
        else:
            init = jnp.zeros((GLA_DV, W), F32)
        st = _loop(nb, functools.partial(scan, h), init, 1)
        sn_ref[0, 0, 0, h] = st[:, :GLA_DK].T
        sn_ref[0, 0, 1, h] = st[:, GLA_DK:].T

    fwd_wide = (lax.broadcasted_iota(jnp.int32, (GLA_DV, CB * W), 1) % W) < GLA_DK

    def finish(h, j, carry):
        r = block_rows(j)
        hl = slice(h * W, (h + 1) * W)
        qd = qd_scr[h, r, :]
        zero = jnp.zeros((), BF16)
        qd_wide = jnp.concatenate([jnp.where(row_chunk == c, qd, zero) for c in range(CB)], axis=1)
        s_in = jnp.where(fwd_wide, sf_scr[h, j], sb_scr[h, j])
        o = o_scr[h, r, :] + _dot_nt(qd_wide, s_in)
        ms = jnp.mean(o * o, axis=-1, keepdims=True)
        y_ref[0, r, hl] = (o * lax.rsqrt(ms + EPS) * on_ref[...] * sg_ref[0, r, hl].astype(F32)).astype(BF16)
        return carry

    for h in range(heads):
        _loop(nb, functools.partial(finish, h), 0, 2)


def _gla_call(qk, v, bc, sg, onorm, s0, layer, states):
    bsz, seq, _ = qk.shape
    has_state = s0 is not None
    nb = seq // GLA_BLOCK
    wide = (GLA_BLOCK // GLA_CHUNK) * 2 * GLA_DK
    hps = GLA_HEADS if nb == 1 else 1
    per_head = pl.BlockSpec((1, seq, 128 * hps), lambda b, h: (b, 0, h))
    in_specs = [per_head, per_head, per_head, per_head, pl.BlockSpec((1, GLA_DV), lambda b, h: (0, 0)),
                pl.BlockSpec(memory_space=pl.ANY)]
    args = [qk, v, bc, sg, onorm, states]
    if has_state:
        in_specs.append(pl.BlockSpec((1, 2, hps, GLA_DK, GLA_DV), lambda b, h: (b, 0, h, 0, 0)))
        args.append(s0)
    new_state_spec = pl.BlockSpec((1, 1, 2, hps, GLA_DK, GLA_DV), lambda b, h: (b, layer, 0, h, 0, 0))
    return pl.pallas_call(
        functools.partial(_gla_kernel, seq=seq, has_state=has_state, heads=hps),
        grid=(bsz, GLA_HEADS // hps),
        in_specs=in_specs,
        out_specs=[per_head, new_state_spec],
        out_shape=[jax.ShapeDtypeStruct((bsz, seq, BRANCH), BF16),
                   jax.ShapeDtypeStruct(states.shape, F32)],
        input_output_aliases={5: 1},
        scratch_shapes=[pltpu.VMEM((hps, nb, GLA_DV, wide), F32),
                        pltpu.VMEM((hps, nb, GLA_DV, wide), BF16),
                        pltpu.VMEM((hps, nb, GLA_DV, wide), BF16),
                        pltpu.VMEM((hps, seq, 2 * GLA_DK), BF16),
                        pltpu.VMEM((hps, seq, GLA_DV), F32)],
        compiler_params=_cparams("parallel", "parallel"),
        name="gla",
    )(*args)


def _s5_prep_kernel(lr_ref, li_ref, ld_ref, br_ref, bi_ref, cr_ref, ci_ref,
                    m_ref, win_ref, wout_ref, ar_ref, ai_ref, *, nlev):
    cs = S5_CHUNK
    P2 = 2 * S5_STATE
    lam_re, lam_im = lr_ref[0], li_ref[0]
    dt = jnp.exp(ld_ref[0])
    er, ei = lam_re * dt, lam_im * dt
    mag = jnp.exp(er)
    nr, ni = mag * jnp.cos(ei) - 1.0, mag * jnp.sin(ei)
    den = lam_re * lam_re + lam_im * lam_im
    cr = (nr * lam_re + ni * lam_im) / den
    ci = (ni * lam_re - nr * lam_im) / den
    b_re, b_im = br_ref[0], bi_ref[0]
    bb_re = cr * b_re - ci * b_im
    bb_im = cr * b_im + ci * b_re
    c_re, c_im = cr_ref[0], ci_ref[0]

    def power(k):
        m = jnp.exp(k * er)
        return m * jnp.cos(k * ei), m * jnp.sin(k * ei)

    def outer(a, b):
        return (a[:, None, :] * b[None, :, :]).reshape(S5_ROW, P2)

    def cmul(ar, ai, br, bi):
        return outer(ar, br) - outer(ai, bi), outer(ar, bi) + outer(ai, br)

    s_row = lax.broadcasted_iota(jnp.int32, (cs, P2), 0).astype(F32)
    fwd = lax.broadcasted_iota(jnp.int32, (cs, P2), 1) < S5_STATE
    half = float(cs // 2)
    k_in = jnp.where(fwd, half - s_row, s_row - half)
    l_re, l_im = cmul(*power(k_in), bb_re, bb_im)
    r_re, r_im = cmul(*power(-k_in), c_re, c_im)
    lcat = jnp.concatenate([l_re, l_im], axis=1)
    rcat = jnp.concatenate([r_re, -r_im], axis=1)
    fwd2 = (lax.broadcasted_iota(jnp.int32, (S5_ROW, S5_ROW), 1) % P2) < S5_STATE
    nt = (((1,), (1,)), ((), ()))
    m_f = lax.dot_general(jnp.where(fwd2, lcat, 0.0), rcat, nt, precision=HIGHEST, preferred_element_type=F32)
    m_b = lax.dot_general(jnp.where(fwd2, 0.0, lcat), rcat, nt, precision=HIGHEST, preferred_element_type=F32)
    s_in = lax.broadcasted_iota(jnp.int32, (S5_ROW, S5_ROW), 0) // S5_GROUP
    s_out = lax.broadcasted_iota(jnp.int32, (S5_ROW, S5_ROW), 1) // S5_GROUP
    m_ref[0] = (jnp.where(s_in <= s_out, m_f, 0.0) + jnp.where(s_in >= s_out, m_b, 0.0)).astype(BF16)
    w_re, w_im = cmul(*power(jnp.where(fwd, (cs - 1.0) - s_row, s_row)), bb_re, bb_im)
    win_ref[0] = jnp.concatenate([w_re, w_im], axis=1).astype(BF16)
    o_re, o_im = cmul(*power(jnp.where(fwd, s_row + 1.0, cs - s_row)), c_re, c_im)
    wout_ref[0] = jnp.concatenate([o_re, -o_im], axis=1).astype(BF16)
    lev = lax.shift_left(jnp.int32(cs), lax.broadcasted_iota(jnp.int32, (nlev, P2), 0)).astype(F32)
    ar_ref[0], ai_ref[0] = power(lev)


def _s5_operators(lam_re, lam_im, log_dt, b_re, b_im, c_re, c_im, nlev):
    P2 = 2 * S5_STATE
    lanes = lambda a: a.transpose(1, 0, 2).reshape(S5_GROUPS, 1, P2)
    both = lambda a: jnp.concatenate([a, a], axis=-1)
    args = (lanes(lam_re), lanes(lam_im),
            lanes(jnp.broadcast_to(log_dt[..., None], lam_re.shape)),
            both(b_re.transpose(0, 2, 1)), both(b_im.transpose(0, 2, 1)), both(c_re), both(c_im))
    g3 = lambda g: (g, 0, 0)
    vec = pl.BlockSpec((1, 1, P2), g3)
    mat = pl.BlockSpec((1, S5_GROUP, P2), g3)
    op = pl.BlockSpec((1, S5_ROW, S5_ROW), g3)
    lev = pl.BlockSpec((1, nlev, P2), g3)
    return pl.pallas_call(
        functools.partial(_s5_prep_kernel, nlev=nlev),
        grid=(S5_GROUPS,),
        in_specs=[vec, vec, vec, mat, mat, mat, mat],
        out_specs=[op, op, op, lev, lev],
        out_shape=[jax.ShapeDtypeStruct((S5_GROUPS, S5_ROW, S5_ROW), BF16)] * 3
        + [jax.ShapeDtypeStruct((S5_GROUPS, nlev, P2), F32)] * 2,
        compiler_params=_cparams("parallel"),
        name="s5_prep",
    )(*args)


def _gelu_tanh(x):
    return 0.5 * x * (1.0 + jnp.tanh(math.sqrt(2.0 / math.pi) * (x + 0.044715 * (x * x * x))))


def _slot_transpose(v, slot):
    v = list(v)
    d = len(v) // 2
    while d >= 1:
        take_own = (slot & d) == 0
        for i in range(len(v)):
            if i & d == 0:
                a, b = v[i], v[i + d]
                v[i] = jnp.where(take_own, a, pltpu.roll(b, d * S5_GROUP, 1))
                v[i + d] = jnp.where(take_own, pltpu.roll(a, 128 - d * S5_GROUP, 1), b)
        d //= 2
    return v


def _s5_kernel(*refs, nseq, nrow, nlev, has_state):
    if has_state:
        (u_ref, m_ref, win_ref, wout_ref, ar_ref, ai_ref, d_ref, h0_ref,
         z_ref, last_ref, t_scr, y_scr, tmp_r, tmp_i) = refs
    else:
        (u_ref, m_ref, win_ref, wout_ref, ar_ref, ai_ref, d_ref,
         z_ref, last_ref, t_scr, y_scr, tmp_r, tmp_i) = refs
    P2 = 2 * S5_STATE
    R = nseq * nrow
    GP = S5_SLAB_GROUPS
    TR = 16
    slot = lax.broadcasted_iota(jnp.int32, (TR, 128), 1) // S5_GROUP

    def token_rows(nt, s):
        return pl.ds(pl.multiple_of(nt * (TR * S5_CHUNK), TR * S5_CHUNK) + s, TR, stride=S5_CHUNK)

    def gather(nt, carry):
        for j in range(2):
            src = [u_ref[0, token_rows(nt, 8 * j + s8), :] for s8 in range(8)]
            by_group = _slot_transpose(src, slot)
            for gl in range(GP):
                t_scr[gl, pl.ds(pl.multiple_of(nt * TR, TR), TR), j * 128:(j + 1) * 128] = by_group[gl].astype(BF16)
        return carry

    lax.fori_loop(0, R // TR, gather, 0, unroll=2)

    n = lax.broadcasted_iota(jnp.int32, (R, P2), 0) % nrow
    row = lax.broadcasted_iota(jnp.int32, (R, P2), 0)
    fwd = lax.broadcasted_iota(jnp.int32, (R, P2), 1) < S5_STATE
    fwd_b = lax.broadcasted_iota(jnp.int32, (nseq, P2), 1) < S5_STATE

    def shift(a, k):
        dn = pltpu.roll(a, k, 0)
        up = pltpu.roll(a, R - k, 0)
        return jnp.where(fwd, jnp.where(n >= k, dn, 0.0), jnp.where(n < nrow - k, up, 0.0))

    def ends(t):
        if nseq == 1:
            return jnp.where(fwd_b, t[nrow - 1:nrow, :], t[0:1, :])
        return jnp.where(fwd_b, t[pl.ds(nrow - 1, nseq, stride=nrow), :], t[pl.ds(0, nseq, stride=nrow), :])

    def per_group(gl, carry):
        ub = t_scr[gl]
        x = _dot(ub, win_ref[gl])
        xr, xi = x[:, :P2], x[:, P2:]
        er, ei = shift(xr, 1), shift(xi, 1)
        if has_state:
            h0 = h0_ref[0, gl]
            for b in range(nseq):
                first = row == jnp.where(fwd, b * nrow, b * nrow + nrow - 1)
                er = jnp.where(first, h0[b:b + 1, :P2], er)
                ei = jnp.where(first, h0[b:b + 1, P2:], ei)
        for j in range(nlev):
            k = 2 ** j
            ar = ar_ref[gl, j:j + 1, :]
            ai = ai_ref[gl, j:j + 1, :]
            sr, si = shift(er, k), shift(ei, k)
            er, ei = er + ar * sr - ai * si, ei + ar * si + ai * sr
        ecat = jnp.concatenate([er, ei], axis=1).astype(BF16)
        y_scr[gl] = _dot(ub, m_ref[gl]) + _dot_nt(ecat, wout_ref[gl])
        a1r = ar_ref[gl, 0:1, :]
        a1i = ai_ref[gl, 0:1, :]
        tmp_r[...] = a1r * er - a1i * ei + xr
        tmp_i[...] = a1r * ei + a1i * er + xi
        last_ref[0, gl, :, :P2] = ends(tmp_r)
        last_ref[0, gl, :, P2:] = ends(tmp_i)
        return carry

    lax.fori_loop(0, GP, per_group, 0, unroll=2)

    def scatter(nt, carry):
        for j in range(2):
            ys = [y_scr[gl, pl.ds(pl.multiple_of(nt * TR, TR), TR), j * 128:(j + 1) * 128] for gl in range(GP)]
            by_token = _slot_transpose(ys, slot)
            for s8 in range(8):
                rows = token_rows(nt, 8 * j + s8)
                z_ref[0, rows, :] = _gelu_tanh(by_token[s8] + d_ref[...] * u_ref[0, rows, :])
        return carry

    lax.fori_loop(0, R // TR, scatter, 0, unroll=2)


def _s5_call(u, ops, d, h0, seq):
    m, win, wout, ar, ai = ops
    bm, lm, _ = u.shape
    nrow = seq // S5_CHUNK
    nlev = int(math.log2(nrow))
    total = bm * lm // seq
    nseq = max(1, min(total, S5_BLOCK_ROWS // nrow))
    nblk = total // nseq
    R = nseq * nrow
    tok = R * S5_CHUNK
    GP = S5_SLAB_GROUPS
    ns = BRANCH // 128
    has_state = h0 is not None
    slab = pl.BlockSpec((1, tok, 128), lambda t, i: (i, 0, t))
    op = pl.BlockSpec((GP, S5_ROW, S5_ROW), lambda t, i: (t, 0, 0))
    lev = pl.BlockSpec((GP, ar.shape[1], 2 * S5_STATE), lambda t, i: (t, 0, 0))
    state = pl.BlockSpec((1, GP, nseq, 4 * S5_STATE), lambda t, i: (i, t, 0, 0))
    in_specs = [slab, op, op, op, lev, lev, pl.BlockSpec((1, 128), lambda t, i: (0, t))]
    args = [u.reshape(nblk, tok, BRANCH), m, win, wout, ar, ai, d]
    if has_state:
        in_specs.append(state)
        args.append(h0.reshape(nblk, nseq, S5_GROUPS, 4 * S5_STATE).transpose(0, 2, 1, 3))
    z, last = pl.pallas_call(
        functools.partial(_s5_kernel, nseq=nseq, nrow=nrow, nlev=nlev, has_state=has_state),
        grid=(ns, nblk),
        in_specs=in_specs,
        out_specs=[slab, state],
        out_shape=[jax.ShapeDtypeStruct((nblk, tok, BRANCH), F32),
                   jax.ShapeDtypeStruct((nblk, S5_GROUPS, nseq, 4 * S5_STATE), F32)],
        scratch_shapes=[pltpu.VMEM((GP, R, S5_ROW), BF16), pltpu.VMEM((GP, R, S5_ROW), F32),
                        pltpu.VMEM((R, 2 * S5_STATE), F32), pltpu.VMEM((R, 2 * S5_STATE), F32)],
        compiler_params=_cparams("parallel", "parallel"),
        name="s5",
    )(*args)
    last = last.transpose(0, 2, 1, 3).reshape(total, S5_GROUPS, 4 * S5_STATE)
    return z.reshape(bm, lm, BRANCH), last


HEAD_ORDER = (0, 2, 1, 3)


def _stack_heads(q, heads):
    head = lax.broadcasted_iota(jnp.int32, q.shape, 1) // ATT_HD
    zero = jnp.zeros((), q.dtype)
    return jnp.concatenate([jnp.where(head == h, q, zero) for h in heads], axis=0)


def _key_value_operands(kv):
    low = lax.broadcasted_iota(jnp.int32, kv.shape, 1) < ATT_HD
    sw = pltpu.roll(kv, ATT_HD, 1)
    kk = jnp.where(low, kv, sw).astype(BF16)
    return (jnp.concatenate([kk, kk], axis=1),
            jnp.where(low, sw, 1.0).astype(BF16), jnp.where(low, 1.0, kv).astype(BF16))


LOG2E = 1.4426950408889634
Q_SCALE = (ATT_HD ** -0.5) * LOG2E


def _attend_pair(q, parity, key_sets, sink_ref):
    rows = q.shape[0]
    heads = (parity, parity + 2)
    qs = _stack_heads(q, heads)
    sink = jnp.concatenate([jnp.broadcast_to(sink_ref[h:h + 1, 0:1], (rows, 1)) for h in heads], axis=0) * LOG2E
    scores = []
    t = None
    for k4, _, _, bias in key_sets:
        s = _dot_nt(qs, k4)
        if bias is not None:
            s = s + jnp.concatenate([bias, bias], axis=0)
        scores.append(s)
        for c in range(0, s.shape[1], 128):
            t = s[:, c:c + 128] if t is None else jnp.maximum(t, s[:, c:c + 128])
    m = jnp.maximum(jnp.max(t, axis=-1, keepdims=True), sink)
    lane = lax.broadcasted_iota(jnp.int32, (2 * rows, 2 * ATT_HD), 1)
    den_lanes = (lane >= ATT_HD) if parity == 0 else (lane < ATT_HD)
    o = jnp.where(den_lanes, jnp.exp2(sink - m), 0.0)
    for s, (_, va, vb, _) in zip(scores, key_sets):
        o = o + _dot(jnp.exp2(s - m).astype(BF16), va if parity == 0 else vb)
    return o / pltpu.roll(o, ATT_HD, 1)


def _attend_stacked(q, key_sets, sink_ref):
    rows = q.shape[0]
    qs = _stack_heads(q, HEAD_ORDER)
    sink = jnp.concatenate([jnp.broadcast_to(sink_ref[h:h + 1, 0:1], (rows, 1)) for h in HEAD_ORDER],
                           axis=0) * LOG2E
    scores = []
    t = None
    for k4, _, _, bias in key_sets:
        s = _dot_nt(qs, k4)
        if bias is not None:
            s = s + jnp.concatenate([bias] * ATT_GROUPS, axis=0)
        scores.append(s)
        for c in range(0, s.shape[1], 128):
            t = s[:, c:c + 128] if t is None else jnp.maximum(t, s[:, c:c + 128])
    m = jnp.maximum(jnp.max(t, axis=-1, keepdims=True), sink)
    e_sink = jnp.exp2(sink - m)
    low = lax.broadcasted_iota(jnp.int32, (2 * rows, 2 * ATT_HD), 1) < ATT_HD
    o_even = jnp.where(low, 0.0, e_sink[:2 * rows])
    o_odd = jnp.where(low, e_sink[2 * rows:], 0.0)
    for s, (_, va, vb, _) in zip(scores, key_sets):
        p = jnp.exp2(s - m).astype(BF16)
        o_even = o_even + _dot(p[:2 * rows], va)
        o_odd = o_odd + _dot(p[2 * rows:], vb)
    return o_even / pltpu.roll(o_even, ATT_HD, 1), o_odd / pltpu.roll(o_odd, ATT_HD, 1)


def _attend(q, key_sets, sink_ref, split):
    rows = q.shape[0]
    if split:
        o_even = _attend_pair(q, 0, key_sets, sink_ref)
        o_odd = _attend_pair(q, 1, key_sets, sink_ref)
    else:
        o_even, o_odd = _attend_stacked(q, key_sets, sink_ref)
    low = lax.broadcasted_iota(jnp.int32, o_even.shape, 1) < ATT_HD
    pair = jnp.where(low, o_even, o_odd)
    return jnp.concatenate([pair[:rows], pair[rows:]], axis=1)


def _att_ctx_kernel(q_ref, kv_ref, sa_ref, sink_ref, kprev_ref, vprev_ref, y_ref, kc_ref, vc_ref):
    del kprev_ref, vprev_ref
    W = ATT_GROUPS * ATT_HD
    results = []
    for g in range(ATT_KV_HEADS):
        kv = kv_ref[0, :, g * 2 * ATT_HD:(g + 1) * 2 * ATT_HD]
        heads = slice(g * W, (g + 1) * W)
        out = _attend(q_ref[0, :, heads], [_key_value_operands(kv) + (None,)], sink_ref.at[g], split=True)
        results.append((kv, (out * sa_ref[0, :, heads].astype(F32)).astype(BF16)))
    for g, (kv, y) in enumerate(results):
        kc_ref[0, 0, g] = kv[:, :ATT_HD]
        vc_ref[0, 0, g] = kv[:, ATT_HD:]
        y_ref[0, :, g * W:(g + 1) * W] = y


def _att_ctx_call(q, kv, sa, sink, layer, caches):
    bsz, seq, _ = q.shape
    row = lambda b: (b, 0, 0)
    cache_spec = pl.BlockSpec((1, 1, ATT_KV_HEADS, seq, ATT_HD), lambda b: (b, layer, 0, 0, 0))
    cache_shape = jax.ShapeDtypeStruct(caches[0].shape, F32)
    return pl.pallas_call(
        _att_ctx_kernel,
        grid=(bsz,),
        in_specs=[pl.BlockSpec((1, seq, BRANCH), row), pl.BlockSpec((1, seq, kv.shape[-1]), row),
                  pl.BlockSpec((1, seq, BRANCH), row), pl.BlockSpec(sink.shape, lambda b: (0, 0, 0)),
                  pl.BlockSpec(memory_space=pl.ANY), pl.BlockSpec(memory_space=pl.ANY)],
        out_specs=[pl.BlockSpec((1, seq, BRANCH), row), cache_spec, cache_spec],
        out_shape=[jax.ShapeDtypeStruct((bsz, seq, BRANCH), BF16), cache_shape, cache_shape],
        input_output_aliases={4: 1, 5: 2},
        compiler_params=_cparams("parallel"),
        name="att_ctx",
    )(q, kv, sa, sink, *caches)


def _att_lat_kernel(q_ref, kv_ref, sa_ref, sink_ref, kc_ref, vc_ref, y_ref, k4_scr, va_scr, vb_scr, *, seq):
    nb = seq // QBLK
    KW = 3 * QBLK

    def k_body(i, carry):
        r = pl.ds(pl.multiple_of(i * QBLK, QBLK), QBLK)
        k4, va, vb = _key_value_operands(kv_ref[0, r, :])
        k4_scr[r, :] = k4
        va_scr[r, :] = va
        vb_scr[r, :] = vb
        return carry

    lax.fori_loop(0, nb, k_body, 0)
    kc = kc_ref[0, 0, 0]
    vc = vc_ref[0, 0, 0]
    ones = jnp.ones_like(vc)
    kc4 = jnp.concatenate([kc] * ATT_GROUPS, axis=1).astype(BF16)
    vca = jnp.concatenate([vc, ones], axis=1).astype(BF16)
    vcb = jnp.concatenate([ones, vc], axis=1).astype(BF16)
    qi = lax.broadcasted_iota(jnp.int32, (QBLK, KW), 0)
    kj = lax.broadcasted_iota(jnp.int32, (QBLK, KW), 1)

    def q_body(i, carry):
        r = pl.ds(pl.multiple_of(i * QBLK, QBLK), QBLK)
        ws = pl.multiple_of(jnp.clip(i * QBLK - QBLK, 0, seq - KW), QBLK)
        win = pl.ds(ws, KW)
        bias = jnp.where(jnp.abs((i * QBLK + qi) - (ws + kj)) <= WINDOW, 0.0, NEG)
        key_sets = [(k4_scr[win, :], va_scr[win, :], vb_scr[win, :], bias), (kc4, vca, vcb, None)]
        out = _attend(q_ref[0, r, :], key_sets, sink_ref.at[0], split=False)
        y_ref[0, r, :] = (out * sa_ref[0, r, :].astype(F32)).astype(BF16)
        return carry

    lax.fori_loop(0, nb, q_body, 0, unroll=2)


def _rope_tables(seq, heads, pad_heads):
    pos = jnp.arange(seq)
    row = (pos // GRID_W).astype(F32)[:, None]
    col = (pos % GRID_W).astype(F32)[:, None]
    nf = ATT_HD // 4
    freq = ROPE_BASE ** (-jnp.arange(nf, dtype=F32) / nf)
    ar, ac = row * freq, col * freq
    cos = jnp.concatenate([jnp.cos(ar), jnp.cos(ar), jnp.cos(ac), jnp.cos(ac)], axis=1)
    sin = jnp.concatenate([-jnp.sin(ar), jnp.sin(ar), -jnp.sin(ac), jnp.sin(ac)], axis=1)
    cos = jnp.concatenate([jnp.tile(cos, (1, heads)), jnp.ones((seq, pad_heads * ATT_HD), F32)], axis=1)
    sin = jnp.concatenate([jnp.tile(sin, (1, heads)), jnp.zeros((seq, pad_heads * ATT_HD), F32)], axis=1)
    return cos, sin


def _att_lat_call(q, kv, sa, sink, cache_k, cache_v, e):
    bsz, seq, _ = q.shape
    past = cache_k.shape[3]
    W = ATT_GROUPS * ATT_HD
    per_kv = lambda b, k: (b, 0, k)
    cache_spec = pl.BlockSpec((1, 1, 1, past, ATT_HD), lambda b, k: (b, e, k, 0, 0))
    return pl.pallas_call(
        functools.partial(_att_lat_kernel, seq=seq),
        grid=(bsz, ATT_KV_HEADS),
        in_specs=[
            pl.BlockSpec((1, seq, W), per_kv),
            pl.BlockSpec((1, seq, 2 * ATT_HD), per_kv),
            pl.BlockSpec((1, seq, W), per_kv),
            pl.BlockSpec((1, 8, 128), lambda b, k: (k, 0, 0)),
            cache_spec,
            cache_spec,
        ],
        out_specs=pl.BlockSpec((1, seq, W), per_kv),
        out_shape=jax.ShapeDtypeStruct((bsz, seq, BRANCH), BF16),
        scratch_shapes=[pltpu.VMEM((seq, W), BF16), pltpu.VMEM((seq, 2 * ATT_HD), BF16),
                        pltpu.VMEM((seq, 2 * ATT_HD), BF16)],
        compiler_params=_cparams("parallel", "parallel"),
        name="att_lat",
    )(q, kv, sa, sink, cache_k, cache_v)


def _even_out_compute(x, refs):
    mod_ref, yg_ref, z_ref, ss_ref, wg_ref, bg_ref, wo_ref = refs
    z = z_ref[0]
    glu = z * jax.nn.sigmoid(_dot(z.astype(BF16), wg_ref[...]) + bg_ref[...])
    ys = (glu * ss_ref[0].astype(F32)).astype(BF16)
    out = _dot(yg_ref[0], wo_ref[:BRANCH, :]) + _dot(ys, wo_ref[BRANCH:, :])
    return x + mod_ref[0, 2:3, :] * out


def _even_out_part(tm, mod, yg, z, ss, wglu, bglu, wout):
    args = [mod, yg, z, ss, wglu, bglu, wout]
    specs = [_mod_spec(), _tile_spec(tm, BRANCH), _tile_spec(tm, BRANCH), _tile_spec(tm, BRANCH),
             _whole_spec(wglu), _whole_spec(bglu), _whole_spec(wout)]
    return _even_out_compute, args, specs


def _odd_out_compute(x, refs, seq, tm):
    mod_ref, ya_ref, p_ref, pprev_ref, pnext_ref, bgz_ref, cw_ref, cb_ref, wo_ref = refs
    p = p_ref[0].astype(F32)
    pos = (pl.program_id(1) * tm + lax.broadcasted_iota(jnp.int32, (tm, BRANCH), 0)) % seq
    rowi = lax.broadcasted_iota(jnp.int32, (tm, BRANCH), 0)
    halo = CONV_HALO_ROWS
    p_before = pprev_ref[0].astype(F32)[halo - 1:halo, :]
    p_after = pnext_ref[0].astype(F32)[0:1, :]
    prev = jnp.where(rowi == 0, p_before, pltpu.roll(p, 1, 0))
    nxt = jnp.where(rowi == tm - 1, p_after, pltpu.roll(p, tm - 1, 0))
    prev = jnp.where(pos == 0, 0.0, prev)
    nxt = jnp.where(pos == seq - 1, 0.0, nxt)
    conv = prev * cw_ref[0:1, :] + p * cw_ref[1:2, :] + nxt * cw_ref[2:3, :] + cb_ref[...]
    yc = (bgz_ref[0].astype(F32) * conv).astype(BF16)
    out = _dot(ya_ref[0], wo_ref[:BRANCH, :]) + _dot(yc, wo_ref[BRANCH:, :])
    return x + mod_ref[0, 2:3, :] * out


def _odd_out_part(tm, lm, seq, mod, ya, p, bgz, convw, convb, wout):
    hr = CONV_HALO_ROWS
    nth = lm // hr
    prev_spec = pl.BlockSpec((1, hr, BRANCH), lambda b, i: (b, jnp.maximum(i * (tm // hr) - 1, 0), 0))
    next_spec = pl.BlockSpec((1, hr, BRANCH), lambda b, i: (b, jnp.minimum((i + 1) * (tm // hr), nth - 1), 0))
    args = [mod, ya, p, p, p, bgz, convw, convb, wout]
    specs = [_mod_spec(), _tile_spec(tm, BRANCH), _tile_spec(tm, BRANCH), prev_spec, next_spec,
             _tile_spec(tm, BRANCH), _whole_spec(convw), _whole_spec(convb), _whole_spec(wout)]
    return functools.partial(_odd_out_compute, seq=seq, tm=tm), args, specs


def _even_in_weight(w):
    dk = GLA_HEADS * GLA_DK
    q, k = w[:, :dk], w[:, dk:2 * dk]
    v = w[:, 2 * dk:2 * dk + BRANCH]
    o = 2 * dk + BRANCH
    lr = w[:, o:o + 2 * GLA_RANK]
    rest = w[:, o + 2 * GLA_RANK:]
    qk = jnp.concatenate([jnp.concatenate([q[:, h * GLA_DK:(h + 1) * GLA_DK], k[:, h * GLA_DK:(h + 1) * GLA_DK]],
                                          axis=1) for h in range(GLA_HEADS)], axis=1)
    lr = jnp.pad(lr, ((0, 0), (0, 128 - 2 * GLA_RANK)))
    return jnp.concatenate([qk, v, lr, rest], axis=1).astype(BF16)


def _odd_in_weight(w):
    q = w[:, :BRANCH]
    k = w[:, BRANCH:BRANCH + ATT_KV_HEADS * ATT_HD]
    v = w[:, BRANCH + ATT_KV_HEADS * ATT_HD:BRANCH + 2 * ATT_KV_HEADS * ATT_HD]
    rest = w[:, BRANCH + 2 * ATT_KV_HEADS * ATT_HD:]
    kv = jnp.concatenate([jnp.concatenate([k[:, h * ATT_HD:(h + 1) * ATT_HD], v[:, h * ATT_HD:(h + 1) * ATT_HD]],
                                          axis=1) for h in range(ATT_KV_HEADS)], axis=1)
    return jnp.concatenate([q, kv, rest], axis=1).astype(BF16)


def _gate_weights(w2, b2):
    zf = jnp.zeros((GLA_RANK, GLA_DK), F32)
    cols, bias = [], []
    for h in range(GLA_HEADS):
        sl = slice(h * GLA_DK, (h + 1) * GLA_DK)
        cols.append(jnp.concatenate([jnp.concatenate([w2[0][:, sl], zf], axis=1),
                                     jnp.concatenate([zf, w2[1][:, sl]], axis=1)], axis=0))
        bias += [b2[0][sl], b2[1][sl]]
    w = jnp.pad(jnp.concatenate(cols, axis=1), ((0, 128 - 2 * GLA_RANK), (0, 0)))
    w_hi = w.astype(BF16)
    w_lo = (w - w_hi.astype(F32)).astype(BF16)
    return w_hi, w_lo, jnp.concatenate(bias)[None, :]


def _run_stream(x, mods, seq, params, gla_s0, s5_h0, cache_k, cache_v):
    depth = len(params)
    bm, lm, _ = x.shape
    tm = min(ROW_TILE, lm)
    nseq = bm * lm // seq
    per_seq = lambda a: a.reshape(nseq, seq, a.shape[-1])
    flat = lambda a: a.reshape(bm, lm, a.shape[-1])
    latent = cache_k is not None
    tables = _rope_tables(seq, 2, 0) + _rope_tables(seq, 1, 1) if latent else ()

    def in_part(l):
        p = params[l]
        if l % 2 == 0:
            return _even_in_part(tm, mods[l], p["nw"], p["w_in"], p["gla_w"])
        return _odd_in_part(tm, mods[l], p["nw"], p["w_in"], p["qnw"], p["knw"], tables)

    n_even, n_odd = (depth + 1) // 2, depth // 2
    gla_states = jnp.zeros((nseq, n_even, 2, GLA_HEADS, GLA_DK, GLA_DV), F32)
    cache_shape = (nseq, n_odd, ATT_KV_HEADS, seq, ATT_HD)
    caches = None if latent else (jnp.zeros(cache_shape, F32), jnp.zeros(cache_shape, F32))
    collected = {"s5": []}
    out_part = None
    for l in range(depth + 1):
        res = _stage_call("stage%d" % l, x, out_part, in_part(l) if l < depth else None)
        if out_part is not None:
            x, res = res[0], res[1:]
        if l == depth:
            break
        p, e = params[l], l // 2
        if l % 2 == 0:
            qk, v, bc, sg, u, ss = res
            yg, gla_states = _gla_call(per_seq(qk), per_seq(v), per_seq(bc), per_seq(sg), p["onorm"],
                                       None if gla_s0 is None else gla_s0[:, e], e, gla_states)
            z, last = _s5_call(u, p["s5_ops"], p["d"], None if s5_h0 is None else s5_h0[e], seq)
            collected["s5"].append(last)
            out_part = _even_out_part(tm, mods[l], flat(yg), z, ss, p["wglu"], p["bglu"], p["wout"])
        else:
            q, kv, sa, pc, bgz = res
            if latent:
                ya = _att_lat_call(per_seq(q), per_seq(kv), per_seq(sa), p["sink"], cache_k, cache_v, e)
            else:
                ya, *caches = _att_ctx_call(per_seq(q), per_seq(kv), per_seq(sa), p["sink"], e, caches)
            out_part = _odd_out_part(tm, lm, seq, mods[l], flat(ya), pc, bgz, p["convw"], p["convb"], p["wout"])
    collected["gla"] = gla_states
    collected["caches"] = caches
    return x, collected


def kernel(x_prompt, x_sample, c, state_gla, state_s5_re, state_s5_im, cache_k, cache_v, c_ctx, norm_w, w_ada, b_ada, w_in_e, w_out_e, gla_w2, gla_b2, gla_onorm, s5_lam_re, s5_lam_im, s5_log_dt, s5_b_re, s5_b_im, s5_c_re, s5_c_im, s5_d, s5_w_glu, s5_b_glu, w_in_o, w_out_o, q_norm_w, k_norm_w, sink, conv_w, conv_b):
    depth = norm_w.shape[0]
    bp, lp, _ = x_prompt.shape
    bs, ls, _ = x_sample.shape

    rows = 8 * ((1 + bs + 7) // 8)
    cs = jnp.zeros((rows, D_MODEL), F32).at[0].set(c_ctx).at[1:1 + bs].set(c)
    mods = _ada_call(cs, w_ada, b_ada)

    params, s5_h0 = [], []
    for l in range(depth):
        e = l // 2
        p = {"nw": norm_w[l][None, :]}
        if l % 2 == 0:
            p.update(
                w_in=_even_in_weight(w_in_e[e]), wout=w_out_e[e].astype(BF16),
                gla_w=_gate_weights(gla_w2[e], gla_b2[e]), onorm=gla_onorm[e][None, :],
                d=s5_d[e][None, :], wglu=s5_w_glu[e].astype(BF16), bglu=s5_b_glu[e][None, :],
                s5_ops=_s5_operators(s5_lam_re[e], s5_lam_im[e], s5_log_dt[e], s5_b_re[e], s5_b_im[e],
                                     s5_c_re[e], s5_c_im[e], nlev=int(math.log2(max(lp, ls) // S5_CHUNK))))
            h0 = jnp.concatenate([state_s5_re[:, e], state_s5_im[:, e]], axis=1)
            s5_h0.append(h0.transpose(0, 2, 1, 3).reshape(bs, S5_GROUPS, 4 * S5_STATE))
        else:
            sk = jnp.broadcast_to(sink[e].reshape(ATT_KV_HEADS, ATT_GROUPS, 1), (ATT_KV_HEADS, ATT_GROUPS, 128))
            p.update(
                w_in=_odd_in_weight(w_in_o[e]), wout=w_out_o[e].astype(BF16),
                qnw=jnp.tile(q_norm_w[e], ATT_HEADS)[None, :],
                knw=jnp.tile(jnp.concatenate([k_norm_w[e], jnp.ones((ATT_HD,), F32)]), ATT_KV_HEADS)[None, :],
                sink=jnp.concatenate([sk, jnp.zeros((ATT_KV_HEADS, 8 - ATT_GROUPS, 128), F32)], axis=1),
                convw=conv_w[e], convb=conv_b[e][None, :])
        params.append(p)

    mods_p = [mods[l, 0:1].reshape(1, 3, D_MODEL) for l in range(depth)]
    mods_s = [mods[l, 1:1 + bs].reshape(bs, 3, D_MODEL) for l in range(depth)]
    xp, got = _run_stream(x_prompt.reshape(1, bp * lp, D_MODEL), mods_p, lp, params, None, None, None, None)
    xs, _ = _run_stream(x_sample, mods_s, ls, params, state_gla, s5_h0, cache_k, cache_v)
    last = [t.reshape(bp, S5_GROUPS, 2, 2, S5_STATE).transpose(2, 0, 3, 1, 4) for t in got["s5"]]
    return (xp.reshape(bp, lp, D_MODEL), xs,
            got["gla"],
            jnp.stack([t[0] for t in last], axis=1), jnp.stack([t[1] for t in last], axis=1),
            got["caches"][0], got["caches"][1])
```

```python
import functools
import math

import jax
import jax.numpy as jnp
from jax import lax
from jax.experimental import pallas as pl
from jax.experimental.pallas import tpu as pltpu

F32 = jnp.float32
BF16 = jnp.bfloat16
HIGHEST = lax.Precision.HIGHEST

D_MODEL = 1024
BRANCH = D_MODEL // 2
GLA_HEADS = 4
GLA_DK = 64
GLA_DV = 128
GLA_RANK = 16
GLA_GATE_NORM = 16.0
GLA_CHUNK = 64
GLA_BLOCK = 256
S5_GROUP = 16
S5_GROUPS = BRANCH // S5_GROUP
S5_STATE = 64
S5_CHUNK = 16
S5_ROW = S5_CHUNK * S5_GROUP
S5_SLAB_GROUPS = 128 // S5_GROUP
S5_BLOCK_ROWS = 256
ATT_HEADS = 8
ATT_KV_HEADS = 2
ATT_GROUPS = ATT_HEADS // ATT_KV_HEADS
ATT_HD = 64
WINDOW = 128
QBLK = 128
GRID_W = 64
ROPE_BASE = 10000.0
CONV_W = 3
CONV_HALO_ROWS = 16
EPS = 1e-6
NEG = -1e30

ROW_TILE = 512
VMEM_LIMIT = 48 * 1024 * 1024


def _cparams(*sem):
    return pltpu.CompilerParams(dimension_semantics=sem, vmem_limit_bytes=VMEM_LIMIT)


def _silu(x):
    return x * jax.nn.sigmoid(x)


def _dot(a, b):
    return jnp.dot(a, b, preferred_element_type=F32)


def _dot_nt(a, b):
    return lax.dot_general(a, b, (((1,), (1,)), ((), ())), preferred_element_type=F32)


def _ada_kernel(c_ref, w_ref, b_ref, o_ref):
    c = c_ref[...]
    o_ref[0] = jnp.dot(_silu(c), w_ref[0], precision=HIGHEST, preferred_element_type=F32) + b_ref[0]


def _ada_call(cs, w_ada, b_ada):
    depth = w_ada.shape[0]
    rows = cs.shape[0]
    nt = 3
    return pl.pallas_call(
        _ada_kernel,
        grid=(depth, nt),
        in_specs=[
            pl.BlockSpec((rows, D_MODEL), lambda l, j: (0, 0)),
            pl.BlockSpec((1, D_MODEL, D_MODEL), lambda l, j: (l, 0, j)),
            pl.BlockSpec((1, 1, D_MODEL), lambda l, j: (l, 0, j)),
        ],
        out_specs=pl.BlockSpec((1, rows, D_MODEL), lambda l, j: (l, 0, j)),
        out_shape=jax.ShapeDtypeStruct((depth, rows, 3 * D_MODEL), F32),
        compiler_params=_cparams("arbitrary", "arbitrary"),
        name="adaln",
    )(cs, w_ada, b_ada.reshape(depth, 1, 3 * D_MODEL))


def _modulated(x, mod_ref, nw_ref):
    ms = jnp.mean(x * x, axis=-1, keepdims=True)
    y = x * lax.rsqrt(ms + EPS) * nw_ref[...]
    shift = mod_ref[0, 0:1, :]
    scale = mod_ref[0, 1:2, :]
    return (y * (1.0 + scale) + shift).astype(BF16)


E_COLS = (0, 512, 1024, 1152, 1664, 2176, 2688)


def _log_sigmoid(x):
    return jnp.minimum(x, 0.0) - jnp.log(1.0 + jnp.exp(-jnp.abs(x)))


def _chunk_cumsum(g):
    rows, width = g.shape
    pos = lax.broadcasted_iota(jnp.int32, g.shape, 0) % GLA_CHUNK
    fwd = (lax.broadcasted_iota(jnp.int32, g.shape, 1) % (2 * GLA_DK)) < GLA_DK
    c = g
    k = 1
    while k < GLA_CHUNK:
        c = c + jnp.where(pos >= k, pltpu.roll(c, k, 0), 0.0)
        k *= 2
    total = jnp.concatenate([jnp.broadcast_to(c[r + GLA_CHUNK - 1:r + GLA_CHUNK, :], (GLA_CHUNK, width))
                             for r in range(0, rows, GLA_CHUNK)], axis=0)
    return jnp.where(fwd, c, total - c + g)


def _even_in_compute(x, refs, outs):
    mod_ref, nw_ref, w_ref, w2h_ref, w2l_ref, b2_ref = refs
    qk_ref, v_ref, bc_ref, sg_ref, u_ref, ss_ref = outs
    h = _modulated(x, mod_ref, nw_ref)
    c = E_COLS
    lr = _dot(h, w_ref[:, c[2]:c[3]])
    lr_hi = lr.astype(BF16)
    lr_lo = (lr - lr_hi.astype(F32)).astype(BF16)
    pre = _dot(lr_hi, w2h_ref[...]) + (_dot(lr_lo, w2h_ref[...]) + _dot(lr_hi, w2l_ref[...])) + b2_ref[...]
    bc_ref[0] = _chunk_cumsum(_log_sigmoid(pre) * (1.0 / GLA_GATE_NORM))
    qk_ref[0] = _dot(h, w_ref[:, c[0]:c[1]]).astype(BF16)
    v_ref[0] = _dot(h, w_ref[:, c[1]:c[2]]).astype(BF16)
    sg_ref[0] = _silu(_dot(h, w_ref[:, c[3]:c[4]])).astype(BF16)
    u_ref[0] = _dot(h, w_ref[:, c[4]:c[5]])
    ss_ref[0] = _silu(_dot(h, w_ref[:, c[5]:c[6]])).astype(BF16)


O_COLS = (0, 512, 768, 1280, 1792, 2304, 2816, 3328)


def _rope(x, cos, sin):
    width = x.shape[1]
    lane = lax.broadcasted_iota(jnp.int32, x.shape, 1)
    partner = jnp.where((lane & 16) == 0, pltpu.roll(x, width - 16, 1), pltpu.roll(x, 16, 1))
    return x * cos + partner * sin


def _head_pair_norm(x, w, cos, sin, both):
    low = lax.broadcasted_iota(jnp.int32, x.shape, 1) < ATT_HD
    x2 = x * x
    ms = jnp.sum(jnp.where(low, x2, 0.0), axis=-1, keepdims=True) * (1.0 / ATT_HD)
    rs = lax.rsqrt(ms + EPS)
    if both:
        ms_hi = jnp.sum(jnp.where(low, 0.0, x2), axis=-1, keepdims=True) * (1.0 / ATT_HD)
        rs = jnp.where(low, rs, lax.rsqrt(ms_hi + EPS))
    else:
        rs = jnp.where(low, rs, 1.0)
    y = x * rs * w
    return y if cos is None else _rope(y, cos, sin)


def _odd_in_compute(x, refs, outs, rope):
    if rope:
        mod_ref, nw_ref, w_ref, qn_ref, kn_ref, cq_ref, sq_ref, ck_ref, sk_ref = refs
        cq, sq, ck, sk = cq_ref[...], sq_ref[...], ck_ref[...], sk_ref[...]
    else:
        mod_ref, nw_ref, w_ref, qn_ref, kn_ref = refs
        cq = sq = ck = sk = None
    q_ref, kv_ref, sa_ref, p_ref, bgz_ref = outs
    h = _modulated(x, mod_ref, nw_ref)
    c = O_COLS
    q = _dot(h, w_ref[:, c[0]:c[1]])
    for t in range(BRANCH // 128):
        lanes = slice(t * 128, (t + 1) * 128)
        qt = _head_pair_norm(q[:, lanes], qn_ref[:, lanes], cq, sq, True)
        q_ref[0, :, lanes] = (qt * Q_SCALE).astype(BF16)
    kv = _dot(h, w_ref[:, c[1]:c[2]])
    for t in range(ATT_KV_HEADS):
        lanes = slice(t * 128, (t + 1) * 128)
        kv_ref[0, :, lanes] = _head_pair_norm(kv[:, lanes], kn_ref[:, lanes], ck, sk, False)
    sa_ref[0] = _silu(_dot(h, w_ref[:, c[2]:c[3]])).astype(BF16)
    xc = _dot(h, w_ref[:, c[3]:c[4]])
    bg = _dot(h, w_ref[:, c[4]:c[5]])
    cg = _dot(h, w_ref[:, c[5]:c[6]])
    zc = _dot(h, w_ref[:, c[6]:c[7]])
    p_ref[0] = (cg * xc).astype(BF16)
    bgz_ref[0] = (bg * _silu(zc)).astype(BF16)


def _tile_spec(tm, n):
    return pl.BlockSpec((1, tm, n), lambda b, i: (b, i, 0))


def _whole_spec(a):
    return pl.BlockSpec(a.shape, lambda b, i: (0,) * a.ndim)


def _mod_spec():
    return pl.BlockSpec((1, 3, D_MODEL), lambda b, i: (b, 0, 0))


def _even_in_part(tm, mod, nw, w_in, gla_w):
    args = [mod, nw, w_in, *gla_w]
    specs = [_mod_spec()] + [_whole_spec(a) for a in args[1:]]
    outs = [(BRANCH, dt) for dt in (BF16, BF16, F32, BF16, F32, BF16)]
    return _even_in_compute, args, specs, outs


def _odd_in_part(tm, mod, nw, w_in, qnw, knw, tables):
    args = [mod, nw, w_in, qnw, knw, *tables]
    specs = ([_mod_spec()] + [_whole_spec(a) for a in args[1:5]]
             + [pl.BlockSpec((tm, a.shape[1]), lambda b, i: (i, 0)) for a in tables])
    outs = [(BRANCH, BF16), (2 * ATT_KV_HEADS * ATT_HD, F32), (BRANCH, BF16), (BRANCH, BF16), (BRANCH, BF16)]
    return functools.partial(_odd_in_compute, rope=bool(tables)), args, specs, outs


def _stage_kernel(*refs, n_out_args, n_in_args, out_fn, in_fn):
    x_ref = refs[0]
    out_args = refs[1:1 + n_out_args]
    in_args = refs[1 + n_out_args:1 + n_out_args + n_in_args]
    outs = refs[1 + n_out_args + n_in_args:]
    x = x_ref[0]
    if out_fn is not None:
        x = out_fn(x, out_args)
        outs[0][0] = x
        outs = outs[1:]
    if in_fn is not None:
        in_fn(x, in_args, outs)


def _stage_call(name, x, out_part, in_part):
    bm, lm, _ = x.shape
    tm = min(ROW_TILE, lm)
    args, specs, out_defs = [x], [_tile_spec(tm, D_MODEL)], []
    out_fn = in_fn = None
    n_out_args = n_in_args = 0
    if out_part is not None:
        out_fn, a, s = out_part
        args, specs, n_out_args = args + a, specs + s, len(a)
        out_defs.append((D_MODEL, F32))
    if in_part is not None:
        in_fn, a, s, o = in_part
        args, specs, n_in_args = args + a, specs + s, len(a)
        out_defs += o
    res = pl.pallas_call(
        functools.partial(_stage_kernel, n_out_args=n_out_args, n_in_args=n_in_args, out_fn=out_fn, in_fn=in_fn),
        grid=(bm, lm // tm),
        in_specs=specs,
        out_specs=[_tile_spec(tm, n) for n, _ in out_defs],
        out_shape=[jax.ShapeDtypeStruct((bm, lm, n), dt) for n, dt in out_defs],
        compiler_params=_cparams("parallel", "parallel"),
        name=name,
    )(*args)
    return res


def _loop(n, body, init, unroll):
    if n <= unroll:
        for i in range(n):
            init = body(i, init)
        return init
    return lax.fori_loop(0, n, body, init, unroll=unroll)


def _gla_kernel(*refs, seq, has_state, heads):
    if has_state:
        (qk_ref, v_ref, bc_ref, sg_ref, on_ref, _, s0_ref, y_ref, sn_ref,
         ut_scr, sf_scr, sb_scr, qd_scr, o_scr) = refs
    else:
        (qk_ref, v_ref, bc_ref, sg_ref, on_ref, _, y_ref, sn_ref,
         ut_scr, sf_scr, sb_scr, qd_scr, o_scr) = refs
        s0_ref = None
    C = GLA_CHUNK
    W = 2 * GLA_DK
    R = GLA_BLOCK
    CB = R // C
    nc = seq // C
    nb = seq // R
    ri = lax.broadcasted_iota(jnp.int32, (R, R), 0)
    ci = lax.broadcasted_iota(jnp.int32, (R, R), 1)
    same = (ri // C) == (ci // C)
    tril = same & (ri >= ci)
    triu = same & (ri <= ci)
    fwd = lax.broadcasted_iota(jnp.int32, (R, W), 1) < GLA_DK
    fwd_row = lax.broadcasted_iota(jnp.int32, (1, W), 1) < GLA_DK
    fwd_st = lax.broadcasted_iota(jnp.int32, (GLA_DV, W), 1) < GLA_DK
    row_chunk = lax.broadcasted_iota(jnp.int32, (R, W), 0) // C
    scale = GLA_DK ** -0.5

    def rows(c):
        return pl.ds(pl.multiple_of(c * C, C), C)

    def block_rows(j):
        return pl.ds(pl.multiple_of(j * R, R), R)

    def state_rows(c):
        return pl.ds(pl.multiple_of(c * GLA_DV, GLA_DV), GLA_DV)

    def intra(h, j, carry):
        r = block_rows(j)
        hl = slice(h * W, (h + 1) * W)
        qk = qk_ref[0, r, hl].astype(F32)
        bc = bc_ref[0, r, hl]
        v = v_ref[0, r, hl].astype(F32)
        sw = pltpu.roll(qk, GLA_DK, 1)
        q2 = jnp.where(fwd, qk, sw) * scale
        k2 = jnp.where(fwd, sw, qk)
        b_mid = bc[C // 2:C // 2 + 1, :]
        b_end = jnp.where(fwd_row, bc[C - 1:C, :], bc[0:1, :])
        for c in range(1, CB):
            o0 = c * C
            b_mid = jnp.where(row_chunk == c, bc[o0 + C // 2:o0 + C // 2 + 1, :], b_mid)
            b_end = jnp.where(row_chunk == c, jnp.where(fwd_row, bc[o0 + C - 1:o0 + C, :], bc[o0:o0 + 1, :]), b_end)
        qd = (q2 * jnp.exp(bc)).astype(BF16)
        qh = q2 * jnp.exp(bc - b_mid)
        kh = (k2 * jnp.exp(b_mid - bc)).astype(BF16)
        att = (jnp.where(tril, _dot_nt(jnp.where(fwd, qh, 0.0).astype(BF16), kh), 0.0)
               + jnp.where(triu, _dot_nt(jnp.where(fwd, 0.0, qh).astype(BF16), kh), 0.0))
        o = _dot(att.astype(BF16), v.astype(BF16))
        kd = k2 * jnp.exp(b_end - bc)
        kd_wide = jnp.concatenate([jnp.where(row_chunk == c, kd, 0.0) for c in range(CB)], axis=1)
        ut = _dot(v.T.astype(BF16), kd_wide.astype(BF16))
        qd_scr[h, r, :] = qd
        o_scr[h, r, :] = o
        ut_scr[h, j] = ut
        return carry

    for h in range(heads):
        _loop(nb, functools.partial(intra, h), 0, 2)

    def scan(h, j, st):
        jb = nb - 1 - j
        hl = slice(h * W, (h + 1) * W)
        bcf = bc_ref[0, block_rows(j), hl]
        bcb = bc_ref[0, block_rows(jb), hl]
        utf = ut_scr[h, j]
        utb = ut_scr[h, jb]
        for c in range(CB):
            cb = CB - 1 - c
            stb = st.astype(BF16)
            sf_scr[h, j, :, c * W:(c + 1) * W] = stb
            sb_scr[h, jb, :, cb * W:(cb + 1) * W] = stb
            b_end = jnp.where(fwd_row, bcf[c * C + C - 1:c * C + C, :], bcb[cb * C:cb * C + 1, :])
            inc = jnp.where(fwd_st, utf[:, c * W:(c + 1) * W], utb[:, cb * W:(cb + 1) * W])
            st = st * jnp.exp(b_end) + inc
        return st

    for h in range(heads):
        if has_state:
            init = jnp.concatenate([s0_ref[0, 0, h].T, s0_ref[0, 1, h].T], axis=1)
        else:
            init = jnp.zeros((GLA_DV, W), F32)
        st = _loop(nb, functools.partial(scan, h), init, 1)
        sn_ref[0, 0, 0, h] = st[:, :GLA_DK].T
        sn_ref[0, 0, 1, h] = st[:, GLA_DK:].T

    fwd_wide = (lax.broadcasted_iota(jnp.int32, (GLA_DV, CB * W), 1) % W) < GLA_DK

    def finish(h, j, carry):
        r = block_rows(j)
        hl = slice(h * W, (h + 1) * W)
        qd = qd_scr[h, r, :]
        zero = jnp.zeros((), BF16)
        qd_wide = jnp.concatenate([jnp.where(row_chunk == c, qd, zero) for c in range(CB)], axis=1)
        s_in = jnp.where(fwd_wide, sf_scr[h, j], sb_scr[h, j])
        o = o_scr[h, r, :] + _dot_nt(qd_wide, s_in)
        ms = jnp.mean(o * o, axis=-1, keepdims=True)
        y_ref[0, r, hl] = (o * lax.rsqrt(ms + EPS) * on_ref[...] * sg_ref[0, r, hl].astype(F32)).astype(BF16)
        return carry

    for h in range(heads):
        _loop(nb, functools.partial(finish, h), 0, 2)


def _gla_call(qk, v, bc, sg, onorm, s0, layer, states):
    bsz, seq, _ = qk.shape
    has_state = s0 is not None
    nb = seq // GLA_BLOCK
    wide = (GLA_BLOCK // GLA_CHUNK) * 2 * GLA_DK
    hps = GLA_HEADS if nb == 1 else 1
    per_head = pl.BlockSpec((1, seq, 128 * hps), lambda b, h: (b, 0, h))
    in_specs = [per_head, per_head, per_head, per_head, pl.BlockSpec((1, GLA_DV), lambda b, h: (0, 0)),
                pl.BlockSpec(memory_space=pl.ANY)]
    args = [qk, v, bc, sg, onorm, states]
    if has_state:
        in_specs.append(pl.BlockSpec((1, 2, hps, GLA_DK, GLA_DV), lambda b, h: (b, 0, h, 0, 0)))
        args.append(s0)
    new_state_spec = pl.BlockSpec((1, 1, 2, hps, GLA_DK, GLA_DV), lambda b, h: (b, layer, 0, h, 0, 0))
    return pl.pallas_call(
        functools.partial(_gla_kernel, seq=seq, has_state=has_state, heads=hps),
        grid=(bsz, GLA_HEADS // hps),
        in_specs=in_specs,
        out_specs=[per_head, new_state_spec],
        out_shape=[jax.ShapeDtypeStruct((bsz, seq, BRANCH), BF16),
                   jax.ShapeDtypeStruct(states.shape, F32)],
        input_output_aliases={5: 1},
        scratch_shapes=[pltpu.VMEM((hps, nb, GLA_DV, wide), F32),
                        pltpu.VMEM((hps, nb, GLA_DV, wide), BF16),
                        pltpu.VMEM((hps, nb, GLA_DV, wide), BF16),
                        pltpu.VMEM((hps, seq, 2 * GLA_DK), BF16),
                        pltpu.VMEM((hps, seq, GLA_DV), F32)],
        compiler_params=_cparams("parallel", "parallel"),
        name="gla",
    )(*args)


def _s5_prep_kernel(lr_ref, li_ref, ld_ref, br_ref, bi_ref, cr_ref, ci_ref,
                    m_ref, win_ref, wout_ref, ar_ref, ai_ref, *, nlev):
    cs = S5_CHUNK
    P2 = 2 * S5_STATE
    lam_re, lam_im = lr_ref[0], li_ref[0]
    dt = jnp.exp(ld_ref[0])
    er, ei = lam_re * dt, lam_im * dt
    mag = jnp.exp(er)
    nr, ni = mag * jnp.cos(ei) - 1.0, mag * jnp.sin(ei)
    den = lam_re * lam_re + lam_im * lam_im
    cr = (nr * lam_re + ni * lam_im) / den
    ci = (ni * lam_re - nr * lam_im) / den
    b_re, b_im = br_ref[0], bi_ref[0]
    bb_re = cr * b_re - ci * b_im
    bb_im = cr * b_im + ci * b_re
    c_re, c_im = cr_ref[0], ci_ref[0]

    def power(k):
        m = jnp.exp(k * er)
        return m * jnp.cos(k * ei), m * jnp.sin(k * ei)

    def outer(a, b):
        return (a[:, None, :] * b[None, :, :]).reshape(S5_ROW, P2)

    def cmul(ar, ai, br, bi):
        return outer(ar, br) - outer(ai, bi), outer(ar, bi) + outer(ai, br)

    s_row = lax.broadcasted_iota(jnp.int32, (cs, P2), 0).astype(F32)
    fwd = lax.broadcasted_iota(jnp.int32, (cs, P2), 1) < S5_STATE
    half = float(cs // 2)
    k_in = jnp.where(fwd, half - s_row, s_row - half)
    l_re, l_im = cmul(*power(k_in), bb_re, bb_im)
    r_re, r_im = cmul(*power(-k_in), c_re, c_im)
    lcat = jnp.concatenate([l_re, l_im], axis=1)
    rcat = jnp.concatenate([r_re, -r_im], axis=1)
    fwd2 = (lax.broadcasted_iota(jnp.int32, (S5_ROW, S5_ROW), 1) % P2) < S5_STATE
    nt = (((1,), (1,)), ((), ()))
    m_f = lax.dot_general(jnp.where(fwd2, lcat, 0.0), rcat, nt, precision=HIGHEST, preferred_element_type=F32)
    m_b = lax.dot_general(jnp.where(fwd2, 0.0, lcat), rcat, nt, precision=HIGHEST, preferred_element_type=F32)
    s_in = lax.broadcasted_iota(jnp.int32, (S5_ROW, S5_ROW), 0) // S5_GROUP
    s_out = lax.broadcasted_iota(jnp.int32, (S5_ROW, S5_ROW), 1) // S5_GROUP
    m_ref[0] = (jnp.where(s_in <= s_out, m_f, 0.0) + jnp.where(s_in >= s_out, m_b, 0.0)).astype(BF16)
    w_re, w_im = cmul(*power(jnp.where(fwd, (cs - 1.0) - s_row, s_row)), bb_re, bb_im)
    win_ref[0] = jnp.concatenate([w_re, w_im], axis=1).astype(BF16)
    o_re, o_im = cmul(*power(jnp.where(fwd, s_row + 1.0, cs - s_row)), c_re, c_im)
    wout_ref[0] = jnp.concatenate([o_re, -o_im], axis=1).astype(BF16)
    lev = lax.shift_left(jnp.int32(cs), lax.broadcasted_iota(jnp.int32, (nlev, P2), 0)).astype(F32)
    ar_ref[0], ai_ref[0] = power(lev)


def _s5_operators(lam_re, lam_im, log_dt, b_re, b_im, c_re, c_im, nlev):
    P2 = 2 * S5_STATE
    lanes = lambda a: a.transpose(1, 0, 2).reshape(S5_GROUPS, 1, P2)
    both = lambda a: jnp.concatenate([a, a], axis=-1)
    args = (lanes(lam_re), lanes(lam_im),
            lanes(jnp.broadcast_to(log_dt[..., None], lam_re.shape)),
            both(b_re.transpose(0, 2, 1)), both(b_im.transpose(0, 2, 1)), both(c_re), both(c_im))
    g3 = lambda g: (g, 0, 0)
    vec = pl.BlockSpec((1, 1, P2), g3)
    mat = pl.BlockSpec((1, S5_GROUP, P2), g3)
    op = pl.BlockSpec((1, S5_ROW, S5_ROW), g3)
    lev = pl.BlockSpec((1, nlev, P2), g3)
    return pl.pallas_call(
        functools.partial(_s5_prep_kernel, nlev=nlev),
        grid=(S5_GROUPS,),
        in_specs=[vec, vec, vec, mat, mat, mat, mat],
        out_specs=[op, op, op, lev, lev],
        out_shape=[jax.ShapeDtypeStruct((S5_GROUPS, S5_ROW, S5_ROW), BF16)] * 3
        + [jax.ShapeDtypeStruct((S5_GROUPS, nlev, P2), F32)] * 2,
        compiler_params=_cparams("parallel"),
        name="s5_prep",
    )(*args)


def _gelu_tanh(x):
    return 0.5 * x * (1.0 + jnp.tanh(math.sqrt(2.0 / math.pi) * (x + 0.044715 * (x * x * x))))


def _slot_transpose(v, slot):
    v = list(v)
    d = len(v) // 2
    while d >= 1:
        take_own = (slot & d) == 0
        for i in range(len(v)):
            if i & d == 0:
                a, b = v[i], v[i + d]
                v[i] = jnp.where(take_own, a, pltpu.roll(b, d * S5_GROUP, 1))
                v[i + d] = jnp.where(take_own, pltpu.roll(a, 128 - d * S5_GROUP, 1), b)
        d //= 2
    return v


def _s5_kernel(*refs, nseq, nrow, nlev, has_state):
    if has_state:
        (u_ref, m_ref, win_ref, wout_ref, ar_ref, ai_ref, d_ref, h0_ref,
         z_ref, last_ref, t_scr, y_scr, tmp_r, tmp_i) = refs
    else:
        (u_ref, m_ref, win_ref, wout_ref, ar_ref, ai_ref, d_ref,
         z_ref, last_ref, t_scr, y_scr, tmp_r, tmp_i) = refs
    P2 = 2 * S5_STATE
    R = nseq * nrow
    GP = S5_SLAB_GROUPS
    TR = 16
    slot = lax.broadcasted_iota(jnp.int32, (TR, 128), 1) // S5_GROUP

    def token_rows(nt, s):
        return pl.ds(pl.multiple_of(nt * (TR * S5_CHUNK), TR * S5_CHUNK) + s, TR, stride=S5_CHUNK)

    def gather(nt, carry):
        for j in range(2):
            src = [u_ref[0, token_rows(nt, 8 * j + s8), :] for s8 in range(8)]
            by_group = _slot_transpose(src, slot)
            for gl in range(GP):
                t_scr[gl, pl.ds(pl.multiple_of(nt * TR, TR), TR), j * 128:(j + 1) * 128] = by_group[gl].astype(BF16)
        return carry

    lax.fori_loop(0, R // TR, gather, 0, unroll=2)

    n = lax.broadcasted_iota(jnp.int32, (R, P2), 0) % nrow
    row = lax.broadcasted_iota(jnp.int32, (R, P2), 0)
    fwd = lax.broadcasted_iota(jnp.int32, (R, P2), 1) < S5_STATE
    fwd_b = lax.broadcasted_iota(jnp.int32, (nseq, P2), 1) < S5_STATE

    def shift(a, k):
        dn = pltpu.roll(a, k, 0)
        up = pltpu.roll(a, R - k, 0)
        return jnp.where(fwd, jnp.where(n >= k, dn, 0.0), jnp.where(n < nrow - k, up, 0.0))

    def ends(t):
        if nseq == 1:
            return jnp.where(fwd_b, t[nrow - 1:nrow, :], t[0:1, :])
        return jnp.where(fwd_b, t[pl.ds(nrow - 1, nseq, stride=nrow), :], t[pl.ds(0, nseq, stride=nrow), :])

    def per_group(gl, carry):
        ub = t_scr[gl]
        x = _dot(ub, win_ref[gl])
        xr, xi = x[:, :P2], x[:, P2:]
        er, ei = shift(xr, 1), shift(xi, 1)
        if has_state:
            h0 = h0_ref[0, gl]
            for b in range(nseq):
                first = row == jnp.where(fwd, b * nrow, b * nrow + nrow - 1)
                er = jnp.where(first, h0[b:b + 1, :P2], er)
                ei = jnp.where(first, h0[b:b + 1, P2:], ei)
        for j in range(nlev):
            k = 2 ** j
            ar = ar_ref[gl, j:j + 1, :]
            ai = ai_ref[gl, j:j + 1, :]
            sr, si = shift(er, k), shift(ei, k)
            er, ei = er + ar * sr - ai * si, ei + ar * si + ai * sr
        ecat = jnp.concatenate([er, ei], axis=1).astype(BF16)
        y_scr[gl] = _dot(ub, m_ref[gl]) + _dot_nt(ecat, wout_ref[gl])
        a1r = ar_ref[gl, 0:1, :]
        a1i = ai_ref[gl, 0:1, :]
        tmp_r[...] = a1r * er - a1i * ei + xr
        tmp_i[...] = a1r * ei + a1i * er + xi
        last_ref[0, gl, :, :P2] = ends(tmp_r)
        last_ref[0, gl, :, P2:] = ends(tmp_i)
        return carry

    lax.fori_loop(0, GP, per_group, 0, unroll=2)

    def scatter(nt, carry):
        for j in range(2):
            ys = [y_scr[gl, pl.ds(pl.multiple_of(nt * TR, TR), TR), j * 128:(j + 1) * 128] for gl in range(GP)]
            by_token = _slot_transpose(ys, slot)
            for s8 in range(8):
                rows = token_rows(nt, 8 * j + s8)
                z_ref[0, rows, :] = _gelu_tanh(by_token[s8] + d_ref[...] * u_ref[0, rows, :])
        return carry

    lax.fori_loop(0, R // TR, scatter, 0, unroll=2)


def _s5_call(u, ops, d, h0, seq):
    m, win, wout, ar, ai = ops
    bm, lm, _ = u.shape
    nrow = seq // S5_CHUNK
    nlev = int(math.log2(nrow))
    total = bm * lm // seq
    nseq = max(1, min(total, S5_BLOCK_ROWS // nrow))
    nblk = total // nseq
    R = nseq * nrow
    tok = R * S5_CHUNK
    GP = S5_SLAB_GROUPS
    ns = BRANCH // 128
    has_state = h0 is not None
    slab = pl.BlockSpec((1, tok, 128), lambda t, i: (i, 0, t))
    op = pl.BlockSpec((GP, S5_ROW, S5_ROW), lambda t, i: (t, 0, 0))
    lev = pl.BlockSpec((GP, ar.shape[1], 2 * S5_STATE), lambda t, i: (t, 0, 0))
    state = pl.BlockSpec((1, GP, nseq, 4 * S5_STATE), lambda t, i: (i, t, 0, 0))
    in_specs = [slab, op, op, op, lev, lev, pl.BlockSpec((1, 128), lambda t, i: (0, t))]
    args = [u.reshape(nblk, tok, BRANCH), m, win, wout, ar, ai, d]
    if has_state:
        in_specs.append(state)
        args.append(h0.reshape(nblk, nseq, S5_GROUPS, 4 * S5_STATE).transpose(0, 2, 1, 3))
    z, last = pl.pallas_call(
        functools.partial(_s5_kernel, nseq=nseq, nrow=nrow, nlev=nlev, has_state=has_state),
        grid=(ns, nblk),
        in_specs=in_specs,
        out_specs=[slab, state],
        out_shape=[jax.ShapeDtypeStruct((nblk, tok, BRANCH), F32),
                   jax.ShapeDtypeStruct((nblk, S5_GROUPS, nseq, 4 * S5_STATE), F32)],
        scratch_shapes=[pltpu.VMEM((GP, R, S5_ROW), BF16), pltpu.VMEM((GP, R, S5_ROW), F32),
                        pltpu.VMEM((R, 2 * S5_STATE), F32), pltpu.VMEM((R, 2 * S5_STATE), F32)],
        compiler_params=_cparams("parallel", "parallel"),
        name="s5",
    )(*args)
    last = last.transpose(0, 2, 1, 3).reshape(total, S5_GROUPS, 4 * S5_STATE)
    return z.reshape(bm, lm, BRANCH), last


HEAD_ORDER = (0, 2, 1, 3)


def _stack_heads(q, heads):
    head = lax.broadcasted_iota(jnp.int32, q.shape, 1) // ATT_HD
    zero = jnp.zeros((), q.dtype)
    return jnp.concatenate([jnp.where(head == h, q, zero) for h in heads], axis=0)


def _key_value_operands(kv):
    low = lax.broadcasted_iota(jnp.int32, kv.shape, 1) < ATT_HD
    sw = pltpu.roll(kv, ATT_HD, 1)
    kk = jnp.where(low, kv, sw).astype(BF16)
    return (jnp.concatenate([kk, kk], axis=1),
            jnp.where(low, sw, 1.0).astype(BF16), jnp.where(low, 1.0, kv).astype(BF16))


LOG2E = 1.4426950408889634
Q_SCALE = (ATT_HD ** -0.5) * LOG2E


def _attend_pair(q, parity, key_sets, sink_ref):
    rows = q.shape[0]
    heads = (parity, parity + 2)
    qs = _stack_heads(q, heads)
    sink = jnp.concatenate([jnp.broadcast_to(sink_ref[h:h + 1, 0:1], (rows, 1)) for h in heads], axis=0) * LOG2E
    scores = []
    t = None
    for k4, _, _, bias in key_sets:
        s = _dot_nt(qs, k4)
        if bias is not None:
            s = s + jnp.concatenate([bias, bias], axis=0)
        scores.append(s)
        for c in range(0, s.shape[1], 128):
            t = s[:, c:c + 128] if t is None else jnp.maximum(t, s[:, c:c + 128])
    m = jnp.maximum(jnp.max(t, axis=-1, keepdims=True), sink)
    lane = lax.broadcasted_iota(jnp.int32, (2 * rows, 2 * ATT_HD), 1)
    den_lanes = (lane >= ATT_HD) if parity == 0 else (lane < ATT_HD)
    o = jnp.where(den_lanes, jnp.exp2(sink - m), 0.0)
    for s, (_, va, vb, _) in zip(scores, key_sets):
        o = o + _dot(jnp.exp2(s - m).astype(BF16), va if parity == 0 else vb)
    return o / pltpu.roll(o, ATT_HD, 1)


def _attend_stacked(q, key_sets, sink_ref):
    rows = q.shape[0]
    qs = _stack_heads(q, HEAD_ORDER)
    sink = jnp.concatenate([jnp.broadcast_to(sink_ref[h:h + 1, 0:1], (rows, 1)) for h in HEAD_ORDER],
                           axis=0) * LOG2E
    scores = []
    t = None
    for k4, _, _, bias in key_sets:
        s = _dot_nt(qs, k4)
        if bias is not None:
            s = s + jnp.concatenate([bias] * ATT_GROUPS, axis=0)
        scores.append(s)
        for c in range(0, s.shape[1], 128):
            t = s[:, c:c + 128] if t is None else jnp.maximum(t, s[:, c:c + 128])
    m = jnp.maximum(jnp.max(t, axis=-1, keepdims=True), sink)
    e_sink = jnp.exp2(sink - m)
    low = lax.broadcasted_iota(jnp.int32, (2 * rows, 2 * ATT_HD), 1) < ATT_HD
    o_even = jnp.where(low, 0.0, e_sink[:2 * rows])
    o_odd = jnp.where(low, e_sink[2 * rows:], 0.0)
    for s, (_, va, vb, _) in zip(scores, key_sets):
        p = jnp.exp2(s - m).astype(BF16)
        o_even = o_even + _dot(p[:2 * rows], va)
        o_odd = o_odd + _dot(p[2 * rows:], vb)
    return o_even / pltpu.roll(o_even, ATT_HD, 1), o_odd / pltpu.roll(o_odd, ATT_HD, 1)


def _attend(q, key_sets, sink_ref, split):
    rows = q.shape[0]
    if split:
        o_even = _attend_pair(q, 0, key_sets, sink_ref)
        o_odd = _attend_pair(q, 1, key_sets, sink_ref)
    else:
        o_even, o_odd = _attend_stacked(q, key_sets, sink_ref)
    low = lax.broadcasted_iota(jnp.int32, o_even.shape, 1) < ATT_HD
    pair = jnp.where(low, o_even, o_odd)
    return jnp.concatenate([pair[:rows], pair[rows:]], axis=1)


def _att_ctx_kernel(q_ref, kv_ref, sa_ref, sink_ref, kprev_ref, vprev_ref, y_ref, kc_ref, vc_ref):
    del kprev_ref, vprev_ref
    W = ATT_GROUPS * ATT_HD
    results = []
    for g in range(ATT_KV_HEADS):
        kv = kv_ref[0, :, g * 2 * ATT_HD:(g + 1) * 2 * ATT_HD]
        heads = slice(g * W, (g + 1) * W)
        k4, v_ones, _ = _key_value_operands(kv)
        key_sets = [(k4, _value_rows(v_ones.astype(F32)), None)]
        out = _attend_keys_on_rows(q_ref[0, :, heads], key_sets, sink_ref.at[g])
        results.append((kv, (out * sa_ref[0, :, heads].astype(F32)).astype(BF16)))
    for g, (kv, y) in enumerate(results):
        kc_ref[0, 0, g] = kv[:, :ATT_HD]
        vc_ref[0, 0, g] = kv[:, ATT_HD:]
        y_ref[0, :, g * W:(g + 1) * W] = y


def _att_ctx_call(q, kv, sa, sink, layer, caches):
    bsz, seq, _ = q.shape
    row = lambda b: (b, 0, 0)
    cache_spec = pl.BlockSpec((1, 1, ATT_KV_HEADS, seq, ATT_HD), lambda b: (b, layer, 0, 0, 0))
    cache_shape = jax.ShapeDtypeStruct(caches[0].shape, F32)
    return pl.pallas_call(
        _att_ctx_kernel,
        grid=(bsz,),
        in_specs=[pl.BlockSpec((1, seq, BRANCH), row), pl.BlockSpec((1, seq, kv.shape[-1]), row),
                  pl.BlockSpec((1, seq, BRANCH), row), pl.BlockSpec(sink.shape, lambda b: (0, 0, 0)),
                  pl.BlockSpec(memory_space=pl.ANY), pl.BlockSpec(memory_space=pl.ANY)],
        out_specs=[pl.BlockSpec((1, seq, BRANCH), row), cache_spec, cache_spec],
        out_shape=[jax.ShapeDtypeStruct((bsz, seq, BRANCH), BF16), cache_shape, cache_shape],
        input_output_aliases={4: 1, 5: 2},
        compiler_params=_cparams("parallel"),
        name="att_ctx",
    )(q, kv, sa, sink, *caches)


def _attend_keys_on_rows(q, key_sets, sink_ref):
    rows = q.shape[0]
    qs = _stack_heads(q, range(ATT_GROUPS))
    scores, top = [], None
    for k4, _, bias_t in key_sets:
        s = _dot_nt(k4, qs)
        if bias_t is not None:
            s = s + jnp.concatenate([bias_t] * ATT_GROUPS, axis=1)
        scores.append(s)
        part = jnp.max(s, axis=0, keepdims=True)
        top = part if top is None else jnp.maximum(top, part)
    sink = jnp.concatenate([jnp.broadcast_to(sink_ref[h:h + 1, 0:1], (1, rows)) for h in range(ATT_GROUPS)],
                           axis=1) * LOG2E
    m = jnp.maximum(top, sink)
    o = None
    for s, (_, vat, _) in zip(scores, key_sets):
        p = jnp.exp2(s - m).astype(BF16)
        for c, vat_c in enumerate(vat):
            pv = _dot(vat_c, p[c * 128:(c + 1) * 128, :])
            o = pv if o is None else o + pv
    on = o[:ATT_HD, :] / (o[ATT_HD:, :] + jnp.exp2(sink - m))
    pairs = [jnp.concatenate([on[:, (2 * t) * rows:(2 * t + 1) * rows],
                              on[:, (2 * t + 1) * rows:(2 * t + 2) * rows]], axis=0).T
             for t in range(ATT_GROUPS // 2)]
    return jnp.concatenate(pairs, axis=1)


def _value_rows(v_ones):
    return [v_ones[c:c + 128, :].T.astype(BF16) for c in range(0, v_ones.shape[0], 128)]


def _att_lat_kernel(q_ref, kv_ref, sa_ref, sink_ref, kc_ref, vc_ref, y_ref, k4_scr, vat_scr, *, seq):
    nb = seq // QBLK
    KW = 3 * QBLK
    low = lax.broadcasted_iota(jnp.int32, (QBLK, 2 * ATT_HD), 1) < ATT_HD

    def k_body(i, carry):
        r = pl.ds(pl.multiple_of(i * QBLK, QBLK), QBLK)
        kv = kv_ref[0, r, :]
        sw = pltpu.roll(kv, ATT_HD, 1)
        kk = jnp.where(low, kv, sw).astype(BF16)
        k4_scr[r, :] = jnp.concatenate([kk, kk], axis=1)
        vat_scr[i] = jnp.where(low, sw, 1.0).T.astype(BF16)
        return carry

    lax.fori_loop(0, nb, k_body, 0)
    kc = kc_ref[0, 0, 0]
    vc = vc_ref[0, 0, 0]
    kc4 = jnp.concatenate([kc] * ATT_GROUPS, axis=1).astype(BF16)
    vat_ctx = _value_rows(jnp.concatenate([vc, jnp.ones_like(vc)], axis=1))
    kj = lax.broadcasted_iota(jnp.int32, (KW, QBLK), 0)
    qi = lax.broadcasted_iota(jnp.int32, (KW, QBLK), 1)

    def q_body(i, carry):
        r = pl.ds(pl.multiple_of(i * QBLK, QBLK), QBLK)
        ws = pl.multiple_of(jnp.clip(i * QBLK - QBLK, 0, seq - KW), QBLK)
        jb = ws // QBLK
        bias_t = jnp.where(jnp.abs((i * QBLK + qi) - (ws + kj)) <= WINDOW, 0.0, NEG)
        key_sets = [(k4_scr[pl.ds(ws, KW), :], [vat_scr[jb + c] for c in range(KW // QBLK)], bias_t),
                    (kc4, vat_ctx, None)]
        out = _attend_keys_on_rows(q_ref[0, r, :], key_sets, sink_ref.at[0])
        y_ref[0, r, :] = (out * sa_ref[0, r, :].astype(F32)).astype(BF16)
        return carry

    lax.fori_loop(0, nb, q_body, 0, unroll=2)


def _rope_tables(seq, heads, pad_heads):
    pos = jnp.arange(seq)
    row = (pos // GRID_W).astype(F32)[:, None]
    col = (pos % GRID_W).astype(F32)[:, None]
    nf = ATT_HD // 4
    freq = ROPE_BASE ** (-jnp.arange(nf, dtype=F32) / nf)
    ar, ac = row * freq, col * freq
    cos = jnp.concatenate([jnp.cos(ar), jnp.cos(ar), jnp.cos(ac), jnp.cos(ac)], axis=1)
    sin = jnp.concatenate([-jnp.sin(ar), jnp.sin(ar), -jnp.sin(ac), jnp.sin(ac)], axis=1)
    cos = jnp.concatenate([jnp.tile(cos, (1, heads)), jnp.ones((seq, pad_heads * ATT_HD), F32)], axis=1)
    sin = jnp.concatenate([jnp.tile(sin, (1, heads)), jnp.zeros((seq, pad_heads * ATT_HD), F32)], axis=1)
    return cos, sin


def _att_lat_call(q, kv, sa, sink, cache_k, cache_v, e):
    bsz, seq, _ = q.shape
    past = cache_k.shape[3]
    W = ATT_GROUPS * ATT_HD
    per_kv = lambda b, k: (b, 0, k)
    cache_spec = pl.BlockSpec((1, 1, 1, past, ATT_HD), lambda b, k: (b, e, k, 0, 0))
    return pl.pallas_call(
        functools.partial(_att_lat_kernel, seq=seq),
        grid=(bsz, ATT_KV_HEADS),
        in_specs=[
            pl.BlockSpec((1, seq, W), per_kv),
            pl.BlockSpec((1, seq, 2 * ATT_HD), per_kv),
            pl.BlockSpec((1, seq, W), per_kv),
            pl.BlockSpec((1, 8, 128), lambda b, k: (k, 0, 0)),
            cache_spec,
            cache_spec,
        ],
        out_specs=pl.BlockSpec((1, seq, W), per_kv),
        out_shape=jax.ShapeDtypeStruct((bsz, seq, BRANCH), BF16),
        scratch_shapes=[pltpu.VMEM((seq, W), BF16), pltpu.VMEM((seq // QBLK, 2 * ATT_HD, QBLK), BF16)],
        compiler_params=_cparams("parallel", "parallel"),
        name="att_lat",
    )(q, kv, sa, sink, cache_k, cache_v)


def _even_out_compute(x, refs):
    mod_ref, yg_ref, z_ref, ss_ref, wg_ref, bg_ref, wo_ref = refs
    z = z_ref[0]
    glu = z * jax.nn.sigmoid(_dot(z.astype(BF16), wg_ref[...]) + bg_ref[...])
    ys = (glu * ss_ref[0].astype(F32)).astype(BF16)
    out = _dot(yg_ref[0], wo_ref[:BRANCH, :]) + _dot(ys, wo_ref[BRANCH:, :])
    return x + mod_ref[0, 2:3, :] * out


def _even_out_part(tm, mod, yg, z, ss, wglu, bglu, wout):
    args = [mod, yg, z, ss, wglu, bglu, wout]
    specs = [_mod_spec(), _tile_spec(tm, BRANCH), _tile_spec(tm, BRANCH), _tile_spec(tm, BRANCH),
             _whole_spec(wglu), _whole_spec(bglu), _whole_spec(wout)]
    return _even_out_compute, args, specs


def _odd_out_compute(x, refs, seq, tm):
    mod_ref, ya_ref, p_ref, pprev_ref, pnext_ref, bgz_ref, cw_ref, cb_ref, wo_ref = refs
    p = p_ref[0].astype(F32)
    pos = (pl.program_id(1) * tm + lax.broadcasted_iota(jnp.int32, (tm, BRANCH), 0)) % seq
    rowi = lax.broadcasted_iota(jnp.int32, (tm, BRANCH), 0)
    halo = CONV_HALO_ROWS
    p_before = pprev_ref[0].astype(F32)[halo - 1:halo, :]
    p_after = pnext_ref[0].astype(F32)[0:1, :]
    prev = jnp.where(rowi == 0, p_before, pltpu.roll(p, 1, 0))
    nxt = jnp.where(rowi == tm - 1, p_after, pltpu.roll(p, tm - 1, 0))
    prev = jnp.where(pos == 0, 0.0, prev)
    nxt = jnp.where(pos == seq - 1, 0.0, nxt)
    conv = prev * cw_ref[0:1, :] + p * cw_ref[1:2, :] + nxt * cw_ref[2:3, :] + cb_ref[...]
    yc = (bgz_ref[0].astype(F32) * conv).astype(BF16)
    out = _dot(ya_ref[0], wo_ref[:BRANCH, :]) + _dot(yc, wo_ref[BRANCH:, :])
    return x + mod_ref[0, 2:3, :] * out


def _odd_out_part(tm, lm, seq, mod, ya, p, bgz, convw, convb, wout):
    hr = CONV_HALO_ROWS
    nth = lm // hr
    prev_spec = pl.BlockSpec((1, hr, BRANCH), lambda b, i: (b, jnp.maximum(i * (tm // hr) - 1, 0), 0))
    next_spec = pl.BlockSpec((1, hr, BRANCH), lambda b, i: (b, jnp.minimum((i + 1) * (tm // hr), nth - 1), 0))
    args = [mod, ya, p, p, p, bgz, convw, convb, wout]
    specs = [_mod_spec(), _tile_spec(tm, BRANCH), _tile_spec(tm, BRANCH), prev_spec, next_spec,
             _tile_spec(tm, BRANCH), _whole_spec(convw), _whole_spec(convb), _whole_spec(wout)]
    return functools.partial(_odd_out_compute, seq=seq, tm=tm), args, specs


def _even_in_weight(w):
    dk = GLA_HEADS * GLA_DK
    q, k = w[:, :dk], w[:, dk:2 * dk]
    v = w[:, 2 * dk:2 * dk + BRANCH]
    o = 2 * dk + BRANCH
    lr = w[:, o:o + 2 * GLA_RANK]
    rest = w[:, o + 2 * GLA_RANK:]
    qk = jnp.concatenate([jnp.concatenate([q[:, h * GLA_DK:(h + 1) * GLA_DK], k[:, h * GLA_DK:(h + 1) * GLA_DK]],
                                          axis=1) for h in range(GLA_HEADS)], axis=1)
    lr = jnp.pad(lr, ((0, 0), (0, 128 - 2 * GLA_RANK)))
    return jnp.concatenate([qk, v, lr, rest], axis=1).astype(BF16)


def _odd_in_weight(w):
    q = w[:, :BRANCH]
    k = w[:, BRANCH:BRANCH + ATT_KV_HEADS * ATT_HD]
    v = w[:, BRANCH + ATT_KV_HEADS * ATT_HD:BRANCH + 2 * ATT_KV_HEADS * ATT_HD]
    rest = w[:, BRANCH + 2 * ATT_KV_HEADS * ATT_HD:]
    kv = jnp.concatenate([jnp.concatenate([k[:, h * ATT_HD:(h + 1) * ATT_HD], v[:, h * ATT_HD:(h + 1) * ATT_HD]],
                                          axis=1) for h in range(ATT_KV_HEADS)], axis=1)
    return jnp.concatenate([q, kv, rest], axis=1).astype(BF16)


def _gate_weights(w2, b2):
    zf = jnp.zeros((GLA_RANK, GLA_DK), F32)
    cols, bias = [], []
    for h in range(GLA_HEADS):
        sl = slice(h * GLA_DK, (h + 1) * GLA_DK)
        cols.append(jnp.concatenate([jnp.concatenate([w2[0][:, sl], zf], axis=1),
                                     jnp.concatenate([zf, w2[1][:, sl]], axis=1)], axis=0))
        bias += [b2[0][sl], b2[1][sl]]
    w = jnp.pad(jnp.concatenate(cols, axis=1), ((0, 128 - 2 * GLA_RANK), (0, 0)))
    w_hi = w.astype(BF16)
    w_lo = (w - w_hi.astype(F32)).astype(BF16)
    return w_hi, w_lo, jnp.concatenate(bias)[None, :]


def _run_stream(x, mods, seq, params, gla_s0, s5_h0, cache_k, cache_v):
    depth = len(params)
    bm, lm, _ = x.shape
    tm = min(ROW_TILE, lm)
    nseq = bm * lm // seq
    per_seq = lambda a: a.reshape(nseq, seq, a.shape[-1])
    flat = lambda a: a.reshape(bm, lm, a.shape[-1])
    latent = cache_k is not None
    tables = _rope_tables(seq, 2, 0) + _rope_tables(seq, 1, 1) if latent else ()

    def in_part(l):
        p = params[l]
        if l % 2 == 0:
            return _even_in_part(tm, mods[l], p["nw"], p["w_in"], p["gla_w"])
        return _odd_in_part(tm, mods[l], p["nw"], p["w_in"], p["qnw"], p["knw"], tables)

    n_even, n_odd = (depth + 1) // 2, depth // 2
    gla_states = jnp.zeros((nseq, n_even, 2, GLA_HEADS, GLA_DK, GLA_DV), F32)
    cache_shape = (nseq, n_odd, ATT_KV_HEADS, seq, ATT_HD)
    caches = None if latent else (jnp.zeros(cache_shape, F32), jnp.zeros(cache_shape, F32))
    collected = {"s5": []}
    out_part = None
    for l in range(depth + 1):
        res = _stage_call("stage%d" % l, x, out_part, in_part(l) if l < depth else None)
        if out_part is not None:
            x, res = res[0], res[1:]
        if l == depth:
            break
        p, e = params[l], l // 2
        if l % 2 == 0:
            qk, v, bc, sg, u, ss = res
            yg, gla_states = _gla_call(per_seq(qk), per_seq(v), per_seq(bc), per_seq(sg), p["onorm"],
                                       None if gla_s0 is None else gla_s0[:, e], e, gla_states)
            z, last = _s5_call(u, p["s5_ops"], p["d"], None if s5_h0 is None else s5_h0[e], seq)
            collected["s5"].append(last)
            out_part = _even_out_part(tm, mods[l], flat(yg), z, ss, p["wglu"], p["bglu"], p["wout"])
        else:
            q, kv, sa, pc, bgz = res
            if latent:
                ya = _att_lat_call(per_seq(q), per_seq(kv), per_seq(sa), p["sink"], cache_k, cache_v, e)
            else:
                ya, *caches = _att_ctx_call(per_seq(q), per_seq(kv), per_seq(sa), p["sink"], e, caches)
            out_part = _odd_out_part(tm, lm, seq, mods[l], flat(ya), pc, bgz, p["convw"], p["convb"], p["wout"])
    collected["gla"] = gla_states
    collected["caches"] = caches
    return x, collected


def kernel(x_prompt, x_sample, c, state_gla, state_s5_re, state_s5_im, cache_k, cache_v, c_ctx, norm_w, w_ada, b_ada, w_in_e, w_out_e, gla_w2, gla_b2, gla_onorm, s5_lam_re, s5_lam_im, s5_log_dt, s5_b_re, s5_b_im, s5_c_re, s5_c_im, s5_d, s5_w_glu, s5_b_glu, w_in_o, w_out_o, q_norm_w, k_norm_w, sink, conv_w, conv_b):
    depth = norm_w.shape[0]
    bp, lp, _ = x_prompt.shape
    bs, ls, _ = x_sample.shape

    rows = 8 * ((1 + bs + 7) // 8)
    cs = jnp.zeros((rows, D_MODEL), F32).at[0].set(c_ctx).at[1:1 + bs].set(c)
    mods = _ada_call(cs, w_ada, b_ada)

    params, s5_h0 = [], []
    for l in range(depth):
        e = l // 2
        p = {"nw": norm_w[l][None, :]}
        if l % 2 == 0:
            p.update(
                w_in=_even_in_weight(w_in_e[e]), wout=w_out_e[e].astype(BF16),
                gla_w=_gate_weights(gla_w2[e], gla_b2[e]), onorm=gla_onorm[e][None, :],
                d=s5_d[e][None, :], wglu=s5_w_glu[e].astype(BF16), bglu=s5_b_glu[e][None, :],
                s5_ops=_s5_operators(s5_lam_re[e], s5_lam_im[e], s5_log_dt[e], s5_b_re[e], s5_b_im[e],
                                     s5_c_re[e], s5_c_im[e], nlev=int(math.log2(max(lp, ls) // S5_CHUNK))))
            h0 = jnp.concatenate([state_s5_re[:, e], state_s5_im[:, e]], axis=1)
            s5_h0.append(h0.transpose(0, 2, 1, 3).reshape(bs, S5_GROUPS, 4 * S5_STATE))
        else:
            sk = jnp.broadcast_to(sink[e].reshape(ATT_KV_HEADS, ATT_GROUPS, 1), (ATT_KV_HEADS, ATT_GROUPS, 128))
            p.update(
                w_in=_odd_in_weight(w_in_o[e]), wout=w_out_o[e].astype(BF16),
                qnw=jnp.tile(q_norm_w[e], ATT_HEADS)[None, :],
                knw=jnp.tile(jnp.concatenate([k_norm_w[e], jnp.ones((ATT_HD,), F32)]), ATT_KV_HEADS)[None, :],
                sink=jnp.concatenate([sk, jnp.zeros((ATT_KV_HEADS, 8 - ATT_GROUPS, 128), F32)], axis=1),
                convw=conv_w[e], convb=conv_b[e][None, :])
        params.append(p)

    mods_p = [mods[l, 0:1].reshape(1, 3, D_MODEL) for l in range(depth)]
    mods_s = [mods[l, 1:1 + bs].reshape(bs, 3, D_MODEL) for l in range(depth)]
    xp, got = _run_stream(x_prompt.reshape(1, bp * lp, D_MODEL), mods_p, lp, params, None, None, None, None)
    xs, _ = _run_stream(x_sample, mods_s, ls, params, state_gla, s5_h0, cache_k, cache_v)
    last = [t.reshape(bp, S5_GROUPS, 2, 2, S5_STATE).transpose(2, 0, 3, 1, 4) for t in got["s5"]]
    return (xp.reshape(bp, lp, D_MODEL), xs,
            got["gla"],
            jnp.stack([t[0] for t in last], axis=1), jnp.stack([t[1] for t in last], axis=1),
            got["caches"][0], got["caches"][1])
```

```python
import functools
import math

import jax
import jax.numpy as jnp
from jax import lax
from jax.experimental import pallas as pl
from jax.experimental.pallas import tpu as pltpu

F32 = jnp.float32
BF16 = jnp.bfloat16
HIGHEST = lax.Precision.HIGHEST

D_MODEL = 1024
BRANCH = D_MODEL // 2
GLA_HEADS = 4
GLA_DK = 64
GLA_DV = 128
GLA_RANK = 16
GLA_GATE_NORM = 16.0
GLA_CHUNK = 64
GLA_BLOCK = 256
S5_GROUP = 16
S5_GROUPS = BRANCH // S5_GROUP
S5_STATE = 64
S5_CHUNK = 16
S5_ROW = S5_CHUNK * S5_GROUP
S5_SLAB_GROUPS = 128 // S5_GROUP
S5_BLOCK_ROWS = 256
ATT_HEADS = 8
ATT_KV_HEADS = 2
ATT_GROUPS = ATT_HEADS // ATT_KV_HEADS
ATT_HD = 64
WINDOW = 128
QBLK = 128
GRID_W = 64
ROPE_BASE = 10000.0
CONV_W = 3
CONV_HALO_ROWS = 16
EPS = 1e-6
NEG = -1e30

ROW_TILE = 512
VMEM_LIMIT = 48 * 1024 * 1024


def _cparams(*sem):
    return pltpu.CompilerParams(dimension_semantics=sem, vmem_limit_bytes=VMEM_LIMIT)


def _silu(x):
    return x * jax.nn.sigmoid(x)


def _dot(a, b):
    return jnp.dot(a, b, preferred_element_type=F32)


def _dot_nt(a, b):
    return lax.dot_general(a, b, (((1,), (1,)), ((), ())), preferred_element_type=F32)


def _ada_kernel(c_ref, w_ref, b_ref, o_ref):
    c = c_ref[...]
    o_ref[0] = jnp.dot(_silu(c), w_ref[0], precision=HIGHEST, preferred_element_type=F32) + b_ref[0]


def _ada_call(cs, w_ada, b_ada):
    depth = w_ada.shape[0]
    rows = cs.shape[0]
    nt = 3
    return pl.pallas_call(
        _ada_kernel,
        grid=(depth, nt),
        in_specs=[
            pl.BlockSpec((rows, D_MODEL), lambda l, j: (0, 0)),
            pl.BlockSpec((1, D_MODEL, D_MODEL), lambda l, j: (l, 0, j)),
            pl.BlockSpec((1, 1, D_MODEL), lambda l, j: (l, 0, j)),
        ],
        out_specs=pl.BlockSpec((1, rows, D_MODEL), lambda l, j: (l, 0, j)),
        out_shape=jax.ShapeDtypeStruct((depth, rows, 3 * D_MODEL), F32),
        compiler_params=_cparams("arbitrary", "arbitrary"),
        name="adaln",
    )(cs, w_ada, b_ada.reshape(depth, 1, 3 * D_MODEL))


def _modulated(x, mod_ref, nw_ref):
    ms = jnp.mean(x * x, axis=-1, keepdims=True)
    y = x * lax.rsqrt(ms + EPS) * nw_ref[...]
    shift = mod_ref[0, 0:1, :]
    scale = mod_ref[0, 1:2, :]
    return (y * (1.0 + scale) + shift).astype(BF16)


E_COLS = (0, 512, 1024, 1152, 1664, 2176, 2688)


def _log_sigmoid(x):
    return jnp.minimum(x, 0.0) - jnp.log(1.0 + jnp.exp(-jnp.abs(x)))


def _chunk_cumsum(g):
    rows, width = g.shape
    pos = lax.broadcasted_iota(jnp.int32, g.shape, 0) % GLA_CHUNK
    fwd = (lax.broadcasted_iota(jnp.int32, g.shape, 1) % (2 * GLA_DK)) < GLA_DK
    c = g
    k = 1
    while k < GLA_CHUNK:
        c = c + jnp.where(pos >= k, pltpu.roll(c, k, 0), 0.0)
        k *= 2
    total = jnp.concatenate([jnp.broadcast_to(c[r + GLA_CHUNK - 1:r + GLA_CHUNK, :], (GLA_CHUNK, width))
                             for r in range(0, rows, GLA_CHUNK)], axis=0)
    return jnp.where(fwd, c, total - c + g)


def _even_in_compute(x, refs, outs):
    mod_ref, nw_ref, w_ref, w2h_ref, w2l_ref, b2_ref = refs
    qk_ref, v_ref, bc_ref, sg_ref, u_ref, ss_ref = outs
    h = _modulated(x, mod_ref, nw_ref)
    c = E_COLS
    lr = _dot(h, w_ref[:, c[2]:c[3]])
    lr_hi = lr.astype(BF16)
    lr_lo = (lr - lr_hi.astype(F32)).astype(BF16)
    pre = _dot(lr_hi, w2h_ref[...]) + (_dot(lr_lo, w2h_ref[...]) + _dot(lr_hi, w2l_ref[...])) + b2_ref[...]
    bc_ref[0] = _chunk_cumsum(_log_sigmoid(pre) * (1.0 / GLA_GATE_NORM))
    qk_ref[0] = _dot(h, w_ref[:, c[0]:c[1]]).astype(BF16)
    v_ref[0] = _dot(h, w_ref[:, c[1]:c[2]]).astype(BF16)
    sg_ref[0] = _silu(_dot(h, w_ref[:, c[3]:c[4]])).astype(BF16)
    u_ref[0] = _dot(h, w_ref[:, c[4]:c[5]])
    ss_ref[0] = _silu(_dot(h, w_ref[:, c[5]:c[6]])).astype(BF16)


O_COLS = (0, 512, 768, 1280, 1792, 2304, 2816, 3328)


def _rope(x, cos, sin):
    width = x.shape[1]
    lane = lax.broadcasted_iota(jnp.int32, x.shape, 1)
    partner = jnp.where((lane & 16) == 0, pltpu.roll(x, width - 16, 1), pltpu.roll(x, 16, 1))
    return x * cos + partner * sin


def _head_pair_norm(x, w, cos, sin, both):
    low = lax.broadcasted_iota(jnp.int32, x.shape, 1) < ATT_HD
    x2 = x * x
    ms = jnp.sum(jnp.where(low, x2, 0.0), axis=-1, keepdims=True) * (1.0 / ATT_HD)
    rs = lax.rsqrt(ms + EPS)
    if both:
        ms_hi = jnp.sum(jnp.where(low, 0.0, x2), axis=-1, keepdims=True) * (1.0 / ATT_HD)
        rs = jnp.where(low, rs, lax.rsqrt(ms_hi + EPS))
    else:
        rs = jnp.where(low, rs, 1.0)
    y = x * rs * w
    return y if cos is None else _rope(y, cos, sin)


def _odd_in_compute(x, refs, outs, rope):
    if rope:
        mod_ref, nw_ref, w_ref, qn_ref, kn_ref, cq_ref, sq_ref, ck_ref, sk_ref = refs
        cq, sq, ck, sk = cq_ref[...], sq_ref[...], ck_ref[...], sk_ref[...]
    else:
        mod_ref, nw_ref, w_ref, qn_ref, kn_ref = refs
        cq = sq = ck = sk = None
    q_ref, kv_ref, sa_ref, p_ref, bgz_ref = outs
    h = _modulated(x, mod_ref, nw_ref)
    c = O_COLS
    q = _dot(h, w_ref[:, c[0]:c[1]])
    for t in range(BRANCH // 128):
        lanes = slice(t * 128, (t + 1) * 128)
        qt = _head_pair_norm(q[:, lanes], qn_ref[:, lanes], cq, sq, True)
        q_ref[0, :, lanes] = (qt * Q_SCALE).astype(BF16)
    kv = _dot(h, w_ref[:, c[1]:c[2]])
    for t in range(ATT_KV_HEADS):
        lanes = slice(t * 128, (t + 1) * 128)
        kv_ref[0, :, lanes] = _head_pair_norm(kv[:, lanes], kn_ref[:, lanes], ck, sk, False)
    sa_ref[0] = _silu(_dot(h, w_ref[:, c[2]:c[3]])).astype(BF16)
    xc = _dot(h, w_ref[:, c[3]:c[4]])
    bg = _dot(h, w_ref[:, c[4]:c[5]])
    cg = _dot(h, w_ref[:, c[5]:c[6]])
    zc = _dot(h, w_ref[:, c[6]:c[7]])
    p_ref[0] = (cg * xc).astype(BF16)
    bgz_ref[0] = (bg * _silu(zc)).astype(BF16)


def _tile_spec(tm, n):
    return pl.BlockSpec((1, tm, n), lambda b, i: (b, i, 0))


def _whole_spec(a):
    return pl.BlockSpec(a.shape, lambda b, i: (0,) * a.ndim)


def _mod_spec():
    return pl.BlockSpec((1, 3, D_MODEL), lambda b, i: (b, 0, 0))


def _even_in_part(tm, mod, nw, w_in, gla_w):
    args = [mod, nw, w_in, *gla_w]
    specs = [_mod_spec()] + [_whole_spec(a) for a in args[1:]]
    outs = [(BRANCH, dt) for dt in (BF16, BF16, F32, BF16, F32, BF16)]
    return _even_in_compute, args, specs, outs


def _odd_in_part(tm, mod, nw, w_in, qnw, knw, tables):
    args = [mod, nw, w_in, qnw, knw, *tables]
    specs = ([_mod_spec()] + [_whole_spec(a) for a in args[1:5]]
             + [pl.BlockSpec((tm, a.shape[1]), lambda b, i: (i, 0)) for a in tables])
    outs = [(BRANCH, BF16), (2 * ATT_KV_HEADS * ATT_HD, F32), (BRANCH, BF16), (BRANCH, BF16), (BRANCH, BF16)]
    return functools.partial(_odd_in_compute, rope=bool(tables)), args, specs, outs


def _stage_kernel(*refs, n_out_args, n_in_args, out_fn, in_fn):
    x_ref = refs[0]
    out_args = refs[1:1 + n_out_args]
    in_args = refs[1 + n_out_args:1 + n_out_args + n_in_args]
    outs = refs[1 + n_out_args + n_in_args:]
    x = x_ref[0]
    if out_fn is not None:
        x = out_fn(x, out_args)
        outs[0][0] = x
        outs = outs[1:]
    if in_fn is not None:
        in_fn(x, in_args, outs)


def _stage_call(name, x, out_part, in_part):
    bm, lm, _ = x.shape
    tm = min(ROW_TILE, lm)
    args, specs, out_defs = [x], [_tile_spec(tm, D_MODEL)], []
    out_fn = in_fn = None
    n_out_args = n_in_args = 0
    if out_part is not None:
        out_fn, a, s = out_part
        args, specs, n_out_args = args + a, specs + s, len(a)
        out_defs.append((D_MODEL, F32))
    if in_part is not None:
        in_fn, a, s, o = in_part
        args, specs, n_in_args = args + a, specs + s, len(a)
        out_defs += o
    res = pl.pallas_call(
        functools.partial(_stage_kernel, n_out_args=n_out_args, n_in_args=n_in_args, out_fn=out_fn, in_fn=in_fn),
        grid=(bm, lm // tm),
        in_specs=specs,
        out_specs=[_tile_spec(tm, n) for n, _ in out_defs],
        out_shape=[jax.ShapeDtypeStruct((bm, lm, n), dt) for n, dt in out_defs],
        compiler_params=_cparams("parallel", "parallel"),
        name=name,
    )(*args)
    return res


def _loop(n, body, init, unroll):
    if n <= unroll:
        for i in range(n):
            init = body(i, init)
        return init
    return lax.fori_loop(0, n, body, init, unroll=unroll)


def _gla_kernel(*refs, seq, has_state, heads):
    if has_state:
        (qk_ref, v_ref, bc_ref, sg_ref, on_ref, _, s0_ref, y_ref, sn_ref,
         ut_scr, sf_scr, sb_scr, qd_scr, o_scr) = refs
    else:
        (qk_ref, v_ref, bc_ref, sg_ref, on_ref, _, y_ref, sn_ref,
         ut_scr, sf_scr, sb_scr, qd_scr, o_scr) = refs
        s0_ref = None
    C = GLA_CHUNK
    W = 2 * GLA_DK
    R = GLA_BLOCK
    CB = R // C
    nc = seq // C
    nb = seq // R
    ri = lax.broadcasted_iota(jnp.int32, (R, R), 0)
    ci = lax.broadcasted_iota(jnp.int32, (R, R), 1)
    same = (ri // C) == (ci // C)
    tril = same & (ri >= ci)
    triu = same & (ri <= ci)
    fwd = lax.broadcasted_iota(jnp.int32, (R, W), 1) < GLA_DK
    fwd_row = lax.broadcasted_iota(jnp.int32, (1, W), 1) < GLA_DK
    fwd_st = lax.broadcasted_iota(jnp.int32, (GLA_DV, W), 1) < GLA_DK
    row_chunk = lax.broadcasted_iota(jnp.int32, (R, W), 0) // C
    scale = GLA_DK ** -0.5

    def rows(c):
        return pl.ds(pl.multiple_of(c * C, C), C)

    def block_rows(j):
        return pl.ds(pl.multiple_of(j * R, R), R)

    def state_rows(c):
        return pl.ds(pl.multiple_of(c * GLA_DV, GLA_DV), GLA_DV)

    def intra(h, j, carry):
        r = block_rows(j)
        hl = slice(h * W, (h + 1) * W)
        qk = qk_ref[0, r, hl].astype(F32)
        bc = bc_ref[0, r, hl]
        v = v_ref[0, r, hl].astype(F32)
        sw = pltpu.roll(qk, GLA_DK, 1)
        q2 = jnp.where(fwd, qk, sw) * scale
        k2 = jnp.where(fwd, sw, qk)
        b_mid = bc[C // 2:C // 2 + 1, :]
        b_end = jnp.where(fwd_row, bc[C - 1:C, :], bc[0:1, :])
        for c in range(1, CB):
            o0 = c * C
            b_mid = jnp.where(row_chunk == c, bc[o0 + C // 2:o0 + C // 2 + 1, :], b_mid)
            b_end = jnp.where(row_chunk == c, jnp.where(fwd_row, bc[o0 + C - 1:o0 + C, :], bc[o0:o0 + 1, :]), b_end)
        qd = (q2 * jnp.exp(bc)).astype(BF16)
        qh = q2 * jnp.exp(bc - b_mid)
        kh = (k2 * jnp.exp(b_mid - bc)).astype(BF16)
        att = (jnp.where(tril, _dot_nt(jnp.where(fwd, qh, 0.0).astype(BF16), kh), 0.0)
               + jnp.where(triu, _dot_nt(jnp.where(fwd, 0.0, qh).astype(BF16), kh), 0.0))
        o = _dot(att.astype(BF16), v.astype(BF16))
        kd = k2 * jnp.exp(b_end - bc)
        kd_wide = jnp.concatenate([jnp.where(row_chunk == c, kd, 0.0) for c in range(CB)], axis=1)
        ut = _dot(v.T.astype(BF16), kd_wide.astype(BF16))
        qd_scr[h, r, :] = qd
        o_scr[h, r, :] = o
        ut_scr[h, j] = ut
        return carry

    for h in range(heads):
        _loop(nb, functools.partial(intra, h), 0, 4)

    def scan(h, j, st):
        jb = nb - 1 - j
        hl = slice(h * W, (h + 1) * W)
        bcf = bc_ref[0, block_rows(j), hl]
        bcb = bc_ref[0, block_rows(jb), hl]
        utf = ut_scr[h, j]
        utb = ut_scr[h, jb]
        for c in range(CB):
            cb = CB - 1 - c
            stb = st.astype(BF16)
            sf_scr[h, j, :, c * W:(c + 1) * W] = stb
            sb_scr[h, jb, :, cb * W:(cb + 1) * W] = stb
            b_end = jnp.where(fwd_row, bcf[c * C + C - 1:c * C + C, :], bcb[cb * C:cb * C + 1, :])
            inc = jnp.where(fwd_st, utf[:, c * W:(c + 1) * W], utb[:, cb * W:(cb + 1) * W])
            st = st * jnp.exp(b_end) + inc
        return st

    for h in range(heads):
        if has_state:
            init = jnp.concatenate([s0_ref[0, 0, h].T, s0_ref[0, 1, h].T], axis=1)
        else:
            init = jnp.zeros((GLA_DV, W), F32)
        st = _loop(nb, functools.partial(scan, h), init, 1)
        sn_ref[0, 0, 0, h] = st[:, :GLA_DK].T
        sn_ref[0, 0, 1, h] = st[:, GLA_DK:].T

    fwd_wide = (lax.broadcasted_iota(jnp.int32, (GLA_DV, CB * W), 1) % W) < GLA_DK

    def finish(h, j, carry):
        r = block_rows(j)
        hl = slice(h * W, (h + 1) * W)
        qd = qd_scr[h, r, :]
        zero = jnp.zeros((), BF16)
        qd_wide = jnp.concatenate([jnp.where(row_chunk == c, qd, zero) for c in range(CB)], axis=1)
        s_in = jnp.where(fwd_wide, sf_scr[h, j], sb_scr[h, j])
        o = o_scr[h, r, :] + _dot_nt(qd_wide, s_in)
        ms = jnp.mean(o * o, axis=-1, keepdims=True)
        y_ref[0, r, hl] = (o * lax.rsqrt(ms + EPS) * on_ref[...] * sg_ref[0, r, hl].astype(F32)).astype(BF16)
        return carry

    for h in range(heads):
        _loop(nb, functools.partial(finish, h), 0, 4)


def _gla_call(qk, v, bc, sg, onorm, s0, layer, states):
    bsz, seq, _ = qk.shape
    has_state = s0 is not None
    nb = seq // GLA_BLOCK
    wide = (GLA_BLOCK // GLA_CHUNK) * 2 * GLA_DK
    hps = GLA_HEADS if nb == 1 else 1
    per_head = pl.BlockSpec((1, seq, 128 * hps), lambda b, h: (b, 0, h))
    in_specs = [per_head, per_head, per_head, per_head, pl.BlockSpec((1, GLA_DV), lambda b, h: (0, 0)),
                pl.BlockSpec(memory_space=pl.ANY)]
    args = [qk, v, bc, sg, onorm, states]
    if has_state:
        in_specs.append(pl.BlockSpec((1, 2, hps, GLA_DK, GLA_DV), lambda b, h: (b, 0, h, 0, 0)))
        args.append(s0)
    new_state_spec = pl.BlockSpec((1, 1, 2, hps, GLA_DK, GLA_DV), lambda b, h: (b, layer, 0, h, 0, 0))
    return pl.pallas_call(
        functools.partial(_gla_kernel, seq=seq, has_state=has_state, heads=hps),
        grid=(bsz, GLA_HEADS // hps),
        in_specs=in_specs,
        out_specs=[per_head, new_state_spec],
        out_shape=[jax.ShapeDtypeStruct((bsz, seq, BRANCH), BF16),
                   jax.ShapeDtypeStruct(states.shape, F32)],
        input_output_aliases={5: 1},
        scratch_shapes=[pltpu.VMEM((hps, nb, GLA_DV, wide), F32),
                        pltpu.VMEM((hps, nb, GLA_DV, wide), BF16),
                        pltpu.VMEM((hps, nb, GLA_DV, wide), BF16),
                        pltpu.VMEM((hps, seq, 2 * GLA_DK), BF16),
                        pltpu.VMEM((hps, seq, GLA_DV), F32)],
        compiler_params=_cparams("parallel", "parallel"),
        name="gla",
    )(*args)


def _s5_prep_kernel(lr_ref, li_ref, ld_ref, br_ref, bi_ref, cr_ref, ci_ref,
                    m_ref, win_ref, wout_ref, ar_ref, ai_ref, *, nlev):
    cs = S5_CHUNK
    P2 = 2 * S5_STATE
    lam_re, lam_im = lr_ref[0], li_ref[0]
    dt = jnp.exp(ld_ref[0])
    er, ei = lam_re * dt, lam_im * dt
    mag = jnp.exp(er)
    nr, ni = mag * jnp.cos(ei) - 1.0, mag * jnp.sin(ei)
    den = lam_re * lam_re + lam_im * lam_im
    cr = (nr * lam_re + ni * lam_im) / den
    ci = (ni * lam_re - nr * lam_im) / den
    b_re, b_im = br_ref[0], bi_ref[0]
    bb_re = cr * b_re - ci * b_im
    bb_im = cr * b_im + ci * b_re
    c_re, c_im = cr_ref[0], ci_ref[0]

    def power(k):
        m = jnp.exp(k * er)
        return m * jnp.cos(k * ei), m * jnp.sin(k * ei)

    def outer(a, b):
        return (a[:, None, :] * b[None, :, :]).reshape(S5_ROW, P2)

    def cmul(ar, ai, br, bi):
        return outer(ar, br) - outer(ai, bi), outer(ar, bi) + outer(ai, br)

    s_row = lax.broadcasted_iota(jnp.int32, (cs, P2), 0).astype(F32)
    fwd = lax.broadcasted_iota(jnp.int32, (cs, P2), 1) < S5_STATE
    half = float(cs // 2)
    k_in = jnp.where(fwd, half - s_row, s_row - half)
    l_re, l_im = cmul(*power(k_in), bb_re, bb_im)
    r_re, r_im = cmul(*power(-k_in), c_re, c_im)
    lcat = jnp.concatenate([l_re, l_im], axis=1)
    rcat = jnp.concatenate([r_re, -r_im], axis=1)
    fwd2 = (lax.broadcasted_iota(jnp.int32, (S5_ROW, S5_ROW), 1) % P2) < S5_STATE
    nt = (((1,), (1,)), ((), ()))
    m_f = lax.dot_general(jnp.where(fwd2, lcat, 0.0), rcat, nt, precision=HIGHEST, preferred_element_type=F32)
    m_b = lax.dot_general(jnp.where(fwd2, 0.0, lcat), rcat, nt, precision=HIGHEST, preferred_element_type=F32)
    s_in = lax.broadcasted_iota(jnp.int32, (S5_ROW, S5_ROW), 0) // S5_GROUP
    s_out = lax.broadcasted_iota(jnp.int32, (S5_ROW, S5_ROW), 1) // S5_GROUP
    m_ref[0] = (jnp.where(s_in <= s_out, m_f, 0.0) + jnp.where(s_in >= s_out, m_b, 0.0)).astype(BF16)
    w_re, w_im = cmul(*power(jnp.where(fwd, (cs - 1.0) - s_row, s_row)), bb_re, bb_im)
    win_ref[0] = jnp.concatenate([w_re, w_im], axis=1).astype(BF16)
    o_re, o_im = cmul(*power(jnp.where(fwd, s_row + 1.0, cs - s_row)), c_re, c_im)
    wout_ref[0] = jnp.concatenate([o_re, -o_im], axis=1).astype(BF16)
    lev = lax.shift_left(jnp.int32(cs), lax.broadcasted_iota(jnp.int32, (nlev, P2), 0)).astype(F32)
    ar_ref[0], ai_ref[0] = power(lev)


def _s5_operators(lam_re, lam_im, log_dt, b_re, b_im, c_re, c_im, nlev):
    P2 = 2 * S5_STATE
    lanes = lambda a: a.transpose(1, 0, 2).reshape(S5_GROUPS, 1, P2)
    both = lambda a: jnp.concatenate([a, a], axis=-1)
    args = (lanes(lam_re), lanes(lam_im),
            lanes(jnp.broadcast_to(log_dt[..., None], lam_re.shape)),
            both(b_re.transpose(0, 2, 1)), both(b_im.transpose(0, 2, 1)), both(c_re), both(c_im))
    g3 = lambda g: (g, 0, 0)
    vec = pl.BlockSpec((1, 1, P2), g3)
    mat = pl.BlockSpec((1, S5_GROUP, P2), g3)
    op = pl.BlockSpec((1, S5_ROW, S5_ROW), g3)
    lev = pl.BlockSpec((1, nlev, P2), g3)
    return pl.pallas_call(
        functools.partial(_s5_prep_kernel, nlev=nlev),
        grid=(S5_GROUPS,),
        in_specs=[vec, vec, vec, mat, mat, mat, mat],
        out_specs=[op, op, op, lev, lev],
        out_shape=[jax.ShapeDtypeStruct((S5_GROUPS, S5_ROW, S5_ROW), BF16)] * 3
        + [jax.ShapeDtypeStruct((S5_GROUPS, nlev, P2), F32)] * 2,
        compiler_params=_cparams("parallel"),
        name="s5_prep",
    )(*args)


def _gelu_tanh(x):
    return 0.5 * x * (1.0 + jnp.tanh(math.sqrt(2.0 / math.pi) * (x + 0.044715 * (x * x * x))))


def _slot_transpose(v, slot):
    v = list(v)
    d = len(v) // 2
    while d >= 1:
        take_own = (slot & d) == 0
        for i in range(len(v)):
            if i & d == 0:
                a, b = v[i], v[i + d]
                v[i] = jnp.where(take_own, a, pltpu.roll(b, d * S5_GROUP, 1))
                v[i + d] = jnp.where(take_own, pltpu.roll(a, 128 - d * S5_GROUP, 1), b)
        d //= 2
    return v


def _s5_kernel(*refs, nseq, nrow, nlev, has_state):
    if has_state:
        (u_ref, m_ref, win_ref, wout_ref, ar_ref, ai_ref, d_ref, h0_ref,
         z_ref, last_ref, t_scr, y_scr, tmp_r, tmp_i) = refs
    else:
        (u_ref, m_ref, win_ref, wout_ref, ar_ref, ai_ref, d_ref,
         z_ref, last_ref, t_scr, y_scr, tmp_r, tmp_i) = refs
    P2 = 2 * S5_STATE
    R = nseq * nrow
    GP = S5_SLAB_GROUPS
    TR = 16
    slot = lax.broadcasted_iota(jnp.int32, (TR, 128), 1) // S5_GROUP

    def token_rows(nt, s):
        return pl.ds(pl.multiple_of(nt * (TR * S5_CHUNK), TR * S5_CHUNK) + s, TR, stride=S5_CHUNK)

    def gather(nt, carry):
        for j in range(2):
            src = [u_ref[0, token_rows(nt, 8 * j + s8), :] for s8 in range(8)]
            by_group = _slot_transpose(src, slot)
            for gl in range(GP):
                t_scr[gl, pl.ds(pl.multiple_of(nt * TR, TR), TR), j * 128:(j + 1) * 128] = by_group[gl].astype(BF16)
        return carry

    lax.fori_loop(0, R // TR, gather, 0, unroll=2)

    n = lax.broadcasted_iota(jnp.int32, (R, P2), 0) % nrow
    row = lax.broadcasted_iota(jnp.int32, (R, P2), 0)
    fwd = lax.broadcasted_iota(jnp.int32, (R, P2), 1) < S5_STATE
    fwd_b = lax.broadcasted_iota(jnp.int32, (nseq, P2), 1) < S5_STATE

    def shift(a, k):
        dn = pltpu.roll(a, k, 0)
        up = pltpu.roll(a, R - k, 0)
        return jnp.where(fwd, jnp.where(n >= k, dn, 0.0), jnp.where(n < nrow - k, up, 0.0))

    def ends(t):
        if nseq == 1:
            return jnp.where(fwd_b, t[nrow - 1:nrow, :], t[0:1, :])
        return jnp.where(fwd_b, t[pl.ds(nrow - 1, nseq, stride=nrow), :], t[pl.ds(0, nseq, stride=nrow), :])

    def per_group(gl, carry):
        ub = t_scr[gl]
        x = _dot(ub, win_ref[gl])
        xr, xi = x[:, :P2], x[:, P2:]
        er, ei = shift(xr, 1), shift(xi, 1)
        if has_state:
            h0 = h0_ref[0, gl]
            for b in range(nseq):
                first = row == jnp.where(fwd, b * nrow, b * nrow + nrow - 1)
                er = jnp.where(first, h0[b:b + 1, :P2], er)
                ei = jnp.where(first, h0[b:b + 1, P2:], ei)
        for j in range(nlev):
            k = 2 ** j
            ar = ar_ref[gl, j:j + 1, :]
            ai = ai_ref[gl, j:j + 1, :]
            sr, si = shift(er, k), shift(ei, k)
            er, ei = er + ar * sr - ai * si, ei + ar * si + ai * sr
        ecat = jnp.concatenate([er, ei], axis=1).astype(BF16)
        y_scr[gl] = _dot(ub, m_ref[gl]) + _dot_nt(ecat, wout_ref[gl])
        a1r = ar_ref[gl, 0:1, :]
        a1i = ai_ref[gl, 0:1, :]
        tmp_r[...] = a1r * er - a1i * ei + xr
        tmp_i[...] = a1r * ei + a1i * er + xi
        last_ref[0, gl, :, :P2] = ends(tmp_r)
        last_ref[0, gl, :, P2:] = ends(tmp_i)
        return carry

    lax.fori_loop(0, GP, per_group, 0, unroll=2)

    def scatter(nt, carry):
        for j in range(2):
            ys = [y_scr[gl, pl.ds(pl.multiple_of(nt * TR, TR), TR), j * 128:(j + 1) * 128] for gl in range(GP)]
            by_token = _slot_transpose(ys, slot)
            for s8 in range(8):
                rows = token_rows(nt, 8 * j + s8)
                z_ref[0, rows, :] = _gelu_tanh(by_token[s8] + d_ref[...] * u_ref[0, rows, :])
        return carry

    lax.fori_loop(0, R // TR, scatter, 0, unroll=2)


def _s5_call(u, ops, d, h0, seq):
    m, win, wout, ar, ai = ops
    bm, lm, _ = u.shape
    nrow = seq // S5_CHUNK
    nlev = int(math.log2(nrow))
    total = bm * lm // seq
    nseq = max(1, min(total, S5_BLOCK_ROWS // nrow))
    nblk = total // nseq
    R = nseq * nrow
    tok = R * S5_CHUNK
    GP = S5_SLAB_GROUPS
    ns = BRANCH // 128
    has_state = h0 is not None
    slab = pl.BlockSpec((1, tok, 128), lambda t, i: (i, 0, t))
    op = pl.BlockSpec((GP, S5_ROW, S5_ROW), lambda t, i: (t, 0, 0))
    lev = pl.BlockSpec((GP, ar.shape[1], 2 * S5_STATE), lambda t, i: (t, 0, 0))
    state = pl.BlockSpec((1, GP, nseq, 4 * S5_STATE), lambda t, i: (i, t, 0, 0))
    in_specs = [slab, op, op, op, lev, lev, pl.BlockSpec((1, 128), lambda t, i: (0, t))]
    args = [u.reshape(nblk, tok, BRANCH), m, win, wout, ar, ai, d]
    if has_state:
        in_specs.append(state)
        args.append(h0.reshape(nblk, nseq, S5_GROUPS, 4 * S5_STATE).transpose(0, 2, 1, 3))
    z, last = pl.pallas_call(
        functools.partial(_s5_kernel, nseq=nseq, nrow=nrow, nlev=nlev, has_state=has_state),
        grid=(ns, nblk),
        in_specs=in_specs,
        out_specs=[slab, state],
        out_shape=[jax.ShapeDtypeStruct((nblk, tok, BRANCH), F32),
                   jax.ShapeDtypeStruct((nblk, S5_GROUPS, nseq, 4 * S5_STATE), F32)],
        scratch_shapes=[pltpu.VMEM((GP, R, S5_ROW), BF16), pltpu.VMEM((GP, R, S5_ROW), F32),
                        pltpu.VMEM((R, 2 * S5_STATE), F32), pltpu.VMEM((R, 2 * S5_STATE), F32)],
        compiler_params=_cparams("parallel", "parallel"),
        name="s5",
    )(*args)
    last = last.transpose(0, 2, 1, 3).reshape(total, S5_GROUPS, 4 * S5_STATE)
    return z.reshape(bm, lm, BRANCH), last


HEAD_ORDER = (0, 2, 1, 3)


def _stack_heads(q, heads):
    head = lax.broadcasted_iota(jnp.int32, q.shape, 1) // ATT_HD
    zero = jnp.zeros((), q.dtype)
    return jnp.concatenate([jnp.where(head == h, q, zero) for h in heads], axis=0)


def _key_value_operands(kv):
    low = lax.broadcasted_iota(jnp.int32, kv.shape, 1) < ATT_HD
    sw = pltpu.roll(kv, ATT_HD, 1)
    kk = jnp.where(low, kv, sw).astype(BF16)
    return (jnp.concatenate([kk, kk], axis=1),
            jnp.where(low, sw, 1.0).astype(BF16), jnp.where(low, 1.0, kv).astype(BF16))


LOG2E = 1.4426950408889634
Q_SCALE = (ATT_HD ** -0.5) * LOG2E


def _attend_pair(q, parity, key_sets, sink_ref):
    rows = q.shape[0]
    heads = (parity, parity + 2)
    qs = _stack_heads(q, heads)
    sink = jnp.concatenate([jnp.broadcast_to(sink_ref[h:h + 1, 0:1], (rows, 1)) for h in heads], axis=0) * LOG2E
    scores = []
    t = None
    for k4, _, _, bias in key_sets:
        s = _dot_nt(qs, k4)
        if bias is not None:
            s = s + jnp.concatenate([bias, bias], axis=0)
        scores.append(s)
        for c in range(0, s.shape[1], 128):
            t = s[:, c:c + 128] if t is None else jnp.maximum(t, s[:, c:c + 128])
    m = jnp.maximum(jnp.max(t, axis=-1, keepdims=True), sink)
    lane = lax.broadcasted_iota(jnp.int32, (2 * rows, 2 * ATT_HD), 1)
    den_lanes = (lane >= ATT_HD) if parity == 0 else (lane < ATT_HD)
    o = jnp.where(den_lanes, jnp.exp2(sink - m), 0.0)
    for s, (_, va, vb, _) in zip(scores, key_sets):
        o = o + _dot(jnp.exp2(s - m).astype(BF16), va if parity == 0 else vb)
    return o / pltpu.roll(o, ATT_HD, 1)


def _attend_stacked(q, key_sets, sink_ref):
    rows = q.shape[0]
    qs = _stack_heads(q, HEAD_ORDER)
    sink = jnp.concatenate([jnp.broadcast_to(sink_ref[h:h + 1, 0:1], (rows, 1)) for h in HEAD_ORDER],
                           axis=0) * LOG2E
    scores = []
    t = None
    for k4, _, _, bias in key_sets:
        s = _dot_nt(qs, k4)
        if bias is not None:
            s = s + jnp.concatenate([bias] * ATT_GROUPS, axis=0)
        scores.append(s)
        for c in range(0, s.shape[1], 128):
            t = s[:, c:c + 128] if t is None else jnp.maximum(t, s[:, c:c + 128])
    m = jnp.maximum(jnp.max(t, axis=-1, keepdims=True), sink)
    e_sink = jnp.exp2(sink - m)
    low = lax.broadcasted_iota(jnp.int32, (2 * rows, 2 * ATT_HD), 1) < ATT_HD
    o_even = jnp.where(low, 0.0, e_sink[:2 * rows])
    o_odd = jnp.where(low, e_sink[2 * rows:], 0.0)
    for s, (_, va, vb, _) in zip(scores, key_sets):
        p = jnp.exp2(s - m).astype(BF16)
        o_even = o_even + _dot(p[:2 * rows], va)
        o_odd = o_odd + _dot(p[2 * rows:], vb)
    return o_even / pltpu.roll(o_even, ATT_HD, 1), o_odd / pltpu.roll(o_odd, ATT_HD, 1)


def _attend(q, key_sets, sink_ref, split):
    rows = q.shape[0]
    if split:
        o_even = _attend_pair(q, 0, key_sets, sink_ref)
        o_odd = _attend_pair(q, 1, key_sets, sink_ref)
    else:
        o_even, o_odd = _attend_stacked(q, key_sets, sink_ref)
    low = lax.broadcasted_iota(jnp.int32, o_even.shape, 1) < ATT_HD
    pair = jnp.where(low, o_even, o_odd)
    return jnp.concatenate([pair[:rows], pair[rows:]], axis=1)


def _att_ctx_kernel(q_ref, kv_ref, sa_ref, sink_ref, kprev_ref, vprev_ref, y_ref, kc_ref, vc_ref):
    del kprev_ref, vprev_ref
    W = ATT_GROUPS * ATT_HD
    results = []
    for g in range(ATT_KV_HEADS):
        kv = kv_ref[0, :, g * 2 * ATT_HD:(g + 1) * 2 * ATT_HD]
        heads = slice(g * W, (g + 1) * W)
        k4, v_ones, _ = _key_value_operands(kv)
        key_sets = [(k4, _value_rows(v_ones.astype(F32)), None)]
        out = _attend_keys_on_rows(q_ref[0, :, heads], key_sets, sink_ref.at[g])
        results.append((kv, (out * sa_ref[0, :, heads].astype(F32)).astype(BF16)))
    for g, (kv, y) in enumerate(results):
        kc_ref[0, 0, g] = kv[:, :ATT_HD]
        vc_ref[0, 0, g] = kv[:, ATT_HD:]
        y_ref[0, :, g * W:(g + 1) * W] = y


def _att_ctx_call(q, kv, sa, sink, layer, caches):
    bsz, seq, _ = q.shape
    row = lambda b: (b, 0, 0)
    cache_spec = pl.BlockSpec((1, 1, ATT_KV_HEADS, seq, ATT_HD), lambda b: (b, layer, 0, 0, 0))
    cache_shape = jax.ShapeDtypeStruct(caches[0].shape, F32)
    return pl.pallas_call(
        _att_ctx_kernel,
        grid=(bsz,),
        in_specs=[pl.BlockSpec((1, seq, BRANCH), row), pl.BlockSpec((1, seq, kv.shape[-1]), row),
                  pl.BlockSpec((1, seq, BRANCH), row), pl.BlockSpec(sink.shape, lambda b: (0, 0, 0)),
                  pl.BlockSpec(memory_space=pl.ANY), pl.BlockSpec(memory_space=pl.ANY)],
        out_specs=[pl.BlockSpec((1, seq, BRANCH), row), cache_spec, cache_spec],
        out_shape=[jax.ShapeDtypeStruct((bsz, seq, BRANCH), BF16), cache_shape, cache_shape],
        input_output_aliases={4: 1, 5: 2},
        compiler_params=_cparams("parallel"),
        name="att_ctx",
    )(q, kv, sa, sink, *caches)


def _attend_keys_on_rows(q, key_sets, sink_ref):
    rows = q.shape[0]
    qs = _stack_heads(q, range(ATT_GROUPS))
    scores, top = [], None
    for k4, _, bias_t in key_sets:
        s = _dot_nt(k4, qs)
        if bias_t is not None:
            s = s + jnp.concatenate([bias_t] * ATT_GROUPS, axis=1)
        scores.append(s)
        part = jnp.max(s, axis=0, keepdims=True)
        top = part if top is None else jnp.maximum(top, part)
    sink = jnp.concatenate([jnp.broadcast_to(sink_ref[h:h + 1, 0:1], (1, rows)) for h in range(ATT_GROUPS)],
                           axis=1) * LOG2E
    m = jnp.maximum(top, sink)
    o = None
    for s, (_, vat, _) in zip(scores, key_sets):
        p = jnp.exp2(s - m).astype(BF16)
        for c, vat_c in enumerate(vat):
            pv = _dot(vat_c, p[c * 128:(c + 1) * 128, :])
            o = pv if o is None else o + pv
    on = o[:ATT_HD, :] / (o[ATT_HD:, :] + jnp.exp2(sink - m))
    pairs = [jnp.concatenate([on[:, (2 * t) * rows:(2 * t + 1) * rows],
                              on[:, (2 * t + 1) * rows:(2 * t + 2) * rows]], axis=0).T
             for t in range(ATT_GROUPS // 2)]
    return jnp.concatenate(pairs, axis=1)


def _value_rows(v_ones):
    return [v_ones[c:c + 128, :].T.astype(BF16) for c in range(0, v_ones.shape[0], 128)]


def _att_lat_kernel(q_ref, kv_ref, sa_ref, sink_ref, kc_ref, vc_ref, y_ref, k4_scr, vat_scr, *, seq):
    nb = seq // QBLK
    KW = 3 * QBLK
    low = lax.broadcasted_iota(jnp.int32, (QBLK, 2 * ATT_HD), 1) < ATT_HD

    def k_body(i, carry):
        r = pl.ds(pl.multiple_of(i * QBLK, QBLK), QBLK)
        kv = kv_ref[0, r, :]
        sw = pltpu.roll(kv, ATT_HD, 1)
        kk = jnp.where(low, kv, sw).astype(BF16)
        k4_scr[r, :] = jnp.concatenate([kk, kk], axis=1)
        vat_scr[i] = jnp.where(low, sw, 1.0).T.astype(BF16)
        return carry

    lax.fori_loop(0, nb, k_body, 0, unroll=4)
    kc = kc_ref[0, 0, 0]
    vc = vc_ref[0, 0, 0]
    kc4 = jnp.concatenate([kc] * ATT_GROUPS, axis=1).astype(BF16)
    vat_ctx = _value_rows(jnp.concatenate([vc, jnp.ones_like(vc)], axis=1))
    kj = lax.broadcasted_iota(jnp.int32, (KW, QBLK), 0)
    qi = lax.broadcasted_iota(jnp.int32, (KW, QBLK), 1)

    def q_body(i, carry):
        r = pl.ds(pl.multiple_of(i * QBLK, QBLK), QBLK)
        ws = pl.multiple_of(jnp.clip(i * QBLK - QBLK, 0, seq - KW), QBLK)
        jb = ws // QBLK
        bias_t = jnp.where(jnp.abs((i * QBLK + qi) - (ws + kj)) <= WINDOW, 0.0, NEG)
        key_sets = [(k4_scr[pl.ds(ws, KW), :], [vat_scr[jb + c] for c in range(KW // QBLK)], bias_t),
                    (kc4, vat_ctx, None)]
        out = _attend_keys_on_rows(q_ref[0, r, :], key_sets, sink_ref.at[0])
        y_ref[0, r, :] = (out * sa_ref[0, r, :].astype(F32)).astype(BF16)
        return carry

    lax.fori_loop(0, nb, q_body, 0, unroll=4)


def _rope_tables(seq, heads, pad_heads):
    pos = jnp.arange(seq)
    row = (pos // GRID_W).astype(F32)[:, None]
    col = (pos % GRID_W).astype(F32)[:, None]
    nf = ATT_HD // 4
    freq = ROPE_BASE ** (-jnp.arange(nf, dtype=F32) / nf)
    ar, ac = row * freq, col * freq
    cos = jnp.concatenate([jnp.cos(ar), jnp.cos(ar), jnp.cos(ac), jnp.cos(ac)], axis=1)
    sin = jnp.concatenate([-jnp.sin(ar), jnp.sin(ar), -jnp.sin(ac), jnp.sin(ac)], axis=1)
    cos = jnp.concatenate([jnp.tile(cos, (1, heads)), jnp.ones((seq, pad_heads * ATT_HD), F32)], axis=1)
    sin = jnp.concatenate([jnp.tile(sin, (1, heads)), jnp.zeros((seq, pad_heads * ATT_HD), F32)], axis=1)
    return cos, sin


def _att_lat_call(q, kv, sa, sink, cache_k, cache_v, e):
    bsz, seq, _ = q.shape
    past = cache_k.shape[3]
    W = ATT_GROUPS * ATT_HD
    per_kv = lambda b, k: (b, 0, k)
    cache_spec = pl.BlockSpec((1, 1, 1, past, ATT_HD), lambda b, k: (b, e, k, 0, 0))
    return pl.pallas_call(
        functools.partial(_att_lat_kernel, seq=seq),
        grid=(bsz, ATT_KV_HEADS),
        in_specs=[
            pl.BlockSpec((1, seq, W), per_kv),
            pl.BlockSpec((1, seq, 2 * ATT_HD), per_kv),
            pl.BlockSpec((1, seq, W), per_kv),
            pl.BlockSpec((1, 8, 128), lambda b, k: (k, 0, 0)),
            cache_spec,
            cache_spec,
        ],
        out_specs=pl.BlockSpec((1, seq, W), per_kv),
        out_shape=jax.ShapeDtypeStruct((bsz, seq, BRANCH), BF16),
        scratch_shapes=[pltpu.VMEM((seq, W), BF16), pltpu.VMEM((seq // QBLK, 2 * ATT_HD, QBLK), BF16)],
        compiler_params=_cparams("parallel", "parallel"),
        name="att_lat",
    )(q, kv, sa, sink, cache_k, cache_v)


def _even_out_compute(x, refs):
    mod_ref, yg_ref, z_ref, ss_ref, wg_ref, bg_ref, wo_ref = refs
    z = z_ref[0]
    glu = z * jax.nn.sigmoid(_dot(z.astype(BF16), wg_ref[...]) + bg_ref[...])
    ys = (glu * ss_ref[0].astype(F32)).astype(BF16)
    out = _dot(yg_ref[0], wo_ref[:BRANCH, :]) + _dot(ys, wo_ref[BRANCH:, :])
    return x + mod_ref[0, 2:3, :] * out


def _even_out_part(tm, mod, yg, z, ss, wglu, bglu, wout):
    args = [mod, yg, z, ss, wglu, bglu, wout]
    specs = [_mod_spec(), _tile_spec(tm, BRANCH), _tile_spec(tm, BRANCH), _tile_spec(tm, BRANCH),
             _whole_spec(wglu), _whole_spec(bglu), _whole_spec(wout)]
    return _even_out_compute, args, specs


def _odd_out_compute(x, refs, seq, tm):
    mod_ref, ya_ref, p_ref, pprev_ref, pnext_ref, bgz_ref, cw_ref, cb_ref, wo_ref = refs
    p = p_ref[0].astype(F32)
    pos = (pl.program_id(1) * tm + lax.broadcasted_iota(jnp.int32, (tm, BRANCH), 0)) % seq
    rowi = lax.broadcasted_iota(jnp.int32, (tm, BRANCH), 0)
    halo = CONV_HALO_ROWS
    p_before = pprev_ref[0].astype(F32)[halo - 1:halo, :]
    p_after = pnext_ref[0].astype(F32)[0:1, :]
    prev = jnp.where(rowi == 0, p_before, pltpu.roll(p, 1, 0))
    nxt = jnp.where(rowi == tm - 1, p_after, pltpu.roll(p, tm - 1, 0))
    prev = jnp.where(pos == 0, 0.0, prev)
    nxt = jnp.where(pos == seq - 1, 0.0, nxt)
    conv = prev * cw_ref[0:1, :] + p * cw_ref[1:2, :] + nxt * cw_ref[2:3, :] + cb_ref[...]
    yc = (bgz_ref[0].astype(F32) * conv).astype(BF16)
    out = _dot(ya_ref[0], wo_ref[:BRANCH, :]) + _dot(yc, wo_ref[BRANCH:, :])
    return x + mod_ref[0, 2:3, :] * out


def _odd_out_part(tm, lm, seq, mod, ya, p, bgz, convw, convb, wout):
    hr = CONV_HALO_ROWS
    nth = lm // hr
    prev_spec = pl.BlockSpec((1, hr, BRANCH), lambda b, i: (b, jnp.maximum(i * (tm // hr) - 1, 0), 0))
    next_spec = pl.BlockSpec((1, hr, BRANCH), lambda b, i: (b, jnp.minimum((i + 1) * (tm // hr), nth - 1), 0))
    args = [mod, ya, p, p, p, bgz, convw, convb, wout]
    specs = [_mod_spec(), _tile_spec(tm, BRANCH), _tile_spec(tm, BRANCH), prev_spec, next_spec,
             _tile_spec(tm, BRANCH), _whole_spec(convw), _whole_spec(convb), _whole_spec(wout)]
    return functools.partial(_odd_out_compute, seq=seq, tm=tm), args, specs


def _even_in_weight(w):
    dk = GLA_HEADS * GLA_DK
    q, k = w[:, :dk], w[:, dk:2 * dk]
    v = w[:, 2 * dk:2 * dk + BRANCH]
    o = 2 * dk + BRANCH
    lr = w[:, o:o + 2 * GLA_RANK]
    rest = w[:, o + 2 * GLA_RANK:]
    qk = jnp.concatenate([jnp.concatenate([q[:, h * GLA_DK:(h + 1) * GLA_DK], k[:, h * GLA_DK:(h + 1) * GLA_DK]],
                                          axis=1) for h in range(GLA_HEADS)], axis=1)
    lr = jnp.pad(lr, ((0, 0), (0, 128 - 2 * GLA_RANK)))
    return jnp.concatenate([qk, v, lr, rest], axis=1).astype(BF16)


def _odd_in_weight(w):
    q = w[:, :BRANCH]
    k = w[:, BRANCH:BRANCH + ATT_KV_HEADS * ATT_HD]
    v = w[:, BRANCH + ATT_KV_HEADS * ATT_HD:BRANCH + 2 * ATT_KV_HEADS * ATT_HD]
    rest = w[:, BRANCH + 2 * ATT_KV_HEADS * ATT_HD:]
    kv = jnp.concatenate([jnp.concatenate([k[:, h * ATT_HD:(h + 1) * ATT_HD], v[:, h * ATT_HD:(h + 1) * ATT_HD]],
                                          axis=1) for h in range(ATT_KV_HEADS)], axis=1)
    return jnp.concatenate([q, kv, rest], axis=1).astype(BF16)


def _gate_weights(w2, b2):
    zf = jnp.zeros((GLA_RANK, GLA_DK), F32)
    cols, bias = [], []
    for h in range(GLA_HEADS):
        sl = slice(h * GLA_DK, (h + 1) * GLA_DK)
        cols.append(jnp.concatenate([jnp.concatenate([w2[0][:, sl], zf], axis=1),
                                     jnp.concatenate([zf, w2[1][:, sl]], axis=1)], axis=0))
        bias += [b2[0][sl], b2[1][sl]]
    w = jnp.pad(jnp.concatenate(cols, axis=1), ((0, 128 - 2 * GLA_RANK), (0, 0)))
    w_hi = w.astype(BF16)
    w_lo = (w - w_hi.astype(F32)).astype(BF16)
    return w_hi, w_lo, jnp.concatenate(bias)[None, :]


def _run_stream(x, mods, seq, params, gla_s0, s5_h0, cache_k, cache_v):
    depth = len(params)
    bm, lm, _ = x.shape
    tm = min(ROW_TILE, lm)
    nseq = bm * lm // seq
    per_seq = lambda a: a.reshape(nseq, seq, a.shape[-1])
    flat = lambda a: a.reshape(bm, lm, a.shape[-1])
    latent = cache_k is not None
    tables = _rope_tables(seq, 2, 0) + _rope_tables(seq, 1, 1) if latent else ()

    def in_part(l):
        p = params[l]
        if l % 2 == 0:
            return _even_in_part(tm, mods[l], p["nw"], p["w_in"], p["gla_w"])
        return _odd_in_part(tm, mods[l], p["nw"], p["w_in"], p["qnw"], p["knw"], tables)

    n_even, n_odd = (depth + 1) // 2, depth // 2
    gla_states = jnp.zeros((nseq, n_even, 2, GLA_HEADS, GLA_DK, GLA_DV), F32)
    cache_shape = (nseq, n_odd, ATT_KV_HEADS, seq, ATT_HD)
    caches = None if latent else (jnp.zeros(cache_shape, F32), jnp.zeros(cache_shape, F32))
    collected = {"s5": []}
    out_part = None
    for l in range(depth + 1):
        res = _stage_call("stage%d" % l, x, out_part, in_part(l) if l < depth else None)
        if out_part is not None:
            x, res = res[0], res[1:]
        if l == depth:
            break
        p, e = params[l], l // 2
        if l % 2 == 0:
            qk, v, bc, sg, u, ss = res
            yg, gla_states = _gla_call(per_seq(qk), per_seq(v), per_seq(bc), per_seq(sg), p["onorm"],
                                       None if gla_s0 is None else gla_s0[:, e], e, gla_states)
            z, last = _s5_call(u, p["s5_ops"], p["d"], None if s5_h0 is None else s5_h0[e], seq)
            collected["s5"].append(last)
            out_part = _even_out_part(tm, mods[l], flat(yg), z, ss, p["wglu"], p["bglu"], p["wout"])
        else:
            q, kv, sa, pc, bgz = res
            if latent:
                ya = _att_lat_call(per_seq(q), per_seq(kv), per_seq(sa), p["sink"], cache_k, cache_v, e)
            else:
                ya, *caches = _att_ctx_call(per_seq(q), per_seq(kv), per_seq(sa), p["sink"], e, caches)
            out_part = _odd_out_part(tm, lm, seq, mods[l], flat(ya), pc, bgz, p["convw"], p["convb"], p["wout"])
    collected["gla"] = gla_states
    collected["caches"] = caches
    return x, collected


def kernel(x_prompt, x_sample, c, state_gla, state_s5_re, state_s5_im, cache_k, cache_v, c_ctx, norm_w, w_ada, b_ada, w_in_e, w_out_e, gla_w2, gla_b2, gla_onorm, s5_lam_re, s5_lam_im, s5_log_dt, s5_b_re, s5_b_im, s5_c_re, s5_c_im, s5_d, s5_w_glu, s5_b_glu, w_in_o, w_out_o, q_norm_w, k_norm_w, sink, conv_w, conv_b):
    depth = norm_w.shape[0]
    bp, lp, _ = x_prompt.shape
    bs, ls, _ = x_sample.shape

    rows = 8 * ((1 + bs + 7) // 8)
    cs = jnp.zeros((rows, D_MODEL), F32).at[0].set(c_ctx).at[1:1 + bs].set(c)
    mods = _ada_call(cs, w_ada, b_ada)

    params, s5_h0 = [], []
    for l in range(depth):
        e = l // 2
        p = {"nw": norm_w[l][None, :]}
        if l % 2 == 0:
            p.update(
                w_in=_even_in_weight(w_in_e[e]), wout=w_out_e[e].astype(BF16),
                gla_w=_gate_weights(gla_w2[e], gla_b2[e]), onorm=gla_onorm[e][None, :],
                d=s5_d[e][None, :], wglu=s5_w_glu[e].astype(BF16), bglu=s5_b_glu[e][None, :],
                s5_ops=_s5_operators(s5_lam_re[e], s5_lam_im[e], s5_log_dt[e], s5_b_re[e], s5_b_im[e],
                                     s5_c_re[e], s5_c_im[e], nlev=int(math.log2(max(lp, ls) // S5_CHUNK))))
            h0 = jnp.concatenate([state_s5_re[:, e], state_s5_im[:, e]], axis=1)
            s5_h0.append(h0.transpose(0, 2, 1, 3).reshape(bs, S5_GROUPS, 4 * S5_STATE))
        else:
            sk = jnp.broadcast_to(sink[e].reshape(ATT_KV_HEADS, ATT_GROUPS, 1), (ATT_KV_HEADS, ATT_GROUPS, 128))
            p.update(
                w_in=_odd_in_weight(w_in_o[e]), wout=w_out_o[e].astype(BF16),
                qnw=jnp.tile(q_norm_w[e], ATT_HEADS)[None, :],
                knw=jnp.tile(jnp.concatenate([k_norm_w[e], jnp.ones((ATT_HD,), F32)]), ATT_KV_HEADS)[None, :],
                sink=jnp.concatenate([sk, jnp.zeros((ATT_KV_HEADS, 8 - ATT_GROUPS, 128), F32)], axis=1),
                convw=conv_w[e], convb=conv_b[e][None, :])
        params.append(p)

    mods_p = [mods[l, 0:1].reshape(1, 3, D_MODEL) for l in range(depth)]
    mods_s = [mods[l, 1:1 + bs].reshape(bs, 3, D_MODEL) for l in range(depth)]
    xp, got = _run_stream(x_prompt.reshape(1, bp * lp, D_MODEL), mods_p, lp, params, None, None, None, None)
    xs, _ = _run_stream(x_sample, mods_s, ls, params, state_gla, s5_h0, cache_k, cache_v)
    last = [t.reshape(bp, S5_GROUPS, 2, 2, S5_STATE).transpose(2, 0, 3, 1, 4) for t in got["s5"]]
    return (xp.reshape(bp, lp, D_MODEL), xs,
            got["gla"],
            jnp.stack([t[0] for t in last], axis=1), jnp.stack([t[1] for t in last], axis=1),
            got["caches"][0], got["caches"][1])
```

```python
import functools
import math

import jax
import jax.numpy as jnp
from jax import lax
from jax.experimental import pallas as pl
from jax.experimental.pallas import tpu as pltpu

F32 = jnp.float32
BF16 = jnp.bfloat16
HIGHEST = lax.Precision.HIGHEST

D_MODEL = 1024
BRANCH = D_MODEL // 2
GLA_HEADS = 4
GLA_DK = 64
GLA_DV = 128
GLA_RANK = 16
GLA_GATE_NORM = 16.0
GLA_CHUNK = 64
GLA_BLOCK = 256
S5_GROUP = 16
S5_GROUPS = BRANCH // S5_GROUP
S5_STATE = 64
S5_CHUNK = 16
S5_ROW = S5_CHUNK * S5_GROUP
S5_SLAB_GROUPS = 128 // S5_GROUP
S5_BLOCK_ROWS = 256
ATT_HEADS = 8
ATT_KV_HEADS = 2
ATT_GROUPS = ATT_HEADS // ATT_KV_HEADS
ATT_HD = 64
WINDOW = 128
QBLK = 128
GRID_W = 64
ROPE_BASE = 10000.0
CONV_W = 3
CONV_HALO_ROWS = 16
EPS = 1e-6
NEG = -1e30

ROW_TILE = 512
VMEM_LIMIT = 48 * 1024 * 1024


def _cparams(*sem):
    return pltpu.CompilerParams(dimension_semantics=sem, vmem_limit_bytes=VMEM_LIMIT)


def _silu(x):
    return x * jax.nn.sigmoid(x)


def _dot(a, b):
    return jnp.dot(a, b, preferred_element_type=F32)


def _dot_nt(a, b):
    return lax.dot_general(a, b, (((1,), (1,)), ((), ())), preferred_element_type=F32)


def _ada_kernel(c_ref, w_ref, b_ref, o_ref):
    c = c_ref[...]
    o_ref[0] = jnp.dot(_silu(c), w_ref[0], precision=HIGHEST, preferred_element_type=F32) + b_ref[0]


def _ada_call(cs, w_ada, b_ada):
    depth = w_ada.shape[0]
    rows = cs.shape[0]
    nt = 3
    return pl.pallas_call(
        _ada_kernel,
        grid=(depth, nt),
        in_specs=[
            pl.BlockSpec((rows, D_MODEL), lambda l, j: (0, 0)),
            pl.BlockSpec((1, D_MODEL, D_MODEL), lambda l, j: (l, 0, j)),
            pl.BlockSpec((1, 1, D_MODEL), lambda l, j: (l, 0, j)),
        ],
        out_specs=pl.BlockSpec((1, rows, D_MODEL), lambda l, j: (l, 0, j)),
        out_shape=jax.ShapeDtypeStruct((depth, rows, 3 * D_MODEL), F32),
        compiler_params=_cparams("arbitrary", "arbitrary"),
        name="adaln",
    )(cs, w_ada, b_ada.reshape(depth, 1, 3 * D_MODEL))


def _modulated(x, mod_ref, nw_ref):
    ms = jnp.mean(x * x, axis=-1, keepdims=True)
    y = x * lax.rsqrt(ms + EPS) * nw_ref[...]
    shift = mod_ref[0, 0:1, :]
    scale = mod_ref[0, 1:2, :]
    return (y * (1.0 + scale) + shift).astype(BF16)


E_COLS = (0, 512, 1024, 1152, 1664, 2176, 2688)


def _log_sigmoid(x):
    return jnp.minimum(x, 0.0) - jnp.log(1.0 + jnp.exp(-jnp.abs(x)))


def _chunk_cumsum(g):
    rows, width = g.shape
    pos = lax.broadcasted_iota(jnp.int32, g.shape, 0) % GLA_CHUNK
    fwd = (lax.broadcasted_iota(jnp.int32, g.shape, 1) % (2 * GLA_DK)) < GLA_DK
    c = g
    k = 1
    while k < GLA_CHUNK:
        c = c + jnp.where(pos >= k, pltpu.roll(c, k, 0), 0.0)
        k *= 2
    total = jnp.concatenate([jnp.broadcast_to(c[r + GLA_CHUNK - 1:r + GLA_CHUNK, :], (GLA_CHUNK, width))
                             for r in range(0, rows, GLA_CHUNK)], axis=0)
    return jnp.where(fwd, c, total - c + g)


def _even_in_compute(x, refs, outs):
    mod_ref, nw_ref, w_ref, w2h_ref, w2l_ref, b2_ref = refs
    qk_ref, v_ref, bc_ref, sg_ref, u_ref, ss_ref = outs
    h = _modulated(x, mod_ref, nw_ref)
    c = E_COLS
    lr = _dot(h, w_ref[:, c[2]:c[3]])
    lr_hi = lr.astype(BF16)
    lr_lo = (lr - lr_hi.astype(F32)).astype(BF16)
    pre = _dot(lr_hi, w2h_ref[...]) + (_dot(lr_lo, w2h_ref[...]) + _dot(lr_hi, w2l_ref[...])) + b2_ref[...]
    bc_ref[0] = _chunk_cumsum(_log_sigmoid(pre) * (1.0 / GLA_GATE_NORM))
    qk_ref[0] = _dot(h, w_ref[:, c[0]:c[1]]).astype(BF16)
    v_ref[0] = _dot(h, w_ref[:, c[1]:c[2]]).astype(BF16)
    sg_ref[0] = _silu(_dot(h, w_ref[:, c[3]:c[4]])).astype(BF16)
    u_ref[0] = _dot(h, w_ref[:, c[4]:c[5]])
    ss_ref[0] = _silu(_dot(h, w_ref[:, c[5]:c[6]])).astype(BF16)


O_COLS = (0, 512, 768, 1280, 1792, 2304, 2816, 3328)


def _rope(x, cos, sin):
    width = x.shape[1]
    lane = lax.broadcasted_iota(jnp.int32, x.shape, 1)
    partner = jnp.where((lane & 16) == 0, pltpu.roll(x, width - 16, 1), pltpu.roll(x, 16, 1))
    return x * cos + partner * sin


def _head_pair_norm(x, w, cos, sin, both):
    low = lax.broadcasted_iota(jnp.int32, x.shape, 1) < ATT_HD
    x2 = x * x
    ms = jnp.sum(jnp.where(low, x2, 0.0), axis=-1, keepdims=True) * (1.0 / ATT_HD)
    rs = lax.rsqrt(ms + EPS)
    if both:
        ms_hi = jnp.sum(jnp.where(low, 0.0, x2), axis=-1, keepdims=True) * (1.0 / ATT_HD)
        rs = jnp.where(low, rs, lax.rsqrt(ms_hi + EPS))
    else:
        rs = jnp.where(low, rs, 1.0)
    y = x * rs * w
    return y if cos is None else _rope(y, cos, sin)


def _odd_in_compute(x, refs, outs, rope):
    if rope:
        mod_ref, nw_ref, w_ref, wkv_ref, qn_ref, kn_ref, cq_ref, sq_ref, ck_ref, sk_ref = refs
        cq, sq, ck, sk = cq_ref[...], sq_ref[...], ck_ref[...], sk_ref[...]
    else:
        mod_ref, nw_ref, w_ref, wkv_ref, qn_ref, kn_ref = refs
        cq = sq = ck = sk = None
    q_ref, kv_ref, sa_ref, p_ref, bgz_ref = outs
    h = _modulated(x, mod_ref, nw_ref)
    c = O_COLS
    q = _dot(h, w_ref[:, c[0]:c[1]])
    for t in range(BRANCH // 128):
        lanes = slice(t * 128, (t + 1) * 128)
        qt = _head_pair_norm(q[:, lanes], qn_ref[:, lanes], cq, sq, True)
        q_ref[0, :, lanes] = (qt * Q_SCALE).astype(BF16)
    kv = _dot(h, wkv_ref[...])
    for t in range(ATT_KV_HEADS):
        lanes = slice(t * 128, (t + 1) * 128)
        kv_ref[0, :, lanes] = _head_pair_norm(kv[:, lanes], kn_ref[:, lanes], ck, sk, False)
    sa_ref[0] = _silu(_dot(h, w_ref[:, c[2]:c[3]])).astype(BF16)
    xc = _dot(h, w_ref[:, c[3]:c[4]])
    bg = _dot(h, w_ref[:, c[4]:c[5]])
    cg = _dot(h, w_ref[:, c[5]:c[6]])
    zc = _dot(h, w_ref[:, c[6]:c[7]])
    p_ref[0] = (cg * xc).astype(BF16)
    bgz_ref[0] = (bg * _silu(zc)).astype(BF16)


def _tile_spec(tm, n):
    return pl.BlockSpec((1, tm, n), lambda b, i: (b, i, 0))


def _whole_spec(a):
    return pl.BlockSpec(a.shape, lambda b, i: (0,) * a.ndim)


def _mod_spec():
    return pl.BlockSpec((1, 3, D_MODEL), lambda b, i: (b, 0, 0))


def _even_in_part(tm, mod, nw, w_in, gla_w):
    args = [mod, nw, w_in, *gla_w]
    specs = [_mod_spec()] + [_whole_spec(a) for a in args[1:]]
    outs = [(BRANCH, dt) for dt in (BF16, BF16, F32, BF16, F32, BF16)]
    return _even_in_compute, args, specs, outs


def _odd_in_part(tm, mod, nw, w_in, qnw, knw, tables):
    args = [mod, nw, *w_in, qnw, knw, *tables]
    specs = ([_mod_spec()] + [_whole_spec(a) for a in args[1:6]]
             + [pl.BlockSpec((tm, a.shape[1]), lambda b, i: (i, 0)) for a in tables])
    outs = [(BRANCH, BF16), (2 * ATT_KV_HEADS * ATT_HD, F32), (BRANCH, BF16), (BRANCH, BF16), (BRANCH, BF16)]
    return functools.partial(_odd_in_compute, rope=bool(tables)), args, specs, outs


def _stage_kernel(*refs, n_out_args, n_in_args, out_fn, in_fn):
    x_ref = refs[0]
    out_args = refs[1:1 + n_out_args]
    in_args = refs[1 + n_out_args:1 + n_out_args + n_in_args]
    outs = refs[1 + n_out_args + n_in_args:]
    x = x_ref[0]
    if out_fn is not None:
        x = out_fn(x, out_args)
        outs[0][0] = x
        outs = outs[1:]
    if in_fn is not None:
        in_fn(x, in_args, outs)


def _stage_call(name, x, out_part, in_part):
    bm, lm, _ = x.shape
    tm = min(ROW_TILE, lm)
    args, specs, out_defs = [x], [_tile_spec(tm, D_MODEL)], []
    out_fn = in_fn = None
    n_out_args = n_in_args = 0
    if out_part is not None:
        out_fn, a, s = out_part
        args, specs, n_out_args = args + a, specs + s, len(a)
        out_defs.append((D_MODEL, F32))
    if in_part is not None:
        in_fn, a, s, o = in_part
        args, specs, n_in_args = args + a, specs + s, len(a)
        out_defs += o
    res = pl.pallas_call(
        functools.partial(_stage_kernel, n_out_args=n_out_args, n_in_args=n_in_args, out_fn=out_fn, in_fn=in_fn),
        grid=(bm, lm // tm),
        in_specs=specs,
        out_specs=[_tile_spec(tm, n) for n, _ in out_defs],
        out_shape=[jax.ShapeDtypeStruct((bm, lm, n), dt) for n, dt in out_defs],
        compiler_params=_cparams("parallel", "parallel"),
        name=name,
    )(*args)
    return res


def _loop(n, body, init, unroll):
    if n <= unroll:
        for i in range(n):
            init = body(i, init)
        return init
    return lax.fori_loop(0, n, body, init, unroll=unroll)


def _gla_kernel(*refs, seq, has_state, heads):
    if has_state:
        (qk_ref, v_ref, bc_ref, sg_ref, on_ref, _, s0_ref, y_ref, sn_ref,
         ut_scr, sf_scr, sb_scr, qd_scr, o_scr) = refs
    else:
        (qk_ref, v_ref, bc_ref, sg_ref, on_ref, _, y_ref, sn_ref,
         ut_scr, sf_scr, sb_scr, qd_scr, o_scr) = refs
        s0_ref = None
    C = GLA_CHUNK
    W = 2 * GLA_DK
    R = GLA_BLOCK
    CB = R // C
    nc = seq // C
    nb = seq // R
    ri = lax.broadcasted_iota(jnp.int32, (R, R), 0)
    ci = lax.broadcasted_iota(jnp.int32, (R, R), 1)
    same = (ri // C) == (ci // C)
    tril = same & (ri >= ci)
    triu = same & (ri <= ci)
    fwd = lax.broadcasted_iota(jnp.int32, (R, W), 1) < GLA_DK
    fwd_row = lax.broadcasted_iota(jnp.int32, (1, W), 1) < GLA_DK
    fwd_st = lax.broadcasted_iota(jnp.int32, (GLA_DV, W), 1) < GLA_DK
    row_chunk = lax.broadcasted_iota(jnp.int32, (R, W), 0) // C
    scale = GLA_DK ** -0.5

    def rows(c):
        return pl.ds(pl.multiple_of(c * C, C), C)

    def block_rows(j):
        return pl.ds(pl.multiple_of(j * R, R), R)

    def state_rows(c):
        return pl.ds(pl.multiple_of(c * GLA_DV, GLA_DV), GLA_DV)

    def intra(h, j, carry):
        r = block_rows(j)
        hl = slice(h * W, (h + 1) * W)
        qk = qk_ref[0, r, hl].astype(F32)
        bc = bc_ref[0, r, hl]
        v = v_ref[0, r, hl].astype(F32)
        sw = pltpu.roll(qk, GLA_DK, 1)
        q2 = jnp.where(fwd, qk, sw) * scale
        k2 = jnp.where(fwd, sw, qk)
        b_mid = bc[C // 2:C // 2 + 1, :]
        b_end = jnp.where(fwd_row, bc[C - 1:C, :], bc[0:1, :])
        for c in range(1, CB):
            o0 = c * C
            b_mid = jnp.where(row_chunk == c, bc[o0 + C // 2:o0 + C // 2 + 1, :], b_mid)
            b_end = jnp.where(row_chunk == c, jnp.where(fwd_row, bc[o0 + C - 1:o0 + C, :], bc[o0:o0 + 1, :]), b_end)
        qd = (q2 * jnp.exp(bc)).astype(BF16)
        qh = q2 * jnp.exp(bc - b_mid)
        kh = (k2 * jnp.exp(b_mid - bc)).astype(BF16)
        att = (jnp.where(tril, _dot_nt(jnp.where(fwd, qh, 0.0).astype(BF16), kh), 0.0)
               + jnp.where(triu, _dot_nt(jnp.where(fwd, 0.0, qh).astype(BF16), kh), 0.0))
        o = _dot(att.astype(BF16), v.astype(BF16))
        kd = k2 * jnp.exp(b_end - bc)
        kd_wide = jnp.concatenate([jnp.where(row_chunk == c, kd, 0.0) for c in range(CB)], axis=1)
        ut = _dot(v.T.astype(BF16), kd_wide.astype(BF16))
        qd_scr[h, r, :] = qd
        o_scr[h, r, :] = o
        ut_scr[h, j] = ut
        return carry

    for h in range(heads):
        _loop(nb, functools.partial(intra, h), 0, 4)

    def scan(h, j, st):
        jb = nb - 1 - j
        hl = slice(h * W, (h + 1) * W)
        bcf = bc_ref[0, block_rows(j), hl]
        bcb = bc_ref[0, block_rows(jb), hl]
        utf = ut_scr[h, j]
        utb = ut_scr[h, jb]
        for c in range(CB):
            cb = CB - 1 - c
            stb = st.astype(BF16)
            sf_scr[h, j, :, c * W:(c + 1) * W] = stb
            sb_scr[h, jb, :, cb * W:(cb + 1) * W] = stb
            b_end = jnp.where(fwd_row, bcf[c * C + C - 1:c * C + C, :], bcb[cb * C:cb * C + 1, :])
            inc = jnp.where(fwd_st, utf[:, c * W:(c + 1) * W], utb[:, cb * W:(cb + 1) * W])
            st = st * jnp.exp(b_end) + inc
        return st

    for h in range(heads):
        if has_state:
            init = jnp.concatenate([s0_ref[0, 0, h].T, s0_ref[0, 1, h].T], axis=1)
        else:
            init = jnp.zeros((GLA_DV, W), F32)
        st = _loop(nb, functools.partial(scan, h), init, 2)
        sn_ref[0, 0, 0, h] = st[:, :GLA_DK].T
        sn_ref[0, 0, 1, h] = st[:, GLA_DK:].T

    fwd_wide = (lax.broadcasted_iota(jnp.int32, (GLA_DV, CB * W), 1) % W) < GLA_DK

    def finish(h, j, carry):
        r = block_rows(j)
        hl = slice(h * W, (h + 1) * W)
        qd = qd_scr[h, r, :]
        zero = jnp.zeros((), BF16)
        qd_wide = jnp.concatenate([jnp.where(row_chunk == c, qd, zero) for c in range(CB)], axis=1)
        s_in = jnp.where(fwd_wide, sf_scr[h, j], sb_scr[h, j])
        o = o_scr[h, r, :] + _dot_nt(qd_wide, s_in)
        ms = jnp.mean(o * o, axis=-1, keepdims=True)
        y_ref[0, r, hl] = (o * lax.rsqrt(ms + EPS) * on_ref[...] * sg_ref[0, r, hl].astype(F32)).astype(BF16)
        return carry

    for h in range(heads):
        _loop(nb, functools.partial(finish, h), 0, 4)


def _gla_call(qk, v, bc, sg, onorm, s0, layer, states):
    bsz, seq, _ = qk.shape
    has_state = s0 is not None
    nb = seq // GLA_BLOCK
    wide = (GLA_BLOCK // GLA_CHUNK) * 2 * GLA_DK
    hps = GLA_HEADS if nb == 1 else 1
    per_head = pl.BlockSpec((1, seq, 128 * hps), lambda b, h: (b, 0, h))
    in_specs = [per_head, per_head, per_head, per_head, pl.BlockSpec((1, GLA_DV), lambda b, h: (0, 0)),
                pl.BlockSpec(memory_space=pl.ANY)]
    args = [qk, v, bc, sg, onorm, states]
    if has_state:
        in_specs.append(pl.BlockSpec((1, 2, hps, GLA_DK, GLA_DV), lambda b, h: (b, 0, h, 0, 0)))
        args.append(s0)
    new_state_spec = pl.BlockSpec((1, 1, 2, hps, GLA_DK, GLA_DV), lambda b, h: (b, layer, 0, h, 0, 0))
    return pl.pallas_call(
        functools.partial(_gla_kernel, seq=seq, has_state=has_state, heads=hps),
        grid=(bsz, GLA_HEADS // hps),
        in_specs=in_specs,
        out_specs=[per_head, new_state_spec],
        out_shape=[jax.ShapeDtypeStruct((bsz, seq, BRANCH), BF16),
                   jax.ShapeDtypeStruct(states.shape, F32)],
        input_output_aliases={5: 1},
        scratch_shapes=[pltpu.VMEM((hps, nb, GLA_DV, wide), F32),
                        pltpu.VMEM((hps, nb, GLA_DV, wide), BF16),
                        pltpu.VMEM((hps, nb, GLA_DV, wide), BF16),
                        pltpu.VMEM((hps, seq, 2 * GLA_DK), BF16),
                        pltpu.VMEM((hps, seq, GLA_DV), F32)],
        compiler_params=_cparams("parallel", "parallel"),
        name="gla",
    )(*args)


def _s5_prep_kernel(lr_ref, li_ref, ld_ref, br_ref, bi_ref, cr_ref, ci_ref,
                    m_ref, win_ref, wout_ref, ar_ref, ai_ref, *, nlev):
    cs = S5_CHUNK
    P2 = 2 * S5_STATE
    lam_re, lam_im = lr_ref[0], li_ref[0]
    dt = jnp.exp(ld_ref[0])
    er, ei = lam_re * dt, lam_im * dt
    mag = jnp.exp(er)
    nr, ni = mag * jnp.cos(ei) - 1.0, mag * jnp.sin(ei)
    den = lam_re * lam_re + lam_im * lam_im
    cr = (nr * lam_re + ni * lam_im) / den
    ci = (ni * lam_re - nr * lam_im) / den
    b_re, b_im = br_ref[0], bi_ref[0]
    bb_re = cr * b_re - ci * b_im
    bb_im = cr * b_im + ci * b_re
    c_re, c_im = cr_ref[0], ci_ref[0]

    def power(k):
        m = jnp.exp(k * er)
        return m * jnp.cos(k * ei), m * jnp.sin(k * ei)

    def outer(a, b):
        return (a[:, None, :] * b[None, :, :]).reshape(S5_ROW, P2)

    def cmul(ar, ai, br, bi):
        return outer(ar, br) - outer(ai, bi), outer(ar, bi) + outer(ai, br)

    s_row = lax.broadcasted_iota(jnp.int32, (cs, P2), 0).astype(F32)
    fwd = lax.broadcasted_iota(jnp.int32, (cs, P2), 1) < S5_STATE
    half = float(cs // 2)
    k_in = jnp.where(fwd, half - s_row, s_row - half)
    l_re, l_im = cmul(*power(k_in), bb_re, bb_im)
    r_re, r_im = cmul(*power(-k_in), c_re, c_im)
    lcat = jnp.concatenate([l_re, l_im], axis=1)
    rcat = jnp.concatenate([r_re, -r_im], axis=1)
    fwd2 = (lax.broadcasted_iota(jnp.int32, (S5_ROW, S5_ROW), 1) % P2) < S5_STATE
    nt = (((1,), (1,)), ((), ()))
    m_f = lax.dot_general(jnp.where(fwd2, lcat, 0.0), rcat, nt, precision=HIGHEST, preferred_element_type=F32)
    m_b = lax.dot_general(jnp.where(fwd2, 0.0, lcat), rcat, nt, precision=HIGHEST, preferred_element_type=F32)
    s_in = lax.broadcasted_iota(jnp.int32, (S5_ROW, S5_ROW), 0) // S5_GROUP
    s_out = lax.broadcasted_iota(jnp.int32, (S5_ROW, S5_ROW), 1) // S5_GROUP
    m_ref[0] = (jnp.where(s_in <= s_out, m_f, 0.0) + jnp.where(s_in >= s_out, m_b, 0.0)).astype(BF16)
    w_re, w_im = cmul(*power(jnp.where(fwd, (cs - 1.0) - s_row, s_row)), bb_re, bb_im)
    win_ref[0] = jnp.concatenate([w_re, w_im], axis=1).astype(BF16)
    o_re, o_im = cmul(*power(jnp.where(fwd, s_row + 1.0, cs - s_row)), c_re, c_im)
    wout_ref[0] = jnp.concatenate([o_re, -o_im], axis=1).astype(BF16)
    lev = lax.shift_left(jnp.int32(cs), lax.broadcasted_iota(jnp.int32, (nlev, P2), 0)).astype(F32)
    ar_ref[0], ai_ref[0] = power(lev)


def _s5_operators(lam_re, lam_im, log_dt, b_re, b_im, c_re, c_im, nlev):
    P2 = 2 * S5_STATE
    lanes = lambda a: a.transpose(1, 0, 2).reshape(S5_GROUPS, 1, P2)
    both = lambda a: jnp.concatenate([a, a], axis=-1)
    args = (lanes(lam_re), lanes(lam_im),
            lanes(jnp.broadcast_to(log_dt[..., None], lam_re.shape)),
            both(b_re.transpose(0, 2, 1)), both(b_im.transpose(0, 2, 1)), both(c_re), both(c_im))
    g3 = lambda g: (g, 0, 0)
    vec = pl.BlockSpec((1, 1, P2), g3)
    mat = pl.BlockSpec((1, S5_GROUP, P2), g3)
    op = pl.BlockSpec((1, S5_ROW, S5_ROW), g3)
    lev = pl.BlockSpec((1, nlev, P2), g3)
    return pl.pallas_call(
        functools.partial(_s5_prep_kernel, nlev=nlev),
        grid=(S5_GROUPS,),
        in_specs=[vec, vec, vec, mat, mat, mat, mat],
        out_specs=[op, op, op, lev, lev],
        out_shape=[jax.ShapeDtypeStruct((S5_GROUPS, S5_ROW, S5_ROW), BF16)] * 3
        + [jax.ShapeDtypeStruct((S5_GROUPS, nlev, P2), F32)] * 2,
        compiler_params=_cparams("parallel"),
        name="s5_prep",
    )(*args)


def _gelu_tanh(x):
    return 0.5 * x * (1.0 + jnp.tanh(math.sqrt(2.0 / math.pi) * (x + 0.044715 * (x * x * x))))


def _slot_transpose(v, slot):
    v = list(v)
    d = len(v) // 2
    while d >= 1:
        take_own = (slot & d) == 0
        for i in range(len(v)):
            if i & d == 0:
                a, b = v[i], v[i + d]
                v[i] = jnp.where(take_own, a, pltpu.roll(b, d * S5_GROUP, 1))
                v[i + d] = jnp.where(take_own, pltpu.roll(a, 128 - d * S5_GROUP, 1), b)
        d //= 2
    return v


def _s5_kernel(*refs, nseq, nrow, nlev, has_state):
    if has_state:
        (u_ref, m_ref, win_ref, wout_ref, ar_ref, ai_ref, d_ref, h0_ref,
         z_ref, last_ref, t_scr, y_scr, tmp_r, tmp_i) = refs
    else:
        (u_ref, m_ref, win_ref, wout_ref, ar_ref, ai_ref, d_ref,
         z_ref, last_ref, t_scr, y_scr, tmp_r, tmp_i) = refs
    P2 = 2 * S5_STATE
    R = nseq * nrow
    GP = S5_SLAB_GROUPS
    TR = 16
    slot = lax.broadcasted_iota(jnp.int32, (TR, 128), 1) // S5_GROUP

    def token_rows(nt, s):
        return pl.ds(pl.multiple_of(nt * (TR * S5_CHUNK), TR * S5_CHUNK) + s, TR, stride=S5_CHUNK)

    def gather(nt, carry):
        for j in range(2):
            src = [u_ref[0, token_rows(nt, 8 * j + s8), :] for s8 in range(8)]
            by_group = _slot_transpose(src, slot)
            for gl in range(GP):
                t_scr[gl, pl.ds(pl.multiple_of(nt * TR, TR), TR), j * 128:(j + 1) * 128] = by_group[gl].astype(BF16)
        return carry

    lax.fori_loop(0, R // TR, gather, 0, unroll=2)

    n = lax.broadcasted_iota(jnp.int32, (R, P2), 0) % nrow
    row = lax.broadcasted_iota(jnp.int32, (R, P2), 0)
    fwd = lax.broadcasted_iota(jnp.int32, (R, P2), 1) < S5_STATE
    fwd_b = lax.broadcasted_iota(jnp.int32, (nseq, P2), 1) < S5_STATE

    def shift(a, k):
        dn = pltpu.roll(a, k, 0)
        up = pltpu.roll(a, R - k, 0)
        return jnp.where(fwd, jnp.where(n >= k, dn, 0.0), jnp.where(n < nrow - k, up, 0.0))

    def ends(t):
        if nseq == 1:
            return jnp.where(fwd_b, t[nrow - 1:nrow, :], t[0:1, :])
        return jnp.where(fwd_b, t[pl.ds(nrow - 1, nseq, stride=nrow), :], t[pl.ds(0, nseq, stride=nrow), :])

    def per_group(gl, carry):
        ub = t_scr[gl]
        x = _dot(ub, win_ref[gl])
        xr, xi = x[:, :P2], x[:, P2:]
        er, ei = shift(xr, 1), shift(xi, 1)
        if has_state:
            h0 = h0_ref[0, gl]
            for b in range(nseq):
                first = row == jnp.where(fwd, b * nrow, b * nrow + nrow - 1)
                er = jnp.where(first, h0[b:b + 1, :P2], er)
                ei = jnp.where(first, h0[b:b + 1, P2:], ei)
        for j in range(nlev):
            k = 2 ** j
            ar = ar_ref[gl, j:j + 1, :]
            ai = ai_ref[gl, j:j + 1, :]
            sr, si = shift(er, k), shift(ei, k)
            er, ei = er + ar * sr - ai * si, ei + ar * si + ai * sr
        ecat = jnp.concatenate([er, ei], axis=1).astype(BF16)
        y_scr[gl] = _dot(ub, m_ref[gl]) + _dot_nt(ecat, wout_ref[gl])
        a1r = ar_ref[gl, 0:1, :]
        a1i = ai_ref[gl, 0:1, :]
        tmp_r[...] = a1r * er - a1i * ei + xr
        tmp_i[...] = a1r * ei + a1i * er + xi
        last_ref[0, gl, :, :P2] = ends(tmp_r)
        last_ref[0, gl, :, P2:] = ends(tmp_i)
        return carry

    lax.fori_loop(0, GP, per_group, 0, unroll=4)

    def scatter(nt, carry):
        for j in range(2):
            ys = [y_scr[gl, pl.ds(pl.multiple_of(nt * TR, TR), TR), j * 128:(j + 1) * 128] for gl in range(GP)]
            by_token = _slot_transpose(ys, slot)
            for s8 in range(8):
                rows = token_rows(nt, 8 * j + s8)
                z_ref[0, rows, :] = _gelu_tanh(by_token[s8] + d_ref[...] * u_ref[0, rows, :])
        return carry

    lax.fori_loop(0, R // TR, scatter, 0, unroll=2)


def _s5_call(u, ops, d, h0, seq):
    m, win, wout, ar, ai = ops
    bm, lm, _ = u.shape
    nrow = seq // S5_CHUNK
    nlev = int(math.log2(nrow))
    total = bm * lm // seq
    nseq = max(1, min(total, S5_BLOCK_ROWS // nrow))
    nblk = total // nseq
    R = nseq * nrow
    tok = R * S5_CHUNK
    GP = S5_SLAB_GROUPS
    ns = BRANCH // 128
    has_state = h0 is not None
    slab = pl.BlockSpec((1, tok, 128), lambda t, i: (i, 0, t))
    op = pl.BlockSpec((GP, S5_ROW, S5_ROW), lambda t, i: (t, 0, 0))
    lev = pl.BlockSpec((GP, ar.shape[1], 2 * S5_STATE), lambda t, i: (t, 0, 0))
    state = pl.BlockSpec((1, GP, nseq, 4 * S5_STATE), lambda t, i: (i, t, 0, 0))
    in_specs = [slab, op, op, op, lev, lev, pl.BlockSpec((1, 128), lambda t, i: (0, t))]
    args = [u.reshape(nblk, tok, BRANCH), m, win, wout, ar, ai, d]
    if has_state:
        in_specs.append(state)
        args.append(h0.reshape(nblk, nseq, S5_GROUPS, 4 * S5_STATE).transpose(0, 2, 1, 3))
    z, last = pl.pallas_call(
        functools.partial(_s5_kernel, nseq=nseq, nrow=nrow, nlev=nlev, has_state=has_state),
        grid=(ns, nblk),
        in_specs=in_specs,
        out_specs=[slab, state],
        out_shape=[jax.ShapeDtypeStruct((nblk, tok, BRANCH), F32),
                   jax.ShapeDtypeStruct((nblk, S5_GROUPS, nseq, 4 * S5_STATE), F32)],
        scratch_shapes=[pltpu.VMEM((GP, R, S5_ROW), BF16), pltpu.VMEM((GP, R, S5_ROW), F32),
                        pltpu.VMEM((R, 2 * S5_STATE), F32), pltpu.VMEM((R, 2 * S5_STATE), F32)],
        compiler_params=_cparams("parallel", "parallel"),
        name="s5",
    )(*args)
    last = last.transpose(0, 2, 1, 3).reshape(total, S5_GROUPS, 4 * S5_STATE)
    return z.reshape(bm, lm, BRANCH), last


HEAD_ORDER = (0, 2, 1, 3)


def _stack_heads(q, heads):
    head = lax.broadcasted_iota(jnp.int32, q.shape, 1) // ATT_HD
    zero = jnp.zeros((), q.dtype)
    return jnp.concatenate([jnp.where(head == h, q, zero) for h in heads], axis=0)


def _key_value_operands(kv):
    low = lax.broadcasted_iota(jnp.int32, kv.shape, 1) < ATT_HD
    sw = pltpu.roll(kv, ATT_HD, 1)
    kk = jnp.where(low, kv, sw).astype(BF16)
    return (jnp.concatenate([kk, kk], axis=1),
            jnp.where(low, sw, 1.0).astype(BF16), jnp.where(low, 1.0, kv).astype(BF16))


LOG2E = 1.4426950408889634
Q_SCALE = (ATT_HD ** -0.5) * LOG2E


def _attend_pair(q, parity, key_sets, sink_ref):
    rows = q.shape[0]
    heads = (parity, parity + 2)
    qs = _stack_heads(q, heads)
    sink = jnp.concatenate([jnp.broadcast_to(sink_ref[h:h + 1, 0:1], (rows, 1)) for h in heads], axis=0) * LOG2E
    scores = []
    t = None
    for k4, _, _, bias in key_sets:
        s = _dot_nt(qs, k4)
        if bias is not None:
            s = s + jnp.concatenate([bias, bias], axis=0)
        scores.append(s)
        for c in range(0, s.shape[1], 128):
            t = s[:, c:c + 128] if t is None else jnp.maximum(t, s[:, c:c + 128])
    m = jnp.maximum(jnp.max(t, axis=-1, keepdims=True), sink)
    lane = lax.broadcasted_iota(jnp.int32, (2 * rows, 2 * ATT_HD), 1)
    den_lanes = (lane >= ATT_HD) if parity == 0 else (lane < ATT_HD)
    o = jnp.where(den_lanes, jnp.exp2(sink - m), 0.0)
    for s, (_, va, vb, _) in zip(scores, key_sets):
        o = o + _dot(jnp.exp2(s - m).astype(BF16), va if parity == 0 else vb)
    return o / pltpu.roll(o, ATT_HD, 1)


def _attend_stacked(q, key_sets, sink_ref):
    rows = q.shape[0]
    qs = _stack_heads(q, HEAD_ORDER)
    sink = jnp.concatenate([jnp.broadcast_to(sink_ref[h:h + 1, 0:1], (rows, 1)) for h in HEAD_ORDER],
                           axis=0) * LOG2E
    scores = []
    t = None
    for k4, _, _, bias in key_sets:
        s = _dot_nt(qs, k4)
        if bias is not None:
            s = s + jnp.concatenate([bias] * ATT_GROUPS, axis=0)
        scores.append(s)
        for c in range(0, s.shape[1], 128):
            t = s[:, c:c + 128] if t is None else jnp.maximum(t, s[:, c:c + 128])
    m = jnp.maximum(jnp.max(t, axis=-1, keepdims=True), sink)
    e_sink = jnp.exp2(sink - m)
    low = lax.broadcasted_iota(jnp.int32, (2 * rows, 2 * ATT_HD), 1) < ATT_HD
    o_even = jnp.where(low, 0.0, e_sink[:2 * rows])
    o_odd = jnp.where(low, e_sink[2 * rows:], 0.0)
    for s, (_, va, vb, _) in zip(scores, key_sets):
        p = jnp.exp2(s - m).astype(BF16)
        o_even = o_even + _dot(p[:2 * rows], va)
        o_odd = o_odd + _dot(p[2 * rows:], vb)
    return o_even / pltpu.roll(o_even, ATT_HD, 1), o_odd / pltpu.roll(o_odd, ATT_HD, 1)


def _attend(q, key_sets, sink_ref, split):
    rows = q.shape[0]
    if split:
        o_even = _attend_pair(q, 0, key_sets, sink_ref)
        o_odd = _attend_pair(q, 1, key_sets, sink_ref)
    else:
        o_even, o_odd = _attend_stacked(q, key_sets, sink_ref)
    low = lax.broadcasted_iota(jnp.int32, o_even.shape, 1) < ATT_HD
    pair = jnp.where(low, o_even, o_odd)
    return jnp.concatenate([pair[:rows], pair[rows:]], axis=1)


def _att_ctx_kernel(q_ref, kv_ref, sa_ref, sink_ref, kprev_ref, vprev_ref, y_ref, kc_ref, vc_ref):
    del kprev_ref, vprev_ref
    W = ATT_GROUPS * ATT_HD
    results = []
    for g in range(ATT_KV_HEADS):
        kv = kv_ref[0, :, g * 2 * ATT_HD:(g + 1) * 2 * ATT_HD]
        heads = slice(g * W, (g + 1) * W)
        k4, v_ones, _ = _key_value_operands(kv)
        key_sets = [(k4, _value_rows(v_ones.astype(F32)), None)]
        out = _attend_keys_on_rows(q_ref[0, :, heads], key_sets, sink_ref.at[g])
        results.append((kv, (out * sa_ref[0, :, heads].astype(F32)).astype(BF16)))
    for g, (kv, y) in enumerate(results):
        kc_ref[0, 0, g] = kv[:, :ATT_HD]
        vc_ref[0, 0, g] = kv[:, ATT_HD:]
        y_ref[0, :, g * W:(g + 1) * W] = y


def _att_ctx_call(q, kv, sa, sink, layer, caches):
    bsz, seq, _ = q.shape
    row = lambda b: (b, 0, 0)
    cache_spec = pl.BlockSpec((1, 1, ATT_KV_HEADS, seq, ATT_HD), lambda b: (b, layer, 0, 0, 0))
    cache_shape = jax.ShapeDtypeStruct(caches[0].shape, F32)
    return pl.pallas_call(
        _att_ctx_kernel,
        grid=(bsz,),
        in_specs=[pl.BlockSpec((1, seq, BRANCH), row), pl.BlockSpec((1, seq, kv.shape[-1]), row),
                  pl.BlockSpec((1, seq, BRANCH), row), pl.BlockSpec(sink.shape, lambda b: (0, 0, 0)),
                  pl.BlockSpec(memory_space=pl.ANY), pl.BlockSpec(memory_space=pl.ANY)],
        out_specs=[pl.BlockSpec((1, seq, BRANCH), row), cache_spec, cache_spec],
        out_shape=[jax.ShapeDtypeStruct((bsz, seq, BRANCH), BF16), cache_shape, cache_shape],
        input_output_aliases={4: 1, 5: 2},
        compiler_params=_cparams("parallel"),
        name="att_ctx",
    )(q, kv, sa, sink, *caches)


def _attend_keys_on_rows(q, key_sets, sink_ref):
    rows = q.shape[0]
    qs = _stack_heads(q, range(ATT_GROUPS))
    scores, top = [], None
    for k4, _, bias_t in key_sets:
        s = _dot_nt(k4, qs)
        if bias_t is not None:
            s = s + jnp.concatenate([bias_t] * ATT_GROUPS, axis=1)
        scores.append(s)
        part = jnp.max(s, axis=0, keepdims=True)
        top = part if top is None else jnp.maximum(top, part)
    sink = jnp.concatenate([jnp.broadcast_to(sink_ref[h:h + 1, 0:1], (1, rows)) for h in range(ATT_GROUPS)],
                           axis=1) * LOG2E
    m = jnp.maximum(top, sink)
    o = None
    for s, (_, vat, _) in zip(scores, key_sets):
        p = jnp.exp2(s - m).astype(BF16)
        for c, vat_c in enumerate(vat):
            pv = _dot(vat_c, p[c * 128:(c + 1) * 128, :])
            o = pv if o is None else o + pv
    on = o[:ATT_HD, :] / (o[ATT_HD:, :] + jnp.exp2(sink - m))
    pairs = [jnp.concatenate([on[:, (2 * t) * rows:(2 * t + 1) * rows],
                              on[:, (2 * t + 1) * rows:(2 * t + 2) * rows]], axis=0).T
             for t in range(ATT_GROUPS // 2)]
    return jnp.concatenate(pairs, axis=1)


def _value_rows(v_ones):
    return [v_ones[c:c + 128, :].T.astype(BF16) for c in range(0, v_ones.shape[0], 128)]


def _att_lat_kernel(q_ref, kv_ref, sa_ref, sink_ref, kc_ref, vc_ref, y_ref, k4_scr, vat_scr, *, seq):
    nb = seq // QBLK
    KW = 3 * QBLK
    low = lax.broadcasted_iota(jnp.int32, (QBLK, 2 * ATT_HD), 1) < ATT_HD

    def k_body(i, carry):
        r = pl.ds(pl.multiple_of(i * QBLK, QBLK), QBLK)
        kv = kv_ref[0, r, :]
        sw = pltpu.roll(kv, ATT_HD, 1)
        kk = jnp.where(low, kv, sw).astype(BF16)
        k4_scr[r, :] = jnp.concatenate([kk, kk], axis=1)
        vat_scr[i] = jnp.where(low, sw, 1.0).T.astype(BF16)
        return carry

    lax.fori_loop(0, nb, k_body, 0, unroll=4)
    kc = kc_ref[0, 0, 0]
    vc = vc_ref[0, 0, 0]
    kc4 = jnp.concatenate([kc] * ATT_GROUPS, axis=1).astype(BF16)
    vat_ctx = _value_rows(jnp.concatenate([vc, jnp.ones_like(vc)], axis=1))
    kj = lax.broadcasted_iota(jnp.int32, (KW, QBLK), 0)
    qi = lax.broadcasted_iota(jnp.int32, (KW, QBLK), 1)

    def q_body(i, carry):
        r = pl.ds(pl.multiple_of(i * QBLK, QBLK), QBLK)
        ws = pl.multiple_of(jnp.clip(i * QBLK - QBLK, 0, seq - KW), QBLK)
        jb = ws // QBLK
        bias_t = jnp.where(jnp.abs((i * QBLK + qi) - (ws + kj)) <= WINDOW, 0.0, NEG)
        key_sets = [(k4_scr[pl.ds(ws, KW), :], [vat_scr[jb + c] for c in range(KW // QBLK)], bias_t),
                    (kc4, vat_ctx, None)]
        out = _attend_keys_on_rows(q_ref[0, r, :], key_sets, sink_ref.at[0])
        y_ref[0, r, :] = (out * sa_ref[0, r, :].astype(F32)).astype(BF16)
        return carry

    lax.fori_loop(0, nb, q_body, 0, unroll=4)


def _rope_tables(seq, heads, pad_heads):
    pos = jnp.arange(seq)
    row = (pos // GRID_W).astype(F32)[:, None]
    col = (pos % GRID_W).astype(F32)[:, None]
    nf = ATT_HD // 4
    freq = ROPE_BASE ** (-jnp.arange(nf, dtype=F32) / nf)
    ar, ac = row * freq, col * freq
    cos = jnp.concatenate([jnp.cos(ar), jnp.cos(ar), jnp.cos(ac), jnp.cos(ac)], axis=1)
    sin = jnp.concatenate([-jnp.sin(ar), jnp.sin(ar), -jnp.sin(ac), jnp.sin(ac)], axis=1)
    cos = jnp.concatenate([jnp.tile(cos, (1, heads)), jnp.ones((seq, pad_heads * ATT_HD), F32)], axis=1)
    sin = jnp.concatenate([jnp.tile(sin, (1, heads)), jnp.zeros((seq, pad_heads * ATT_HD), F32)], axis=1)
    return cos, sin


def _att_lat_call(q, kv, sa, sink, cache_k, cache_v, e):
    bsz, seq, _ = q.shape
    past = cache_k.shape[3]
    W = ATT_GROUPS * ATT_HD
    per_kv = lambda b, k: (b, 0, k)
    cache_spec = pl.BlockSpec((1, 1, 1, past, ATT_HD), lambda b, k: (b, e, k, 0, 0))
    return pl.pallas_call(
        functools.partial(_att_lat_kernel, seq=seq),
        grid=(bsz, ATT_KV_HEADS),
        in_specs=[
            pl.BlockSpec((1, seq, W), per_kv),
            pl.BlockSpec((1, seq, 2 * ATT_HD), per_kv),
            pl.BlockSpec((1, seq, W), per_kv),
            pl.BlockSpec((1, 8, 128), lambda b, k: (k, 0, 0)),
            cache_spec,
            cache_spec,
        ],
        out_specs=pl.BlockSpec((1, seq, W), per_kv),
        out_shape=jax.ShapeDtypeStruct((bsz, seq, BRANCH), BF16),
        scratch_shapes=[pltpu.VMEM((seq, W), BF16), pltpu.VMEM((seq // QBLK, 2 * ATT_HD, QBLK), BF16)],
        compiler_params=_cparams("parallel", "parallel"),
        name="att_lat",
    )(q, kv, sa, sink, cache_k, cache_v)


def _even_out_compute(x, refs):
    mod_ref, yg_ref, z_ref, ss_ref, wg_ref, bg_ref, wo_ref = refs
    z = z_ref[0]
    glu = z * jax.nn.sigmoid(_dot(z.astype(BF16), wg_ref[...]) + bg_ref[...])
    ys = (glu * ss_ref[0].astype(F32)).astype(BF16)
    out = _dot(yg_ref[0], wo_ref[:BRANCH, :]) + _dot(ys, wo_ref[BRANCH:, :])
    return x + mod_ref[0, 2:3, :] * out


def _even_out_part(tm, mod, yg, z, ss, wglu, bglu, wout):
    args = [mod, yg, z, ss, wglu, bglu, wout]
    specs = [_mod_spec(), _tile_spec(tm, BRANCH), _tile_spec(tm, BRANCH), _tile_spec(tm, BRANCH),
             _whole_spec(wglu), _whole_spec(bglu), _whole_spec(wout)]
    return _even_out_compute, args, specs


def _odd_out_compute(x, refs, seq, tm):
    mod_ref, ya_ref, p_ref, pprev_ref, pnext_ref, bgz_ref, cw_ref, cb_ref, wo_ref = refs
    p = p_ref[0].astype(F32)
    pos = (pl.program_id(1) * tm + lax.broadcasted_iota(jnp.int32, (tm, BRANCH), 0)) % seq
    rowi = lax.broadcasted_iota(jnp.int32, (tm, BRANCH), 0)
    halo = CONV_HALO_ROWS
    p_before = pprev_ref[0].astype(F32)[halo - 1:halo, :]
    p_after = pnext_ref[0].astype(F32)[0:1, :]
    prev = jnp.where(rowi == 0, p_before, pltpu.roll(p, 1, 0))
    nxt = jnp.where(rowi == tm - 1, p_after, pltpu.roll(p, tm - 1, 0))
    prev = jnp.where(pos == 0, 0.0, prev)
    nxt = jnp.where(pos == seq - 1, 0.0, nxt)
    conv = prev * cw_ref[0:1, :] + p * cw_ref[1:2, :] + nxt * cw_ref[2:3, :] + cb_ref[...]
    yc = (bgz_ref[0].astype(F32) * conv).astype(BF16)
    out = _dot(ya_ref[0], wo_ref[:BRANCH, :]) + _dot(yc, wo_ref[BRANCH:, :])
    return x + mod_ref[0, 2:3, :] * out


def _odd_out_part(tm, lm, seq, mod, ya, p, bgz, convw, convb, wout):
    hr = CONV_HALO_ROWS
    nth = lm // hr
    prev_spec = pl.BlockSpec((1, hr, BRANCH), lambda b, i: (b, jnp.maximum(i * (tm // hr) - 1, 0), 0))
    next_spec = pl.BlockSpec((1, hr, BRANCH), lambda b, i: (b, jnp.minimum((i + 1) * (tm // hr), nth - 1), 0))
    args = [mod, ya, p, p, p, bgz, convw, convb, wout]
    specs = [_mod_spec(), _tile_spec(tm, BRANCH), _tile_spec(tm, BRANCH), prev_spec, next_spec,
             _tile_spec(tm, BRANCH), _whole_spec(convw), _whole_spec(convb), _whole_spec(wout)]
    return functools.partial(_odd_out_compute, seq=seq, tm=tm), args, specs


def _even_in_weight(w):
    dk = GLA_HEADS * GLA_DK
    q, k = w[:, :dk], w[:, dk:2 * dk]
    v = w[:, 2 * dk:2 * dk + BRANCH]
    o = 2 * dk + BRANCH
    lr = w[:, o:o + 2 * GLA_RANK]
    rest = w[:, o + 2 * GLA_RANK:]
    qk = jnp.concatenate([jnp.concatenate([q[:, h * GLA_DK:(h + 1) * GLA_DK], k[:, h * GLA_DK:(h + 1) * GLA_DK]],
                                          axis=1) for h in range(GLA_HEADS)], axis=1)
    lr = jnp.pad(lr, ((0, 0), (0, 128 - 2 * GLA_RANK)))
    return jnp.concatenate([qk, v, lr, rest], axis=1).astype(BF16)


def _odd_in_weight(w):
    k = w[:, BRANCH:BRANCH + ATT_KV_HEADS * ATT_HD]
    v = w[:, BRANCH + ATT_KV_HEADS * ATT_HD:BRANCH + 2 * ATT_KV_HEADS * ATT_HD]
    kv = jnp.concatenate([jnp.concatenate([k[:, h * ATT_HD:(h + 1) * ATT_HD], v[:, h * ATT_HD:(h + 1) * ATT_HD]],
                                          axis=1) for h in range(ATT_KV_HEADS)], axis=1)
    return w.astype(BF16), kv.astype(BF16)


def _gate_weights(w2, b2):
    zf = jnp.zeros((GLA_RANK, GLA_DK), F32)
    cols, bias = [], []
    for h in range(GLA_HEADS):
        sl = slice(h * GLA_DK, (h + 1) * GLA_DK)
        cols.append(jnp.concatenate([jnp.concatenate([w2[0][:, sl], zf], axis=1),
                                     jnp.concatenate([zf, w2[1][:, sl]], axis=1)], axis=0))
        bias += [b2[0][sl], b2[1][sl]]
    w = jnp.pad(jnp.concatenate(cols, axis=1), ((0, 128 - 2 * GLA_RANK), (0, 0)))
    w_hi = w.astype(BF16)
    w_lo = (w - w_hi.astype(F32)).astype(BF16)
    return w_hi, w_lo, jnp.concatenate(bias)[None, :]


def _run_stream(x, mods, seq, params, gla_s0, s5_h0, cache_k, cache_v):
    depth = len(params)
    bm, lm, _ = x.shape
    tm = min(ROW_TILE, lm)
    nseq = bm * lm // seq
    per_seq = lambda a: a.reshape(nseq, seq, a.shape[-1])
    flat = lambda a: a.reshape(bm, lm, a.shape[-1])
    latent = cache_k is not None
    tables = _rope_tables(seq, 2, 0) + _rope_tables(seq, 1, 1) if latent else ()

    def in_part(l):
        p = params[l]
        if l % 2 == 0:
            return _even_in_part(tm, mods[l], p["nw"], p["w_in"], p["gla_w"])
        return _odd_in_part(tm, mods[l], p["nw"], p["w_in"], p["qnw"], p["knw"], tables)

    n_even, n_odd = (depth + 1) // 2, depth // 2
    gla_states = jnp.zeros((nseq, n_even, 2, GLA_HEADS, GLA_DK, GLA_DV), F32)
    cache_shape = (nseq, n_odd, ATT_KV_HEADS, seq, ATT_HD)
    caches = None if latent else (jnp.zeros(cache_shape, F32), jnp.zeros(cache_shape, F32))
    collected = {"s5": []}
    out_part = None
    for l in range(depth + 1):
        res = _stage_call("stage%d" % l, x, out_part, in_part(l) if l < depth else None)
        if out_part is not None:
            x, res = res[0], res[1:]
        if l == depth:
            break
        p, e = params[l], l // 2
        if l % 2 == 0:
            qk, v, bc, sg, u, ss = res
            yg, gla_states = _gla_call(per_seq(qk), per_seq(v), per_seq(bc), per_seq(sg), p["onorm"],
                                       None if gla_s0 is None else gla_s0[:, e], e, gla_states)
            z, last = _s5_call(u, p["s5_ops"], p["d"], None if s5_h0 is None else s5_h0[e], seq)
            collected["s5"].append(last)
            out_part = _even_out_part(tm, mods[l], flat(yg), z, ss, p["wglu"], p["bglu"], p["wout"])
        else:
            q, kv, sa, pc, bgz = res
            if latent:
                ya = _att_lat_call(per_seq(q), per_seq(kv), per_seq(sa), p["sink"], cache_k, cache_v, e)
            else:
                ya, *caches = _att_ctx_call(per_seq(q), per_seq(kv), per_seq(sa), p["sink"], e, caches)
            out_part = _odd_out_part(tm, lm, seq, mods[l], flat(ya), pc, bgz, p["convw"], p["convb"], p["wout"])
    collected["gla"] = gla_states
    collected["caches"] = caches
    return x, collected


def kernel(x_prompt, x_sample, c, state_gla, state_s5_re, state_s5_im, cache_k, cache_v, c_ctx, norm_w, w_ada, b_ada, w_in_e, w_out_e, gla_w2, gla_b2, gla_onorm, s5_lam_re, s5_lam_im, s5_log_dt, s5_b_re, s5_b_im, s5_c_re, s5_c_im, s5_d, s5_w_glu, s5_b_glu, w_in_o, w_out_o, q_norm_w, k_norm_w, sink, conv_w, conv_b):
    depth = norm_w.shape[0]
    bp, lp, _ = x_prompt.shape
    bs, ls, _ = x_sample.shape

    rows = 8 * ((1 + bs + 7) // 8)
    cs = jnp.zeros((rows, D_MODEL), F32).at[0].set(c_ctx).at[1:1 + bs].set(c)
    mods = _ada_call(cs, w_ada, b_ada)

    params, s5_h0 = [], []
    for l in range(depth):
        e = l // 2
        p = {"nw": norm_w[l][None, :]}
        if l % 2 == 0:
            p.update(
                w_in=_even_in_weight(w_in_e[e]), wout=w_out_e[e].astype(BF16),
                gla_w=_gate_weights(gla_w2[e], gla_b2[e]), onorm=gla_onorm[e][None, :],
                d=s5_d[e][None, :], wglu=s5_w_glu[e].astype(BF16), bglu=s5_b_glu[e][None, :],
                s5_ops=_s5_operators(s5_lam_re[e], s5_lam_im[e], s5_log_dt[e], s5_b_re[e], s5_b_im[e],
                                     s5_c_re[e], s5_c_im[e], nlev=int(math.log2(max(lp, ls) // S5_CHUNK))))
            h0 = jnp.concatenate([state_s5_re[:, e], state_s5_im[:, e]], axis=1)
            s5_h0.append(h0.transpose(0, 2, 1, 3).reshape(bs, S5_GROUPS, 4 * S5_STATE))
        else:
            sk = jnp.broadcast_to(sink[e].reshape(ATT_KV_HEADS, ATT_GROUPS, 1), (ATT_KV_HEADS, ATT_GROUPS, 128))
            p.update(
                w_in=_odd_in_weight(w_in_o[e]), wout=w_out_o[e].astype(BF16),
                qnw=jnp.tile(q_norm_w[e], ATT_HEADS)[None, :],
                knw=jnp.tile(jnp.concatenate([k_norm_w[e], jnp.ones((ATT_HD,), F32)]), ATT_KV_HEADS)[None, :],
                sink=jnp.concatenate([sk, jnp.zeros((ATT_KV_HEADS, 8 - ATT_GROUPS, 128), F32)], axis=1),
                convw=conv_w[e], convb=conv_b[e][None, :])
        params.append(p)

    mods_p = [mods[l, 0:1].reshape(1, 3, D_MODEL) for l in range(depth)]
    mods_s = [mods[l, 1:1 + bs].reshape(bs, 3, D_MODEL) for l in range(depth)]
    xp, got = _run_stream(x_prompt.reshape(1, bp * lp, D_MODEL), mods_p, lp, params, None, None, None, None)
    xs, _ = _run_stream(x_sample, mods_s, ls, params, state_gla, s5_h0, cache_k, cache_v)
    last = [t.reshape(bp, S5_GROUPS, 2, 2, S5_STATE).transpose(2, 0, 3, 1, 4) for t in got["s5"]]
    return (xp.reshape(bp, lp, D_MODEL), xs,
            got["gla"],
            jnp.stack([t[0] for t in last], axis=1), jnp.stack([t[1] for t in last], axis=1),
            got["caches"][0], got["caches"][1])
```

```python
import functools
import math

import jax
import jax.numpy as jnp
from jax import lax
from jax.experimental import pallas as pl
from jax.experimental.pallas import tpu as pltpu

F32 = jnp.float32
BF16 = jnp.bfloat16
HIGHEST = lax.Precision.HIGHEST

D_MODEL = 1024
BRANCH = D_MODEL // 2
GLA_HEADS = 4
GLA_DK = 64
GLA_DV = 128
GLA_RANK = 16
GLA_GATE_NORM = 16.0
GLA_CHUNK = 64
GLA_BLOCK = 256
S5_GROUP = 16
S5_GROUPS = BRANCH // S5_GROUP
S5_STATE = 64
S5_CHUNK = 16
S5_ROW = S5_CHUNK * S5_GROUP
S5_SLAB_GROUPS = 128 // S5_GROUP
S5_BLOCK_ROWS = 256
ATT_HEADS = 8
ATT_KV_HEADS = 2
ATT_GROUPS = ATT_HEADS // ATT_KV_HEADS
ATT_HD = 64
WINDOW = 128
QBLK = 128
GRID_W = 64
ROPE_BASE = 10000.0
CONV_W = 3
CONV_HALO_ROWS = 16
EPS = 1e-6
NEG = -1e30

ROW_TILE = 512
VMEM_LIMIT = 48 * 1024 * 1024


def _cparams(*sem):
    return pltpu.CompilerParams(dimension_semantics=sem, vmem_limit_bytes=VMEM_LIMIT)


def _silu(x):
    return x * jax.nn.sigmoid(x)


def _dot(a, b):
    return jnp.dot(a, b, preferred_element_type=F32)


def _dot_nt(a, b):
    return lax.dot_general(a, b, (((1,), (1,)), ((), ())), preferred_element_type=F32)


def _ada_kernel(c_ref, w_ref, b_ref, o_ref):
    c = c_ref[...]
    o_ref[0] = jnp.dot(_silu(c), w_ref[0], precision=HIGHEST, preferred_element_type=F32) + b_ref[0]


def _ada_call(cs, w_ada, b_ada):
    depth = w_ada.shape[0]
    rows = cs.shape[0]
    nt = 3
    return pl.pallas_call(
        _ada_kernel,
        grid=(depth, nt),
        in_specs=[
            pl.BlockSpec((rows, D_MODEL), lambda l, j: (0, 0)),
            pl.BlockSpec((1, D_MODEL, D_MODEL), lambda l, j: (l, 0, j)),
            pl.BlockSpec((1, 1, D_MODEL), lambda l, j: (l, 0, j)),
        ],
        out_specs=pl.BlockSpec((1, rows, D_MODEL), lambda l, j: (l, 0, j)),
        out_shape=jax.ShapeDtypeStruct((depth, rows, 3 * D_MODEL), F32),
        compiler_params=_cparams("arbitrary", "arbitrary"),
        name="adaln",
    )(cs, w_ada, b_ada.reshape(depth, 1, 3 * D_MODEL))


def _modulated(x, mod_ref, nw_ref):
    ms = jnp.mean(x * x, axis=-1, keepdims=True)
    y = x * lax.rsqrt(ms + EPS) * nw_ref[...]
    shift = mod_ref[0, 0:1, :]
    scale = mod_ref[0, 1:2, :]
    return (y * (1.0 + scale) + shift).astype(BF16)


E_COLS = (0, 512, 1024, 1152, 1664, 2176, 2688)


def _log_sigmoid(x):
    return jnp.minimum(x, 0.0) - jnp.log(1.0 + jnp.exp(-jnp.abs(x)))


def _chunk_cumsum(g):
    rows, width = g.shape
    pos = lax.broadcasted_iota(jnp.int32, g.shape, 0) % GLA_CHUNK
    fwd = (lax.broadcasted_iota(jnp.int32, g.shape, 1) % (2 * GLA_DK)) < GLA_DK
    c = g
    k = 1
    while k < GLA_CHUNK:
        c = c + jnp.where(pos >= k, pltpu.roll(c, k, 0), 0.0)
        k *= 2
    total = jnp.concatenate([jnp.broadcast_to(c[r + GLA_CHUNK - 1:r + GLA_CHUNK, :], (GLA_CHUNK, width))
                             for r in range(0, rows, GLA_CHUNK)], axis=0)
    return jnp.where(fwd, c, total - c + g)


def _even_in_compute(x, refs, outs):
    mod_ref, nw_ref, w_ref, w2h_ref, w2l_ref, b2_ref = refs
    qk_ref, v_ref, bc_ref, sg_ref, u_ref, ss_ref = outs
    h = _modulated(x, mod_ref, nw_ref)
    c = E_COLS
    lr = _dot(h, w_ref[:, c[2]:c[3]])
    lr_hi = lr.astype(BF16)
    lr_lo = (lr - lr_hi.astype(F32)).astype(BF16)
    pre = _dot(lr_hi, w2h_ref[...]) + (_dot(lr_lo, w2h_ref[...]) + _dot(lr_hi, w2l_ref[...])) + b2_ref[...]
    bc_ref[0] = _chunk_cumsum(_log_sigmoid(pre) * (1.0 / GLA_GATE_NORM))
    qk_ref[0] = _dot(h, w_ref[:, c[0]:c[1]]).astype(BF16)
    v_ref[0] = _dot(h, w_ref[:, c[1]:c[2]]).astype(BF16)
    sg_ref[0] = _silu(_dot(h, w_ref[:, c[3]:c[4]])).astype(BF16)
    u_ref[0] = _dot(h, w_ref[:, c[4]:c[5]])
    ss_ref[0] = _silu(_dot(h, w_ref[:, c[5]:c[6]])).astype(BF16)


O_COLS = (0, 512, 768, 1280, 1792, 2304, 2816, 3328)


def _rope(x, cos, sin):
    width = x.shape[1]
    lane = lax.broadcasted_iota(jnp.int32, x.shape, 1)
    partner = jnp.where((lane & 16) == 0, pltpu.roll(x, width - 16, 1), pltpu.roll(x, 16, 1))
    return x * cos + partner * sin


def _head_pair_norm(x, w, cos, sin, both):
    low = lax.broadcasted_iota(jnp.int32, x.shape, 1) < ATT_HD
    x2 = x * x
    ms = jnp.sum(jnp.where(low, x2, 0.0), axis=-1, keepdims=True) * (1.0 / ATT_HD)
    rs = lax.rsqrt(ms + EPS)
    if both:
        ms_hi = jnp.sum(jnp.where(low, 0.0, x2), axis=-1, keepdims=True) * (1.0 / ATT_HD)
        rs = jnp.where(low, rs, lax.rsqrt(ms_hi + EPS))
    else:
        rs = jnp.where(low, rs, 1.0)
    y = x * rs * w
    return y if cos is None else _rope(y, cos, sin)


def _odd_in_compute(x, refs, outs, rope):
    if rope:
        mod_ref, nw_ref, w_ref, wkv_ref, qn_ref, kn_ref, cq_ref, sq_ref, ck_ref, sk_ref = refs
        cq, sq, ck, sk = cq_ref[...], sq_ref[...], ck_ref[...], sk_ref[...]
    else:
        mod_ref, nw_ref, w_ref, wkv_ref, qn_ref, kn_ref = refs
        cq = sq = ck = sk = None
    q_ref, kv_ref, sa_ref, p_ref, bgz_ref = outs
    h = _modulated(x, mod_ref, nw_ref)
    c = O_COLS
    q = _dot(h, w_ref[:, c[0]:c[1]])
    for t in range(BRANCH // 128):
        lanes = slice(t * 128, (t + 1) * 128)
        qt = _head_pair_norm(q[:, lanes], qn_ref[:, lanes], cq, sq, True)
        q_ref[0, :, lanes] = (qt * Q_SCALE).astype(BF16)
    kv = _dot(h, wkv_ref[...])
    for t in range(ATT_KV_HEADS):
        lanes = slice(t * 128, (t + 1) * 128)
        kv_ref[0, :, lanes] = _head_pair_norm(kv[:, lanes], kn_ref[:, lanes], ck, sk, False)
    sa_ref[0] = _silu(_dot(h, w_ref[:, c[2]:c[3]])).astype(BF16)
    xc = _dot(h, w_ref[:, c[3]:c[4]])
    bg = _dot(h, w_ref[:, c[4]:c[5]])
    cg = _dot(h, w_ref[:, c[5]:c[6]])
    zc = _dot(h, w_ref[:, c[6]:c[7]])
    p_ref[0] = (cg * xc).astype(BF16)
    bgz_ref[0] = (bg * _silu(zc)).astype(BF16)


def _tile_spec(tm, n):
    return pl.BlockSpec((1, tm, n), lambda b, i: (b, i, 0))


def _whole_spec(a):
    return pl.BlockSpec(a.shape, lambda b, i: (0,) * a.ndim)


def _mod_spec():
    return pl.BlockSpec((1, 3, D_MODEL), lambda b, i: (b, 0, 0))


def _even_in_part(tm, mod, nw, w_in, gla_w):
    args = [mod, nw, w_in, *gla_w]
    specs = [_mod_spec()] + [_whole_spec(a) for a in args[1:]]
    outs = [(BRANCH, dt) for dt in (BF16, BF16, F32, BF16, F32, BF16)]
    return _even_in_compute, args, specs, outs


def _odd_in_part(tm, mod, nw, w_in, qnw, knw, tables):
    args = [mod, nw, *w_in, qnw, knw, *tables]
    specs = ([_mod_spec()] + [_whole_spec(a) for a in args[1:6]]
             + [pl.BlockSpec((tm, a.shape[1]), lambda b, i: (i, 0)) for a in tables])
    outs = [(BRANCH, BF16), (2 * ATT_KV_HEADS * ATT_HD, F32), (BRANCH, BF16), (BRANCH, BF16), (BRANCH, BF16)]
    return functools.partial(_odd_in_compute, rope=bool(tables)), args, specs, outs


def _stage_kernel(*refs, n_out_args, n_in_args, out_fn, in_fn):
    x_ref = refs[0]
    out_args = refs[1:1 + n_out_args]
    in_args = refs[1 + n_out_args:1 + n_out_args + n_in_args]
    outs = refs[1 + n_out_args + n_in_args:]
    x = x_ref[0]
    if out_fn is not None:
        x = out_fn(x, out_args)
        outs[0][0] = x
        outs = outs[1:]
    if in_fn is not None:
        in_fn(x, in_args, outs)


def _stage_call(name, x, out_part, in_part):
    bm, lm, _ = x.shape
    tm = min(ROW_TILE, lm)
    args, specs, out_defs = [x], [_tile_spec(tm, D_MODEL)], []
    out_fn = in_fn = None
    n_out_args = n_in_args = 0
    if out_part is not None:
        out_fn, a, s = out_part
        args, specs, n_out_args = args + a, specs + s, len(a)
        out_defs.append((D_MODEL, F32))
    if in_part is not None:
        in_fn, a, s, o = in_part
        args, specs, n_in_args = args + a, specs + s, len(a)
        out_defs += o
    res = pl.pallas_call(
        functools.partial(_stage_kernel, n_out_args=n_out_args, n_in_args=n_in_args, out_fn=out_fn, in_fn=in_fn),
        grid=(bm, lm // tm),
        in_specs=specs,
        out_specs=[_tile_spec(tm, n) for n, _ in out_defs],
        out_shape=[jax.ShapeDtypeStruct((bm, lm, n), dt) for n, dt in out_defs],
        compiler_params=_cparams("parallel", "parallel"),
        name=name,
    )(*args)
    return res


def _loop(n, body, init, unroll):
    if n <= unroll:
        for i in range(n):
            init = body(i, init)
        return init
    return lax.fori_loop(0, n, body, init, unroll=unroll)


def _gla_kernel(*refs, seq, has_state, heads):
    if has_state:
        (qk_ref, v_ref, bc_ref, sg_ref, on_ref, _, s0_ref, y_ref, sn_ref,
         ut_scr, sf_scr, sb_scr, qd_scr, o_scr) = refs
    else:
        (qk_ref, v_ref, bc_ref, sg_ref, on_ref, _, y_ref, sn_ref,
         ut_scr, sf_scr, sb_scr, qd_scr, o_scr) = refs
        s0_ref = None
    C = GLA_CHUNK
    W = 2 * GLA_DK
    R = GLA_BLOCK
    CB = R // C
    nc = seq // C
    nb = seq // R
    ri = lax.broadcasted_iota(jnp.int32, (R, R), 0)
    ci = lax.broadcasted_iota(jnp.int32, (R, R), 1)
    same = (ri // C) == (ci // C)
    tril = same & (ri >= ci)
    triu = same & (ri <= ci)
    fwd = lax.broadcasted_iota(jnp.int32, (R, W), 1) < GLA_DK
    fwd_row = lax.broadcasted_iota(jnp.int32, (1, W), 1) < GLA_DK
    fwd_st = lax.broadcasted_iota(jnp.int32, (GLA_DV, W), 1) < GLA_DK
    row_chunk = lax.broadcasted_iota(jnp.int32, (R, W), 0) // C
    scale = GLA_DK ** -0.5

    def rows(c):
        return pl.ds(pl.multiple_of(c * C, C), C)

    def block_rows(j):
        return pl.ds(pl.multiple_of(j * R, R), R)

    def state_rows(c):
        return pl.ds(pl.multiple_of(c * GLA_DV, GLA_DV), GLA_DV)

    def intra(h, j, carry):
        r = block_rows(j)
        hl = slice(h * W, (h + 1) * W)
        qk = qk_ref[0, r, hl].astype(F32)
        bc = bc_ref[0, r, hl]
        v = v_ref[0, r, hl].astype(F32)
        sw = pltpu.roll(qk, GLA_DK, 1)
        q2 = jnp.where(fwd, qk, sw) * scale
        k2 = jnp.where(fwd, sw, qk)
        b_mid = bc[C // 2:C // 2 + 1, :]
        b_end = jnp.where(fwd_row, bc[C - 1:C, :], bc[0:1, :])
        for c in range(1, CB):
            o0 = c * C
            b_mid = jnp.where(row_chunk == c, bc[o0 + C // 2:o0 + C // 2 + 1, :], b_mid)
            b_end = jnp.where(row_chunk == c, jnp.where(fwd_row, bc[o0 + C - 1:o0 + C, :], bc[o0:o0 + 1, :]), b_end)
        qd = (q2 * jnp.exp(bc)).astype(BF16)
        qh = q2 * jnp.exp(bc - b_mid)
        kh = (k2 * jnp.exp(b_mid - bc)).astype(BF16)
        att = (jnp.where(tril, _dot_nt(jnp.where(fwd, qh, 0.0).astype(BF16), kh), 0.0)
               + jnp.where(triu, _dot_nt(jnp.where(fwd, 0.0, qh).astype(BF16), kh), 0.0))
        o = _dot(att.astype(BF16), v.astype(BF16))
        kd = k2 * jnp.exp(b_end - bc)
        kd_wide = jnp.concatenate([jnp.where(row_chunk == c, kd, 0.0) for c in range(CB)], axis=1)
        ut = _dot(v.T.astype(BF16), kd_wide.astype(BF16))
        qd_scr[h, r, :] = qd
        o_scr[h, r, :] = o
        ut_scr[h, j] = ut
        return carry

    for h in range(heads):
        _loop(nb, functools.partial(intra, h), 0, 4)

    def scan(h, j, st):
        jb = nb - 1 - j
        hl = slice(h * W, (h + 1) * W)
        bcf = bc_ref[0, block_rows(j), hl]
        bcb = bc_ref[0, block_rows(jb), hl]
        utf = ut_scr[h, j]
        utb = ut_scr[h, jb]
        for c in range(CB):
            cb = CB - 1 - c
            stb = st.astype(BF16)
            sf_scr[h, j, :, c * W:(c + 1) * W] = stb
            sb_scr[h, jb, :, cb * W:(cb + 1) * W] = stb
            b_end = jnp.where(fwd_row, bcf[c * C + C - 1:c * C + C, :], bcb[cb * C:cb * C + 1, :])
            inc = jnp.where(fwd_st, utf[:, c * W:(c + 1) * W], utb[:, cb * W:(cb + 1) * W])
            st = st * jnp.exp(b_end) + inc
        return st

    for h in range(heads):
        if has_state:
            init = jnp.concatenate([s0_ref[0, 0, h].T, s0_ref[0, 1, h].T], axis=1)
        else:
            init = jnp.zeros((GLA_DV, W), F32)
        st = _loop(nb, functools.partial(scan, h), init, 2)
        sn_ref[0, 0, 0, h] = st[:, :GLA_DK].T
        sn_ref[0, 0, 1, h] = st[:, GLA_DK:].T

    fwd_wide = (lax.broadcasted_iota(jnp.int32, (GLA_DV, CB * W), 1) % W) < GLA_DK

    def finish(h, j, carry):
        r = block_rows(j)
        hl = slice(h * W, (h + 1) * W)
        qd = qd_scr[h, r, :]
        zero = jnp.zeros((), BF16)
        qd_wide = jnp.concatenate([jnp.where(row_chunk == c, qd, zero) for c in range(CB)], axis=1)
        s_in = jnp.where(fwd_wide, sf_scr[h, j], sb_scr[h, j])
        o = o_scr[h, r, :] + _dot_nt(qd_wide, s_in)
        ms = jnp.mean(o * o, axis=-1, keepdims=True)
        y_ref[0, r, hl] = (o * lax.rsqrt(ms + EPS) * on_ref[...] * sg_ref[0, r, hl].astype(F32)).astype(BF16)
        return carry

    for h in range(heads):
        _loop(nb, functools.partial(finish, h), 0, 8)


def _gla_call(qk, v, bc, sg, onorm, s0, layer, states):
    bsz, seq, _ = qk.shape
    has_state = s0 is not None
    nb = seq // GLA_BLOCK
    wide = (GLA_BLOCK // GLA_CHUNK) * 2 * GLA_DK
    hps = GLA_HEADS if nb == 1 else 1
    per_head = pl.BlockSpec((1, seq, 128 * hps), lambda b, h: (b, 0, h))
    in_specs = [per_head, per_head, per_head, per_head, pl.BlockSpec((1, GLA_DV), lambda b, h: (0, 0)),
                pl.BlockSpec(memory_space=pl.ANY)]
    args = [qk, v, bc, sg, onorm, states]
    if has_state:
        in_specs.append(pl.BlockSpec((1, 2, hps, GLA_DK, GLA_DV), lambda b, h: (b, 0, h, 0, 0)))
        args.append(s0)
    new_state_spec = pl.BlockSpec((1, 1, 2, hps, GLA_DK, GLA_DV), lambda b, h: (b, layer, 0, h, 0, 0))
    return pl.pallas_call(
        functools.partial(_gla_kernel, seq=seq, has_state=has_state, heads=hps),
        grid=(bsz, GLA_HEADS // hps),
        in_specs=in_specs,
        out_specs=[per_head, new_state_spec],
        out_shape=[jax.ShapeDtypeStruct((bsz, seq, BRANCH), BF16),
                   jax.ShapeDtypeStruct(states.shape, F32)],
        input_output_aliases={5: 1},
        scratch_shapes=[pltpu.VMEM((hps, nb, GLA_DV, wide), F32),
                        pltpu.VMEM((hps, nb, GLA_DV, wide), BF16),
                        pltpu.VMEM((hps, nb, GLA_DV, wide), BF16),
                        pltpu.VMEM((hps, seq, 2 * GLA_DK), BF16),
                        pltpu.VMEM((hps, seq, GLA_DV), F32)],
        compiler_params=_cparams("parallel", "parallel"),
        name="gla",
    )(*args)


def _s5_prep_kernel(lr_ref, li_ref, ld_ref, br_ref, bi_ref, cr_ref, ci_ref,
                    m_ref, win_ref, wout_ref, ar_ref, ai_ref, *, nlev):
    cs = S5_CHUNK
    P2 = 2 * S5_STATE
    lam_re, lam_im = lr_ref[0], li_ref[0]
    dt = jnp.exp(ld_ref[0])
    er, ei = lam_re * dt, lam_im * dt
    mag = jnp.exp(er)
    nr, ni = mag * jnp.cos(ei) - 1.0, mag * jnp.sin(ei)
    den = lam_re * lam_re + lam_im * lam_im
    cr = (nr * lam_re + ni * lam_im) / den
    ci = (ni * lam_re - nr * lam_im) / den
    b_re, b_im = br_ref[0], bi_ref[0]
    bb_re = cr * b_re - ci * b_im
    bb_im = cr * b_im + ci * b_re
    c_re, c_im = cr_ref[0], ci_ref[0]

    def power(k):
        m = jnp.exp(k * er)
        return m * jnp.cos(k * ei), m * jnp.sin(k * ei)

    def outer(a, b):
        return (a[:, None, :] * b[None, :, :]).reshape(S5_ROW, P2)

    def cmul(ar, ai, br, bi):
        return outer(ar, br) - outer(ai, bi), outer(ar, bi) + outer(ai, br)

    s_row = lax.broadcasted_iota(jnp.int32, (cs, P2), 0).astype(F32)
    fwd = lax.broadcasted_iota(jnp.int32, (cs, P2), 1) < S5_STATE
    half = float(cs // 2)
    k_in = jnp.where(fwd, half - s_row, s_row - half)
    l_re, l_im = cmul(*power(k_in), bb_re, bb_im)
    r_re, r_im = cmul(*power(-k_in), c_re, c_im)
    lcat = jnp.concatenate([l_re, l_im], axis=1)
    rcat = jnp.concatenate([r_re, -r_im], axis=1)
    fwd2 = (lax.broadcasted_iota(jnp.int32, (S5_ROW, S5_ROW), 1) % P2) < S5_STATE
    nt = (((1,), (1,)), ((), ()))
    m_f = lax.dot_general(jnp.where(fwd2, lcat, 0.0), rcat, nt, precision=HIGHEST, preferred_element_type=F32)
    m_b = lax.dot_general(jnp.where(fwd2, 0.0, lcat), rcat, nt, precision=HIGHEST, preferred_element_type=F32)
    s_in = lax.broadcasted_iota(jnp.int32, (S5_ROW, S5_ROW), 0) // S5_GROUP
    s_out = lax.broadcasted_iota(jnp.int32, (S5_ROW, S5_ROW), 1) // S5_GROUP
    m_ref[0] = (jnp.where(s_in <= s_out, m_f, 0.0) + jnp.where(s_in >= s_out, m_b, 0.0)).astype(BF16)
    w_re, w_im = cmul(*power(jnp.where(fwd, (cs - 1.0) - s_row, s_row)), bb_re, bb_im)
    win_ref[0] = jnp.concatenate([w_re, w_im], axis=1).astype(BF16)
    o_re, o_im = cmul(*power(jnp.where(fwd, s_row + 1.0, cs - s_row)), c_re, c_im)
    wout_ref[0] = jnp.concatenate([o_re, -o_im], axis=1).astype(BF16)
    lev = lax.shift_left(jnp.int32(cs), lax.broadcasted_iota(jnp.int32, (nlev, P2), 0)).astype(F32)
    ar_ref[0], ai_ref[0] = power(lev)


def _s5_operators(lam_re, lam_im, log_dt, b_re, b_im, c_re, c_im, nlev):
    P2 = 2 * S5_STATE
    lanes = lambda a: a.transpose(1, 0, 2).reshape(S5_GROUPS, 1, P2)
    both = lambda a: jnp.concatenate([a, a], axis=-1)
    args = (lanes(lam_re), lanes(lam_im),
            lanes(jnp.broadcast_to(log_dt[..., None], lam_re.shape)),
            both(b_re.transpose(0, 2, 1)), both(b_im.transpose(0, 2, 1)), both(c_re), both(c_im))
    g3 = lambda g: (g, 0, 0)
    vec = pl.BlockSpec((1, 1, P2), g3)
    mat = pl.BlockSpec((1, S5_GROUP, P2), g3)
    op = pl.BlockSpec((1, S5_ROW, S5_ROW), g3)
    lev = pl.BlockSpec((1, nlev, P2), g3)
    return pl.pallas_call(
        functools.partial(_s5_prep_kernel, nlev=nlev),
        grid=(S5_GROUPS,),
        in_specs=[vec, vec, vec, mat, mat, mat, mat],
        out_specs=[op, op, op, lev, lev],
        out_shape=[jax.ShapeDtypeStruct((S5_GROUPS, S5_ROW, S5_ROW), BF16)] * 3
        + [jax.ShapeDtypeStruct((S5_GROUPS, nlev, P2), F32)] * 2,
        compiler_params=_cparams("parallel"),
        name="s5_prep",
    )(*args)


def _gelu_tanh(x):
    return 0.5 * x * (1.0 + jnp.tanh(math.sqrt(2.0 / math.pi) * (x + 0.044715 * (x * x * x))))


def _slot_transpose(v, slot):
    v = list(v)
    d = len(v) // 2
    while d >= 1:
        take_own = (slot & d) == 0
        for i in range(len(v)):
            if i & d == 0:
                a, b = v[i], v[i + d]
                v[i] = jnp.where(take_own, a, pltpu.roll(b, d * S5_GROUP, 1))
                v[i + d] = jnp.where(take_own, pltpu.roll(a, 128 - d * S5_GROUP, 1), b)
        d //= 2
    return v


def _s5_kernel(*refs, nseq, nrow, nlev, has_state):
    if has_state:
        (u_ref, m_ref, win_ref, wout_ref, ar_ref, ai_ref, d_ref, h0_ref,
         z_ref, last_ref, t_scr, y_scr, tmp_r, tmp_i) = refs
    else:
        (u_ref, m_ref, win_ref, wout_ref, ar_ref, ai_ref, d_ref,
         z_ref, last_ref, t_scr, y_scr, tmp_r, tmp_i) = refs
    P2 = 2 * S5_STATE
    R = nseq * nrow
    GP = S5_SLAB_GROUPS
    TR = 16
    slot = lax.broadcasted_iota(jnp.int32, (TR, 128), 1) // S5_GROUP

    def token_rows(nt, s):
        return pl.ds(pl.multiple_of(nt * (TR * S5_CHUNK), TR * S5_CHUNK) + s, TR, stride=S5_CHUNK)

    def gather(nt, carry):
        for j in range(2):
            src = [u_ref[0, token_rows(nt, 8 * j + s8), :] for s8 in range(8)]
            by_group = _slot_transpose(src, slot)
            for gl in range(GP):
                t_scr[gl, pl.ds(pl.multiple_of(nt * TR, TR), TR), j * 128:(j + 1) * 128] = by_group[gl].astype(BF16)
        return carry

    lax.fori_loop(0, R // TR, gather, 0, unroll=2)

    n = lax.broadcasted_iota(jnp.int32, (R, P2), 0) % nrow
    row = lax.broadcasted_iota(jnp.int32, (R, P2), 0)
    fwd = lax.broadcasted_iota(jnp.int32, (R, P2), 1) < S5_STATE
    fwd_b = lax.broadcasted_iota(jnp.int32, (nseq, P2), 1) < S5_STATE

    def shift(a, k):
        dn = pltpu.roll(a, k, 0)
        up = pltpu.roll(a, R - k, 0)
        return jnp.where(fwd, jnp.where(n >= k, dn, 0.0), jnp.where(n < nrow - k, up, 0.0))

    def ends(t):
        if nseq == 1:
            return jnp.where(fwd_b, t[nrow - 1:nrow, :], t[0:1, :])
        return jnp.where(fwd_b, t[pl.ds(nrow - 1, nseq, stride=nrow), :], t[pl.ds(0, nseq, stride=nrow), :])

    def per_group(gl, carry):
        ub = t_scr[gl]
        x = _dot(ub, win_ref[gl])
        xr, xi = x[:, :P2], x[:, P2:]
        er, ei = shift(xr, 1), shift(xi, 1)
        if has_state:
            h0 = h0_ref[0, gl]
            for b in range(nseq):
                first = row == jnp.where(fwd, b * nrow, b * nrow + nrow - 1)
                er = jnp.where(first, h0[b:b + 1, :P2], er)
                ei = jnp.where(first, h0[b:b + 1, P2:], ei)
        for j in range(nlev):
            k = 2 ** j
            ar = ar_ref[gl, j:j + 1, :]
            ai = ai_ref[gl, j:j + 1, :]
            sr, si = shift(er, k), shift(ei, k)
            er, ei = er + ar * sr - ai * si, ei + ar * si + ai * sr
        ecat = jnp.concatenate([er, ei], axis=1).astype(BF16)
        y_scr[gl] = _dot(ub, m_ref[gl]) + _dot_nt(ecat, wout_ref[gl])
        a1r = ar_ref[gl, 0:1, :]
        a1i = ai_ref[gl, 0:1, :]
        tmp_r[...] = a1r * er - a1i * ei + xr
        tmp_i[...] = a1r * ei + a1i * er + xi
        last_ref[0, gl, :, :P2] = ends(tmp_r)
        last_ref[0, gl, :, P2:] = ends(tmp_i)
        return carry

    lax.fori_loop(0, GP, per_group, 0, unroll=4)

    def scatter(nt, carry):
        for j in range(2):
            ys = [y_scr[gl, pl.ds(pl.multiple_of(nt * TR, TR), TR), j * 128:(j + 1) * 128] for gl in range(GP)]
            by_token = _slot_transpose(ys, slot)
            for s8 in range(8):
                rows = token_rows(nt, 8 * j + s8)
                z_ref[0, rows, :] = _gelu_tanh(by_token[s8] + d_ref[...] * u_ref[0, rows, :])
        return carry

    lax.fori_loop(0, R // TR, scatter, 0, unroll=2)


def _s5_call(u, ops, d, h0, seq):
    m, win, wout, ar, ai = ops
    bm, lm, _ = u.shape
    nrow = seq // S5_CHUNK
    nlev = int(math.log2(nrow))
    total = bm * lm // seq
    nseq = max(1, min(total, S5_BLOCK_ROWS // nrow))
    nblk = total // nseq
    R = nseq * nrow
    tok = R * S5_CHUNK
    GP = S5_SLAB_GROUPS
    ns = BRANCH // 128
    has_state = h0 is not None
    slab = pl.BlockSpec((1, tok, 128), lambda t, i: (i, 0, t))
    op = pl.BlockSpec((GP, S5_ROW, S5_ROW), lambda t, i: (t, 0, 0))
    lev = pl.BlockSpec((GP, ar.shape[1], 2 * S5_STATE), lambda t, i: (t, 0, 0))
    state = pl.BlockSpec((1, GP, nseq, 4 * S5_STATE), lambda t, i: (i, t, 0, 0))
    in_specs = [slab, op, op, op, lev, lev, pl.BlockSpec((1, 128), lambda t, i: (0, t))]
    args = [u.reshape(nblk, tok, BRANCH), m, win, wout, ar, ai, d]
    if has_state:
        in_specs.append(state)
        args.append(h0.reshape(nblk, nseq, S5_GROUPS, 4 * S5_STATE).transpose(0, 2, 1, 3))
    z, last = pl.pallas_call(
        functools.partial(_s5_kernel, nseq=nseq, nrow=nrow, nlev=nlev, has_state=has_state),
        grid=(ns, nblk),
        in_specs=in_specs,
        out_specs=[slab, state],
        out_shape=[jax.ShapeDtypeStruct((nblk, tok, BRANCH), F32),
                   jax.ShapeDtypeStruct((nblk, S5_GROUPS, nseq, 4 * S5_STATE), F32)],
        scratch_shapes=[pltpu.VMEM((GP, R, S5_ROW), BF16), pltpu.VMEM((GP, R, S5_ROW), F32),
                        pltpu.VMEM((R, 2 * S5_STATE), F32), pltpu.VMEM((R, 2 * S5_STATE), F32)],
        compiler_params=_cparams("parallel", "parallel"),
        name="s5",
    )(*args)
    last = last.transpose(0, 2, 1, 3).reshape(total, S5_GROUPS, 4 * S5_STATE)
    return z.reshape(bm, lm, BRANCH), last


HEAD_ORDER = (0, 2, 1, 3)


def _stack_heads(q, heads):
    head = lax.broadcasted_iota(jnp.int32, q.shape, 1) // ATT_HD
    zero = jnp.zeros((), q.dtype)
    return jnp.concatenate([jnp.where(head == h, q, zero) for h in heads], axis=0)


def _key_value_operands(kv):
    low = lax.broadcasted_iota(jnp.int32, kv.shape, 1) < ATT_HD
    sw = pltpu.roll(kv, ATT_HD, 1)
    kk = jnp.where(low, kv, sw).astype(BF16)
    return (jnp.concatenate([kk, kk], axis=1),
            jnp.where(low, sw, 1.0).astype(BF16), jnp.where(low, 1.0, kv).astype(BF16))


LOG2E = 1.4426950408889634
Q_SCALE = (ATT_HD ** -0.5) * LOG2E


def _attend_pair(q, parity, key_sets, sink_ref):
    rows = q.shape[0]
    heads = (parity, parity + 2)
    qs = _stack_heads(q, heads)
    sink = jnp.concatenate([jnp.broadcast_to(sink_ref[h:h + 1, 0:1], (rows, 1)) for h in heads], axis=0) * LOG2E
    scores = []
    t = None
    for k4, _, _, bias in key_sets:
        s = _dot_nt(qs, k4)
        if bias is not None:
            s = s + jnp.concatenate([bias, bias], axis=0)
        scores.append(s)
        for c in range(0, s.shape[1], 128):
            t = s[:, c:c + 128] if t is None else jnp.maximum(t, s[:, c:c + 128])
    m = jnp.maximum(jnp.max(t, axis=-1, keepdims=True), sink)
    lane = lax.broadcasted_iota(jnp.int32, (2 * rows, 2 * ATT_HD), 1)
    den_lanes = (lane >= ATT_HD) if parity == 0 else (lane < ATT_HD)
    o = jnp.where(den_lanes, jnp.exp2(sink - m), 0.0)
    for s, (_, va, vb, _) in zip(scores, key_sets):
        o = o + _dot(jnp.exp2(s - m).astype(BF16), va if parity == 0 else vb)
    return o / pltpu.roll(o, ATT_HD, 1)


def _attend_stacked(q, key_sets, sink_ref):
    rows = q.shape[0]
    qs = _stack_heads(q, HEAD_ORDER)
    sink = jnp.concatenate([jnp.broadcast_to(sink_ref[h:h + 1, 0:1], (rows, 1)) for h in HEAD_ORDER],
                           axis=0) * LOG2E
    scores = []
    t = None
    for k4, _, _, bias in key_sets:
        s = _dot_nt(qs, k4)
        if bias is not None:
            s = s + jnp.concatenate([bias] * ATT_GROUPS, axis=0)
        scores.append(s)
        for c in range(0, s.shape[1], 128):
            t = s[:, c:c + 128] if t is None else jnp.maximum(t, s[:, c:c + 128])
    m = jnp.maximum(jnp.max(t, axis=-1, keepdims=True), sink)
    e_sink = jnp.exp2(sink - m)
    low = lax.broadcasted_iota(jnp.int32, (2 * rows, 2 * ATT_HD), 1) < ATT_HD
    o_even = jnp.where(low, 0.0, e_sink[:2 * rows])
    o_odd = jnp.where(low, e_sink[2 * rows:], 0.0)
    for s, (_, va, vb, _) in zip(scores, key_sets):
        p = jnp.exp2(s - m).astype(BF16)
        o_even = o_even + _dot(p[:2 * rows], va)
        o_odd = o_odd + _dot(p[2 * rows:], vb)
    return o_even / pltpu.roll(o_even, ATT_HD, 1), o_odd / pltpu.roll(o_odd, ATT_HD, 1)


def _attend(q, key_sets, sink_ref, split):
    rows = q.shape[0]
    if split:
        o_even = _attend_pair(q, 0, key_sets, sink_ref)
        o_odd = _attend_pair(q, 1, key_sets, sink_ref)
    else:
        o_even, o_odd = _attend_stacked(q, key_sets, sink_ref)
    low = lax.broadcasted_iota(jnp.int32, o_even.shape, 1) < ATT_HD
    pair = jnp.where(low, o_even, o_odd)
    return jnp.concatenate([pair[:rows], pair[rows:]], axis=1)


def _att_ctx_kernel(q_ref, kv_ref, sa_ref, sink_ref, kprev_ref, vprev_ref, y_ref, kc_ref, vc_ref):
    del kprev_ref, vprev_ref
    W = ATT_GROUPS * ATT_HD
    results = []
    for g in range(ATT_KV_HEADS):
        kv = kv_ref[0, :, g * 2 * ATT_HD:(g + 1) * 2 * ATT_HD]
        heads = slice(g * W, (g + 1) * W)
        k4, v_ones, _ = _key_value_operands(kv)
        key_sets = [(k4, _value_rows(v_ones.astype(F32)), None)]
        out = _attend_keys_on_rows(q_ref[0, :, heads], key_sets, sink_ref.at[g])
        results.append((kv, (out * sa_ref[0, :, heads].astype(F32)).astype(BF16)))
    for g, (kv, y) in enumerate(results):
        kc_ref[0, 0, g] = kv[:, :ATT_HD]
        vc_ref[0, 0, g] = kv[:, ATT_HD:]
        y_ref[0, :, g * W:(g + 1) * W] = y


def _att_ctx_call(q, kv, sa, sink, layer, caches):
    bsz, seq, _ = q.shape
    row = lambda b: (b, 0, 0)
    cache_spec = pl.BlockSpec((1, 1, ATT_KV_HEADS, seq, ATT_HD), lambda b: (b, layer, 0, 0, 0))
    cache_shape = jax.ShapeDtypeStruct(caches[0].shape, F32)
    return pl.pallas_call(
        _att_ctx_kernel,
        grid=(bsz,),
        in_specs=[pl.BlockSpec((1, seq, BRANCH), row), pl.BlockSpec((1, seq, kv.shape[-1]), row),
                  pl.BlockSpec((1, seq, BRANCH), row), pl.BlockSpec(sink.shape, lambda b: (0, 0, 0)),
                  pl.BlockSpec(memory_space=pl.ANY), pl.BlockSpec(memory_space=pl.ANY)],
        out_specs=[pl.BlockSpec((1, seq, BRANCH), row), cache_spec, cache_spec],
        out_shape=[jax.ShapeDtypeStruct((bsz, seq, BRANCH), BF16), cache_shape, cache_shape],
        input_output_aliases={4: 1, 5: 2},
        compiler_params=_cparams("parallel"),
        name="att_ctx",
    )(q, kv, sa, sink, *caches)


def _attend_keys_on_rows(q, key_sets, sink_ref):
    rows = q.shape[0]
    qs = _stack_heads(q, range(ATT_GROUPS))
    scores, top = [], None
    for k4, _, bias_t in key_sets:
        s = _dot_nt(k4, qs)
        if bias_t is not None:
            s = s + jnp.concatenate([bias_t] * ATT_GROUPS, axis=1)
        scores.append(s)
        part = jnp.max(s, axis=0, keepdims=True)
        top = part if top is None else jnp.maximum(top, part)
    sink = jnp.concatenate([jnp.broadcast_to(sink_ref[h:h + 1, 0:1], (1, rows)) for h in range(ATT_GROUPS)],
                           axis=1) * LOG2E
    m = jnp.maximum(top, sink)
    o = None
    for s, (_, vat, _) in zip(scores, key_sets):
        p = jnp.exp2(s - m).astype(BF16)
        for c, vat_c in enumerate(vat):
            pv = _dot(vat_c, p[c * 128:(c + 1) * 128, :])
            o = pv if o is None else o + pv
    on = o[:ATT_HD, :] / (o[ATT_HD:, :] + jnp.exp2(sink - m))
    pairs = [jnp.concatenate([on[:, (2 * t) * rows:(2 * t + 1) * rows],
                              on[:, (2 * t + 1) * rows:(2 * t + 2) * rows]], axis=0).T
             for t in range(ATT_GROUPS // 2)]
    return jnp.concatenate(pairs, axis=1)


def _value_rows(v_ones):
    return [v_ones[c:c + 128, :].T.astype(BF16) for c in range(0, v_ones.shape[0], 128)]


def _att_lat_kernel(q_ref, kv_ref, sa_ref, sink_ref, kc_ref, vc_ref, y_ref, k4_scr, vat_scr, *, seq):
    nb = seq // QBLK
    KW = 3 * QBLK
    low = lax.broadcasted_iota(jnp.int32, (QBLK, 2 * ATT_HD), 1) < ATT_HD

    def k_body(i, carry):
        r = pl.ds(pl.multiple_of(i * QBLK, QBLK), QBLK)
        kv = kv_ref[0, r, :]
        sw = pltpu.roll(kv, ATT_HD, 1)
        kk = jnp.where(low, kv, sw).astype(BF16)
        k4_scr[r, :] = jnp.concatenate([kk, kk], axis=1)
        vat_scr[i] = jnp.where(low, sw, 1.0).T.astype(BF16)
        return carry

    lax.fori_loop(0, nb, k_body, 0, unroll=4)
    kc = kc_ref[0, 0, 0]
    vc = vc_ref[0, 0, 0]
    kc4 = jnp.concatenate([kc] * ATT_GROUPS, axis=1).astype(BF16)
    vat_ctx = _value_rows(jnp.concatenate([vc, jnp.ones_like(vc)], axis=1))
    kj = lax.broadcasted_iota(jnp.int32, (KW, QBLK), 0)
    qi = lax.broadcasted_iota(jnp.int32, (KW, QBLK), 1)

    def q_body(i, carry):
        r = pl.ds(pl.multiple_of(i * QBLK, QBLK), QBLK)
        ws = pl.multiple_of(jnp.clip(i * QBLK - QBLK, 0, seq - KW), QBLK)
        jb = ws // QBLK
        bias_t = jnp.where(jnp.abs((i * QBLK + qi) - (ws + kj)) <= WINDOW, 0.0, NEG)
        key_sets = [(k4_scr[pl.ds(ws, KW), :], [vat_scr[jb + c] for c in range(KW // QBLK)], bias_t),
                    (kc4, vat_ctx, None)]
        out = _attend_keys_on_rows(q_ref[0, r, :], key_sets, sink_ref.at[0])
        y_ref[0, r, :] = (out * sa_ref[0, r, :].astype(F32)).astype(BF16)
        return carry

    lax.fori_loop(0, nb, q_body, 0, unroll=8)


def _rope_tables(seq, heads, pad_heads):
    pos = jnp.arange(seq)
    row = (pos // GRID_W).astype(F32)[:, None]
    col = (pos % GRID_W).astype(F32)[:, None]
    nf = ATT_HD // 4
    freq = ROPE_BASE ** (-jnp.arange(nf, dtype=F32) / nf)
    ar, ac = row * freq, col * freq
    cos = jnp.concatenate([jnp.cos(ar), jnp.cos(ar), jnp.cos(ac), jnp.cos(ac)], axis=1)
    sin = jnp.concatenate([-jnp.sin(ar), jnp.sin(ar), -jnp.sin(ac), jnp.sin(ac)], axis=1)
    cos = jnp.concatenate([jnp.tile(cos, (1, heads)), jnp.ones((seq, pad_heads * ATT_HD), F32)], axis=1)
    sin = jnp.concatenate([jnp.tile(sin, (1, heads)), jnp.zeros((seq, pad_heads * ATT_HD), F32)], axis=1)
    return cos, sin


def _att_lat_call(q, kv, sa, sink, cache_k, cache_v, e):
    bsz, seq, _ = q.shape
    past = cache_k.shape[3]
    W = ATT_GROUPS * ATT_HD
    per_kv = lambda b, k: (b, 0, k)
    cache_spec = pl.BlockSpec((1, 1, 1, past, ATT_HD), lambda b, k: (b, e, k, 0, 0))
    return pl.pallas_call(
        functools.partial(_att_lat_kernel, seq=seq),
        grid=(bsz, ATT_KV_HEADS),
        in_specs=[
            pl.BlockSpec((1, seq, W), per_kv),
            pl.BlockSpec((1, seq, 2 * ATT_HD), per_kv),
            pl.BlockSpec((1, seq, W), per_kv),
            pl.BlockSpec((1, 8, 128), lambda b, k: (k, 0, 0)),
            cache_spec,
            cache_spec,
        ],
        out_specs=pl.BlockSpec((1, seq, W), per_kv),
        out_shape=jax.ShapeDtypeStruct((bsz, seq, BRANCH), BF16),
        scratch_shapes=[pltpu.VMEM((seq, W), BF16), pltpu.VMEM((seq // QBLK, 2 * ATT_HD, QBLK), BF16)],
        compiler_params=_cparams("parallel", "parallel"),
        name="att_lat",
    )(q, kv, sa, sink, cache_k, cache_v)


def _even_out_compute(x, refs):
    mod_ref, yg_ref, z_ref, ss_ref, wg_ref, bg_ref, wo_ref = refs
    z = z_ref[0]
    glu = z * jax.nn.sigmoid(_dot(z.astype(BF16), wg_ref[...]) + bg_ref[...])
    ys = (glu * ss_ref[0].astype(F32)).astype(BF16)
    out = _dot(yg_ref[0], wo_ref[:BRANCH, :]) + _dot(ys, wo_ref[BRANCH:, :])
    return x + mod_ref[0, 2:3, :] * out


def _even_out_part(tm, mod, yg, z, ss, wglu, bglu, wout):
    args = [mod, yg, z, ss, wglu, bglu, wout]
    specs = [_mod_spec(), _tile_spec(tm, BRANCH), _tile_spec(tm, BRANCH), _tile_spec(tm, BRANCH),
             _whole_spec(wglu), _whole_spec(bglu), _whole_spec(wout)]
    return _even_out_compute, args, specs


def _odd_out_compute(x, refs, seq, tm):
    mod_ref, ya_ref, p_ref, pprev_ref, pnext_ref, bgz_ref, cw_ref, cb_ref, wo_ref = refs
    p = p_ref[0].astype(F32)
    pos = (pl.program_id(1) * tm + lax.broadcasted_iota(jnp.int32, (tm, BRANCH), 0)) % seq
    rowi = lax.broadcasted_iota(jnp.int32, (tm, BRANCH), 0)
    halo = CONV_HALO_ROWS
    p_before = pprev_ref[0].astype(F32)[halo - 1:halo, :]
    p_after = pnext_ref[0].astype(F32)[0:1, :]
    prev = jnp.where(rowi == 0, p_before, pltpu.roll(p, 1, 0))
    nxt = jnp.where(rowi == tm - 1, p_after, pltpu.roll(p, tm - 1, 0))
    prev = jnp.where(pos == 0, 0.0, prev)
    nxt = jnp.where(pos == seq - 1, 0.0, nxt)
    conv = prev * cw_ref[0:1, :] + p * cw_ref[1:2, :] + nxt * cw_ref[2:3, :] + cb_ref[...]
    yc = (bgz_ref[0].astype(F32) * conv).astype(BF16)
    out = _dot(ya_ref[0], wo_ref[:BRANCH, :]) + _dot(yc, wo_ref[BRANCH:, :])
    return x + mod_ref[0, 2:3, :] * out


def _odd_out_part(tm, lm, seq, mod, ya, p, bgz, convw, convb, wout):
    hr = CONV_HALO_ROWS
    nth = lm // hr
    prev_spec = pl.BlockSpec((1, hr, BRANCH), lambda b, i: (b, jnp.maximum(i * (tm // hr) - 1, 0), 0))
    next_spec = pl.BlockSpec((1, hr, BRANCH), lambda b, i: (b, jnp.minimum((i + 1) * (tm // hr), nth - 1), 0))
    args = [mod, ya, p, p, p, bgz, convw, convb, wout]
    specs = [_mod_spec(), _tile_spec(tm, BRANCH), _tile_spec(tm, BRANCH), prev_spec, next_spec,
             _tile_spec(tm, BRANCH), _whole_spec(convw), _whole_spec(convb), _whole_spec(wout)]
    return functools.partial(_odd_out_compute, seq=seq, tm=tm), args, specs


def _even_in_weight(w):
    dk = GLA_HEADS * GLA_DK
    q, k = w[:, :dk], w[:, dk:2 * dk]
    v = w[:, 2 * dk:2 * dk + BRANCH]
    o = 2 * dk + BRANCH
    lr = w[:, o:o + 2 * GLA_RANK]
    rest = w[:, o + 2 * GLA_RANK:]
    qk = jnp.concatenate([jnp.concatenate([q[:, h * GLA_DK:(h + 1) * GLA_DK], k[:, h * GLA_DK:(h + 1) * GLA_DK]],
                                          axis=1) for h in range(GLA_HEADS)], axis=1)
    lr = jnp.pad(lr, ((0, 0), (0, 128 - 2 * GLA_RANK)))
    return jnp.concatenate([qk, v, lr, rest], axis=1).astype(BF16)


def _odd_in_weight(w):
    k = w[:, BRANCH:BRANCH + ATT_KV_HEADS * ATT_HD]
    v = w[:, BRANCH + ATT_KV_HEADS * ATT_HD:BRANCH + 2 * ATT_KV_HEADS * ATT_HD]
    kv = jnp.concatenate([jnp.concatenate([k[:, h * ATT_HD:(h + 1) * ATT_HD], v[:, h * ATT_HD:(h + 1) * ATT_HD]],
                                          axis=1) for h in range(ATT_KV_HEADS)], axis=1)
    return w.astype(BF16), kv.astype(BF16)


def _gate_weights(w2, b2):
    zf = jnp.zeros((GLA_RANK, GLA_DK), F32)
    cols, bias = [], []
    for h in range(GLA_HEADS):
        sl = slice(h * GLA_DK, (h + 1) * GLA_DK)
        cols.append(jnp.concatenate([jnp.concatenate([w2[0][:, sl], zf], axis=1),
                                     jnp.concatenate([zf, w2[1][:, sl]], axis=1)], axis=0))
        bias += [b2[0][sl], b2[1][sl]]
    w = jnp.pad(jnp.concatenate(cols, axis=1), ((0, 128 - 2 * GLA_RANK), (0, 0)))
    w_hi = w.astype(BF16)
    w_lo = (w - w_hi.astype(F32)).astype(BF16)
    return w_hi, w_lo, jnp.concatenate(bias)[None, :]


def _run_stream(x, mods, seq, params, gla_s0, s5_h0, cache_k, cache_v):
    depth = len(params)
    bm, lm, _ = x.shape
    tm = min(ROW_TILE, lm)
    nseq = bm * lm // seq
    per_seq = lambda a: a.reshape(nseq, seq, a.shape[-1])
    flat = lambda a: a.reshape(bm, lm, a.shape[-1])
    latent = cache_k is not None
    tables = _rope_tables(seq, 2, 0) + _rope_tables(seq, 1, 1) if latent else ()

    def in_part(l):
        p = params[l]
        if l % 2 == 0:
            return _even_in_part(tm, mods[l], p["nw"], p["w_in"], p["gla_w"])
        return _odd_in_part(tm, mods[l], p["nw"], p["w_in"], p["qnw"], p["knw"], tables)

    n_even, n_odd = (depth + 1) // 2, depth // 2
    gla_states = jnp.zeros((nseq, n_even, 2, GLA_HEADS, GLA_DK, GLA_DV), F32)
    cache_shape = (nseq, n_odd, ATT_KV_HEADS, seq, ATT_HD)
    caches = None if latent else (jnp.zeros(cache_shape, F32), jnp.zeros(cache_shape, F32))
    collected = {"s5": []}
    out_part = None
    for l in range(depth + 1):
        res = _stage_call("stage%d" % l, x, out_part, in_part(l) if l < depth else None)
        if out_part is not None:
            x, res = res[0], res[1:]
        if l == depth:
            break
        p, e = params[l], l // 2
        if l % 2 == 0:
            qk, v, bc, sg, u, ss = res
            yg, gla_states = _gla_call(per_seq(qk), per_seq(v), per_seq(bc), per_seq(sg), p["onorm"],
                                       None if gla_s0 is None else gla_s0[:, e], e, gla_states)
            z, last = _s5_call(u, p["s5_ops"], p["d"], None if s5_h0 is None else s5_h0[e], seq)
            collected["s5"].append(last)
            out_part = _even_out_part(tm, mods[l], flat(yg), z, ss, p["wglu"], p["bglu"], p["wout"])
        else:
            q, kv, sa, pc, bgz = res
            if latent:
                ya = _att_lat_call(per_seq(q), per_seq(kv), per_seq(sa), p["sink"], cache_k, cache_v, e)
            else:
                ya, *caches = _att_ctx_call(per_seq(q), per_seq(kv), per_seq(sa), p["sink"], e, caches)
            out_part = _odd_out_part(tm, lm, seq, mods[l], flat(ya), pc, bgz, p["convw"], p["convb"], p["wout"])
    collected["gla"] = gla_states
    collected["caches"] = caches
    return x, collected


def kernel(x_prompt, x_sample, c, state_gla, state_s5_re, state_s5_im, cache_k, cache_v, c_ctx, norm_w, w_ada, b_ada, w_in_e, w_out_e, gla_w2, gla_b2, gla_onorm, s5_lam_re, s5_lam_im, s5_log_dt, s5_b_re, s5_b_im, s5_c_re, s5_c_im, s5_d, s5_w_glu, s5_b_glu, w_in_o, w_out_o, q_norm_w, k_norm_w, sink, conv_w, conv_b):
    depth = norm_w.shape[0]
    bp, lp, _ = x_prompt.shape
    bs, ls, _ = x_sample.shape

    rows = 8 * ((1 + bs + 7) // 8)
    cs = jnp.zeros((rows, D_MODEL), F32).at[0].set(c_ctx).at[1:1 + bs].set(c)
    mods = _ada_call(cs, w_ada, b_ada)

    params, s5_h0 = [], []
    for l in range(depth):
        e = l // 2
        p = {"nw": norm_w[l][None, :]}
        if l % 2 == 0:
            p.update(
                w_in=_even_in_weight(w_in_e[e]), wout=w_out_e[e].astype(BF16),
                gla_w=_gate_weights(gla_w2[e], gla_b2[e]), onorm=gla_onorm[e][None, :],
                d=s5_d[e][None, :], wglu=s5_w_glu[e].astype(BF16), bglu=s5_b_glu[e][None, :],
                s5_ops=_s5_operators(s5_lam_re[e], s5_lam_im[e], s5_log_dt[e], s5_b_re[e], s5_b_im[e],
                                     s5_c_re[e], s5_c_im[e], nlev=int(math.log2(max(lp, ls) // S5_CHUNK))))
            h0 = jnp.concatenate([state_s5_re[:, e], state_s5_im[:, e]], axis=1)
            s5_h0.append(h0.transpose(0, 2, 1, 3).reshape(bs, S5_GROUPS, 4 * S5_STATE))
        else:
            sk = jnp.broadcast_to(sink[e].reshape(ATT_KV_HEADS, ATT_GROUPS, 1), (ATT_KV_HEADS, ATT_GROUPS, 128))
            p.update(
                w_in=_odd_in_weight(w_in_o[e]), wout=w_out_o[e].astype(BF16),
                qnw=jnp.tile(q_norm_w[e], ATT_HEADS)[None, :],
                knw=jnp.tile(jnp.concatenate([k_norm_w[e], jnp.ones((ATT_HD,), F32)]), ATT_KV_HEADS)[None, :],
                sink=jnp.concatenate([sk, jnp.zeros((ATT_KV_HEADS, 8 - ATT_GROUPS, 128), F32)], axis=1),
                convw=conv_w[e], convb=conv_b[e][None, :])
        params.append(p)

    mods_p = [mods[l, 0:1].reshape(1, 3, D_MODEL) for l in range(depth)]
    mods_s = [mods[l, 1:1 + bs].reshape(bs, 3, D_MODEL) for l in range(depth)]
    xp, got = _run_stream(x_prompt.reshape(1, bp * lp, D_MODEL), mods_p, lp, params, None, None, None, None)
    xs, _ = _run_stream(x_sample, mods_s, ls, params, state_gla, s5_h0, cache_k, cache_v)
    last = [t.reshape(bp, S5_GROUPS, 2, 2, S5_STATE).transpose(2, 0, 3, 1, 4) for t in got["s5"]]
    return (xp.reshape(bp, lp, D_MODEL), xs,
            got["gla"],
            jnp.stack([t[0] for t in last], axis=1), jnp.stack([t[1] for t in last], axis=1),
            got["caches"][0], got["caches"][1])
```

```python
import functools
import math

import jax
import jax.numpy as jnp
from jax import lax
from jax.experimental import pallas as pl
from jax.experimental.pallas import tpu as pltpu

F32 = jnp.float32
BF16 = jnp.bfloat16
HIGHEST = lax.Precision.HIGHEST

D_MODEL = 1024
BRANCH = D_MODEL // 2
GLA_HEADS = 4
GLA_DK = 64
GLA_DV = 128
GLA_RANK = 16
GLA_GATE_NORM = 16.0
GLA_CHUNK = 64
GLA_BLOCK = 256
S5_GROUP = 16
S5_GROUPS = BRANCH // S5_GROUP
S5_STATE = 64
S5_CHUNK = 16
S5_ROW = S5_CHUNK * S5_GROUP
S5_SLAB_GROUPS = 128 // S5_GROUP
S5_BLOCK_ROWS = 256
ATT_HEADS = 8
ATT_KV_HEADS = 2
ATT_GROUPS = ATT_HEADS // ATT_KV_HEADS
ATT_HD = 64
WINDOW = 128
QBLK = 128
GRID_W = 64
ROPE_BASE = 10000.0
CONV_W = 3
CONV_HALO_ROWS = 16
EPS = 1e-6
NEG = -1e30

ROW_TILE = 512
VMEM_LIMIT = 48 * 1024 * 1024


def _cparams(*sem):
    return pltpu.CompilerParams(dimension_semantics=sem, vmem_limit_bytes=VMEM_LIMIT)


def _silu(x):
    return x * jax.nn.sigmoid(x)


def _dot(a, b):
    return jnp.dot(a, b, preferred_element_type=F32)


def _dot_nt(a, b):
    return lax.dot_general(a, b, (((1,), (1,)), ((), ())), preferred_element_type=F32)


def _ada_kernel(c_ref, w_ref, b_ref, o_ref):
    c = c_ref[...]
    o_ref[0] = jnp.dot(_silu(c), w_ref[0], precision=HIGHEST, preferred_element_type=F32) + b_ref[0]


def _ada_call(cs, w_ada, b_ada):
    depth = w_ada.shape[0]
    rows = cs.shape[0]
    nt = 3
    return pl.pallas_call(
        _ada_kernel,
        grid=(depth, nt),
        in_specs=[
            pl.BlockSpec((rows, D_MODEL), lambda l, j: (0, 0)),
            pl.BlockSpec((1, D_MODEL, D_MODEL), lambda l, j: (l, 0, j)),
            pl.BlockSpec((1, 1, D_MODEL), lambda l, j: (l, 0, j)),
        ],
        out_specs=pl.BlockSpec((1, rows, D_MODEL), lambda l, j: (l, 0, j)),
        out_shape=jax.ShapeDtypeStruct((depth, rows, 3 * D_MODEL), F32),
        compiler_params=_cparams("arbitrary", "arbitrary"),
        name="adaln",
    )(cs, w_ada, b_ada.reshape(depth, 1, 3 * D_MODEL))


def _modulated(x, mod_ref, nw_ref):
    ms = jnp.mean(x * x, axis=-1, keepdims=True)
    y = x * lax.rsqrt(ms + EPS) * nw_ref[...]
    shift = mod_ref[0, 0:1, :]
    scale = mod_ref[0, 1:2, :]
    return (y * (1.0 + scale) + shift).astype(BF16)


E_COLS = (0, 512, 1024, 1152, 1664, 2176, 2688)


def _log_sigmoid(x):
    return jnp.minimum(x, 0.0) - jnp.log(1.0 + jnp.exp(-jnp.abs(x)))


def _chunk_cumsum(g):
    rows, width = g.shape
    pos = lax.broadcasted_iota(jnp.int32, g.shape, 0) % GLA_CHUNK
    fwd = (lax.broadcasted_iota(jnp.int32, g.shape, 1) % (2 * GLA_DK)) < GLA_DK
    ri = lax.broadcasted_iota(jnp.int32, (GLA_CHUNK, GLA_CHUNK), 0)
    ci = lax.broadcasted_iota(jnp.int32, (GLA_CHUNK, GLA_CHUNK), 1)
    tri = jnp.where(ri >= ci, 1.0, 0.0).astype(BF16)
    g_hi = g.astype(BF16)
    g_lo = (g - g_hi.astype(F32)).astype(BF16)
    c = jnp.concatenate([_dot(tri, g_hi[r:r + GLA_CHUNK, :]) + _dot(tri, g_lo[r:r + GLA_CHUNK, :])
                         for r in range(0, rows, GLA_CHUNK)], axis=0)
    del pos
    total = jnp.concatenate([jnp.broadcast_to(c[r + GLA_CHUNK - 1:r + GLA_CHUNK, :], (GLA_CHUNK, width))
                             for r in range(0, rows, GLA_CHUNK)], axis=0)
    return jnp.where(fwd, c, total - c + g)


def _even_in_compute(x, refs, outs):
    mod_ref, nw_ref, w_ref, w2h_ref, w2l_ref, b2_ref = refs
    qk_ref, v_ref, bc_ref, sg_ref, u_ref, ss_ref = outs
    h = _modulated(x, mod_ref, nw_ref)
    c = E_COLS
    lr = _dot(h, w_ref[:, c[2]:c[3]])
    lr_hi = lr.astype(BF16)
    lr_lo = (lr - lr_hi.astype(F32)).astype(BF16)
    pre = _dot(lr_hi, w2h_ref[...]) + (_dot(lr_lo, w2h_ref[...]) + _dot(lr_hi, w2l_ref[...])) + b2_ref[...]
    bc_ref[0] = _chunk_cumsum(_log_sigmoid(pre) * (1.0 / GLA_GATE_NORM))
    qk_ref[0] = _dot(h, w_ref[:, c[0]:c[1]]).astype(BF16)
    v_ref[0] = _dot(h, w_ref[:, c[1]:c[2]]).astype(BF16)
    sg_ref[0] = _silu(_dot(h, w_ref[:, c[3]:c[4]])).astype(BF16)
    u_ref[0] = _dot(h, w_ref[:, c[4]:c[5]])
    ss_ref[0] = _silu(_dot(h, w_ref[:, c[5]:c[6]])).astype(BF16)


O_COLS = (0, 512, 768, 1280, 1792, 2304, 2816, 3328)


def _rope(x, cos, sin):
    width = x.shape[1]
    lane = lax.broadcasted_iota(jnp.int32, x.shape, 1)
    partner = jnp.where((lane & 16) == 0, pltpu.roll(x, width - 16, 1), pltpu.roll(x, 16, 1))
    return x * cos + partner * sin


def _head_pair_norm(x, w, cos, sin, both):
    low = lax.broadcasted_iota(jnp.int32, x.shape, 1) < ATT_HD
    x2 = x * x
    ms = jnp.sum(jnp.where(low, x2, 0.0), axis=-1, keepdims=True) * (1.0 / ATT_HD)
    rs = lax.rsqrt(ms + EPS)
    if both:
        ms_hi = jnp.sum(jnp.where(low, 0.0, x2), axis=-1, keepdims=True) * (1.0 / ATT_HD)
        rs = jnp.where(low, rs, lax.rsqrt(ms_hi + EPS))
    else:
        rs = jnp.where(low, rs, 1.0)
    y = x * rs * w
    return y if cos is None else _rope(y, cos, sin)


def _odd_in_compute(x, refs, outs, rope):
    if rope:
        mod_ref, nw_ref, w_ref, wkv_ref, qn_ref, kn_ref, cq_ref, sq_ref, ck_ref, sk_ref = refs
        cq, sq, ck, sk = cq_ref[...], sq_ref[...], ck_ref[...], sk_ref[...]
    else:
        mod_ref, nw_ref, w_ref, wkv_ref, qn_ref, kn_ref = refs
        cq = sq = ck = sk = None
    q_ref, kv_ref, sa_ref, p_ref, bgz_ref = outs
    h = _modulated(x, mod_ref, nw_ref)
    c = O_COLS
    q = _dot(h, w_ref[:, c[0]:c[1]])
    for t in range(BRANCH // 128):
        lanes = slice(t * 128, (t + 1) * 128)
        qt = _head_pair_norm(q[:, lanes], qn_ref[:, lanes], cq, sq, True)
        q_ref[0, :, lanes] = (qt * Q_SCALE).astype(BF16)
    kv = _dot(h, wkv_ref[...])
    for t in range(ATT_KV_HEADS):
        lanes = slice(t * 128, (t + 1) * 128)
        kv_ref[0, :, lanes] = _head_pair_norm(kv[:, lanes], kn_ref[:, lanes], ck, sk, False)
    sa_ref[0] = _silu(_dot(h, w_ref[:, c[2]:c[3]])).astype(BF16)
    xc = _dot(h, w_ref[:, c[3]:c[4]])
    bg = _dot(h, w_ref[:, c[4]:c[5]])
    cg = _dot(h, w_ref[:, c[5]:c[6]])
    zc = _dot(h, w_ref[:, c[6]:c[7]])
    p_ref[0] = (cg * xc).astype(BF16)
    bgz_ref[0] = (bg * _silu(zc)).astype(BF16)


def _tile_spec(tm, n):
    return pl.BlockSpec((1, tm, n), lambda b, i: (b, i, 0))


def _whole_spec(a):
    return pl.BlockSpec(a.shape, lambda b, i: (0,) * a.ndim)


def _mod_spec():
    return pl.BlockSpec((1, 3, D_MODEL), lambda b, i: (b, 0, 0))


def _even_in_part(tm, mod, nw, w_in, gla_w):
    args = [mod, nw, w_in, *gla_w]
    specs = [_mod_spec()] + [_whole_spec(a) for a in args[1:]]
    outs = [(BRANCH, dt) for dt in (BF16, BF16, F32, BF16, F32, BF16)]
    return _even_in_compute, args, specs, outs


def _odd_in_part(tm, mod, nw, w_in, qnw, knw, tables):
    args = [mod, nw, *w_in, qnw, knw, *tables]
    specs = ([_mod_spec()] + [_whole_spec(a) for a in args[1:6]]
             + [pl.BlockSpec((tm, a.shape[1]), lambda b, i: (i, 0)) for a in tables])
    outs = [(BRANCH, BF16), (2 * ATT_KV_HEADS * ATT_HD, F32), (BRANCH, BF16), (BRANCH, BF16), (BRANCH, BF16)]
    return functools.partial(_odd_in_compute, rope=bool(tables)), args, specs, outs


def _stage_kernel(*refs, n_out_args, n_in_args, out_fn, in_fn):
    x_ref = refs[0]
    out_args = refs[1:1 + n_out_args]
    in_args = refs[1 + n_out_args:1 + n_out_args + n_in_args]
    outs = refs[1 + n_out_args + n_in_args:]
    x = x_ref[0]
    if out_fn is not None:
        x = out_fn(x, out_args)
        outs[0][0] = x
        outs = outs[1:]
    if in_fn is not None:
        in_fn(x, in_args, outs)


def _stage_call(name, x, out_part, in_part):
    bm, lm, _ = x.shape
    tm = min(ROW_TILE, lm)
    args, specs, out_defs = [x], [_tile_spec(tm, D_MODEL)], []
    out_fn = in_fn = None
    n_out_args = n_in_args = 0
    if out_part is not None:
        out_fn, a, s = out_part
        args, specs, n_out_args = args + a, specs + s, len(a)
        out_defs.append((D_MODEL, F32))
    if in_part is not None:
        in_fn, a, s, o = in_part
        args, specs, n_in_args = args + a, specs + s, len(a)
        out_defs += o
    res = pl.pallas_call(
        functools.partial(_stage_kernel, n_out_args=n_out_args, n_in_args=n_in_args, out_fn=out_fn, in_fn=in_fn),
        grid=(bm, lm // tm),
        in_specs=specs,
        out_specs=[_tile_spec(tm, n) for n, _ in out_defs],
        out_shape=[jax.ShapeDtypeStruct((bm, lm, n), dt) for n, dt in out_defs],
        compiler_params=_cparams("parallel", "parallel"),
        name=name,
    )(*args)
    return res


def _loop(n, body, init, unroll):
    if n <= unroll:
        for i in range(n):
            init = body(i, init)
        return init
    return lax.fori_loop(0, n, body, init, unroll=unroll)


def _gla_kernel(*refs, seq, has_state, heads):
    if has_state:
        (qk_ref, v_ref, bc_ref, sg_ref, on_ref, _, s0_ref, y_ref, sn_ref,
         ut_scr, sf_scr, sb_scr, qd_scr, o_scr) = refs
    else:
        (qk_ref, v_ref, bc_ref, sg_ref, on_ref, _, y_ref, sn_ref,
         ut_scr, sf_scr, sb_scr, qd_scr, o_scr) = refs
        s0_ref = None
    C = GLA_CHUNK
    W = 2 * GLA_DK
    R = GLA_BLOCK
    CB = R // C
    nc = seq // C
    nb = seq // R
    ri = lax.broadcasted_iota(jnp.int32, (R, R), 0)
    ci = lax.broadcasted_iota(jnp.int32, (R, R), 1)
    same = (ri // C) == (ci // C)
    tril = same & (ri >= ci)
    triu = same & (ri <= ci)
    fwd = lax.broadcasted_iota(jnp.int32, (R, W), 1) < GLA_DK
    fwd_row = lax.broadcasted_iota(jnp.int32, (1, W), 1) < GLA_DK
    fwd_st = lax.broadcasted_iota(jnp.int32, (GLA_DV, W), 1) < GLA_DK
    row_chunk = lax.broadcasted_iota(jnp.int32, (R, W), 0) // C
    scale = GLA_DK ** -0.5

    def rows(c):
        return pl.ds(pl.multiple_of(c * C, C), C)

    def block_rows(j):
        return pl.ds(pl.multiple_of(j * R, R), R)

    def state_rows(c):
        return pl.ds(pl.multiple_of(c * GLA_DV, GLA_DV), GLA_DV)

    def intra(h, j, carry):
        r = block_rows(j)
        hl = slice(h * W, (h + 1) * W)
        qk = qk_ref[0, r, hl].astype(F32)
        bc = bc_ref[0, r, hl]
        v = v_ref[0, r, hl].astype(F32)
        sw = pltpu.roll(qk, GLA_DK, 1)
        q2 = jnp.where(fwd, qk, sw) * scale
        k2 = jnp.where(fwd, sw, qk)
        b_mid = bc[C // 2:C // 2 + 1, :]
        b_end = jnp.where(fwd_row, bc[C - 1:C, :], bc[0:1, :])
        for c in range(1, CB):
            o0 = c * C
            b_mid = jnp.where(row_chunk == c, bc[o0 + C // 2:o0 + C // 2 + 1, :], b_mid)
            b_end = jnp.where(row_chunk == c, jnp.where(fwd_row, bc[o0 + C - 1:o0 + C, :], bc[o0:o0 + 1, :]), b_end)
        qd = (q2 * jnp.exp(bc)).astype(BF16)
        qh = q2 * jnp.exp(bc - b_mid)
        kh = (k2 * jnp.exp(b_mid - bc)).astype(BF16)
        att = (jnp.where(tril, _dot_nt(jnp.where(fwd, qh, 0.0).astype(BF16), kh), 0.0)
               + jnp.where(triu, _dot_nt(jnp.where(fwd, 0.0, qh).astype(BF16), kh), 0.0))
        o = _dot(att.astype(BF16), v.astype(BF16))
        kd = k2 * jnp.exp(b_end - bc)
        kd_wide = jnp.concatenate([jnp.where(row_chunk == c, kd, 0.0) for c in range(CB)], axis=1)
        ut = _dot(v.T.astype(BF16), kd_wide.astype(BF16))
        qd_scr[h, r, :] = qd
        o_scr[h, r, :] = o
        ut_scr[h, j] = ut
        return carry

    for h in range(heads):
        _loop(nb, functools.partial(intra, h), 0, 4)

    def scan(h, j, st):
        jb = nb - 1 - j
        hl = slice(h * W, (h + 1) * W)
        bcf = bc_ref[0, block_rows(j), hl]
        bcb = bc_ref[0, block_rows(jb), hl]
        utf = ut_scr[h, j]
        utb = ut_scr[h, jb]
        for c in range(CB):
            cb = CB - 1 - c
            stb = st.astype(BF16)
            sf_scr[h, j, :, c * W:(c + 1) * W] = stb
            sb_scr[h, jb, :, cb * W:(cb + 1) * W] = stb
            b_end = jnp.where(fwd_row, bcf[c * C + C - 1:c * C + C, :], bcb[cb * C:cb * C + 1, :])
            inc = jnp.where(fwd_st, utf[:, c * W:(c + 1) * W], utb[:, cb * W:(cb + 1) * W])
            st = st * jnp.exp(b_end) + inc
        return st

    for h in range(heads):
        if has_state:
            init = jnp.concatenate([s0_ref[0, 0, h].T, s0_ref[0, 1, h].T], axis=1)
        else:
            init = jnp.zeros((GLA_DV, W), F32)
        st = _loop(nb, functools.partial(scan, h), init, 2)
        sn_ref[0, 0, 0, h] = st[:, :GLA_DK].T
        sn_ref[0, 0, 1, h] = st[:, GLA_DK:].T

    fwd_wide = (lax.broadcasted_iota(jnp.int32, (GLA_DV, CB * W), 1) % W) < GLA_DK

    def finish(h, j, carry):
        r = block_rows(j)
        hl = slice(h * W, (h + 1) * W)
        qd = qd_scr[h, r, :]
        zero = jnp.zeros((), BF16)
        qd_wide = jnp.concatenate([jnp.where(row_chunk == c, qd, zero) for c in range(CB)], axis=1)
        s_in = jnp.where(fwd_wide, sf_scr[h, j], sb_scr[h, j])
        o = o_scr[h, r, :] + _dot_nt(qd_wide, s_in)
        ms = jnp.mean(o * o, axis=-1, keepdims=True)
        y_ref[0, r, hl] = (o * lax.rsqrt(ms + EPS) * on_ref[...] * sg_ref[0, r, hl].astype(F32)).astype(BF16)
        return carry

    for h in range(heads):
        _loop(nb, functools.partial(finish, h), 0, 8)


def _gla_call(qk, v, bc, sg, onorm, s0, layer, states):
    bsz, seq, _ = qk.shape
    has_state = s0 is not None
    nb = seq // GLA_BLOCK
    wide = (GLA_BLOCK // GLA_CHUNK) * 2 * GLA_DK
    hps = GLA_HEADS if nb == 1 else 1
    per_head = pl.BlockSpec((1, seq, 128 * hps), lambda b, h: (b, 0, h))
    in_specs = [per_head, per_head, per_head, per_head, pl.BlockSpec((1, GLA_DV), lambda b, h: (0, 0)),
                pl.BlockSpec(memory_space=pl.ANY)]
    args = [qk, v, bc, sg, onorm, states]
    if has_state:
        in_specs.append(pl.BlockSpec((1, 2, hps, GLA_DK, GLA_DV), lambda b, h: (b, 0, h, 0, 0)))
        args.append(s0)
    new_state_spec = pl.BlockSpec((1, 1, 2, hps, GLA_DK, GLA_DV), lambda b, h: (b, layer, 0, h, 0, 0))
    return pl.pallas_call(
        functools.partial(_gla_kernel, seq=seq, has_state=has_state, heads=hps),
        grid=(bsz, GLA_HEADS // hps),
        in_specs=in_specs,
        out_specs=[per_head, new_state_spec],
        out_shape=[jax.ShapeDtypeStruct((bsz, seq, BRANCH), BF16),
                   jax.ShapeDtypeStruct(states.shape, F32)],
        input_output_aliases={5: 1},
        scratch_shapes=[pltpu.VMEM((hps, nb, GLA_DV, wide), F32),
                        pltpu.VMEM((hps, nb, GLA_DV, wide), BF16),
                        pltpu.VMEM((hps, nb, GLA_DV, wide), BF16),
                        pltpu.VMEM((hps, seq, 2 * GLA_DK), BF16),
                        pltpu.VMEM((hps, seq, GLA_DV), F32)],
        compiler_params=_cparams("parallel", "parallel"),
        name="gla",
    )(*args)


def _s5_prep_kernel(lr_ref, li_ref, ld_ref, br_ref, bi_ref, cr_ref, ci_ref,
                    m_ref, win_ref, wout_ref, ar_ref, ai_ref, *, nlev):
    cs = S5_CHUNK
    P2 = 2 * S5_STATE
    lam_re, lam_im = lr_ref[0], li_ref[0]
    dt = jnp.exp(ld_ref[0])
    er, ei = lam_re * dt, lam_im * dt
    mag = jnp.exp(er)
    nr, ni = mag * jnp.cos(ei) - 1.0, mag * jnp.sin(ei)
    den = lam_re * lam_re + lam_im * lam_im
    cr = (nr * lam_re + ni * lam_im) / den
    ci = (ni * lam_re - nr * lam_im) / den
    b_re, b_im = br_ref[0], bi_ref[0]
    bb_re = cr * b_re - ci * b_im
    bb_im = cr * b_im + ci * b_re
    c_re, c_im = cr_ref[0], ci_ref[0]

    def power(k):
        m = jnp.exp(k * er)
        return m * jnp.cos(k * ei), m * jnp.sin(k * ei)

    def outer(a, b):
        return (a[:, None, :] * b[None, :, :]).reshape(S5_ROW, P2)

    def cmul(ar, ai, br, bi):
        return outer(ar, br) - outer(ai, bi), outer(ar, bi) + outer(ai, br)

    s_row = lax.broadcasted_iota(jnp.int32, (cs, P2), 0).astype(F32)
    fwd = lax.broadcasted_iota(jnp.int32, (cs, P2), 1) < S5_STATE
    half = float(cs // 2)
    k_in = jnp.where(fwd, half - s_row, s_row - half)
    l_re, l_im = cmul(*power(k_in), bb_re, bb_im)
    r_re, r_im = cmul(*power(-k_in), c_re, c_im)
    lcat = jnp.concatenate([l_re, l_im], axis=1)
    rcat = jnp.concatenate([r_re, -r_im], axis=1)
    fwd2 = (lax.broadcasted_iota(jnp.int32, (S5_ROW, S5_ROW), 1) % P2) < S5_STATE
    nt = (((1,), (1,)), ((), ()))
    m_f = lax.dot_general(jnp.where(fwd2, lcat, 0.0), rcat, nt, precision=HIGHEST, preferred_element_type=F32)
    m_b = lax.dot_general(jnp.where(fwd2, 0.0, lcat), rcat, nt, precision=HIGHEST, preferred_element_type=F32)
    s_in = lax.broadcasted_iota(jnp.int32, (S5_ROW, S5_ROW), 0) // S5_GROUP
    s_out = lax.broadcasted_iota(jnp.int32, (S5_ROW, S5_ROW), 1) // S5_GROUP
    m_ref[0] = (jnp.where(s_in <= s_out, m_f, 0.0) + jnp.where(s_in >= s_out, m_b, 0.0)).astype(BF16)
    w_re, w_im = cmul(*power(jnp.where(fwd, (cs - 1.0) - s_row, s_row)), bb_re, bb_im)
    win_ref[0] = jnp.concatenate([w_re, w_im], axis=1).astype(BF16)
    o_re, o_im = cmul(*power(jnp.where(fwd, s_row + 1.0, cs - s_row)), c_re, c_im)
    wout_ref[0] = jnp.concatenate([o_re, -o_im], axis=1).astype(BF16)
    lev = lax.shift_left(jnp.int32(cs), lax.broadcasted_iota(jnp.int32, (nlev, P2), 0)).astype(F32)
    ar_ref[0], ai_ref[0] = power(lev)


def _s5_operators(lam_re, lam_im, log_dt, b_re, b_im, c_re, c_im, nlev):
    P2 = 2 * S5_STATE
    lanes = lambda a: a.transpose(1, 0, 2).reshape(S5_GROUPS, 1, P2)
    both = lambda a: jnp.concatenate([a, a], axis=-1)
    args = (lanes(lam_re), lanes(lam_im),
            lanes(jnp.broadcast_to(log_dt[..., None], lam_re.shape)),
            both(b_re.transpose(0, 2, 1)), both(b_im.transpose(0, 2, 1)), both(c_re), both(c_im))
    g3 = lambda g: (g, 0, 0)
    vec = pl.BlockSpec((1, 1, P2), g3)
    mat = pl.BlockSpec((1, S5_GROUP, P2), g3)
    op = pl.BlockSpec((1, S5_ROW, S5_ROW), g3)
    lev = pl.BlockSpec((1, nlev, P2), g3)
    return pl.pallas_call(
        functools.partial(_s5_prep_kernel, nlev=nlev),
        grid=(S5_GROUPS,),
        in_specs=[vec, vec, vec, mat, mat, mat, mat],
        out_specs=[op, op, op, lev, lev],
        out_shape=[jax.ShapeDtypeStruct((S5_GROUPS, S5_ROW, S5_ROW), BF16)] * 3
        + [jax.ShapeDtypeStruct((S5_GROUPS, nlev, P2), F32)] * 2,
        compiler_params=_cparams("parallel"),
        name="s5_prep",
    )(*args)


def _gelu_tanh(x):
    return 0.5 * x * (1.0 + jnp.tanh(math.sqrt(2.0 / math.pi) * (x + 0.044715 * (x * x * x))))


def _slot_transpose(v, slot):
    v = list(v)
    d = len(v) // 2
    while d >= 1:
        take_own = (slot & d) == 0
        for i in range(len(v)):
            if i & d == 0:
                a, b = v[i], v[i + d]
                v[i] = jnp.where(take_own, a, pltpu.roll(b, d * S5_GROUP, 1))
                v[i + d] = jnp.where(take_own, pltpu.roll(a, 128 - d * S5_GROUP, 1), b)
        d //= 2
    return v


def _s5_kernel(*refs, nseq, nrow, nlev, has_state):
    if has_state:
        (u_ref, m_ref, win_ref, wout_ref, ar_ref, ai_ref, d_ref, h0_ref,
         z_ref, last_ref, t_scr, y_scr, tmp_r, tmp_i) = refs
    else:
        (u_ref, m_ref, win_ref, wout_ref, ar_ref, ai_ref, d_ref,
         z_ref, last_ref, t_scr, y_scr, tmp_r, tmp_i) = refs
    P2 = 2 * S5_STATE
    R = nseq * nrow
    GP = S5_SLAB_GROUPS
    TR = 16
    slot = lax.broadcasted_iota(jnp.int32, (TR, 128), 1) // S5_GROUP

    def token_rows(nt, s):
        return pl.ds(pl.multiple_of(nt * (TR * S5_CHUNK), TR * S5_CHUNK) + s, TR, stride=S5_CHUNK)

    def gather(nt, carry):
        for j in range(2):
            src = [u_ref[0, token_rows(nt, 8 * j + s8), :] for s8 in range(8)]
            by_group = _slot_transpose(src, slot)
            for gl in range(GP):
                t_scr[gl, pl.ds(pl.multiple_of(nt * TR, TR), TR), j * 128:(j + 1) * 128] = by_group[gl].astype(BF16)
        return carry

    lax.fori_loop(0, R // TR, gather, 0, unroll=2)

    n = lax.broadcasted_iota(jnp.int32, (R, P2), 0) % nrow
    row = lax.broadcasted_iota(jnp.int32, (R, P2), 0)
    fwd = lax.broadcasted_iota(jnp.int32, (R, P2), 1) < S5_STATE
    fwd_b = lax.broadcasted_iota(jnp.int32, (nseq, P2), 1) < S5_STATE

    def shift(a, k):
        dn = pltpu.roll(a, k, 0)
        up = pltpu.roll(a, R - k, 0)
        return jnp.where(fwd, jnp.where(n >= k, dn, 0.0), jnp.where(n < nrow - k, up, 0.0))

    def ends(t):
        if nseq == 1:
            return jnp.where(fwd_b, t[nrow - 1:nrow, :], t[0:1, :])
        return jnp.where(fwd_b, t[pl.ds(nrow - 1, nseq, stride=nrow), :], t[pl.ds(0, nseq, stride=nrow), :])

    def per_group(gl, carry):
        ub = t_scr[gl]
        x = _dot(ub, win_ref[gl])
        xr, xi = x[:, :P2], x[:, P2:]
        er, ei = shift(xr, 1), shift(xi, 1)
        if has_state:
            h0 = h0_ref[0, gl]
            for b in range(nseq):
                first = row == jnp.where(fwd, b * nrow, b * nrow + nrow - 1)
                er = jnp.where(first, h0[b:b + 1, :P2], er)
                ei = jnp.where(first, h0[b:b + 1, P2:], ei)
        for j in range(nlev):
            k = 2 ** j
            ar = ar_ref[gl, j:j + 1, :]
            ai = ai_ref[gl, j:j + 1, :]
            sr, si = shift(er, k), shift(ei, k)
            er, ei = er + ar * sr - ai * si, ei + ar * si + ai * sr
        ecat = jnp.concatenate([er, ei], axis=1).astype(BF16)
        y_scr[gl] = _dot(ub, m_ref[gl]) + _dot_nt(ecat, wout_ref[gl])
        a1r = ar_ref[gl, 0:1, :]
        a1i = ai_ref[gl, 0:1, :]
        tmp_r[...] = a1r * er - a1i * ei + xr
        tmp_i[...] = a1r * ei + a1i * er + xi
        last_ref[0, gl, :, :P2] = ends(tmp_r)
        last_ref[0, gl, :, P2:] = ends(tmp_i)
        return carry

    lax.fori_loop(0, GP, per_group, 0, unroll=4)

    def scatter(nt, carry):
        for j in range(2):
            ys = [y_scr[gl, pl.ds(pl.multiple_of(nt * TR, TR), TR), j * 128:(j + 1) * 128] for gl in range(GP)]
            by_token = _slot_transpose(ys, slot)
            for s8 in range(8):
                rows = token_rows(nt, 8 * j + s8)
                z_ref[0, rows, :] = _gelu_tanh(by_token[s8] + d_ref[...] * u_ref[0, rows, :])
        return carry

    lax.fori_loop(0, R // TR, scatter, 0, unroll=2)


def _s5_call(u, ops, d, h0, seq):
    m, win, wout, ar, ai = ops
    bm, lm, _ = u.shape
    nrow = seq // S5_CHUNK
    nlev = int(math.log2(nrow))
    total = bm * lm // seq
    nseq = max(1, min(total, S5_BLOCK_ROWS // nrow))
    nblk = total // nseq
    R = nseq * nrow
    tok = R * S5_CHUNK
    GP = S5_SLAB_GROUPS
    ns = BRANCH // 128
    has_state = h0 is not None
    slab = pl.BlockSpec((1, tok, 128), lambda t, i: (i, 0, t))
    op = pl.BlockSpec((GP, S5_ROW, S5_ROW), lambda t, i: (t, 0, 0))
    lev = pl.BlockSpec((GP, ar.shape[1], 2 * S5_STATE), lambda t, i: (t, 0, 0))
    state = pl.BlockSpec((1, GP, nseq, 4 * S5_STATE), lambda t, i: (i, t, 0, 0))
    in_specs = [slab, op, op, op, lev, lev, pl.BlockSpec((1, 128), lambda t, i: (0, t))]
    args = [u.reshape(nblk, tok, BRANCH), m, win, wout, ar, ai, d]
    if has_state:
        in_specs.append(state)
        args.append(h0.reshape(nblk, nseq, S5_GROUPS, 4 * S5_STATE).transpose(0, 2, 1, 3))
    z, last = pl.pallas_call(
        functools.partial(_s5_kernel, nseq=nseq, nrow=nrow, nlev=nlev, has_state=has_state),
        grid=(ns, nblk),
        in_specs=in_specs,
        out_specs=[slab, state],
        out_shape=[jax.ShapeDtypeStruct((nblk, tok, BRANCH), F32),
                   jax.ShapeDtypeStruct((nblk, S5_GROUPS, nseq, 4 * S5_STATE), F32)],
        scratch_shapes=[pltpu.VMEM((GP, R, S5_ROW), BF16), pltpu.VMEM((GP, R, S5_ROW), F32),
                        pltpu.VMEM((R, 2 * S5_STATE), F32), pltpu.VMEM((R, 2 * S5_STATE), F32)],
        compiler_params=_cparams("parallel", "parallel"),
        name="s5",
    )(*args)
    last = last.transpose(0, 2, 1, 3).reshape(total, S5_GROUPS, 4 * S5_STATE)
    return z.reshape(bm, lm, BRANCH), last


HEAD_ORDER = (0, 2, 1, 3)


def _stack_heads(q, heads):
    head = lax.broadcasted_iota(jnp.int32, q.shape, 1) // ATT_HD
    zero = jnp.zeros((), q.dtype)
    return jnp.concatenate([jnp.where(head == h, q, zero) for h in heads], axis=0)


def _key_value_operands(kv):
    low = lax.broadcasted_iota(jnp.int32, kv.shape, 1) < ATT_HD
    sw = pltpu.roll(kv, ATT_HD, 1)
    kk = jnp.where(low, kv, sw).astype(BF16)
    return (jnp.concatenate([kk, kk], axis=1),
            jnp.where(low, sw, 1.0).astype(BF16), jnp.where(low, 1.0, kv).astype(BF16))


LOG2E = 1.4426950408889634
Q_SCALE = (ATT_HD ** -0.5) * LOG2E


def _attend_pair(q, parity, key_sets, sink_ref):
    rows = q.shape[0]
    heads = (parity, parity + 2)
    qs = _stack_heads(q, heads)
    sink = jnp.concatenate([jnp.broadcast_to(sink_ref[h:h + 1, 0:1], (rows, 1)) for h in heads], axis=0) * LOG2E
    scores = []
    t = None
    for k4, _, _, bias in key_sets:
        s = _dot_nt(qs, k4)
        if bias is not None:
            s = s + jnp.concatenate([bias, bias], axis=0)
        scores.append(s)
        for c in range(0, s.shape[1], 128):
            t = s[:, c:c + 128] if t is None else jnp.maximum(t, s[:, c:c + 128])
    m = jnp.maximum(jnp.max(t, axis=-1, keepdims=True), sink)
    lane = lax.broadcasted_iota(jnp.int32, (2 * rows, 2 * ATT_HD), 1)
    den_lanes = (lane >= ATT_HD) if parity == 0 else (lane < ATT_HD)
    o = jnp.where(den_lanes, jnp.exp2(sink - m), 0.0)
    for s, (_, va, vb, _) in zip(scores, key_sets):
        o = o + _dot(jnp.exp2(s - m).astype(BF16), va if parity == 0 else vb)
    return o / pltpu.roll(o, ATT_HD, 1)


def _attend_stacked(q, key_sets, sink_ref):
    rows = q.shape[0]
    qs = _stack_heads(q, HEAD_ORDER)
    sink = jnp.concatenate([jnp.broadcast_to(sink_ref[h:h + 1, 0:1], (rows, 1)) for h in HEAD_ORDER],
                           axis=0) * LOG2E
    scores = []
    t = None
    for k4, _, _, bias in key_sets:
        s = _dot_nt(qs, k4)
        if bias is not None:
            s = s + jnp.concatenate([bias] * ATT_GROUPS, axis=0)
        scores.append(s)
        for c in range(0, s.shape[1], 128):
            t = s[:, c:c + 128] if t is None else jnp.maximum(t, s[:, c:c + 128])
    m = jnp.maximum(jnp.max(t, axis=-1, keepdims=True), sink)
    e_sink = jnp.exp2(sink - m)
    low = lax.broadcasted_iota(jnp.int32, (2 * rows, 2 * ATT_HD), 1) < ATT_HD
    o_even = jnp.where(low, 0.0, e_sink[:2 * rows])
    o_odd = jnp.where(low, e_sink[2 * rows:], 0.0)
    for s, (_, va, vb, _) in zip(scores, key_sets):
        p = jnp.exp2(s - m).astype(BF16)
        o_even = o_even + _dot(p[:2 * rows], va)
        o_odd = o_odd + _dot(p[2 * rows:], vb)
    return o_even / pltpu.roll(o_even, ATT_HD, 1), o_odd / pltpu.roll(o_odd, ATT_HD, 1)


def _attend(q, key_sets, sink_ref, split):
    rows = q.shape[0]
    if split:
        o_even = _attend_pair(q, 0, key_sets, sink_ref)
        o_odd = _attend_pair(q, 1, key_sets, sink_ref)
    else:
        o_even, o_odd = _attend_stacked(q, key_sets, sink_ref)
    low = lax.broadcasted_iota(jnp.int32, o_even.shape, 1) < ATT_HD
    pair = jnp.where(low, o_even, o_odd)
    return jnp.concatenate([pair[:rows], pair[rows:]], axis=1)


def _att_ctx_kernel(q_ref, kv_ref, sa_ref, sink_ref, kprev_ref, vprev_ref, y_ref, kc_ref, vc_ref):
    del kprev_ref, vprev_ref
    W = ATT_GROUPS * ATT_HD
    results = []
    for g in range(ATT_KV_HEADS):
        kv = kv_ref[0, :, g * 2 * ATT_HD:(g + 1) * 2 * ATT_HD]
        heads = slice(g * W, (g + 1) * W)
        k4, v_ones, _ = _key_value_operands(kv)
        key_sets = [(k4, _value_rows(v_ones.astype(F32)), None)]
        out = _attend_keys_on_rows(q_ref[0, :, heads], key_sets, sink_ref.at[g])
        results.append((kv, (out * sa_ref[0, :, heads].astype(F32)).astype(BF16)))
    for g, (kv, y) in enumerate(results):
        kc_ref[0, 0, g] = kv[:, :ATT_HD]
        vc_ref[0, 0, g] = kv[:, ATT_HD:]
        y_ref[0, :, g * W:(g + 1) * W] = y


def _att_ctx_call(q, kv, sa, sink, layer, caches):
    bsz, seq, _ = q.shape
    row = lambda b: (b, 0, 0)
    cache_spec = pl.BlockSpec((1, 1, ATT_KV_HEADS, seq, ATT_HD), lambda b: (b, layer, 0, 0, 0))
    cache_shape = jax.ShapeDtypeStruct(caches[0].shape, F32)
    return pl.pallas_call(
        _att_ctx_kernel,
        grid=(bsz,),
        in_specs=[pl.BlockSpec((1, seq, BRANCH), row), pl.BlockSpec((1, seq, kv.shape[-1]), row),
                  pl.BlockSpec((1, seq, BRANCH), row), pl.BlockSpec(sink.shape, lambda b: (0, 0, 0)),
                  pl.BlockSpec(memory_space=pl.ANY), pl.BlockSpec(memory_space=pl.ANY)],
        out_specs=[pl.BlockSpec((1, seq, BRANCH), row), cache_spec, cache_spec],
        out_shape=[jax.ShapeDtypeStruct((bsz, seq, BRANCH), BF16), cache_shape, cache_shape],
        input_output_aliases={4: 1, 5: 2},
        compiler_params=_cparams("parallel"),
        name="att_ctx",
    )(q, kv, sa, sink, *caches)


def _attend_keys_on_rows(q, key_sets, sink_ref):
    rows = q.shape[0]
    qs = _stack_heads(q, range(ATT_GROUPS))
    scores, top = [], None
    for k4, _, bias_t in key_sets:
        s = _dot_nt(k4, qs)
        if bias_t is not None:
            s = s + jnp.concatenate([bias_t] * ATT_GROUPS, axis=1)
        scores.append(s)
        part = jnp.max(s, axis=0, keepdims=True)
        top = part if top is None else jnp.maximum(top, part)
    sink = jnp.concatenate([jnp.broadcast_to(sink_ref[h:h + 1, 0:1], (1, rows)) for h in range(ATT_GROUPS)],
                           axis=1) * LOG2E
    m = jnp.maximum(top, sink)
    o = None
    for s, (_, vat, _) in zip(scores, key_sets):
        p = jnp.exp2(s - m).astype(BF16)
        for c, vat_c in enumerate(vat):
            pv = _dot(vat_c, p[c * 128:(c + 1) * 128, :])
            o = pv if o is None else o + pv
    on = o[:ATT_HD, :] / (o[ATT_HD:, :] + jnp.exp2(sink - m))
    pairs = [jnp.concatenate([on[:, (2 * t) * rows:(2 * t + 1) * rows],
                              on[:, (2 * t + 1) * rows:(2 * t + 2) * rows]], axis=0).T
             for t in range(ATT_GROUPS // 2)]
    return jnp.concatenate(pairs, axis=1)


def _value_rows(v_ones):
    return [v_ones[c:c + 128, :].T.astype(BF16) for c in range(0, v_ones.shape[0], 128)]


def _att_lat_kernel(q_ref, kv_ref, sa_ref, sink_ref, kc_ref, vc_ref, y_ref, k4_scr, vat_scr, *, seq):
    nb = seq // QBLK
    KW = 3 * QBLK
    low = lax.broadcasted_iota(jnp.int32, (QBLK, 2 * ATT_HD), 1) < ATT_HD

    def k_body(i, carry):
        r = pl.ds(pl.multiple_of(i * QBLK, QBLK), QBLK)
        kv = kv_ref[0, r, :]
        sw = pltpu.roll(kv, ATT_HD, 1)
        kk = jnp.where(low, kv, sw).astype(BF16)
        k4_scr[r, :] = jnp.concatenate([kk, kk], axis=1)
        vat_scr[i] = jnp.where(low, sw, 1.0).T.astype(BF16)
        return carry

    lax.fori_loop(0, nb, k_body, 0, unroll=4)
    kc = kc_ref[0, 0, 0]
    vc = vc_ref[0, 0, 0]
    kc4 = jnp.concatenate([kc] * ATT_GROUPS, axis=1).astype(BF16)
    vat_ctx = _value_rows(jnp.concatenate([vc, jnp.ones_like(vc)], axis=1))
    kj = lax.broadcasted_iota(jnp.int32, (KW, QBLK), 0)
    qi = lax.broadcasted_iota(jnp.int32, (KW, QBLK), 1)

    def q_body(i, carry):
        r = pl.ds(pl.multiple_of(i * QBLK, QBLK), QBLK)
        ws = pl.multiple_of(jnp.clip(i * QBLK - QBLK, 0, seq - KW), QBLK)
        jb = ws // QBLK
        bias_t = jnp.where(jnp.abs((i * QBLK + qi) - (ws + kj)) <= WINDOW, 0.0, NEG)
        key_sets = [(k4_scr[pl.ds(ws, KW), :], [vat_scr[jb + c] for c in range(KW // QBLK)], bias_t),
                    (kc4, vat_ctx, None)]
        out = _attend_keys_on_rows(q_ref[0, r, :], key_sets, sink_ref.at[0])
        y_ref[0, r, :] = (out * sa_ref[0, r, :].astype(F32)).astype(BF16)
        return carry

    lax.fori_loop(0, nb, q_body, 0, unroll=8)


def _rope_tables(seq, heads, pad_heads):
    pos = jnp.arange(seq)
    row = (pos // GRID_W).astype(F32)[:, None]
    col = (pos % GRID_W).astype(F32)[:, None]
    nf = ATT_HD // 4
    freq = ROPE_BASE ** (-jnp.arange(nf, dtype=F32) / nf)
    ar, ac = row * freq, col * freq
    cos = jnp.concatenate([jnp.cos(ar), jnp.cos(ar), jnp.cos(ac), jnp.cos(ac)], axis=1)
    sin = jnp.concatenate([-jnp.sin(ar), jnp.sin(ar), -jnp.sin(ac), jnp.sin(ac)], axis=1)
    cos = jnp.concatenate([jnp.tile(cos, (1, heads)), jnp.ones((seq, pad_heads * ATT_HD), F32)], axis=1)
    sin = jnp.concatenate([jnp.tile(sin, (1, heads)), jnp.zeros((seq, pad_heads * ATT_HD), F32)], axis=1)
    return cos, sin


def _att_lat_call(q, kv, sa, sink, cache_k, cache_v, e):
    bsz, seq, _ = q.shape
    past = cache_k.shape[3]
    W = ATT_GROUPS * ATT_HD
    per_kv = lambda b, k: (b, 0, k)
    cache_spec = pl.BlockSpec((1, 1, 1, past, ATT_HD), lambda b, k: (b, e, k, 0, 0))
    return pl.pallas_call(
        functools.partial(_att_lat_kernel, seq=seq),
        grid=(bsz, ATT_KV_HEADS),
        in_specs=[
            pl.BlockSpec((1, seq, W), per_kv),
            pl.BlockSpec((1, seq, 2 * ATT_HD), per_kv),
            pl.BlockSpec((1, seq, W), per_kv),
            pl.BlockSpec((1, 8, 128), lambda b, k: (k, 0, 0)),
            cache_spec,
            cache_spec,
        ],
        out_specs=pl.BlockSpec((1, seq, W), per_kv),
        out_shape=jax.ShapeDtypeStruct((bsz, seq, BRANCH), BF16),
        scratch_shapes=[pltpu.VMEM((seq, W), BF16), pltpu.VMEM((seq // QBLK, 2 * ATT_HD, QBLK), BF16)],
        compiler_params=_cparams("parallel", "parallel"),
        name="att_lat",
    )(q, kv, sa, sink, cache_k, cache_v)


def _even_out_compute(x, refs):
    mod_ref, yg_ref, z_ref, ss_ref, wg_ref, bg_ref, wo_ref = refs
    z = z_ref[0]
    glu = z * jax.nn.sigmoid(_dot(z.astype(BF16), wg_ref[...]) + bg_ref[...])
    ys = (glu * ss_ref[0].astype(F32)).astype(BF16)
    out = _dot(yg_ref[0], wo_ref[:BRANCH, :]) + _dot(ys, wo_ref[BRANCH:, :])
    return x + mod_ref[0, 2:3, :] * out


def _even_out_part(tm, mod, yg, z, ss, wglu, bglu, wout):
    args = [mod, yg, z, ss, wglu, bglu, wout]
    specs = [_mod_spec(), _tile_spec(tm, BRANCH), _tile_spec(tm, BRANCH), _tile_spec(tm, BRANCH),
             _whole_spec(wglu), _whole_spec(bglu), _whole_spec(wout)]
    return _even_out_compute, args, specs


def _odd_out_compute(x, refs, seq, tm):
    mod_ref, ya_ref, p_ref, pprev_ref, pnext_ref, bgz_ref, cw_ref, cb_ref, wo_ref = refs
    p = p_ref[0].astype(F32)
    pos = (pl.program_id(1) * tm + lax.broadcasted_iota(jnp.int32, (tm, BRANCH), 0)) % seq
    rowi = lax.broadcasted_iota(jnp.int32, (tm, BRANCH), 0)
    halo = CONV_HALO_ROWS
    p_before = pprev_ref[0].astype(F32)[halo - 1:halo, :]
    p_after = pnext_ref[0].astype(F32)[0:1, :]
    prev = jnp.where(rowi == 0, p_before, pltpu.roll(p, 1, 0))
    nxt = jnp.where(rowi == tm - 1, p_after, pltpu.roll(p, tm - 1, 0))
    prev = jnp.where(pos == 0, 0.0, prev)
    nxt = jnp.where(pos == seq - 1, 0.0, nxt)
    conv = prev * cw_ref[0:1, :] + p * cw_ref[1:2, :] + nxt * cw_ref[2:3, :] + cb_ref[...]
    yc = (bgz_ref[0].astype(F32) * conv).astype(BF16)
    out = _dot(ya_ref[0], wo_ref[:BRANCH, :]) + _dot(yc, wo_ref[BRANCH:, :])
    return x + mod_ref[0, 2:3, :] * out


def _odd_out_part(tm, lm, seq, mod, ya, p, bgz, convw, convb, wout):
    hr = CONV_HALO_ROWS
    nth = lm // hr
    prev_spec = pl.BlockSpec((1, hr, BRANCH), lambda b, i: (b, jnp.maximum(i * (tm // hr) - 1, 0), 0))
    next_spec = pl.BlockSpec((1, hr, BRANCH), lambda b, i: (b, jnp.minimum((i + 1) * (tm // hr), nth - 1), 0))
    args = [mod, ya, p, p, p, bgz, convw, convb, wout]
    specs = [_mod_spec(), _tile_spec(tm, BRANCH), _tile_spec(tm, BRANCH), prev_spec, next_spec,
             _tile_spec(tm, BRANCH), _whole_spec(convw), _whole_spec(convb), _whole_spec(wout)]
    return functools.partial(_odd_out_compute, seq=seq, tm=tm), args, specs


def _even_in_weight(w):
    dk = GLA_HEADS * GLA_DK
    q, k = w[:, :dk], w[:, dk:2 * dk]
    v = w[:, 2 * dk:2 * dk + BRANCH]
    o = 2 * dk + BRANCH
    lr = w[:, o:o + 2 * GLA_RANK]
    rest = w[:, o + 2 * GLA_RANK:]
    qk = jnp.concatenate([jnp.concatenate([q[:, h * GLA_DK:(h + 1) * GLA_DK], k[:, h * GLA_DK:(h + 1) * GLA_DK]],
                                          axis=1) for h in range(GLA_HEADS)], axis=1)
    lr = jnp.pad(lr, ((0, 0), (0, 128 - 2 * GLA_RANK)))
    return jnp.concatenate([qk, v, lr, rest], axis=1).astype(BF16)


def _odd_in_weight(w):
    k = w[:, BRANCH:BRANCH + ATT_KV_HEADS * ATT_HD]
    v = w[:, BRANCH + ATT_KV_HEADS * ATT_HD:BRANCH + 2 * ATT_KV_HEADS * ATT_HD]
    kv = jnp.concatenate([jnp.concatenate([k[:, h * ATT_HD:(h + 1) * ATT_HD], v[:, h * ATT_HD:(h + 1) * ATT_HD]],
                                          axis=1) for h in range(ATT_KV_HEADS)], axis=1)
    return w.astype(BF16), kv.astype(BF16)


def _gate_weights(w2, b2):
    zf = jnp.zeros((GLA_RANK, GLA_DK), F32)
    cols, bias = [], []
    for h in range(GLA_HEADS):
        sl = slice(h * GLA_DK, (h + 1) * GLA_DK)
        cols.append(jnp.concatenate([jnp.concatenate([w2[0][:, sl], zf], axis=1),
                                     jnp.concatenate([zf, w2[1][:, sl]], axis=1)], axis=0))
        bias += [b2[0][sl], b2[1][sl]]
    w = jnp.pad(jnp.concatenate(cols, axis=1), ((0, 128 - 2 * GLA_RANK), (0, 0)))
    w_hi = w.astype(BF16)
    w_lo = (w - w_hi.astype(F32)).astype(BF16)
    return w_hi, w_lo, jnp.concatenate(bias)[None, :]


def _run_stream(x, mods, seq, params, gla_s0, s5_h0, cache_k, cache_v):
    depth = len(params)
    bm, lm, _ = x.shape
    tm = min(ROW_TILE, lm)
    nseq = bm * lm // seq
    per_seq = lambda a: a.reshape(nseq, seq, a.shape[-1])
    flat = lambda a: a.reshape(bm, lm, a.shape[-1])
    latent = cache_k is not None
    tables = _rope_tables(seq, 2, 0) + _rope_tables(seq, 1, 1) if latent else ()

    def in_part(l):
        p = params[l]
        if l % 2 == 0:
            return _even_in_part(tm, mods[l], p["nw"], p["w_in"], p["gla_w"])
        return _odd_in_part(tm, mods[l], p["nw"], p["w_in"], p["qnw"], p["knw"], tables)

    n_even, n_odd = (depth + 1) // 2, depth // 2
    gla_states = jnp.zeros((nseq, n_even, 2, GLA_HEADS, GLA_DK, GLA_DV), F32)
    cache_shape = (nseq, n_odd, ATT_KV_HEADS, seq, ATT_HD)
    caches = None if latent else (jnp.zeros(cache_shape, F32), jnp.zeros(cache_shape, F32))
    collected = {"s5": []}
    out_part = None
    for l in range(depth + 1):
        res = _stage_call("stage%d" % l, x, out_part, in_part(l) if l < depth else None)
        if out_part is not None:
            x, res = res[0], res[1:]
        if l == depth:
            break
        p, e = params[l], l // 2
        if l % 2 == 0:
            qk, v, bc, sg, u, ss = res
            yg, gla_states = _gla_call(per_seq(qk), per_seq(v), per_seq(bc), per_seq(sg), p["onorm"],
                                       None if gla_s0 is None else gla_s0[:, e], e, gla_states)
            z, last = _s5_call(u, p["s5_ops"], p["d"], None if s5_h0 is None else s5_h0[e], seq)
            collected["s5"].append(last)
            out_part = _even_out_part(tm, mods[l], flat(yg), z, ss, p["wglu"], p["bglu"], p["wout"])
        else:
            q, kv, sa, pc, bgz = res
            if latent:
                ya = _att_lat_call(per_seq(q), per_seq(kv), per_seq(sa), p["sink"], cache_k, cache_v, e)
            else:
                ya, *caches = _att_ctx_call(per_seq(q), per_seq(kv), per_seq(sa), p["sink"], e, caches)
            out_part = _odd_out_part(tm, lm, seq, mods[l], flat(ya), pc, bgz, p["convw"], p["convb"], p["wout"])
    collected["gla"] = gla_states
    collected["caches"] = caches
    return x, collected


def kernel(x_prompt, x_sample, c, state_gla, state_s5_re, state_s5_im, cache_k, cache_v, c_ctx, norm_w, w_ada, b_ada, w_in_e, w_out_e, gla_w2, gla_b2, gla_onorm, s5_lam_re, s5_lam_im, s5_log_dt, s5_b_re, s5_b_im, s5_c_re, s5_c_im, s5_d, s5_w_glu, s5_b_glu, w_in_o, w_out_o, q_norm_w, k_norm_w, sink, conv_w, conv_b):
    depth = norm_w.shape[0]
    bp, lp, _ = x_prompt.shape
    bs, ls, _ = x_sample.shape

    rows = 8 * ((1 + bs + 7) // 8)
    cs = jnp.zeros((rows, D_MODEL), F32).at[0].set(c_ctx).at[1:1 + bs].set(c)
    mods = _ada_call(cs, w_ada, b_ada)

    params, s5_h0 = [], []
    for l in range(depth):
        e = l // 2
        p = {"nw": norm_w[l][None, :]}
        if l % 2 == 0:
            p.update(
                w_in=_even_in_weight(w_in_e[e]), wout=w_out_e[e].astype(BF16),
                gla_w=_gate_weights(gla_w2[e], gla_b2[e]), onorm=gla_onorm[e][None, :],
                d=s5_d[e][None, :], wglu=s5_w_glu[e].astype(BF16), bglu=s5_b_glu[e][None, :],
                s5_ops=_s5_operators(s5_lam_re[e], s5_lam_im[e], s5_log_dt[e], s5_b_re[e], s5_b_im[e],
                                     s5_c_re[e], s5_c_im[e], nlev=int(math.log2(max(lp, ls) // S5_CHUNK))))
            h0 = jnp.concatenate([state_s5_re[:, e], state_s5_im[:, e]], axis=1)
            s5_h0.append(h0.transpose(0, 2, 1, 3).reshape(bs, S5_GROUPS, 4 * S5_STATE))
        else:
            sk = jnp.broadcast_to(sink[e].reshape(ATT_KV_HEADS, ATT_GROUPS, 1), (ATT_KV_HEADS, ATT_GROUPS, 128))
            p.update(
                w_in=_odd_in_weight(w_in_o[e]), wout=w_out_o[e].astype(BF16),
                qnw=jnp.tile(q_norm_w[e], ATT_HEADS)[None, :],
                knw=jnp.tile(jnp.concatenate([k_norm_w[e], jnp.ones((ATT_HD,), F32)]), ATT_KV_HEADS)[None, :],
                sink=jnp.concatenate([sk, jnp.zeros((ATT_KV_HEADS, 8 - ATT_GROUPS, 128), F32)], axis=1),
                convw=conv_w[e], convb=conv_b[e][None, :])
        params.append(p)

    mods_p = [mods[l, 0:1].reshape(1, 3, D_MODEL) for l in range(depth)]
    mods_s = [mods[l, 1:1 + bs].reshape(bs, 3, D_MODEL) for l in range(depth)]
    xp, got = _run_stream(x_prompt.reshape(1, bp * lp, D_MODEL), mods_p, lp, params, None, None, None, None)
    xs, _ = _run_stream(x_sample, mods_s, ls, params, state_gla, s5_h0, cache_k, cache_v)
    last = [t.reshape(bp, S5_GROUPS, 2, 2, S5_STATE).transpose(2, 0, 3, 1, 4) for t in got["s5"]]
    return (xp.reshape(bp, lp, D_MODEL), xs,
            got["gla"],
            jnp.stack([t[0] for t in last], axis=1), jnp.stack([t[1] for t in last], axis=1),
            got["caches"][0], got["caches"][1])
```
